```python
import jax, jax.numpy as jnp
from jax import lax
import numpy as np

D_MODEL = 2048
BATCH = 4
SEQ = 8192
DEPTH = 1
DEC_BATCH = 8
DEC_SEQ = 16
PAST_LEN = 1024

CHUNK = 64
HEAD_DIM = 128
H_A = 8
H_B = 8
W_A = H_A * HEAD_DIM
W_B = H_B * HEAD_DIM
H_IDX = 4
D_IDX = 64
TOPK_MAX = 256
SB_BLOCK = 128
D_FF = 4 * D_MODEL
CONV_W = 3
ROPE_THETA = 10000.0
RMS_EPS = 1e-6
N_MOD = 6
N_IN = 3 * W_A + 3 * W_B + H_IDX * D_IDX + D_IDX + H_IDX

kernel_name = 'stickbreak_dsa_convffn_stream_step'


def rms_norm(x, g):
    xf = x.astype(jnp.float32)
    y = xf * lax.rsqrt(jnp.mean(xf * xf, axis=-1, keepdims=True) + RMS_EPS)
    return (y * g.astype(jnp.float32)).astype(x.dtype)


def rope(x, pos):
    d = x.shape[-1]
    half = d // 2
    inv = ROPE_THETA ** (-jnp.arange(half, dtype=jnp.float32) * 2.0 / d)
    ang = pos.astype(jnp.float32)[:, None] * inv[None, :]
    cos = jnp.cos(ang)[None, :, None, :]
    sin = jnp.sin(ang)[None, :, None, :]
    xf = x.astype(jnp.float32)
    x1, x2 = xf[..., :half], xf[..., half:]
    return jnp.concatenate([x1 * cos - x2 * sin, x2 * cos + x1 * sin], axis=-1).astype(x.dtype)


def query_block_map(fn, q_args, q_pos, block):
    B, T = q_args[0].shape[:2]
    nb = T // block

    def split(a):
        return jnp.moveaxis(a.reshape((B, nb, block) + a.shape[2:]), 1, 0)

    blocks = tuple(split(a) for a in q_args) + (q_pos.reshape(nb, block),)
    out = lax.map(lambda blk: fn(*blk), blocks)
    out = jnp.moveaxis(out, 0, 1)
    return out.reshape((B, T) + out.shape[3:])


def stick_breaking_block(k, v, k_pos):
    kf = k.astype(jnp.float32)
    vf = v.astype(jnp.float32)
    scale = HEAD_DIM ** -0.5

    def block(qb, qpos):
        z = jnp.einsum('bqhd,bkhd->bhqk', qb.astype(jnp.float32), kf) * scale
        mask = (k_pos[None, :] < qpos[:, None])[None, None]
        log_1m = jnp.where(mask, jax.nn.log_sigmoid(-z), 0.0)
        after = lax.cumsum(log_1m, axis=3, reverse=True) - log_1m
        a = jnp.where(mask, jnp.exp(jax.nn.log_sigmoid(z) + after), 0.0)
        return jnp.einsum('bhqk,bkhd->bqhd', a, vf).astype(qb.dtype)

    return block


def dsa_block(k, v, k_idx, n_sel):
    k_idx_f = k_idx.astype(jnp.float32)
    n_keys = k.shape[1]
    key_chunk = jnp.arange(n_keys, dtype=jnp.int32) // CHUNK

    def block(qb, qib, wb, qpos):
        s = jnp.einsum('bqhe,bke->bqhk', qib.astype(jnp.float32), k_idx_f) * (D_IDX ** -0.5)
        score = jnp.einsum('bqh,bqhk->bqk', wb.astype(jnp.float32) * (H_IDX ** -0.5), jax.nn.relu(s))
        q_chunk = qpos // CHUNK
        adm = key_chunk[None, :] <= q_chunk[:, None]
        score = jnp.where(adm[None], score, -jnp.inf)
        _, idx = lax.top_k(score, n_sel)
        valid = (idx // CHUNK) <= q_chunk[None, :, None]
        kg = jax.vmap(lambda kb, ib: kb[ib])(k, idx)
        vg = jax.vmap(lambda vb, ib: vb[ib])(v, idx)
        logits = jnp.einsum('bqhd,bqkhd->bhqk', qb.astype(jnp.float32), kg.astype(jnp.float32)) * (HEAD_DIM ** -0.5)
        logits = jnp.where(valid[:, None], logits, -jnp.inf)
        p = jax.nn.softmax(logits, axis=-1)
        return jnp.einsum('bhqk,bqkhd->bqhd', p, vg.astype(jnp.float32)).astype(qb.dtype)

    return block


def trunk_layer(x, c, past_a_k, past_a_v, past_b_k, past_b_v, past_b_kidx, conv_state,
                w_mod, b_mod, g_pre_mix, w_in, w_merge_gate, b_merge_gate, w_proj_a, w_proj_b,
                w_out, g_post_mix, g_pre_ffn, w_ffn_gate, w_ffn_up, w_conv, b_conv, w_ffn_down,
                g_post_ffn):
    B, T, D = x.shape
    P = past_a_k.shape[1]
    q_pos = P + jnp.arange(T, dtype=jnp.int32)
    k_pos = jnp.arange(P + T, dtype=jnp.int32)

    mod = (jax.nn.silu(c) @ w_mod + b_mod).reshape(B, N_MOD, 1, D)
    shift_m, scale_m, gate_m = mod[:, 0], mod[:, 1], mod[:, 2]
    shift_f, scale_f, gate_f = mod[:, 3], mod[:, 4], mod[:, 5]

    h = rms_norm(x, g_pre_mix) * (1.0 + scale_m) + shift_m
    proj = h @ w_in
    splits = [int(v) for v in np.cumsum([W_A, W_A, W_A, W_B, W_B, W_B, H_IDX * D_IDX, D_IDX])]
    qa, ka, va, qb, kb, vb, qi, ki, wi = jnp.split(proj, splits, axis=-1)
    qa = qa.reshape(B, T, H_A, HEAD_DIM)
    ka = ka.reshape(B, T, H_A, HEAD_DIM)
    va = va.reshape(B, T, H_A, HEAD_DIM)
    qb = rope(qb.reshape(B, T, H_B, HEAD_DIM), q_pos)
    kb = rope(kb.reshape(B, T, H_B, HEAD_DIM), q_pos)
    vb = vb.reshape(B, T, H_B, HEAD_DIM)
    qi = rope(qi.reshape(B, T, H_IDX, D_IDX), q_pos)
    ki = rope(ki[:, :, None, :], q_pos)[:, :, 0, :]

    ka_all = jnp.concatenate([past_a_k.astype(ka.dtype), ka], axis=1)
    va_all = jnp.concatenate([past_a_v.astype(va.dtype), va], axis=1)
    kb_all = jnp.concatenate([past_b_k.astype(kb.dtype), kb], axis=1)
    vb_all = jnp.concatenate([past_b_v.astype(vb.dtype), vb], axis=1)
    ki_all = jnp.concatenate([past_b_kidx.astype(ki.dtype), ki], axis=1)

    o_a = query_block_map(stick_breaking_block(ka_all, va_all, k_pos), (qa,), q_pos, min(SB_BLOCK, T))
    n_sel = min(TOPK_MAX, (P + T) // 4)
    o_b = query_block_map(dsa_block(kb_all, vb_all, ki_all, n_sel), (qb, qi, wi), q_pos, min(CHUNK, T))

    gates = jax.nn.sigmoid(h @ w_merge_gate + b_merge_gate)
    merged = (gates[..., :D] * (o_a.reshape(B, T, W_A) @ w_proj_a)
              + gates[..., D:] * (o_b.reshape(B, T, W_B) @ w_proj_b))
    x = x + gate_m * rms_norm(merged @ w_out, g_post_mix)

    h2 = rms_norm(x, g_pre_ffn) * (1.0 + scale_f) + shift_f
    up_g = h2 @ w_ffn_gate
    up_v = h2 @ w_ffn_up
    gp = jnp.concatenate([conv_state.astype(up_g.dtype), up_g], axis=1)
    conv = b_conv
    for i in range(CONV_W):
        conv = conv + gp[:, i:i + T] * w_conv[i]
    f = (jax.nn.gelu(conv, approximate=True) * up_v) @ w_ffn_down
    x = x + gate_f * rms_norm(f, g_post_ffn)
    new_conv = gp[:, T:]
    return x, (ka, va, kb, vb, ki, new_conv)


def setup_inputs(seed: int = 0) -> dict:
    key = jax.random.key(seed)
    ks = iter(jax.random.split(key, 40))

    def nrm(shape, scale=1.0):
        return jax.random.normal(next(ks), shape, jnp.float32) * scale

    def gain(shape):
        return 1.0 + 0.05 * jax.random.normal(next(ks), shape, jnp.float32)

    kv_shape = (DEPTH, DEC_BATCH, PAST_LEN, H_A, HEAD_DIM)
    return {
        'x_prompt': nrm((BATCH, SEQ, D_MODEL)),
        'x_sample': nrm((DEC_BATCH, DEC_SEQ, D_MODEL)),
        'c_prompt': nrm((BATCH, D_MODEL)),
        'c_sample': nrm((DEC_BATCH, D_MODEL)),
        'cache_a_k': nrm(kv_shape),
        'cache_a_v': nrm(kv_shape),
        'cache_b_k': nrm((DEPTH, DEC_BATCH, PAST_LEN, H_B, HEAD_DIM)),
        'cache_b_v': nrm((DEPTH, DEC_BATCH, PAST_LEN, H_B, HEAD_DIM)),
        'cache_b_kidx': nrm((DEPTH, DEC_BATCH, PAST_LEN, D_IDX)),
        'state_ffn_conv': nrm((DEPTH, DEC_BATCH, CONV_W - 1, D_FF)),
        'w_mod': nrm((DEPTH, D_MODEL, N_MOD * D_MODEL), 0.5 * D_MODEL ** -0.5),
        'b_mod': nrm((DEPTH, N_MOD * D_MODEL), 0.01),
        'g_pre_mix': gain((DEPTH, D_MODEL)),
        'w_in': nrm((DEPTH, D_MODEL, N_IN), D_MODEL ** -0.5),
        'w_merge_gate': nrm((DEPTH, D_MODEL, 2 * D_MODEL), D_MODEL ** -0.5),
        'b_merge_gate': nrm((DEPTH, 2 * D_MODEL), 0.01),
        'w_proj_a': nrm((DEPTH, W_A, D_MODEL), W_A ** -0.5),
        'w_proj_b': nrm((DEPTH, W_B, D_MODEL), W_B ** -0.5),
        'w_out': nrm((DEPTH, D_MODEL, D_MODEL), D_MODEL ** -0.5),
        'g_post_mix': gain((DEPTH, D_MODEL)),
        'g_pre_ffn': gain((DEPTH, D_MODEL)),
        'w_ffn_gate': nrm((DEPTH, D_MODEL, D_FF), D_MODEL ** -0.5),
        'w_ffn_up': nrm((DEPTH, D_MODEL, D_FF), D_MODEL ** -0.5),
        'w_conv': nrm((DEPTH, CONV_W, D_FF), CONV_W ** -0.5),
        'b_conv': nrm((DEPTH, D_FF), 0.01),
        'w_ffn_down': nrm((DEPTH, D_FF, D_MODEL), D_FF ** -0.5),
        'g_post_ffn': gain((DEPTH, D_MODEL)),
    }


def reference(x_prompt, x_sample, c_prompt, c_sample, cache_a_k, cache_a_v, cache_b_k, cache_b_v,
              cache_b_kidx, state_ffn_conv, w_mod, b_mod, g_pre_mix, w_in, w_merge_gate, b_merge_gate,
              w_proj_a, w_proj_b, w_out, g_post_mix, g_pre_ffn, w_ffn_gate, w_ffn_up, w_conv, b_conv,
              w_ffn_down, g_post_ffn):
    dt = x_prompt.dtype
    empty_a = jnp.zeros((BATCH, 0, H_A, HEAD_DIM), dt)
    empty_b = jnp.zeros((BATCH, 0, H_B, HEAD_DIM), dt)
    empty_i = jnp.zeros((BATCH, 0, D_IDX), dt)
    zero_conv = jnp.zeros((BATCH, CONV_W - 1, D_FF), dt)

    y_p = x_prompt
    y_s = x_sample
    st_p = [[] for _ in range(6)]
    st_s = [[] for _ in range(6)]
    for l in range(DEPTH):
        params = (w_mod[l], b_mod[l], g_pre_mix[l], w_in[l], w_merge_gate[l], b_merge_gate[l],
                  w_proj_a[l], w_proj_b[l], w_out[l], g_post_mix[l], g_pre_ffn[l], w_ffn_gate[l],
                  w_ffn_up[l], w_conv[l], b_conv[l], w_ffn_down[l], g_post_ffn[l])
        y_p, new_p = trunk_layer(y_p, c_prompt, empty_a, empty_a, empty_b, empty_b, empty_i, zero_conv, *params)
        y_s, new_s = trunk_layer(y_s, c_sample, cache_a_k[l], cache_a_v[l], cache_b_k[l], cache_b_v[l],
                                 cache_b_kidx[l], state_ffn_conv[l], *params)
        for i in range(6):
            st_p[i].append(new_p[i])
            st_s[i].append(new_s[i])
    sp = [jnp.stack(s, axis=0) for s in st_p]
    ss = [jnp.stack(s, axis=0) for s in st_s]
    return (y_p, y_s, sp[0], sp[1], sp[2], sp[3], sp[4], sp[5], ss[0], ss[1], ss[2], ss[3], ss[4], ss[5])
```

```python
import functools

import jax
import jax.numpy as jnp
from jax import lax
from jax.experimental import pallas as pl
from jax.experimental.pallas import tpu as pltpu

F32 = jnp.float32
BF16 = jnp.bfloat16
I32 = jnp.int32

CHUNK = 64
HEAD_DIM = 128
H_IDX = 4
D_IDX = 64
TOPK_MAX = 256
CONV_W = 3
ROPE_THETA = 10000.0
RMS_EPS = 1e-6
N_MOD = 6
IDX_COLS = 384
IDX_K = 256
LANES = 128
HALO = 16
NEG = -1e30
INT_MIN = -2 ** 31
VMEM_LIMIT = 56 * 1024 * 1024


def _tile(n, pref, mult):
    t = min(pref, n)
    t -= t % mult
    while t >= mult:
        if n % t == 0:
            return t
        t -= mult
    return n


def _params(sem):
    return pltpu.CompilerParams(dimension_semantics=sem, vmem_limit_bytes=VMEM_LIMIT)


def _split(x):
    hi = x.astype(BF16)
    lo = (x - hi.astype(F32)).astype(BF16)
    return hi, lo


def _dot(a, b):
    return jnp.dot(a, b, preferred_element_type=F32)


def _dot_t(a, b):
    return lax.dot_general(a, b, (((1,), (1,)), ((), ())), preferred_element_type=F32)


def _rms(x, g):
    ms = jnp.mean(x * x, axis=-1, keepdims=True)
    return x * lax.rsqrt(ms + RMS_EPS) * g


def _mod_kernel(c_ref, w_ref, b_ref, o_ref):
    c = c_ref[...]
    s = c / (1.0 + jnp.exp(-c))
    s_hi, s_lo = _split(s)
    w_hi, w_lo = _split(w_ref[...])
    o_ref[...] = _dot(s_hi, w_hi) + _dot(s_hi, w_lo) + _dot(s_lo, w_hi) + b_ref[...]


def _mod(c, w, b):
    bc, d = c.shape
    n = w.shape[1]
    tn = _tile(n, 1024, LANES)
    return pl.pallas_call(
        _mod_kernel,
        grid=(n // tn,),
        in_specs=[pl.BlockSpec((bc, d), lambda j: (0, 0)),
                  pl.BlockSpec((d, tn), lambda j: (0, j)),
                  pl.BlockSpec((1, tn), lambda j: (0, j))],
        out_specs=pl.BlockSpec((bc, tn), lambda j: (0, j)),
        out_shape=jax.ShapeDtypeStruct((bc, n), F32),
        compiler_params=_params(("arbitrary",)),
        name="mod",
    )(c, w, b.reshape(1, n))


def _norm_kernel(x_ref, g_ref, mod_ref, hi_ref, lo_ref):
    m = mod_ref[0]
    h = _rms(x_ref[...], g_ref[...]) * (1.0 + m[1:2, :]) + m[0:1, :]
    hi, lo = _split(h)
    hi_ref[...] = hi
    lo_ref[...] = lo


def _norm(x2, g, mod, seq):
    n, d = x2.shape
    tr = _tile(seq, 256, 16)
    tps = seq // tr
    return pl.pallas_call(
        _norm_kernel,
        grid=(n // tr,),
        in_specs=[pl.BlockSpec((tr, d), lambda i: (i, 0)),
                  pl.BlockSpec((1, d), lambda i: (0, 0)),
                  pl.BlockSpec((1, N_MOD, d), lambda i: (i // tps, 0, 0))],
        out_specs=[pl.BlockSpec((tr, d), lambda i: (i, 0))] * 2,
        out_shape=[jax.ShapeDtypeStruct((n, d), BF16)] * 2,
        compiler_params=_params(("arbitrary",)),
        name="norm",
    )(x2, g.reshape(1, d), mod)


def _inproj_kernel(a_ref, w_ref, cos_ref, sin_ref,
                   qa_ref, ka32_ref, ka16_ref, va32_ref, va16_ref,
                   qb_ref, kb32_ref, kb16_ref, vb32_ref, vb16_ref, *, npj, tn, scale):
    grp = pl.program_id(1) // npj
    acc = _dot(a_ref[...], w_ref[...])

    def rope(x):
        parts = []
        for s in range(tn // HEAD_DIM):
            xs = x[:, s * HEAD_DIM:(s + 1) * HEAD_DIM]
            parts.append(xs * cos_ref[...] + pltpu.roll(xs, HEAD_DIM // 2, 1) * sin_ref[...])
        return parts[0] if len(parts) == 1 else jnp.concatenate(parts, axis=1)

    @pl.when(grp == 0)
    def _():
        qa_ref[...] = (acc * scale).astype(BF16)

    @pl.when(grp == 1)
    def _():
        ka32_ref[...] = acc
        ka16_ref[...] = acc.astype(BF16)

    @pl.when(grp == 2)
    def _():
        va32_ref[...] = acc
        va16_ref[...] = acc.astype(BF16)

    @pl.when(grp == 3)
    def _():
        qb_ref[...] = (rope(acc) * scale).astype(BF16)

    @pl.when(grp == 4)
    def _():
        r = rope(acc)
        kb32_ref[...] = r
        kb16_ref[...] = r.astype(BF16)

    @pl.when(grp == 5)
    def _():
        vb32_ref[...] = acc
        vb16_ref[...] = acc.astype(BF16)


def _inproj(h_hi, w_main, cos_t, sin_t, seq, width):
    n, d = h_hi.shape
    tm = _tile(n, 512, 16)
    tn = _tile(width, 512, LANES)
    npj = width // tn
    if tm <= seq:
        tps = seq // tm
        tab_map = lambda i, j: (i % tps, 0)
    else:
        cos_t = jnp.tile(cos_t, (tm // seq, 1))
        sin_t = jnp.tile(sin_t, (tm // seq, 1))
        tab_map = lambda i, j: (0, 0)

    def omap(g):
        return lambda i, j: (i, jnp.clip(j - g * npj, 0, npj - 1))

    dts = [BF16, F32, BF16, F32, BF16, BF16, F32, BF16, F32, BF16]
    grps = [0, 1, 1, 2, 2, 3, 4, 4, 5, 5]
    return pl.pallas_call(
        functools.partial(_inproj_kernel, npj=npj, tn=tn, scale=HEAD_DIM ** -0.5),
        grid=(n // tm, 6 * npj),
        in_specs=[pl.BlockSpec((tm, d), lambda i, j: (i, 0)),
                  pl.BlockSpec((d, tn), lambda i, j: (0, j)),
                  pl.BlockSpec((tm, HEAD_DIM), tab_map),
                  pl.BlockSpec((tm, HEAD_DIM), tab_map)],
        out_specs=[pl.BlockSpec((tm, tn), omap(g)) for g in grps],
        out_shape=[jax.ShapeDtypeStruct((n, width), dt) for dt in dts],
        compiler_params=_params(("arbitrary", "arbitrary")),
        name="inproj",
    )(h_hi, w_main, cos_t, sin_t)


def _idxproj_kernel(hi_ref, lo_ref, whi_ref, wlo_ref, c_ref, s1_ref, s2_ref,
                    qhi_ref, qlo_ref, kw_ref, khi_ref, klo_ref):
    a_hi = hi_ref[...]
    w_hi = whi_ref[...]
    acc = _dot(a_hi, w_hi) + _dot(a_hi, wlo_ref[...]) + _dot(lo_ref[...], w_hi)
    for s in range(IDX_COLS // LANES):
        sl = slice(s * LANES, (s + 1) * LANES)
        xs = acc[:, sl]
        r = (xs * c_ref[:, sl] + pltpu.roll(xs, LANES - D_IDX // 2, 1) * s1_ref[:, sl]
             + pltpu.roll(xs, D_IDX // 2, 1) * s2_ref[:, sl])
        hi, lo = _split(r)
        if s < 2:
            qhi_ref[:, sl] = hi
            qlo_ref[:, sl] = lo
        else:
            kw_ref[...] = r
            khi_ref[...] = hi
            klo_ref[...] = lo


def _idxproj(h_hi, h_lo, w_hi, w_lo, c_t, s1_t, s2_t, seq):
    n, d = h_hi.shape
    tm = _tile(n, 512, 16)
    if tm <= seq:
        tps = seq // tm
        tab_map = lambda i: (i % tps, 0)
    else:
        c_t, s1_t, s2_t = (jnp.tile(t, (tm // seq, 1)) for t in (c_t, s1_t, s2_t))
        tab_map = lambda i: (0, 0)
    row = lambda i: (i, 0)
    return pl.pallas_call(
        _idxproj_kernel,
        grid=(n // tm,),
        in_specs=[pl.BlockSpec((tm, d), row), pl.BlockSpec((tm, d), row),
                  pl.BlockSpec((d, IDX_COLS), lambda i: (0, 0)),
                  pl.BlockSpec((d, IDX_COLS), lambda i: (0, 0)),
                  pl.BlockSpec((tm, IDX_COLS), tab_map),
                  pl.BlockSpec((tm, IDX_COLS), tab_map),
                  pl.BlockSpec((tm, IDX_COLS), tab_map)],
        out_specs=[pl.BlockSpec((tm, 2 * LANES), row), pl.BlockSpec((tm, 2 * LANES), row),
                   pl.BlockSpec((tm, LANES), row), pl.BlockSpec((tm, LANES), row),
                   pl.BlockSpec((tm, LANES), row)],
        out_shape=[jax.ShapeDtypeStruct((n, 2 * LANES), BF16), jax.ShapeDtypeStruct((n, 2 * LANES), BF16),
                   jax.ShapeDtypeStruct((n, LANES), F32), jax.ShapeDtypeStruct((n, LANES), BF16),
                   jax.ShapeDtypeStruct((n, LANES), BF16)],
        compiler_params=_params(("arbitrary",)),
        name="idxproj",
    )(h_hi, h_lo, w_hi, w_lo, c_t, s1_t, s2_t)


def _sb_kernel(q_ref, k_ref, v_ref, o_ref, *, tq, tk, past):
    q0 = past + pl.program_id(2) * tq
    n_kt = (q0 + tq - 2) // tk + 1
    n_clear = q0 // tk
    q = q_ref[...]
    row_pos = q0 + lax.broadcasted_iota(I32, (tq, 1), 0)
    col = lax.broadcasted_iota(I32, (1, tk), 1)
    suffix = jnp.where(lax.broadcasted_iota(I32, (tk, tk), 0) >= lax.broadcasted_iota(I32, (tk, tk), 1),
                       1.0, 0.0).astype(BF16)

    def tile(j, carry, acc, masked):
        off = pl.multiple_of(j * tk, tk)
        z = _dot_t(q, k_ref[pl.ds(off, tk), :])
        sp = jnp.maximum(z, 0.0) + jnp.log(1.0 + jnp.exp(-jnp.abs(z)))
        if masked:
            valid = (off + col) < row_pos
            sp = jnp.where(valid, sp, 0.0)
        hi, lo = _split(sp)
        incl = _dot(hi, suffix) + _dot(lo, suffix)
        w = jnp.exp(z - incl - carry)
        if masked:
            w = jnp.where(valid, w, 0.0)
        acc = acc + _dot(w.astype(BF16), v_ref[pl.ds(off, tk), :])
        return carry + incl[:, 0:1], acc

    init = (jnp.zeros((tq, 1), F32), jnp.zeros((tq, HEAD_DIM), F32))
    st = lax.fori_loop(0, n_kt - n_clear, lambda t, c: tile(n_kt - 1 - t, c[0], c[1], True), init)
    st = lax.fori_loop(0, n_clear, lambda t, c: tile(n_clear - 1 - t, c[0], c[1], False), st)
    o_ref[...] = st[1].astype(o_ref.dtype)


def _sb(q, k_all, v_all, nb, seq, n_keys, past, nh):
    tq = _tile(seq, 256, 16)
    tk = 256
    nq = seq // tq
    return pl.pallas_call(
        functools.partial(_sb_kernel, tq=tq, tk=tk, past=past),
        grid=(nb, nh, nq),
        in_specs=[pl.BlockSpec((tq, HEAD_DIM), lambda b, h, i: (b * nq + i, h)),
                  pl.BlockSpec((n_keys, HEAD_DIM), lambda b, h, i: (b, h)),
                  pl.BlockSpec((n_keys, HEAD_DIM), lambda b, h, i: (b, h))],
        out_specs=pl.BlockSpec((tq, HEAD_DIM), lambda b, h, i: (b * nq + i, h)),
        out_shape=jax.ShapeDtypeStruct(q.shape, BF16),
        compiler_params=_params(("arbitrary", "arbitrary", "arbitrary")),
        name="sb",
    )(q, k_all, v_all)


def _dsa_kernel(q_ref, k_ref, v_ref, qi_ref, ki_ref, kw_ref, o_ref, key_scr,
                *, tq, tk, past, n_valid, n_sel, nh):
    q0 = past + pl.program_id(1) * tq
    row_pos = q0 + lax.broadcasted_iota(I32, (tq, 1), 0)
    row_lim = jnp.minimum((row_pos // CHUNK + 1) * CHUNK, n_valid)
    n_kt = (jnp.minimum(((q0 + tq - 1) // CHUNK + 1) * CHUNK, n_valid) + tk - 1) // tk
    col = lax.broadcasted_iota(I32, (1, tk), 1)
    ksel = float(n_sel)

    wrow = kw_ref[:, D_IDX:D_IDX + H_IDX] * ((H_IDX ** -0.5) * (D_IDX ** -0.5))

    def score_tile(j, _):
        off = pl.multiple_of(j * tk, tk)
        kc = ki_ref[pl.ds(off, tk), :]
        score = jnp.zeros((tq, tk), F32)
        for h in range(H_IDX):
            s = _dot_t(qi_ref[:, h * IDX_K:(h + 1) * IDX_K], kc)
            score = score + wrow[:, h:h + 1] * jnp.maximum(s, 0.0)
        bits = pltpu.bitcast(score, I32)
        key = jnp.where(bits < 0, INT_MIN - bits, bits)
        key_scr[:, pl.ds(off, tk)] = jnp.where((off + col) < row_lim, key, INT_MIN)
        return 0

    lax.fori_loop(0, n_kt, score_tile, 0)

    def count_ge(thr_s):
        def body(j, cnt):
            off = pl.multiple_of(j * tk, tk)
            key = key_scr[:, pl.ds(off, tk)]
            for s in range(tk // LANES):
                cnt = cnt + jnp.where(key[:, s * LANES:(s + 1) * LANES] >= thr_s, 1.0, 0.0)
            return cnt
        cnt = lax.fori_loop(0, n_kt, body, jnp.zeros((tq, LANES), F32))
        return jnp.sum(cnt, axis=1, keepdims=True)

    def bit_step(it, cand):
        trial = cand | lax.shift_left(jnp.int32(1), 31 - it)
        cnt = count_ge(trial ^ INT_MIN)
        return jnp.where(cnt >= ksel, trial, cand)

    cand = lax.fori_loop(0, 32, bit_step, jnp.zeros((tq, 1), I32))
    thr = cand ^ INT_MIN
    no_thr = thr == INT_MIN
    cnt_gt = count_ge(jnp.where(no_thr, thr, thr + 1))
    need = jnp.where(no_thr, 0.0, ksel - cnt_gt)

    prefix = jnp.where(lax.broadcasted_iota(I32, (tk, tk), 0) <= lax.broadcasted_iota(I32, (tk, tk), 1),
                       1.0, 0.0).astype(BF16)

    def bias_tile(j, carry):
        off = pl.multiple_of(j * tk, tk)
        key = key_scr[:, pl.ds(off, tk)]
        eq = key == thr
        eqf = jnp.where(eq, 1.0, 0.0)
        rank_incl = _dot(eqf.astype(BF16), prefix) + carry
        tie_ok = (rank_incl - eqf) < need
        bias = jnp.where(key > thr, 0.0, jnp.where(eq, jnp.where(tie_ok, 0.0, NEG), NEG))
        key_scr[:, pl.ds(off, tk)] = pltpu.bitcast(bias, I32)
        return rank_incl[:, tk - 1:tk]

    lax.fori_loop(0, n_kt, bias_tile, jnp.zeros((tq, 1), F32))

    for h in range(nh):
        hs = slice(h * HEAD_DIM, (h + 1) * HEAD_DIM)
        qh = q_ref[:, hs]

        def attn_tile(j, st):
            m, l, acc = st
            off = pl.multiple_of(j * tk, tk)
            s = _dot_t(qh, k_ref[pl.ds(off, tk), hs]) + pltpu.bitcast(key_scr[:, pl.ds(off, tk)], F32)
            m_new = jnp.maximum(m, jnp.max(s, axis=1, keepdims=True))
            alpha = jnp.exp(m - m_new)
            p = jnp.exp(s - m_new)
            l = alpha * l + jnp.sum(p, axis=1, keepdims=True)
            acc = alpha * acc + _dot(p.astype(BF16), v_ref[pl.ds(off, tk), hs])
            return m_new, l, acc

        init = (jnp.full((tq, 1), NEG, F32), jnp.zeros((tq, 1), F32), jnp.zeros((tq, HEAD_DIM), F32))
        m, l, acc = lax.fori_loop(0, n_kt, attn_tile, init)
        o_ref[:, hs] = (acc / l).astype(o_ref.dtype)


def _dsa(q, k_all, v_all, qi_cat, ki_cat, kw, nb, seq, n_keys, n_valid, past, nh):
    tq = _tile(seq, 256, 16)
    tk = 256
    nq = seq // tq
    width = nh * HEAD_DIM
    n_sel = min(TOPK_MAX, n_valid // 4)
    once = pl.Buffered(1)
    return pl.pallas_call(
        functools.partial(_dsa_kernel, tq=tq, tk=tk, past=past, n_valid=n_valid, n_sel=n_sel, nh=nh),
        grid=(nb, nq),
        in_specs=[pl.BlockSpec((tq, width), lambda b, i: (b * nq + i, 0)),
                  pl.BlockSpec((n_keys, width), lambda b, i: (b, 0), pipeline_mode=once),
                  pl.BlockSpec((n_keys, width), lambda b, i: (b, 0), pipeline_mode=once),
                  pl.BlockSpec((tq, H_IDX * IDX_K), lambda b, i: (b * nq + i, 0)),
                  pl.BlockSpec((n_keys, IDX_K), lambda b, i: (b, 0), pipeline_mode=once),
                  pl.BlockSpec((tq, LANES), lambda b, i: (b * nq + i, 0))],
        out_specs=pl.BlockSpec((tq, width), lambda b, i: (b * nq + i, 0)),
        out_shape=jax.ShapeDtypeStruct(q.shape, BF16),
        scratch_shapes=[pltpu.VMEM((tq, n_keys), I32)],
        compiler_params=_params(("arbitrary", "arbitrary")),
        name="dsa",
    )(q, k_all, v_all, qi_cat, ki_cat, kw)


def _merge_kernel(h_ref, oa_ref, ob_ref, wg1_ref, wg2_ref, b1_ref, b2_ref, wa_ref, wb_ref, o_ref):
    h = h_ref[...]
    g1 = 1.0 / (1.0 + jnp.exp(-(_dot(h, wg1_ref[...]) + b1_ref[...])))
    g2 = 1.0 / (1.0 + jnp.exp(-(_dot(h, wg2_ref[...]) + b2_ref[...])))
    o_ref[...] = (g1 * _dot(oa_ref[...], wa_ref[...]) + g2 * _dot(ob_ref[...], wb_ref[...])).astype(BF16)


def _merge(h_hi, o_a, o_b, w_gate, b_gate, w_pa, w_pb):
    n, d = h_hi.shape
    width = o_a.shape[1]
    tm = _tile(n, 512, 16)
    tn = _tile(d, 512, LANES)
    nj = d // tn
    b_gate = b_gate.reshape(1, 2 * d)
    row = lambda i, j: (i, 0)
    return pl.pallas_call(
        _merge_kernel,
        grid=(n // tm, nj),
        in_specs=[pl.BlockSpec((tm, d), row), pl.BlockSpec((tm, width), row), pl.BlockSpec((tm, width), row),
                  pl.BlockSpec((d, tn), lambda i, j: (0, j)),
                  pl.BlockSpec((d, tn), lambda i, j: (0, j + nj)),
                  pl.BlockSpec((1, tn), lambda i, j: (0, j)),
                  pl.BlockSpec((1, tn), lambda i, j: (0, j + nj)),
                  pl.BlockSpec((width, tn), lambda i, j: (0, j)),
                  pl.BlockSpec((width, tn), lambda i, j: (0, j))],
        out_specs=pl.BlockSpec((tm, tn), lambda i, j: (i, j)),
        out_shape=jax.ShapeDtypeStruct((n, d), BF16),
        compiler_params=_params(("arbitrary", "arbitrary")),
        name="merge",
    )(h_hi, o_a, o_b, w_gate, w_gate, b_gate, b_gate, w_pa, w_pb)


def _outproj_kernel(m_ref, w_ref, x_ref, mod_ref, gm_ref, gf_ref, x1_ref, h2_ref):
    mod = mod_ref[0]
    x1 = x_ref[...] + mod[2:3, :] * _rms(_dot(m_ref[...], w_ref[...]), gm_ref[...])
    x1_ref[...] = x1
    h2_ref[...] = (_rms(x1, gf_ref[...]) * (1.0 + mod[4:5, :]) + mod[3:4, :]).astype(BF16)


def _outproj(merged, w_out, x2, mod, g_post_mix, g_pre_ffn, seq):
    n, d = x2.shape
    tm = _tile(seq, 256, 16)
    tps = seq // tm
    row = lambda i: (i, 0)
    fix = lambda i: (0, 0)
    return pl.pallas_call(
        _outproj_kernel,
        grid=(n // tm,),
        in_specs=[pl.BlockSpec((tm, d), row), pl.BlockSpec((d, d), fix), pl.BlockSpec((tm, d), row),
                  pl.BlockSpec((1, N_MOD, d), lambda i: (i // tps, 0, 0)),
                  pl.BlockSpec((1, d), fix), pl.BlockSpec((1, d), fix)],
        out_specs=[pl.BlockSpec((tm, d), row), pl.BlockSpec((tm, d), row)],
        out_shape=[jax.ShapeDtypeStruct((n, d), F32), jax.ShapeDtypeStruct((n, d), BF16)],
        compiler_params=_params(("arbitrary",)),
        name="outproj",
    )(merged, w_out, x2, mod, g_post_mix.reshape(1, d), g_pre_ffn.reshape(1, d))


def _gelu_tanh(x):
    return 0.5 * x * (1.0 + jnp.tanh(0.7978845608028654 * (x + 0.044715 * (x * x * x))))


def _ffn_kernel(h_ref, halo_ref, x1_ref, mod_ref, st_ref, wg_ref, wu_ref, wd_ref, wc_ref, bc_ref, gp_ref,
                y_ref, nc_ref, g_scr, act_scr, acc_scr, *, tm, nsub, tps):
    i = pl.program_id(0)
    c = pl.program_id(1)
    sub = tm // nsub
    h = h_ref[...]
    wg = wg_ref[...]
    g = _dot(h, wg)
    u = _dot(h, wu_ref[...])
    wc = wc_ref[...]
    bc = bc_ref[...]
    g_scr[HALO:HALO + tm, :] = g

    def conv(lo, rows):
        return (bc + wc[0:1, :] * g_scr[lo - 2:lo - 2 + rows, :] + wc[1:2, :] * g_scr[lo - 1:lo - 1 + rows, :]
                + wc[2:3, :] * g_scr[lo:lo + rows, :])

    @pl.when(c == 0)
    def _():
        acc_scr[...] = jnp.zeros_like(acc_scr)

    if nsub == 1:
        g_scr[0:HALO, :] = _dot(halo_ref[...], wg)

        @pl.when(i % tps == 0)
        def _():
            g_scr[HALO - 2:HALO, :] = st_ref[0]

        nc_ref[0] = g_scr[HALO + tm - 2:HALO + tm, :]
        act = _gelu_tanh(conv(HALO, tm)) * u
        acc_scr[...] += _dot(act.astype(BF16), wd_ref[...])
    else:
        for s in range(nsub):
            lo = HALO + s * sub
            nc_ref[s] = g_scr[lo + sub - 2:lo + sub, :]
            g_scr[lo - 2:lo, :] = st_ref[s]
            act_scr[s * sub:(s + 1) * sub, :] = _gelu_tanh(conv(lo, sub)) * u[s * sub:(s + 1) * sub, :]
        acc_scr[...] += _dot(act_scr[...].astype(BF16), wd_ref[...])

    @pl.when(c == pl.num_programs(1) - 1)
    def _():
        for s in range(nsub):
            rows = slice(s * sub, (s + 1) * sub)
            y_ref[rows, :] = x1_ref[rows, :] + mod_ref[s][5:6, :] * _rms(acc_scr[rows, :], gp_ref[...])


def _ffn(h2, x1, mod, state, w_gate, w_up, w_down, w_conv, b_conv, g_post, seq):
    n, d = x1.shape
    dff = w_gate.shape[1]
    nb = n // seq
    tm = _tile(n, 512, 16)
    tf = _tile(dff, 512, LANES)
    per_tile_c = lambda i, c: (i, 0, c)
    if tm <= seq:
        nsub, tps = 1, seq // tm
        per_seq = lambda i, c: (i // tps, 0, 0)
        per_seq_c = lambda i, c: (i // tps, 0, c)
    else:
        nsub, tps = tm // seq, 1
        per_seq = lambda i, c: (i, 0, 0)
        per_seq_c = per_tile_c
    halo_blocks = tm // HALO
    row = lambda i, c: (i, 0)
    y, new_conv = pl.pallas_call(
        functools.partial(_ffn_kernel, tm=tm, nsub=nsub, tps=tps),
        grid=(n // tm, dff // tf),
        in_specs=[pl.BlockSpec((tm, d), row),
                  pl.BlockSpec((HALO, d), lambda i, c: (jnp.maximum(i * halo_blocks - 1, 0), 0)),
                  pl.BlockSpec((tm, d), row),
                  pl.BlockSpec((nsub, N_MOD, d), per_seq),
                  pl.BlockSpec((nsub, CONV_W - 1, tf), per_seq_c),
                  pl.BlockSpec((d, tf), lambda i, c: (0, c)),
                  pl.BlockSpec((d, tf), lambda i, c: (0, c)),
                  pl.BlockSpec((tf, d), lambda i, c: (c, 0)),
                  pl.BlockSpec((CONV_W, tf), lambda i, c: (0, c)),
                  pl.BlockSpec((1, tf), lambda i, c: (0, c)),
                  pl.BlockSpec((1, d), lambda i, c: (0, 0))],
        out_specs=[pl.BlockSpec((tm, d), row),
                   pl.BlockSpec((nsub, CONV_W - 1, tf), per_tile_c)],
        out_shape=[jax.ShapeDtypeStruct((n, d), F32),
                   jax.ShapeDtypeStruct((nb * tps, CONV_W - 1, dff), F32)],
        scratch_shapes=[pltpu.VMEM((HALO + tm, tf), F32), pltpu.VMEM((tm, tf), F32), pltpu.VMEM((tm, d), F32)],
        compiler_params=_params(("arbitrary", "arbitrary")),
        name="ffn",
    )(h2, h2, x1, mod, state, w_gate, w_up, w_down, w_conv, b_conv.reshape(1, dff), g_post.reshape(1, d))
    return y, new_conv.reshape(nb, tps, CONV_W - 1, dff)[:, tps - 1]


def _rope_tables(pos):
    def angles(dim):
        half = dim // 2
        inv = ROPE_THETA ** (-jnp.arange(half, dtype=F32) * 2.0 / dim)
        ang = pos.astype(F32)[:, None] * inv[None, :]
        return jnp.cos(ang), jnp.sin(ang)

    cos, sin = angles(HEAD_DIM)
    cos_h = jnp.concatenate([cos, cos], axis=1)
    sin_h = jnp.concatenate([-sin, sin], axis=1)

    cos, sin = angles(D_IDX)
    zero = jnp.zeros_like(sin)
    n_rot = H_IDX + 1
    tail = IDX_COLS - n_rot * D_IDX
    t = pos.shape[0]
    c_i = jnp.concatenate([jnp.tile(jnp.concatenate([cos, cos], 1), (1, n_rot)), jnp.ones((t, tail), F32)], 1)
    s1_i = jnp.concatenate([jnp.tile(jnp.concatenate([-sin, zero], 1), (1, n_rot)), jnp.zeros((t, tail), F32)], 1)
    s2_i = jnp.concatenate([jnp.tile(jnp.concatenate([zero, sin], 1), (1, n_rot)), jnp.zeros((t, tail), F32)], 1)
    return cos_h, sin_h, c_i, s1_i, s2_i


def _with_past(past, new, nb, seq, n_keys):
    w = new.shape[1]
    if past is None and n_keys == seq:
        return new
    parts = [] if past is None else [past.astype(new.dtype).reshape(nb, -1, w)]
    parts.append(new.reshape(nb, seq, w))
    have = sum(p.shape[1] for p in parts)
    if n_keys > have:
        parts.append(jnp.zeros((nb, n_keys - have, w), new.dtype))
    return jnp.concatenate(parts, axis=1).reshape(nb * n_keys, w)


def _layer(x, mod, past, conv_state, p):
    nb, seq, d = x.shape
    n = nb * seq
    nh = p["w_pa"].shape[0] // HEAD_DIM
    width = nh * HEAD_DIM
    n_past = 0 if past is None else past["a_k"].shape[1]
    n_valid = n_past + seq
    n_keys = -(-n_valid // 256) * 256
    x2 = x.reshape(n, d)

    h_hi, h_lo = _norm(x2, p["g_pre_mix"], mod, seq)
    cos_h, sin_h, c_i, s1_i, s2_i = _rope_tables(n_past + jnp.arange(seq, dtype=I32))
    (qa, ka32, ka16, va32, va16, qb, kb32, kb16, vb32, vb16) = _inproj(h_hi, p["w_main"], cos_h, sin_h, seq, width)
    qi_hi, qi_lo, kw, ki_hi, ki_lo = _idxproj(h_hi, h_lo, p["w_idx_hi"], p["w_idx_lo"], c_i, s1_i, s2_i, seq)

    q_hi = qi_hi.reshape(n, H_IDX, D_IDX)
    q_lo = qi_lo.reshape(n, H_IDX, D_IDX)
    qi_cat = jnp.concatenate([q_hi, q_hi, q_lo, jnp.zeros_like(q_hi)], axis=2).reshape(n, H_IDX * IDX_K)
    k_hi, k_lo = ki_hi[:, :D_IDX], ki_lo[:, :D_IDX]
    if past is None:
        pa_k = pa_v = pb_k = pb_v = pi_hi = pi_lo = None
    else:
        pa_k, pa_v, pb_k, pb_v = (past[k].reshape(nb, n_past, width) for k in ("a_k", "a_v", "b_k", "b_v"))
        pi_hi, pi_lo = _split(past["b_kidx"].astype(F32))
    k_hi = _with_past(pi_hi, k_hi, nb, seq, n_keys)
    k_lo = _with_past(pi_lo, k_lo, nb, seq, n_keys)
    ki_cat = jnp.concatenate([k_hi, k_lo, k_hi, jnp.zeros_like(k_hi)], axis=1)

    o_a = _sb(qa, _with_past(pa_k, ka16, nb, seq, n_keys), _with_past(pa_v, va16, nb, seq, n_keys),
              nb, seq, n_keys, n_past, nh)
    o_b = _dsa(qb, _with_past(pb_k, kb16, nb, seq, n_keys), _with_past(pb_v, vb16, nb, seq, n_keys),
               qi_cat, ki_cat, kw, nb, seq, n_keys, n_valid, n_past, nh)

    merged = _merge(h_hi, o_a, o_b, p["w_gate"], p["b_gate"], p["w_pa"], p["w_pb"])
    x1, h2 = _outproj(merged, p["w_out"], x2, mod, p["g_post_mix"], p["g_pre_ffn"], seq)
    y, new_conv = _ffn(h2, x1, mod, conv_state, p["w_ffn_gate"], p["w_ffn_up"], p["w_ffn_down"],
                       p["w_conv"], p["b_conv"], p["g_post_ffn"], seq)

    heads = lambda a: a.reshape(nb, seq, nh, HEAD_DIM)
    new = (heads(ka32), heads(va32), heads(kb32), heads(vb32), kw[:, :D_IDX].reshape(nb, seq, D_IDX), new_conv)
    return y.reshape(nb, seq, d), new


def kernel(x_prompt, x_sample, c_prompt, c_sample, cache_a_k, cache_a_v, cache_b_k, cache_b_v, cache_b_kidx,
           state_ffn_conv, w_mod, b_mod, g_pre_mix, w_in, w_merge_gate, b_merge_gate, w_proj_a, w_proj_b, w_out,
           g_post_mix, g_pre_ffn, w_ffn_gate, w_ffn_up, w_conv, b_conv, w_ffn_down, g_post_ffn):
    depth = w_mod.shape[0]
    nbp, nbs = x_prompt.shape[0], x_sample.shape[0]
    d = x_prompt.shape[2]
    width = w_proj_a.shape[1]
    dff = w_ffn_gate.shape[2]
    assert w_in.shape[2] == 6 * width + H_IDX * D_IDX + D_IDX + H_IDX
    assert cache_b_kidx.shape[-1] == D_IDX and cache_a_k.shape[-1] == HEAD_DIM

    y_p, y_s = x_prompt, x_sample
    st_p = [[] for _ in range(6)]
    st_s = [[] for _ in range(6)]
    for l in range(depth):
        w_idx = jnp.pad(w_in[l][:, 6 * width:], ((0, 0), (0, IDX_COLS - (w_in.shape[2] - 6 * width))))
        w_idx_hi, w_idx_lo = _split(w_idx)
        p = dict(
            g_pre_mix=g_pre_mix[l], w_main=w_in[l][:, :6 * width].astype(BF16), w_idx_hi=w_idx_hi, w_idx_lo=w_idx_lo,
            w_gate=w_merge_gate[l].astype(BF16), b_gate=b_merge_gate[l],
            w_pa=w_proj_a[l].astype(BF16), w_pb=w_proj_b[l].astype(BF16), w_out=w_out[l].astype(BF16),
            g_post_mix=g_post_mix[l], g_pre_ffn=g_pre_ffn[l],
            w_ffn_gate=w_ffn_gate[l].astype(BF16), w_ffn_up=w_ffn_up[l].astype(BF16),
            w_ffn_down=w_ffn_down[l].astype(BF16), w_conv=w_conv[l], b_conv=b_conv[l], g_post_ffn=g_post_ffn[l])

        c_all = jnp.concatenate([c_prompt, c_sample], axis=0)
        pad = -c_all.shape[0] % 16
        mod = _mod(jnp.pad(c_all, ((0, pad), (0, 0))), w_mod[l], b_mod[l]).reshape(-1, N_MOD, d)
        past = dict(a_k=cache_a_k[l], a_v=cache_a_v[l], b_k=cache_b_k[l], b_v=cache_b_v[l], b_kidx=cache_b_kidx[l])
        y_p, new_p = _layer(y_p, mod[:nbp], None, jnp.zeros((nbp, CONV_W - 1, dff), F32), p)
        y_s, new_s = _layer(y_s, mod[nbp:nbp + nbs], past, state_ffn_conv[l], p)
        for i in range(6):
            st_p[i].append(new_p[i])
            st_s[i].append(new_s[i])
    sp = [jnp.stack(s, axis=0) for s in st_p]
    ss = [jnp.stack(s, axis=0) for s in st_s]
    return (y_p, y_s, sp[0], sp[1], sp[2], sp[3], sp[4], sp[5], ss[0], ss[1], ss[2], ss[3], ss[4], ss[5])
```

```python
import functools

import jax
import jax.numpy as jnp
from jax import lax
from jax.experimental import pallas as pl
from jax.experimental.pallas import tpu as pltpu

F32 = jnp.float32
BF16 = jnp.bfloat16
I32 = jnp.int32

CHUNK = 64
HEAD_DIM = 128
H_IDX = 4
D_IDX = 64
TOPK_MAX = 256
CONV_W = 3
ROPE_THETA = 10000.0
RMS_EPS = 1e-6
N_MOD = 6
IDX_COLS = 384
IDX_K = 256
LANES = 128
HALO = 16
LOG2E = 1.4426950408889634
NEG = -1e30
INT_MIN = -2 ** 31
VMEM_LIMIT = 56 * 1024 * 1024


def _tile(n, pref, mult):
    t = min(pref, n)
    t -= t % mult
    while t >= mult:
        if n % t == 0:
            return t
        t -= mult
    return n


def _params(sem):
    return pltpu.CompilerParams(dimension_semantics=sem, vmem_limit_bytes=VMEM_LIMIT)


def _split(x):
    hi = x.astype(BF16)
    lo = (x - hi.astype(F32)).astype(BF16)
    return hi, lo


def _dot(a, b):
    return jnp.dot(a, b, preferred_element_type=F32)


def _dot_t(a, b):
    return lax.dot_general(a, b, (((1,), (1,)), ((), ())), preferred_element_type=F32)


def _rms(x, g):
    ms = jnp.mean(x * x, axis=-1, keepdims=True)
    return x * lax.rsqrt(ms + RMS_EPS) * g


def _mod_kernel(c_ref, w_ref, b_ref, o_ref):
    c = c_ref[...]
    s = c / (1.0 + jnp.exp(-c))
    s_hi, s_lo = _split(s)
    w_hi, w_lo = _split(w_ref[...])
    o_ref[...] = _dot(s_hi, w_hi) + _dot(s_hi, w_lo) + _dot(s_lo, w_hi) + b_ref[...]


def _mod(c, w, b):
    bc, d = c.shape
    n = w.shape[1]
    tn = _tile(n, 1024, LANES)
    return pl.pallas_call(
        _mod_kernel,
        grid=(n // tn,),
        in_specs=[pl.BlockSpec((bc, d), lambda j: (0, 0)),
                  pl.BlockSpec((d, tn), lambda j: (0, j)),
                  pl.BlockSpec((1, tn), lambda j: (0, j))],
        out_specs=pl.BlockSpec((bc, tn), lambda j: (0, j)),
        out_shape=jax.ShapeDtypeStruct((bc, n), F32),
        compiler_params=_params(("arbitrary",)),
        name="mod",
    )(c, w, b.reshape(1, n))


def _norm_kernel(x_ref, g_ref, mod_ref, hi_ref, lo_ref):
    m = mod_ref[0]
    h = _rms(x_ref[...], g_ref[...]) * (1.0 + m[1:2, :]) + m[0:1, :]
    hi, lo = _split(h)
    hi_ref[...] = hi
    lo_ref[...] = lo


def _norm(x2, g, mod, seq):
    n, d = x2.shape
    tr = _tile(seq, 256, 16)
    tps = seq // tr
    return pl.pallas_call(
        _norm_kernel,
        grid=(n // tr,),
        in_specs=[pl.BlockSpec((tr, d), lambda i: (i, 0)),
                  pl.BlockSpec((1, d), lambda i: (0, 0)),
                  pl.BlockSpec((1, N_MOD, d), lambda i: (i // tps, 0, 0))],
        out_specs=[pl.BlockSpec((tr, d), lambda i: (i, 0))] * 2,
        out_shape=[jax.ShapeDtypeStruct((n, d), BF16)] * 2,
        compiler_params=_params(("arbitrary",)),
        name="norm",
    )(x2, g.reshape(1, d), mod)


def _inproj_kernel(a_ref, w_ref, cos_ref, sin_ref,
                   qa_ref, ka32_ref, ka16_ref, va32_ref, va16_ref,
                   qb_ref, kb32_ref, kb16_ref, vb32_ref, vb16_ref, *, npj, tn, scale):
    grp = pl.program_id(1) // npj
    acc = _dot(a_ref[...], w_ref[...])

    def rope(x):
        parts = []
        for s in range(tn // HEAD_DIM):
            xs = x[:, s * HEAD_DIM:(s + 1) * HEAD_DIM]
            parts.append(xs * cos_ref[...] + pltpu.roll(xs, HEAD_DIM // 2, 1) * sin_ref[...])
        return parts[0] if len(parts) == 1 else jnp.concatenate(parts, axis=1)

    @pl.when(grp == 0)
    def _():
        qa_ref[...] = (acc * scale).astype(BF16)

    @pl.when(grp == 1)
    def _():
        ka32_ref[...] = acc
        ka16_ref[...] = acc.astype(BF16)

    @pl.when(grp == 2)
    def _():
        va32_ref[...] = acc
        va16_ref[...] = acc.astype(BF16)

    @pl.when(grp == 3)
    def _():
        qb_ref[...] = (rope(acc) * scale).astype(BF16)

    @pl.when(grp == 4)
    def _():
        r = rope(acc)
        kb32_ref[...] = r
        kb16_ref[...] = r.astype(BF16)

    @pl.when(grp == 5)
    def _():
        vb32_ref[...] = acc
        vb16_ref[...] = acc.astype(BF16)


def _inproj(h_hi, w_main, cos_t, sin_t, seq, width):
    n, d = h_hi.shape
    tm = _tile(n, 512, 16)
    tn = _tile(width, 512, LANES)
    npj = width // tn
    if tm <= seq:
        tps = seq // tm
        tab_map = lambda i, j: (i % tps, 0)
    else:
        cos_t = jnp.tile(cos_t, (tm // seq, 1))
        sin_t = jnp.tile(sin_t, (tm // seq, 1))
        tab_map = lambda i, j: (0, 0)

    def omap(g):
        return lambda i, j: (i, jnp.clip(j - g * npj, 0, npj - 1))

    dts = [BF16, F32, BF16, F32, BF16, BF16, F32, BF16, F32, BF16]
    grps = [0, 1, 1, 2, 2, 3, 4, 4, 5, 5]
    return pl.pallas_call(
        functools.partial(_inproj_kernel, npj=npj, tn=tn, scale=LOG2E * HEAD_DIM ** -0.5),
        grid=(n // tm, 6 * npj),
        in_specs=[pl.BlockSpec((tm, d), lambda i, j: (i, 0)),
                  pl.BlockSpec((d, tn), lambda i, j: (0, j)),
                  pl.BlockSpec((tm, HEAD_DIM), tab_map),
                  pl.BlockSpec((tm, HEAD_DIM), tab_map)],
        out_specs=[pl.BlockSpec((tm, tn), omap(g)) for g in grps],
        out_shape=[jax.ShapeDtypeStruct((n, width), dt) for dt in dts],
        compiler_params=_params(("arbitrary", "arbitrary")),
        name="inproj",
    )(h_hi, w_main, cos_t, sin_t)


def _idxproj_kernel(hi_ref, lo_ref, whi_ref, wlo_ref, c_ref, s1_ref, s2_ref,
                    qhi_ref, qlo_ref, kw_ref, khi_ref, klo_ref):
    a_hi = hi_ref[...]
    w_hi = whi_ref[...]
    acc = _dot(a_hi, w_hi) + _dot(a_hi, wlo_ref[...]) + _dot(lo_ref[...], w_hi)
    for s in range(IDX_COLS // LANES):
        sl = slice(s * LANES, (s + 1) * LANES)
        xs = acc[:, sl]
        r = (xs * c_ref[:, sl] + pltpu.roll(xs, LANES - D_IDX // 2, 1) * s1_ref[:, sl]
             + pltpu.roll(xs, D_IDX // 2, 1) * s2_ref[:, sl])
        hi, lo = _split(r)
        if s < 2:
            qhi_ref[:, sl] = hi
            qlo_ref[:, sl] = lo
        else:
            kw_ref[...] = r
            khi_ref[...] = hi
            klo_ref[...] = lo


def _idxproj(h_hi, h_lo, w_hi, w_lo, c_t, s1_t, s2_t, seq):
    n, d = h_hi.shape
    tm = _tile(n, 512, 16)
    if tm <= seq:
        tps = seq // tm
        tab_map = lambda i: (i % tps, 0)
    else:
        c_t, s1_t, s2_t = (jnp.tile(t, (tm // seq, 1)) for t in (c_t, s1_t, s2_t))
        tab_map = lambda i: (0, 0)
    row = lambda i: (i, 0)
    return pl.pallas_call(
        _idxproj_kernel,
        grid=(n // tm,),
        in_specs=[pl.BlockSpec((tm, d), row), pl.BlockSpec((tm, d), row),
                  pl.BlockSpec((d, IDX_COLS), lambda i: (0, 0)),
                  pl.BlockSpec((d, IDX_COLS), lambda i: (0, 0)),
                  pl.BlockSpec((tm, IDX_COLS), tab_map),
                  pl.BlockSpec((tm, IDX_COLS), tab_map),
                  pl.BlockSpec((tm, IDX_COLS), tab_map)],
        out_specs=[pl.BlockSpec((tm, 2 * LANES), row), pl.BlockSpec((tm, 2 * LANES), row),
                   pl.BlockSpec((tm, LANES), row), pl.BlockSpec((tm, LANES), row),
                   pl.BlockSpec((tm, LANES), row)],
        out_shape=[jax.ShapeDtypeStruct((n, 2 * LANES), BF16), jax.ShapeDtypeStruct((n, 2 * LANES), BF16),
                   jax.ShapeDtypeStruct((n, LANES), F32), jax.ShapeDtypeStruct((n, LANES), BF16),
                   jax.ShapeDtypeStruct((n, LANES), BF16)],
        compiler_params=_params(("arbitrary",)),
        name="idxproj",
    )(h_hi, h_lo, w_hi, w_lo, c_t, s1_t, s2_t)


def _sb_kernel(q_ref, k_ref, v_ref, o_ref, acc_scr, *, tq, tk, past, group):
    q0 = past + pl.program_id(2) * tq
    n_kt = (q0 + tq - 2) // tk + 1
    n_clear = q0 // tk
    row_pos = q0 + lax.broadcasted_iota(I32, (tq, 1), 0)
    col = lax.broadcasted_iota(I32, (1, tk), 1)
    suffix = jnp.where(lax.broadcasted_iota(I32, (tk, tk), 0) >= lax.broadcasted_iota(I32, (tk, tk), 1),
                       1.0, 0.0).astype(BF16)
    acc_scr[...] = jnp.zeros_like(acc_scr)

    def tile(j, carries, masked):
        off = pl.multiple_of(j * tk, tk)
        if masked:
            valid = (off + col) < row_pos
        heads = [slice(h * HEAD_DIM, (h + 1) * HEAD_DIM) for h in range(group)]
        zs = [_dot_t(q_ref[:, hs], k_ref[pl.ds(off, tk), hs]) for hs in heads]
        sps = [jnp.maximum(z, 0.0) + jnp.log2(1.0 + jnp.exp2(-jnp.abs(z))) for z in zs]
        if masked:
            sps = [jnp.where(valid, sp, 0.0) for sp in sps]
        parts = [_split(sp) for sp in sps]
        incls = [_dot(hi, suffix) + _dot(lo, suffix) for hi, lo in parts]
        ws = [jnp.exp2(z - incl - c) for z, incl, c in zip(zs, incls, carries)]
        if masked:
            ws = [jnp.where(valid, w, 0.0) for w in ws]
        for h, hs in enumerate(heads):
            acc_scr[h] += _dot(ws[h].astype(BF16), v_ref[pl.ds(off, tk), hs])
        return tuple(c + incl[:, 0:1] for c, incl in zip(carries, incls))

    st = tuple(jnp.zeros((tq, 1), F32) for _ in range(group))
    st = lax.fori_loop(0, n_kt - n_clear, lambda t, c: tile(n_kt - 1 - t, c, True), st)
    lax.fori_loop(0, n_clear, lambda t, c: tile(n_clear - 1 - t, c, False), st)
    for h in range(group):
        o_ref[:, h * HEAD_DIM:(h + 1) * HEAD_DIM] = acc_scr[h].astype(o_ref.dtype)


def _sb(q, k_all, v_all, nb, seq, n_keys, past, nh):
    tq = _tile(seq, 256, 16)
    tk = 256
    nq = seq // tq
    group = 4 if nh % 4 == 0 else (2 if nh % 2 == 0 else 1)
    gw = group * HEAD_DIM
    return pl.pallas_call(
        functools.partial(_sb_kernel, tq=tq, tk=tk, past=past, group=group),
        grid=(nb, nh // group, nq),
        in_specs=[pl.BlockSpec((tq, gw), lambda b, h, i: (b * nq + i, h)),
                  pl.BlockSpec((n_keys, gw), lambda b, h, i: (b, h)),
                  pl.BlockSpec((n_keys, gw), lambda b, h, i: (b, h))],
        out_specs=pl.BlockSpec((tq, gw), lambda b, h, i: (b * nq + i, h)),
        out_shape=jax.ShapeDtypeStruct(q.shape, BF16),
        scratch_shapes=[pltpu.VMEM((group, tq, HEAD_DIM), F32)],
        compiler_params=_params(("arbitrary", "arbitrary", "arbitrary")),
        name="sb",
    )(q, k_all, v_all)


def _dsa_kernel(q_ref, k_ref, v_ref, qi_ref, ki_ref, kw_ref, o_ref, key_scr, acc_scr, s_scr, p_scr,
                *, tq, tk, past, n_valid, n_sel, nh):
    q0 = past + pl.program_id(1) * tq
    row_pos = q0 + lax.broadcasted_iota(I32, (tq, 1), 0)
    row_lim = jnp.minimum((row_pos // CHUNK + 1) * CHUNK, n_valid)
    n_kt = (jnp.minimum(((q0 + tq - 1) // CHUNK + 1) * CHUNK, n_valid) + tk - 1) // tk
    col = lax.broadcasted_iota(I32, (1, tk), 1)
    ksel = float(n_sel)

    wrow = kw_ref[:, D_IDX:D_IDX + H_IDX] * ((H_IDX ** -0.5) * (D_IDX ** -0.5))

    def score_tile(j, _):
        off = pl.multiple_of(j * tk, tk)
        kc = ki_ref[pl.ds(off, tk), :]
        score = jnp.zeros((tq, tk), F32)
        for h in range(H_IDX):
            s = _dot_t(qi_ref[:, h * IDX_K:(h + 1) * IDX_K], kc)
            score = score + wrow[:, h:h + 1] * jnp.maximum(s, 0.0)
        bits = pltpu.bitcast(score, I32)
        key = jnp.where(bits < 0, INT_MIN - bits, bits)
        key_scr[:, pl.ds(off, tk)] = jnp.where((off + col) < row_lim, key, INT_MIN)
        return 0

    lax.fori_loop(0, n_kt, score_tile, 0)

    rb = min(tq, 128)

    def count_ge(thr_b):
        outs = []
        for r in range(tq // rb):
            rows = slice(r * rb, (r + 1) * rb)
            thr_r = thr_b[rows]

            def body(j, cnt):
                off = pl.multiple_of(j * tk, tk)
                key = key_scr[rows, pl.ds(off, tk)]
                for s in range(tk // LANES):
                    cnt = cnt + jnp.where(key[:, s * LANES:(s + 1) * LANES] >= thr_r, 1.0, 0.0)
                return cnt

            cnt = lax.fori_loop(0, n_kt, body, jnp.zeros((rb, LANES), F32))
            outs.append(jnp.broadcast_to(jnp.sum(cnt, axis=1, keepdims=True), (rb, LANES)))
        return outs[0] if len(outs) == 1 else jnp.concatenate(outs, axis=0)

    def bit_step(it, cand):
        trial = cand | lax.shift_left(jnp.int32(1), 30 - it)
        return jnp.where(count_ge(trial) >= ksel, trial, cand)

    zero_b = jnp.zeros((tq, LANES), I32)
    thr_b = lax.fori_loop(0, 31, bit_step, jnp.where(count_ge(zero_b) >= ksel, zero_b, INT_MIN))
    no_thr_b = thr_b == INT_MIN
    cnt_gt = count_ge(jnp.where(no_thr_b, thr_b, thr_b + 1))[:, 0:1]
    thr = thr_b[:, 0:1]
    no_thr = thr == INT_MIN
    need = jnp.where(no_thr, 0.0, ksel - cnt_gt)

    prefix = jnp.where(lax.broadcasted_iota(I32, (tk, tk), 0) <= lax.broadcasted_iota(I32, (tk, tk), 1),
                       1.0, 0.0).astype(BF16)

    def bias_tile(j, carry):
        off = pl.multiple_of(j * tk, tk)
        key = key_scr[:, pl.ds(off, tk)]
        eq = key == thr
        eqf = jnp.where(eq, 1.0, 0.0)
        rank_incl = _dot(eqf.astype(BF16), prefix) + carry
        tie_ok = (rank_incl - eqf) < need
        bias = jnp.where(key > thr, 0.0, jnp.where(eq, jnp.where(tie_ok, 0.0, NEG), NEG))
        key_scr[:, pl.ds(off, tk)] = pltpu.bitcast(bias, I32)
        return rank_incl[:, tk - 1:tk]

    lax.fori_loop(0, n_kt, bias_tile, jnp.zeros((tq, 1), F32))

    acc_scr[...] = jnp.zeros_like(acc_scr)
    ones_v = jnp.ones((tk, HEAD_DIM), BF16)
    heads = [slice(h * HEAD_DIM, (h + 1) * HEAD_DIM) for h in range(nh)]

    def attn_tile(j, ms):
        off = pl.multiple_of(j * tk, tk)
        bias = pltpu.bitcast(key_scr[:, pl.ds(off, tk)], F32)
        for h, hs in enumerate(heads):
            s_scr[h] = _dot_t(q_ref[:, hs], k_ref[pl.ds(off, tk), hs]) + bias
        new_m = [jnp.maximum(m, jnp.max(s_scr[h], axis=1, keepdims=True)) for h, m in enumerate(ms)]
        for h in range(nh):
            p_scr[h] = jnp.exp2(s_scr[h] - new_m[h]).astype(BF16)
        for h, hs in enumerate(heads):
            v_ext = jnp.concatenate([v_ref[pl.ds(off, tk), hs], ones_v], axis=1)
            acc_scr[h] = jnp.exp2(ms[h] - new_m[h]) * acc_scr[h] + _dot(p_scr[h], v_ext)
        return tuple(new_m)

    lax.fori_loop(0, n_kt, attn_tile, tuple(jnp.full((tq, 1), NEG, F32) for _ in range(nh)))
    for h, hs in enumerate(heads):
        o_ref[:, hs] = (acc_scr[h, :, :HEAD_DIM] / acc_scr[h, :, HEAD_DIM:]).astype(o_ref.dtype)


def _dsa(q, k_all, v_all, qi_cat, ki_cat, kw, nb, seq, n_keys, n_valid, past, nh):
    tq = _tile(seq, 256, 16)
    tk = 256
    nq = seq // tq
    width = nh * HEAD_DIM
    n_sel = min(TOPK_MAX, n_valid // 4)
    once = pl.Buffered(1)
    return pl.pallas_call(
        functools.partial(_dsa_kernel, tq=tq, tk=tk, past=past, n_valid=n_valid, n_sel=n_sel, nh=nh),
        grid=(nb, nq),
        in_specs=[pl.BlockSpec((tq, width), lambda b, i: (b * nq + i, 0)),
                  pl.BlockSpec((n_keys, width), lambda b, i: (b, 0), pipeline_mode=once),
                  pl.BlockSpec((n_keys, width), lambda b, i: (b, 0), pipeline_mode=once),
                  pl.BlockSpec((tq, H_IDX * IDX_K), lambda b, i: (b * nq + i, 0)),
                  pl.BlockSpec((n_keys, IDX_K), lambda b, i: (b, 0), pipeline_mode=once),
                  pl.BlockSpec((tq, LANES), lambda b, i: (b * nq + i, 0))],
        out_specs=pl.BlockSpec((tq, width), lambda b, i: (b * nq + i, 0)),
        out_shape=jax.ShapeDtypeStruct(q.shape, BF16),
        scratch_shapes=[pltpu.VMEM((tq, n_keys), I32), pltpu.VMEM((nh, tq, 2 * HEAD_DIM), F32),
                        pltpu.VMEM((nh, tq, tk), F32), pltpu.VMEM((nh, tq, tk), BF16)],
        compiler_params=_params(("arbitrary", "arbitrary")),
        name="dsa",
    )(q, k_all, v_all, qi_cat, ki_cat, kw)


def _merge_kernel(h_ref, oa_ref, ob_ref, wg1_ref, wg2_ref, b1_ref, b2_ref, wa_ref, wb_ref, o_ref):
    h = h_ref[...]
    g1 = 1.0 / (1.0 + jnp.exp(-(_dot(h, wg1_ref[...]) + b1_ref[...])))
    g2 = 1.0 / (1.0 + jnp.exp(-(_dot(h, wg2_ref[...]) + b2_ref[...])))
    o_ref[...] = (g1 * _dot(oa_ref[...], wa_ref[...]) + g2 * _dot(ob_ref[...], wb_ref[...])).astype(BF16)


def _merge(h_hi, o_a, o_b, w_gate, b_gate, w_pa, w_pb):
    n, d = h_hi.shape
    width = o_a.shape[1]
    tm = _tile(n, 512, 16)
    tn = _tile(d, 512, LANES)
    nj = d // tn
    b_gate = b_gate.reshape(1, 2 * d)
    row = lambda i, j: (i, 0)
    return pl.pallas_call(
        _merge_kernel,
        grid=(n // tm, nj),
        in_specs=[pl.BlockSpec((tm, d), row), pl.BlockSpec((tm, width), row), pl.BlockSpec((tm, width), row),
                  pl.BlockSpec((d, tn), lambda i, j: (0, j)),
                  pl.BlockSpec((d, tn), lambda i, j: (0, j + nj)),
                  pl.BlockSpec((1, tn), lambda i, j: (0, j)),
                  pl.BlockSpec((1, tn), lambda i, j: (0, j + nj)),
                  pl.BlockSpec((width, tn), lambda i, j: (0, j)),
                  pl.BlockSpec((width, tn), lambda i, j: (0, j))],
        out_specs=pl.BlockSpec((tm, tn), lambda i, j: (i, j)),
        out_shape=jax.ShapeDtypeStruct((n, d), BF16),
        compiler_params=_params(("arbitrary", "arbitrary")),
        name="merge",
    )(h_hi, o_a, o_b, w_gate, w_gate, b_gate, b_gate, w_pa, w_pb)


def _outproj_kernel(m_ref, w_ref, x_ref, mod_ref, gm_ref, gf_ref, x1_ref, h2_ref):
    mod = mod_ref[0]
    x1 = x_ref[...] + mod[2:3, :] * _rms(_dot(m_ref[...], w_ref[...]), gm_ref[...])
    x1_ref[...] = x1
    h2_ref[...] = (_rms(x1, gf_ref[...]) * (1.0 + mod[4:5, :]) + mod[3:4, :]).astype(BF16)


def _outproj(merged, w_out, x2, mod, g_post_mix, g_pre_ffn, seq):
    n, d = x2.shape
    tm = _tile(seq, 256, 16)
    tps = seq // tm
    row = lambda i: (i, 0)
    fix = lambda i: (0, 0)
    return pl.pallas_call(
        _outproj_kernel,
        grid=(n // tm,),
        in_specs=[pl.BlockSpec((tm, d), row), pl.BlockSpec((d, d), fix), pl.BlockSpec((tm, d), row),
                  pl.BlockSpec((1, N_MOD, d), lambda i: (i // tps, 0, 0)),
                  pl.BlockSpec((1, d), fix), pl.BlockSpec((1, d), fix)],
        out_specs=[pl.BlockSpec((tm, d), row), pl.BlockSpec((tm, d), row)],
        out_shape=[jax.ShapeDtypeStruct((n, d), F32), jax.ShapeDtypeStruct((n, d), BF16)],
        compiler_params=_params(("arbitrary",)),
        name="outproj",
    )(merged, w_out, x2, mod, g_post_mix.reshape(1, d), g_pre_ffn.reshape(1, d))


def _gelu_tanh(x):
    return 0.5 * x * (1.0 + jnp.tanh(0.7978845608028654 * (x + 0.044715 * (x * x * x))))


def _ffn_kernel(h_ref, halo_ref, x1_ref, mod_ref, st_ref, wg_ref, wu_ref, wd_ref, wc_ref, bc_ref, gp_ref,
                y_ref, nc_ref, g_scr, act_scr, acc_scr, *, tm, nsub, tps):
    i = pl.program_id(0)
    c = pl.program_id(1)
    sub = tm // nsub
    h = h_ref[...]
    wg = wg_ref[...]
    g = _dot(h, wg)
    u = _dot(h, wu_ref[...])
    wc = wc_ref[...]
    bc = bc_ref[...]
    g_scr[HALO:HALO + tm, :] = g

    def conv(lo, rows):
        return (bc + wc[0:1, :] * g_scr[lo - 2:lo - 2 + rows, :] + wc[1:2, :] * g_scr[lo - 1:lo - 1 + rows, :]
                + wc[2:3, :] * g_scr[lo:lo + rows, :])

    @pl.when(c == 0)
    def _():
        acc_scr[...] = jnp.zeros_like(acc_scr)

    if nsub == 1:
        g_scr[0:HALO, :] = _dot(halo_ref[...], wg)

        @pl.when(i % tps == 0)
        def _():
            g_scr[HALO - 2:HALO, :] = st_ref[0]

        nc_ref[0] = g_scr[HALO + tm - 2:HALO + tm, :]
        act = _gelu_tanh(conv(HALO, tm)) * u
        acc_scr[...] += _dot(act.astype(BF16), wd_ref[...])
    else:
        for s in range(nsub):
            lo = HALO + s * sub
            nc_ref[s] = g_scr[lo + sub - 2:lo + sub, :]
            g_scr[lo - 2:lo, :] = st_ref[s]
            act_scr[s * sub:(s + 1) * sub, :] = _gelu_tanh(conv(lo, sub)) * u[s * sub:(s + 1) * sub, :]
        acc_scr[...] += _dot(act_scr[...].astype(BF16), wd_ref[...])

    @pl.when(c == pl.num_programs(1) - 1)
    def _():
        for s in range(nsub):
            rows = slice(s * sub, (s + 1) * sub)
            y_ref[rows, :] = x1_ref[rows, :] + mod_ref[s][5:6, :] * _rms(acc_scr[rows, :], gp_ref[...])


def _ffn(h2, x1, mod, state, w_gate, w_up, w_down, w_conv, b_conv, g_post, seq):
    n, d = x1.shape
    dff = w_gate.shape[1]
    nb = n // seq
    tm = _tile(n, 512, 16)
    tf = _tile(dff, 512, LANES)
    per_tile_c = lambda i, c: (i, 0, c)
    if tm <= seq:
        nsub, tps = 1, seq // tm
        per_seq = lambda i, c: (i // tps, 0, 0)
        per_seq_c = lambda i, c: (i // tps, 0, c)
    else:
        nsub, tps = tm // seq, 1
        per_seq = lambda i, c: (i, 0, 0)
        per_seq_c = per_tile_c
    halo_blocks = tm // HALO
    row = lambda i, c: (i, 0)
    y, new_conv = pl.pallas_call(
        functools.partial(_ffn_kernel, tm=tm, nsub=nsub, tps=tps),
        grid=(n // tm, dff // tf),
        in_specs=[pl.BlockSpec((tm, d), row),
                  pl.BlockSpec((HALO, d), lambda i, c: (jnp.maximum(i * halo_blocks - 1, 0), 0)),
                  pl.BlockSpec((tm, d), row),
                  pl.BlockSpec((nsub, N_MOD, d), per_seq),
                  pl.BlockSpec((nsub, CONV_W - 1, tf), per_seq_c),
                  pl.BlockSpec((d, tf), lambda i, c: (0, c)),
                  pl.BlockSpec((d, tf), lambda i, c: (0, c)),
                  pl.BlockSpec((tf, d), lambda i, c: (c, 0)),
                  pl.BlockSpec((CONV_W, tf), lambda i, c: (0, c)),
                  pl.BlockSpec((1, tf), lambda i, c: (0, c)),
                  pl.BlockSpec((1, d), lambda i, c: (0, 0))],
        out_specs=[pl.BlockSpec((tm, d), row),
                   pl.BlockSpec((nsub, CONV_W - 1, tf), per_tile_c)],
        out_shape=[jax.ShapeDtypeStruct((n, d), F32),
                   jax.ShapeDtypeStruct((nb * tps, CONV_W - 1, dff), F32)],
        scratch_shapes=[pltpu.VMEM((HALO + tm, tf), F32), pltpu.VMEM((tm, tf), F32), pltpu.VMEM((tm, d), F32)],
        compiler_params=_params(("arbitrary", "arbitrary")),
        name="ffn",
    )(h2, h2, x1, mod, state, w_gate, w_up, w_down, w_conv, b_conv.reshape(1, dff), g_post.reshape(1, d))
    return y, new_conv.reshape(nb, tps, CONV_W - 1, dff)[:, tps - 1]


def _rope_tables(pos):
    def angles(dim):
        half = dim // 2
        inv = ROPE_THETA ** (-jnp.arange(half, dtype=F32) * 2.0 / dim)
        ang = pos.astype(F32)[:, None] * inv[None, :]
        return jnp.cos(ang), jnp.sin(ang)

    cos, sin = angles(HEAD_DIM)
    cos_h = jnp.concatenate([cos, cos], axis=1)
    sin_h = jnp.concatenate([-sin, sin], axis=1)

    cos, sin = angles(D_IDX)
    zero = jnp.zeros_like(sin)
    n_rot = H_IDX + 1
    tail = IDX_COLS - n_rot * D_IDX
    t = pos.shape[0]
    c_i = jnp.concatenate([jnp.tile(jnp.concatenate([cos, cos], 1), (1, n_rot)), jnp.ones((t, tail), F32)], 1)
    s1_i = jnp.concatenate([jnp.tile(jnp.concatenate([-sin, zero], 1), (1, n_rot)), jnp.zeros((t, tail), F32)], 1)
    s2_i = jnp.concatenate([jnp.tile(jnp.concatenate([zero, sin], 1), (1, n_rot)), jnp.zeros((t, tail), F32)], 1)
    return cos_h, sin_h, c_i, s1_i, s2_i


def _with_past(past, new, nb, seq, n_keys):
    w = new.shape[1]
    if past is None and n_keys == seq:
        return new
    parts = [] if past is None else [past.astype(new.dtype).reshape(nb, -1, w)]
    parts.append(new.reshape(nb, seq, w))
    have = sum(p.shape[1] for p in parts)
    if n_keys > have:
        parts.append(jnp.zeros((nb, n_keys - have, w), new.dtype))
    return jnp.concatenate(parts, axis=1).reshape(nb * n_keys, w)


def _layer(x, mod, past, conv_state, p):
    nb, seq, d = x.shape
    n = nb * seq
    nh = p["w_pa"].shape[0] // HEAD_DIM
    width = nh * HEAD_DIM
    n_past = 0 if past is None else past["a_k"].shape[1]
    n_valid = n_past + seq
    n_keys = -(-n_valid // 256) * 256
    x2 = x.reshape(n, d)

    h_hi, h_lo = _norm(x2, p["g_pre_mix"], mod, seq)
    cos_h, sin_h, c_i, s1_i, s2_i = _rope_tables(n_past + jnp.arange(seq, dtype=I32))
    (qa, ka32, ka16, va32, va16, qb, kb32, kb16, vb32, vb16) = _inproj(h_hi, p["w_main"], cos_h, sin_h, seq, width)
    qi_hi, qi_lo, kw, ki_hi, ki_lo = _idxproj(h_hi, h_lo, p["w_idx_hi"], p["w_idx_lo"], c_i, s1_i, s2_i, seq)

    q_hi = qi_hi.reshape(n, H_IDX, D_IDX)
    q_lo = qi_lo.reshape(n, H_IDX, D_IDX)
    qi_cat = jnp.concatenate([q_hi, q_hi, q_lo, jnp.zeros_like(q_hi)], axis=2).reshape(n, H_IDX * IDX_K)
    k_hi, k_lo = ki_hi[:, :D_IDX], ki_lo[:, :D_IDX]
    if past is None:
        pa_k = pa_v = pb_k = pb_v = pi_hi = pi_lo = None
    else:
        pa_k, pa_v, pb_k, pb_v = (past[k].reshape(nb, n_past, width) for k in ("a_k", "a_v", "b_k", "b_v"))
        pi_hi, pi_lo = _split(past["b_kidx"].astype(F32))
    k_hi = _with_past(pi_hi, k_hi, nb, seq, n_keys)
    k_lo = _with_past(pi_lo, k_lo, nb, seq, n_keys)
    ki_cat = jnp.concatenate([k_hi, k_lo, k_hi, jnp.zeros_like(k_hi)], axis=1)

    o_a = _sb(qa, _with_past(pa_k, ka16, nb, seq, n_keys), _with_past(pa_v, va16, nb, seq, n_keys),
              nb, seq, n_keys, n_past, nh)
    o_b = _dsa(qb, _with_past(pb_k, kb16, nb, seq, n_keys), _with_past(pb_v, vb16, nb, seq, n_keys),
               qi_cat, ki_cat, kw, nb, seq, n_keys, n_valid, n_past, nh)

    merged = _merge(h_hi, o_a, o_b, p["w_gate"], p["b_gate"], p["w_pa"], p["w_pb"])
    x1, h2 = _outproj(merged, p["w_out"], x2, mod, p["g_post_mix"], p["g_pre_ffn"], seq)
    y, new_conv = _ffn(h2, x1, mod, conv_state, p["w_ffn_gate"], p["w_ffn_up"], p["w_ffn_down"],
                       p["w_conv"], p["b_conv"], p["g_post_ffn"], seq)

    heads = lambda a: a.reshape(nb, seq, nh, HEAD_DIM)
    new = (heads(ka32), heads(va32), heads(kb32), heads(vb32), kw[:, :D_IDX].reshape(nb, seq, D_IDX), new_conv)
    return y.reshape(nb, seq, d), new


def kernel(x_prompt, x_sample, c_prompt, c_sample, cache_a_k, cache_a_v, cache_b_k, cache_b_v, cache_b_kidx,
           state_ffn_conv, w_mod, b_mod, g_pre_mix, w_in, w_merge_gate, b_merge_gate, w_proj_a, w_proj_b, w_out,
           g_post_mix, g_pre_ffn, w_ffn_gate, w_ffn_up, w_conv, b_conv, w_ffn_down, g_post_ffn):
    depth = w_mod.shape[0]
    nbp, nbs = x_prompt.shape[0], x_sample.shape[0]
    d = x_prompt.shape[2]
    width = w_proj_a.shape[1]
    dff = w_ffn_gate.shape[2]
    assert w_in.shape[2] == 6 * width + H_IDX * D_IDX + D_IDX + H_IDX
    assert cache_b_kidx.shape[-1] == D_IDX and cache_a_k.shape[-1] == HEAD_DIM

    y_p, y_s = x_prompt, x_sample
    st_p = [[] for _ in range(6)]
    st_s = [[] for _ in range(6)]
    for l in range(depth):
        w_idx = jnp.pad(w_in[l][:, 6 * width:], ((0, 0), (0, IDX_COLS - (w_in.shape[2] - 6 * width))))
        w_idx_hi, w_idx_lo = _split(w_idx)
        p = dict(
            g_pre_mix=g_pre_mix[l], w_main=w_in[l][:, :6 * width].astype(BF16), w_idx_hi=w_idx_hi, w_idx_lo=w_idx_lo,
            w_gate=w_merge_gate[l].astype(BF16), b_gate=b_merge_gate[l],
            w_pa=w_proj_a[l].astype(BF16), w_pb=w_proj_b[l].astype(BF16), w_out=w_out[l].astype(BF16),
            g_post_mix=g_post_mix[l], g_pre_ffn=g_pre_ffn[l],
            w_ffn_gate=w_ffn_gate[l].astype(BF16), w_ffn_up=w_ffn_up[l].astype(BF16),
            w_ffn_down=w_ffn_down[l].astype(BF16), w_conv=w_conv[l], b_conv=b_conv[l], g_post_ffn=g_post_ffn[l])

        c_all = jnp.concatenate([c_prompt, c_sample], axis=0)
        pad = -c_all.shape[0] % 16
        mod = _mod(jnp.pad(c_all, ((0, pad), (0, 0))), w_mod[l], b_mod[l]).reshape(-1, N_MOD, d)
        past = dict(a_k=cache_a_k[l], a_v=cache_a_v[l], b_k=cache_b_k[l], b_v=cache_b_v[l], b_kidx=cache_b_kidx[l])
        y_p, new_p = _layer(y_p, mod[:nbp], None, jnp.zeros((nbp, CONV_W - 1, dff), F32), p)
        y_s, new_s = _layer(y_s, mod[nbp:nbp + nbs], past, state_ffn_conv[l], p)
        for i in range(6):
            st_p[i].append(new_p[i])
            st_s[i].append(new_s[i])
    sp = [jnp.stack(s, axis=0) for s in st_p]
    ss = [jnp.stack(s, axis=0) for s in st_s]
    return (y_p, y_s, sp[0], sp[1], sp[2], sp[3], sp[4], sp[5], ss[0], ss[1], ss[2], ss[3], ss[4], ss[5])
```

```python
import functools

import jax
import jax.numpy as jnp
from jax import lax
from jax.experimental import pallas as pl
from jax.experimental.pallas import tpu as pltpu

F32 = jnp.float32
BF16 = jnp.bfloat16
I32 = jnp.int32

CHUNK = 64
HEAD_DIM = 128
H_IDX = 4
D_IDX = 64
TOPK_MAX = 256
CONV_W = 3
ROPE_THETA = 10000.0
RMS_EPS = 1e-6
N_MOD = 6
IDX_COLS = 384
IDX_K = 256
LANES = 128
HALO = 16
ONES_ROWS = 16
LOG2E = 1.4426950408889634
NEG = -1e30
INT_MIN = -2 ** 31
VMEM_LIMIT = 56 * 1024 * 1024


def _tile(n, pref, mult):
    t = min(pref, n)
    t -= t % mult
    while t >= mult:
        if n % t == 0:
            return t
        t -= mult
    return n


def _params(sem):
    return pltpu.CompilerParams(dimension_semantics=sem, vmem_limit_bytes=VMEM_LIMIT)


def _split(x):
    hi = x.astype(BF16)
    lo = (x - hi.astype(F32)).astype(BF16)
    return hi, lo


def _dot(a, b):
    return jnp.dot(a, b, preferred_element_type=F32)


def _dot_t(a, b):
    return lax.dot_general(a, b, (((1,), (1,)), ((), ())), preferred_element_type=F32)


def _rms(x, g):
    ms = jnp.mean(x * x, axis=-1, keepdims=True)
    return x * lax.rsqrt(ms + RMS_EPS) * g


def _mod_kernel(c_ref, w_ref, b_ref, o_ref):
    c = c_ref[...]
    s = c / (1.0 + jnp.exp(-c))
    s_hi, s_lo = _split(s)
    w_hi, w_lo = _split(w_ref[...])
    o_ref[...] = _dot(s_hi, w_hi) + _dot(s_hi, w_lo) + _dot(s_lo, w_hi) + b_ref[...]


def _mod(c, w, b):
    bc, d = c.shape
    n = w.shape[1]
    tn = _tile(n, 1024, LANES)
    return pl.pallas_call(
        _mod_kernel,
        grid=(n // tn,),
        in_specs=[pl.BlockSpec((bc, d), lambda j: (0, 0)),
                  pl.BlockSpec((d, tn), lambda j: (0, j)),
                  pl.BlockSpec((1, tn), lambda j: (0, j))],
        out_specs=pl.BlockSpec((bc, tn), lambda j: (0, j)),
        out_shape=jax.ShapeDtypeStruct((bc, n), F32),
        compiler_params=_params(("arbitrary",)),
        name="mod",
    )(c, w, b.reshape(1, n))


def _norm_kernel(x_ref, g_ref, mod_ref, hi_ref, lo_ref):
    m = mod_ref[0]
    h = _rms(x_ref[...], g_ref[...]) * (1.0 + m[1:2, :]) + m[0:1, :]
    hi, lo = _split(h)
    hi_ref[...] = hi
    lo_ref[...] = lo


def _norm(x2, g, mod, seq):
    n, d = x2.shape
    tr = _tile(seq, 256, 16)
    tps = seq // tr
    return pl.pallas_call(
        _norm_kernel,
        grid=(n // tr,),
        in_specs=[pl.BlockSpec((tr, d), lambda i: (i, 0)),
                  pl.BlockSpec((1, d), lambda i: (0, 0)),
                  pl.BlockSpec((1, N_MOD, d), lambda i: (i // tps, 0, 0))],
        out_specs=[pl.BlockSpec((tr, d), lambda i: (i, 0))] * 2,
        out_shape=[jax.ShapeDtypeStruct((n, d), BF16)] * 2,
        compiler_params=_params(("arbitrary",)),
        name="norm",
    )(x2, g.reshape(1, d), mod)


def _inproj_kernel(a_ref, w_ref, cos_ref, sin_ref,
                   qa_ref, ka32_ref, ka16_ref, va32_ref, va16_ref,
                   qb_ref, kb32_ref, kb16_ref, vb32_ref, vb16_ref, *, npj, tn, scale):
    grp = pl.program_id(1) // npj
    acc = _dot(a_ref[...], w_ref[...])

    def rope(x):
        parts = []
        for s in range(tn // HEAD_DIM):
            xs = x[:, s * HEAD_DIM:(s + 1) * HEAD_DIM]
            parts.append(xs * cos_ref[...] + pltpu.roll(xs, HEAD_DIM // 2, 1) * sin_ref[...])
        return parts[0] if len(parts) == 1 else jnp.concatenate(parts, axis=1)

    @pl.when(grp == 0)
    def _():
        qa_ref[...] = (acc * scale).astype(BF16)

    @pl.when(grp == 1)
    def _():
        ka32_ref[...] = acc
        ka16_ref[...] = acc.astype(BF16)

    @pl.when(grp == 2)
    def _():
        va32_ref[...] = acc
        va16_ref[...] = acc.astype(BF16)

    @pl.when(grp == 3)
    def _():
        qb_ref[...] = (rope(acc) * scale).astype(BF16)

    @pl.when(grp == 4)
    def _():
        r = rope(acc)
        kb32_ref[...] = r
        kb16_ref[...] = r.astype(BF16)

    @pl.when(grp == 5)
    def _():
        vb32_ref[...] = acc
        vb16_ref[...] = acc.astype(BF16)


def _inproj(h_hi, w_main, cos_t, sin_t, seq, width):
    n, d = h_hi.shape
    tm = _tile(n, 1024, 16)
    tn = _tile(width, 512, LANES)
    npj = width // tn
    if tm <= seq:
        tps = seq // tm
        tab_map = lambda i, j: (i % tps, 0)
    else:
        cos_t = jnp.tile(cos_t, (tm // seq, 1))
        sin_t = jnp.tile(sin_t, (tm // seq, 1))
        tab_map = lambda i, j: (0, 0)

    def omap(g):
        return lambda i, j: (i, jnp.clip(j - g * npj, 0, npj - 1))

    dts = [BF16, F32, BF16, F32, BF16, BF16, F32, BF16, F32, BF16]
    grps = [0, 1, 1, 2, 2, 3, 4, 4, 5, 5]
    return pl.pallas_call(
        functools.partial(_inproj_kernel, npj=npj, tn=tn, scale=LOG2E * HEAD_DIM ** -0.5),
        grid=(n // tm, 6 * npj),
        in_specs=[pl.BlockSpec((tm, d), lambda i, j: (i, 0)),
                  pl.BlockSpec((d, tn), lambda i, j: (0, j)),
                  pl.BlockSpec((tm, HEAD_DIM), tab_map),
                  pl.BlockSpec((tm, HEAD_DIM), tab_map)],
        out_specs=[pl.BlockSpec((tm, tn), omap(g)) for g in grps],
        out_shape=[jax.ShapeDtypeStruct((n, width), dt) for dt in dts],
        compiler_params=_params(("arbitrary", "arbitrary")),
        name="inproj",
    )(h_hi, w_main, cos_t, sin_t)


def _idxproj_kernel(hi_ref, lo_ref, whi_ref, wlo_ref, c_ref, s1_ref, s2_ref,
                    qhi_ref, qlo_ref, kw_ref, khi_ref, klo_ref):
    a_hi = hi_ref[...]
    w_hi = whi_ref[...]
    acc = _dot(a_hi, w_hi) + _dot(a_hi, wlo_ref[...]) + _dot(lo_ref[...], w_hi)
    for s in range(IDX_COLS // LANES):
        sl = slice(s * LANES, (s + 1) * LANES)
        xs = acc[:, sl]
        r = (xs * c_ref[:, sl] + pltpu.roll(xs, LANES - D_IDX // 2, 1) * s1_ref[:, sl]
             + pltpu.roll(xs, D_IDX // 2, 1) * s2_ref[:, sl])
        hi, lo = _split(r)
        if s < 2:
            qhi_ref[:, sl] = hi
            qlo_ref[:, sl] = lo
        else:
            kw_ref[...] = r
            khi_ref[...] = hi
            klo_ref[...] = lo


def _idxproj(h_hi, h_lo, w_hi, w_lo, c_t, s1_t, s2_t, seq):
    n, d = h_hi.shape
    tm = _tile(n, 512, 16)
    if tm <= seq:
        tps = seq // tm
        tab_map = lambda i: (i % tps, 0)
    else:
        c_t, s1_t, s2_t = (jnp.tile(t, (tm // seq, 1)) for t in (c_t, s1_t, s2_t))
        tab_map = lambda i: (0, 0)
    row = lambda i: (i, 0)
    return pl.pallas_call(
        _idxproj_kernel,
        grid=(n // tm,),
        in_specs=[pl.BlockSpec((tm, d), row), pl.BlockSpec((tm, d), row),
                  pl.BlockSpec((d, IDX_COLS), lambda i: (0, 0)),
                  pl.BlockSpec((d, IDX_COLS), lambda i: (0, 0)),
                  pl.BlockSpec((tm, IDX_COLS), tab_map),
                  pl.BlockSpec((tm, IDX_COLS), tab_map),
                  pl.BlockSpec((tm, IDX_COLS), tab_map)],
        out_specs=[pl.BlockSpec((tm, 2 * LANES), row), pl.BlockSpec((tm, 2 * LANES), row),
                   pl.BlockSpec((tm, LANES), row), pl.BlockSpec((tm, LANES), row),
                   pl.BlockSpec((tm, LANES), row)],
        out_shape=[jax.ShapeDtypeStruct((n, 2 * LANES), BF16), jax.ShapeDtypeStruct((n, 2 * LANES), BF16),
                   jax.ShapeDtypeStruct((n, LANES), F32), jax.ShapeDtypeStruct((n, LANES), BF16),
                   jax.ShapeDtypeStruct((n, LANES), BF16)],
        compiler_params=_params(("arbitrary",)),
        name="idxproj",
    )(h_hi, h_lo, w_hi, w_lo, c_t, s1_t, s2_t)


def _sb_kernel(q_ref, k_ref, v_ref, o_ref, acc_scr, z_scr, i_scr, *, tq, tk, past, group):
    q0 = past + pl.program_id(2) * tq
    n_kt = (q0 + tq - 2) // tk + 1
    n_clear = q0 // tk
    row_pos = q0 + lax.broadcasted_iota(I32, (tq, 1), 0)
    col = lax.broadcasted_iota(I32, (1, tk), 1)
    suffix = jnp.where(lax.broadcasted_iota(I32, (tk, tk), 0) >= lax.broadcasted_iota(I32, (tk, tk), 1),
                       1.0, 0.0).astype(BF16)
    suffix2 = jnp.concatenate([suffix, suffix], axis=0)
    acc_scr[...] = jnp.zeros_like(acc_scr)

    def tile(j, carries, masked):
        off = pl.multiple_of(j * tk, tk)
        if masked:
            valid = (off + col) < row_pos
        heads = [slice(h * HEAD_DIM, (h + 1) * HEAD_DIM) for h in range(group)]
        for h, hs in enumerate(heads):
            z_scr[h] = _dot_t(q_ref[:, hs], k_ref[pl.ds(off, tk), hs])
        for h in range(group):
            z = z_scr[h]
            sp = jnp.maximum(z, 0.0) + jnp.log2(1.0 + jnp.exp2(-jnp.abs(z)))
            if masked:
                sp = jnp.where(valid, sp, 0.0)
            hi, lo = _split(sp)
            i_scr[h] = _dot(jnp.concatenate([hi, lo], axis=1), suffix2)
        for h, hs in enumerate(heads):
            w = jnp.exp2(z_scr[h] - i_scr[h] - carries[h])
            if masked:
                w = jnp.where(valid, w, 0.0)
            acc_scr[h] += _dot(w.astype(BF16), v_ref[pl.ds(off, tk), hs])
        return tuple(c + i_scr[h, :, 0:1] for h, c in enumerate(carries))

    st = tuple(jnp.zeros((tq, 1), F32) for _ in range(group))
    st = lax.fori_loop(0, n_kt - n_clear, lambda t, c: tile(n_kt - 1 - t, c, True), st)
    lax.fori_loop(0, n_clear, lambda t, c: tile(n_clear - 1 - t, c, False), st)
    for h in range(group):
        o_ref[:, h * HEAD_DIM:(h + 1) * HEAD_DIM] = acc_scr[h].astype(o_ref.dtype)


def _sb(q, k_all, v_all, nb, seq, n_keys, past, nh):
    tq = _tile(seq, 256, 16)
    tk = 256
    nq = seq // tq
    group = 4 if nh % 4 == 0 else (2 if nh % 2 == 0 else 1)
    gw = group * HEAD_DIM
    return pl.pallas_call(
        functools.partial(_sb_kernel, tq=tq, tk=tk, past=past, group=group),
        grid=(nb, nh // group, nq),
        in_specs=[pl.BlockSpec((tq, gw), lambda b, h, i: (b * nq + i, h)),
                  pl.BlockSpec((n_keys, gw), lambda b, h, i: (b, h)),
                  pl.BlockSpec((n_keys, gw), lambda b, h, i: (b, h))],
        out_specs=pl.BlockSpec((tq, gw), lambda b, h, i: (b * nq + i, h)),
        out_shape=jax.ShapeDtypeStruct(q.shape, BF16),
        scratch_shapes=[pltpu.VMEM((group, tq, HEAD_DIM), F32), pltpu.VMEM((group, tq, tk), F32),
                        pltpu.VMEM((group, tq, tk), F32)],
        compiler_params=_params(("arbitrary", "arbitrary", "arbitrary")),
        name="sb",
    )(q, k_all, v_all)


def _dsa_kernel(q_ref, k_ref, v_ref, qi_ref, ki_ref, kw_ref, o_ref, key_scr, acc_scr, s_scr, p_scr,
                *, tq, tk, past, n_valid, n_sel, nh):
    q0 = past + pl.program_id(1) * tq
    row_pos = q0 + lax.broadcasted_iota(I32, (tq, 1), 0)
    row_lim = jnp.minimum((row_pos // CHUNK + 1) * CHUNK, n_valid)
    n_kt = (jnp.minimum(((q0 + tq - 1) // CHUNK + 1) * CHUNK, n_valid) + tk - 1) // tk
    col = lax.broadcasted_iota(I32, (1, tk), 1)
    ksel = float(n_sel)

    wrow = kw_ref[:, D_IDX:D_IDX + H_IDX] * ((H_IDX ** -0.5) * (D_IDX ** -0.5))

    def score_tile(j, _):
        off = pl.multiple_of(j * tk, tk)
        kc = ki_ref[pl.ds(off, tk), :]
        score = jnp.zeros((tq, tk), F32)
        for h in range(H_IDX):
            s = _dot_t(qi_ref[:, h * IDX_K:(h + 1) * IDX_K], kc)
            score = score + wrow[:, h:h + 1] * jnp.maximum(s, 0.0)
        bits = pltpu.bitcast(score, I32)
        key = jnp.where(bits < 0, INT_MIN - bits, bits)
        key_scr[:, pl.ds(off, tk)] = jnp.where((off + col) < row_lim, key, INT_MIN)
        return 0

    lax.fori_loop(0, n_kt, score_tile, 0)

    rb = min(tq, 128)

    def count_ge(thr_b):
        outs = []
        for r in range(tq // rb):
            rows = slice(r * rb, (r + 1) * rb)
            thr_r = thr_b[rows]

            def body(j, cnt):
                off = pl.multiple_of(j * tk, tk)
                key = key_scr[rows, pl.ds(off, tk)]
                for s in range(tk // LANES):
                    cnt = cnt + jnp.where(key[:, s * LANES:(s + 1) * LANES] >= thr_r, 1.0, 0.0)
                return cnt

            cnt = lax.fori_loop(0, n_kt, body, jnp.zeros((rb, LANES), F32))
            outs.append(jnp.broadcast_to(jnp.sum(cnt, axis=1, keepdims=True), (rb, LANES)))
        return outs[0] if len(outs) == 1 else jnp.concatenate(outs, axis=0)

    def bit_step(it, cand):
        trial = cand | lax.shift_left(jnp.int32(1), 30 - it)
        return jnp.where(count_ge(trial) >= ksel, trial, cand)

    zero_b = jnp.zeros((tq, LANES), I32)
    thr_b = lax.fori_loop(0, 31, bit_step, jnp.where(count_ge(zero_b) >= ksel, zero_b, INT_MIN))
    no_thr_b = thr_b == INT_MIN
    cnt_gt = count_ge(jnp.where(no_thr_b, thr_b, thr_b + 1))[:, 0:1]
    thr = thr_b[:, 0:1]
    no_thr = thr == INT_MIN
    need = jnp.where(no_thr, 0.0, ksel - cnt_gt)

    prefix = jnp.where(lax.broadcasted_iota(I32, (tk, tk), 0) <= lax.broadcasted_iota(I32, (tk, tk), 1),
                       1.0, 0.0).astype(BF16)

    def bias_tile(j, carry):
        off = pl.multiple_of(j * tk, tk)
        key = key_scr[:, pl.ds(off, tk)]
        eq = key == thr
        eqf = jnp.where(eq, 1.0, 0.0)
        rank_incl = _dot(eqf.astype(BF16), prefix) + carry
        tie_ok = (rank_incl - eqf) < need
        bias = jnp.where(key > thr, 0.0, jnp.where(eq, jnp.where(tie_ok, 0.0, NEG), NEG))
        key_scr[:, pl.ds(off, tk)] = pltpu.bitcast(bias, I32)
        return rank_incl[:, tk - 1:tk]

    lax.fori_loop(0, n_kt, bias_tile, jnp.zeros((tq, 1), F32))

    acc_scr[...] = jnp.zeros_like(acc_scr)
    ones_v = jnp.ones((tk, HEAD_DIM), BF16)
    heads = [slice(h * HEAD_DIM, (h + 1) * HEAD_DIM) for h in range(nh)]

    def attn_tile(j, ms):
        off = pl.multiple_of(j * tk, tk)
        bias = pltpu.bitcast(key_scr[:, pl.ds(off, tk)], F32)
        for h, hs in enumerate(heads):
            s_scr[h] = _dot_t(q_ref[:, hs], k_ref[pl.ds(off, tk), hs]) + bias
        new_m = [jnp.maximum(m, jnp.max(s_scr[h], axis=1, keepdims=True)) for h, m in enumerate(ms)]
        for h in range(nh):
            p_scr[h] = jnp.exp2(s_scr[h] - new_m[h]).astype(BF16)
        for h, hs in enumerate(heads):
            v_ext = jnp.concatenate([v_ref[pl.ds(off, tk), hs], ones_v], axis=1)
            acc_scr[h] = jnp.exp2(ms[h] - new_m[h]) * acc_scr[h] + _dot(p_scr[h], v_ext)
        return tuple(new_m)

    lax.fori_loop(0, n_kt, attn_tile, tuple(jnp.full((tq, 1), NEG, F32) for _ in range(nh)))
    for h, hs in enumerate(heads):
        o_ref[:, hs] = (acc_scr[h, :, :HEAD_DIM] / acc_scr[h, :, HEAD_DIM:]).astype(o_ref.dtype)


def _dsa(q, k_all, v_all, qi_cat, ki_cat, kw, nb, seq, n_keys, n_valid, past, nh):
    tq = _tile(seq, 256, 16)
    tk = 256
    nq = seq // tq
    width = nh * HEAD_DIM
    n_sel = min(TOPK_MAX, n_valid // 4)
    once = pl.Buffered(1)
    return pl.pallas_call(
        functools.partial(_dsa_kernel, tq=tq, tk=tk, past=past, n_valid=n_valid, n_sel=n_sel, nh=nh),
        grid=(nb, nq),
        in_specs=[pl.BlockSpec((tq, width), lambda b, i: (b * nq + i, 0)),
                  pl.BlockSpec((n_keys, width), lambda b, i: (b, 0), pipeline_mode=once),
                  pl.BlockSpec((n_keys, width), lambda b, i: (b, 0), pipeline_mode=once),
                  pl.BlockSpec((tq, H_IDX * IDX_K), lambda b, i: (b * nq + i, 0)),
                  pl.BlockSpec((n_keys, IDX_K), lambda b, i: (b, 0), pipeline_mode=once),
                  pl.BlockSpec((tq, LANES), lambda b, i: (b * nq + i, 0))],
        out_specs=pl.BlockSpec((tq, width), lambda b, i: (b * nq + i, 0)),
        out_shape=jax.ShapeDtypeStruct(q.shape, BF16),
        scratch_shapes=[pltpu.VMEM((tq, n_keys), I32), pltpu.VMEM((nh, tq, 2 * HEAD_DIM), F32),
                        pltpu.VMEM((nh, tq, tk), F32), pltpu.VMEM((nh, tq, tk), BF16)],
        compiler_params=_params(("arbitrary", "arbitrary")),
        name="dsa",
    )(q, k_all, v_all, qi_cat, ki_cat, kw)


def _fold8(parts, op):
    while len(parts) > 1:
        parts = [op(parts[i], parts[i + 1]) if i + 1 < len(parts) else parts[i] for i in range(0, len(parts), 2)]
    return parts[0]


def _dsat_kernel(q_ref, k_ref, vt_ref, qi_ref, ki_ref, kwt_ref, o_ref, key_scr, acc_scr, s_scr, p_scr,
                 *, tq, tk, past, n_valid, n_sel, nh):
    q0 = past + pl.program_id(1) * tq
    col_pos = q0 + lax.broadcasted_iota(I32, (1, tq), 1)
    col_lim = jnp.minimum((col_pos // CHUNK + 1) * CHUNK, n_valid)
    n_kt = (jnp.minimum(((q0 + tq - 1) // CHUNK + 1) * CHUNK, n_valid) + tk - 1) // tk
    row = lax.broadcasted_iota(I32, (tk, 1), 0)
    ksel = float(n_sel)
    groups = [slice(8 * r, 8 * r + 8) for r in range(tk // 8)]

    wrow = kwt_ref[D_IDX:D_IDX + H_IDX, :] * ((H_IDX ** -0.5) * (D_IDX ** -0.5))

    def score_tile(j, _):
        off = pl.multiple_of(j * tk, tk)
        kc = ki_ref[pl.ds(off, tk), :]
        score = jnp.zeros((tk, tq), F32)
        for h in range(H_IDX):
            s = _dot_t(kc, qi_ref[:, h * IDX_K:(h + 1) * IDX_K])
            score = score + wrow[h:h + 1, :] * jnp.maximum(s, 0.0)
        bits = pltpu.bitcast(score, I32)
        key = jnp.where(bits < 0, INT_MIN - bits, bits)
        key_scr[pl.ds(off, tk), :] = jnp.where((off + row) < col_lim, key, INT_MIN)
        return 0

    lax.fori_loop(0, n_kt, score_tile, 0)

    def count_ge(thr8):
        def body(j, cnt):
            off = pl.multiple_of(j * tk, tk)
            key = key_scr[pl.ds(off, tk), :]
            return cnt + _fold8([jnp.where(key[g, :] >= thr8, 1.0, 0.0) for g in groups], jnp.add)

        cnt = lax.fori_loop(0, n_kt, body, jnp.zeros((8, tq), F32))
        return jnp.broadcast_to(jnp.sum(cnt, axis=0, keepdims=True), (8, tq))

    def bit_step(it, cand):
        trial = cand | lax.shift_left(jnp.int32(1), 30 - it)
        return jnp.where(count_ge(trial) >= ksel, trial, cand)

    zero8 = jnp.zeros((8, tq), I32)
    thr8 = lax.fori_loop(0, 31, bit_step, jnp.where(count_ge(zero8) >= ksel, zero8, INT_MIN))
    no_thr8 = thr8 == INT_MIN
    cnt_gt = count_ge(jnp.where(no_thr8, thr8, thr8 + 1))
    need = jnp.where(no_thr8, 0.0, ksel - cnt_gt)[0:1, :]
    thr = thr8[0:1, :]

    prefix = jnp.where(lax.broadcasted_iota(I32, (tk, tk), 0) >= lax.broadcasted_iota(I32, (tk, tk), 1),
                       1.0, 0.0).astype(BF16)

    def bias_tile(j, carry):
        off = pl.multiple_of(j * tk, tk)
        key = key_scr[pl.ds(off, tk), :]
        eq = key == thr
        eqf = jnp.where(eq, 1.0, 0.0)
        rank_incl = _dot(prefix, eqf.astype(BF16)) + carry
        tie_ok = (rank_incl - eqf) < need
        bias = jnp.where(key > thr, 0.0, jnp.where(eq, jnp.where(tie_ok, 0.0, NEG), NEG))
        key_scr[pl.ds(off, tk), :] = pltpu.bitcast(bias, I32)
        return rank_incl[tk - 1:tk, :]

    lax.fori_loop(0, n_kt, bias_tile, jnp.zeros((1, tq), F32))

    acc_scr[...] = jnp.zeros_like(acc_scr)
    ones_v = jnp.ones((ONES_ROWS, tk), BF16)
    heads = [slice(h * HEAD_DIM, (h + 1) * HEAD_DIM) for h in range(nh)]

    def attn_tile(j, ms):
        off = pl.multiple_of(j * tk, tk)
        bias = pltpu.bitcast(key_scr[pl.ds(off, tk), :], F32)
        for h, hs in enumerate(heads):
            s_scr[h] = _dot_t(k_ref[pl.ds(off, tk), hs], q_ref[:, hs]) + bias
        new_m = []
        for h, m in enumerate(ms):
            s = s_scr[h]
            top = jnp.max(_fold8([s[g, :] for g in groups], jnp.maximum), axis=0, keepdims=True)
            new_m.append(jnp.maximum(m, top))
        for h in range(nh):
            p_scr[h] = jnp.exp2(s_scr[h] - new_m[h]).astype(BF16)
        for h, hs in enumerate(heads):
            vt_ext = jnp.concatenate([vt_ref[hs, pl.ds(off, tk)], ones_v], axis=0)
            acc_scr[h] = jnp.exp2(ms[h] - new_m[h]) * acc_scr[h] + _dot(vt_ext, p_scr[h])
        return tuple(new_m)

    lax.fori_loop(0, n_kt, attn_tile, tuple(jnp.full((1, tq), NEG, F32) for _ in range(nh)))
    for h, hs in enumerate(heads):
        out_t = acc_scr[h, :HEAD_DIM, :] / acc_scr[h, HEAD_DIM:HEAD_DIM + 1, :]
        o_ref[:, hs] = out_t.T.astype(o_ref.dtype)


def _dsat(q, k_all, vt_all, qi_cat, ki_cat, kwt, nb, seq, n_keys, n_valid, past, nh):
    tq = _tile(seq, 256, LANES)
    tk = 256
    nq = seq // tq
    width = nh * HEAD_DIM
    n_sel = min(TOPK_MAX, n_valid // 4)
    once = pl.Buffered(1)
    return pl.pallas_call(
        functools.partial(_dsat_kernel, tq=tq, tk=tk, past=past, n_valid=n_valid, n_sel=n_sel, nh=nh),
        grid=(nb, nq),
        in_specs=[pl.BlockSpec((tq, width), lambda b, i: (b * nq + i, 0)),
                  pl.BlockSpec((n_keys, width), lambda b, i: (b, 0), pipeline_mode=once),
                  pl.BlockSpec((width, n_keys), lambda b, i: (b, 0), pipeline_mode=once),
                  pl.BlockSpec((tq, H_IDX * IDX_K), lambda b, i: (b * nq + i, 0)),
                  pl.BlockSpec((n_keys, IDX_K), lambda b, i: (b, 0), pipeline_mode=once),
                  pl.BlockSpec((LANES, tq), lambda b, i: (0, b * nq + i))],
        out_specs=pl.BlockSpec((tq, width), lambda b, i: (b * nq + i, 0)),
        out_shape=jax.ShapeDtypeStruct(q.shape, BF16),
        scratch_shapes=[pltpu.VMEM((n_keys, tq), I32), pltpu.VMEM((nh, HEAD_DIM + ONES_ROWS, tq), F32),
                        pltpu.VMEM((nh, tk, tq), F32), pltpu.VMEM((nh, tk, tq), BF16)],
        compiler_params=_params(("arbitrary", "arbitrary")),
        name="dsa",
    )(q, k_all, vt_all, qi_cat, ki_cat, kwt)


def _merge_kernel(h_ref, oa_ref, ob_ref, wg1_ref, wg2_ref, b1_ref, b2_ref, wa_ref, wb_ref, o_ref):
    h = h_ref[...]
    g1 = 1.0 / (1.0 + jnp.exp(-(_dot(h, wg1_ref[...]) + b1_ref[...])))
    g2 = 1.0 / (1.0 + jnp.exp(-(_dot(h, wg2_ref[...]) + b2_ref[...])))
    o_ref[...] = (g1 * _dot(oa_ref[...], wa_ref[...]) + g2 * _dot(ob_ref[...], wb_ref[...])).astype(BF16)


def _merge(h_hi, o_a, o_b, w_gate, b_gate, w_pa, w_pb):
    n, d = h_hi.shape
    width = o_a.shape[1]
    tm = _tile(n, 1024, 16)
    tn = _tile(d, 512, LANES)
    nj = d // tn
    b_gate = b_gate.reshape(1, 2 * d)
    row = lambda i, j: (i, 0)
    return pl.pallas_call(
        _merge_kernel,
        grid=(n // tm, nj),
        in_specs=[pl.BlockSpec((tm, d), row), pl.BlockSpec((tm, width), row), pl.BlockSpec((tm, width), row),
                  pl.BlockSpec((d, tn), lambda i, j: (0, j)),
                  pl.BlockSpec((d, tn), lambda i, j: (0, j + nj)),
                  pl.BlockSpec((1, tn), lambda i, j: (0, j)),
                  pl.BlockSpec((1, tn), lambda i, j: (0, j + nj)),
                  pl.BlockSpec((width, tn), lambda i, j: (0, j)),
                  pl.BlockSpec((width, tn), lambda i, j: (0, j))],
        out_specs=pl.BlockSpec((tm, tn), lambda i, j: (i, j)),
        out_shape=jax.ShapeDtypeStruct((n, d), BF16),
        compiler_params=_params(("arbitrary", "arbitrary")),
        name="merge",
    )(h_hi, o_a, o_b, w_gate, w_gate, b_gate, b_gate, w_pa, w_pb)


def _outproj_kernel(m_ref, w_ref, x_ref, mod_ref, gm_ref, gf_ref, x1_ref, h2_ref):
    mod = mod_ref[0]
    x1 = x_ref[...] + mod[2:3, :] * _rms(_dot(m_ref[...], w_ref[...]), gm_ref[...])
    x1_ref[...] = x1
    h2_ref[...] = (_rms(x1, gf_ref[...]) * (1.0 + mod[4:5, :]) + mod[3:4, :]).astype(BF16)


def _outproj(merged, w_out, x2, mod, g_post_mix, g_pre_ffn, seq):
    n, d = x2.shape
    tm = _tile(seq, 256, 16)
    tps = seq // tm
    row = lambda i: (i, 0)
    fix = lambda i: (0, 0)
    return pl.pallas_call(
        _outproj_kernel,
        grid=(n // tm,),
        in_specs=[pl.BlockSpec((tm, d), row), pl.BlockSpec((d, d), fix), pl.BlockSpec((tm, d), row),
                  pl.BlockSpec((1, N_MOD, d), lambda i: (i // tps, 0, 0)),
                  pl.BlockSpec((1, d), fix), pl.BlockSpec((1, d), fix)],
        out_specs=[pl.BlockSpec((tm, d), row), pl.BlockSpec((tm, d), row)],
        out_shape=[jax.ShapeDtypeStruct((n, d), F32), jax.ShapeDtypeStruct((n, d), BF16)],
        compiler_params=_params(("arbitrary",)),
        name="outproj",
    )(merged, w_out, x2, mod, g_post_mix.reshape(1, d), g_pre_ffn.reshape(1, d))


def _gelu_tanh(x):
    return 0.5 * x * (1.0 + jnp.tanh(0.7978845608028654 * (x + 0.044715 * (x * x * x))))


def _ffn_kernel(h_ref, halo_ref, x1_ref, mod_ref, st_ref, wg_ref, wu_ref, wd_ref, wc_ref, bc_ref, gp_ref,
                y_ref, nc_ref, g_scr, act_scr, acc_scr, *, tm, nsub, tps):
    i = pl.program_id(0)
    c = pl.program_id(1)
    sub = tm // nsub
    cur = c % 2

    @pl.when(c == 0)
    def _():
        acc_scr[...] = jnp.zeros_like(acc_scr)
        act_scr[1] = jnp.zeros((tm, act_scr.shape[2]), BF16)

    h = h_ref[...]
    wg = wg_ref[...]
    g = _dot(h, wg)
    u = _dot(h, wu_ref[...])
    wc = wc_ref[...]
    bc = bc_ref[...]
    g_scr[HALO:HALO + tm, :] = g

    acc_scr[...] += _dot(act_scr[1 - cur], wd_ref[...])

    def conv(lo, rows):
        return (bc + wc[0:1, :] * g_scr[lo - 2:lo - 2 + rows, :] + wc[1:2, :] * g_scr[lo - 1:lo - 1 + rows, :]
                + wc[2:3, :] * g_scr[lo:lo + rows, :])

    if nsub == 1:
        g_halo = _dot(halo_ref[...], wg)
        g_scr[0:HALO, :] = g_halo
        g_scr[HALO - 2:HALO, :] = jnp.where(i % tps == 0, st_ref[0], g_halo[HALO - 2:HALO, :])
        nc_ref[0] = g_scr[HALO + tm - 2:HALO + tm, :]
        act_scr[cur] = (_gelu_tanh(conv(HALO, tm)) * u).astype(BF16)
    else:
        for s in range(nsub):
            lo = HALO + s * sub
            nc_ref[s] = g_scr[lo + sub - 2:lo + sub, :]
            g_scr[lo - 2:lo, :] = st_ref[s]
            act_scr[cur, s * sub:(s + 1) * sub, :] = (
                _gelu_tanh(conv(lo, sub)) * u[s * sub:(s + 1) * sub, :]).astype(BF16)

    @pl.when(c == pl.num_programs(1) - 1)
    def _():
        for s in range(nsub):
            rows = slice(s * sub, (s + 1) * sub)
            y_ref[rows, :] = x1_ref[rows, :] + mod_ref[s][5:6, :] * _rms(acc_scr[rows, :], gp_ref[...])


def _ffn(h2, x1, mod, state, w_gate, w_up, w_down, w_conv, b_conv, g_post, seq):
    n, d = x1.shape
    dff = w_gate.shape[1]
    nb = n // seq
    tm = _tile(n, 512, 16)
    tf = _tile(dff, 512, LANES)
    n_chunks = dff // tf
    up = lambda c: jnp.minimum(c, n_chunks - 1)
    down = lambda c: jnp.maximum(c - 1, 0)
    per_tile_c = lambda i, c: (i, 0, up(c))
    if tm <= seq:
        nsub, tps = 1, seq // tm
        per_seq = lambda i, c: (i // tps, 0, 0)
        per_seq_c = lambda i, c: (i // tps, 0, up(c))
    else:
        nsub, tps = tm // seq, 1
        per_seq = lambda i, c: (i, 0, 0)
        per_seq_c = per_tile_c
    halo_blocks = tm // HALO
    row = lambda i, c: (i, 0)
    y, new_conv = pl.pallas_call(
        functools.partial(_ffn_kernel, tm=tm, nsub=nsub, tps=tps),
        grid=(n // tm, n_chunks + 1),
        in_specs=[pl.BlockSpec((tm, d), row),
                  pl.BlockSpec((HALO, d), lambda i, c: (jnp.maximum(i * halo_blocks - 1, 0), 0)),
                  pl.BlockSpec((tm, d), row),
                  pl.BlockSpec((nsub, N_MOD, d), per_seq),
                  pl.BlockSpec((nsub, CONV_W - 1, tf), per_seq_c),
                  pl.BlockSpec((d, tf), lambda i, c: (0, up(c))),
                  pl.BlockSpec((d, tf), lambda i, c: (0, up(c))),
                  pl.BlockSpec((tf, d), lambda i, c: (down(c), 0)),
                  pl.BlockSpec((CONV_W, tf), lambda i, c: (0, up(c))),
                  pl.BlockSpec((1, tf), lambda i, c: (0, up(c))),
                  pl.BlockSpec((1, d), lambda i, c: (0, 0))],
        out_specs=[pl.BlockSpec((tm, d), row),
                   pl.BlockSpec((nsub, CONV_W - 1, tf), per_tile_c)],
        out_shape=[jax.ShapeDtypeStruct((n, d), F32),
                   jax.ShapeDtypeStruct((nb * tps, CONV_W - 1, dff), F32)],
        scratch_shapes=[pltpu.VMEM((HALO + tm, tf), F32), pltpu.VMEM((2, tm, tf), BF16),
                        pltpu.VMEM((tm, d), F32)],
        compiler_params=_params(("arbitrary", "arbitrary")),
        name="ffn",
    )(h2, h2, x1, mod, state, w_gate, w_up, w_down, w_conv, b_conv.reshape(1, dff), g_post.reshape(1, d))
    return y, new_conv.reshape(nb, tps, CONV_W - 1, dff)[:, tps - 1]


def _rope_tables(pos):
    def angles(dim):
        half = dim // 2
        inv = ROPE_THETA ** (-jnp.arange(half, dtype=F32) * 2.0 / dim)
        ang = pos.astype(F32)[:, None] * inv[None, :]
        return jnp.cos(ang), jnp.sin(ang)

    cos, sin = angles(HEAD_DIM)
    cos_h = jnp.concatenate([cos, cos], axis=1)
    sin_h = jnp.concatenate([-sin, sin], axis=1)

    cos, sin = angles(D_IDX)
    zero = jnp.zeros_like(sin)
    n_rot = H_IDX + 1
    tail = IDX_COLS - n_rot * D_IDX
    t = pos.shape[0]
    c_i = jnp.concatenate([jnp.tile(jnp.concatenate([cos, cos], 1), (1, n_rot)), jnp.ones((t, tail), F32)], 1)
    s1_i = jnp.concatenate([jnp.tile(jnp.concatenate([-sin, zero], 1), (1, n_rot)), jnp.zeros((t, tail), F32)], 1)
    s2_i = jnp.concatenate([jnp.tile(jnp.concatenate([zero, sin], 1), (1, n_rot)), jnp.zeros((t, tail), F32)], 1)
    return cos_h, sin_h, c_i, s1_i, s2_i


def _with_past(past, new, nb, seq, n_keys):
    w = new.shape[1]
    if past is None and n_keys == seq:
        return new
    parts = [] if past is None else [past.astype(new.dtype).reshape(nb, -1, w)]
    parts.append(new.reshape(nb, seq, w))
    have = sum(p.shape[1] for p in parts)
    if n_keys > have:
        parts.append(jnp.zeros((nb, n_keys - have, w), new.dtype))
    return jnp.concatenate(parts, axis=1).reshape(nb * n_keys, w)


def _layer(x, mod, past, conv_state, p):
    nb, seq, d = x.shape
    n = nb * seq
    nh = p["w_pa"].shape[0] // HEAD_DIM
    width = nh * HEAD_DIM
    n_past = 0 if past is None else past["a_k"].shape[1]
    n_valid = n_past + seq
    n_keys = -(-n_valid // 256) * 256
    x2 = x.reshape(n, d)

    h_hi, h_lo = _norm(x2, p["g_pre_mix"], mod, seq)
    cos_h, sin_h, c_i, s1_i, s2_i = _rope_tables(n_past + jnp.arange(seq, dtype=I32))
    (qa, ka32, ka16, va32, va16, qb, kb32, kb16, vb32, vb16) = _inproj(h_hi, p["w_main"], cos_h, sin_h, seq, width)
    qi_hi, qi_lo, kw, ki_hi, ki_lo = _idxproj(h_hi, h_lo, p["w_idx_hi"], p["w_idx_lo"], c_i, s1_i, s2_i, seq)

    q_hi = qi_hi.reshape(n, H_IDX, D_IDX)
    q_lo = qi_lo.reshape(n, H_IDX, D_IDX)
    qi_cat = jnp.concatenate([q_hi, q_hi, q_lo, jnp.zeros_like(q_hi)], axis=2).reshape(n, H_IDX * IDX_K)
    k_hi, k_lo = ki_hi[:, :D_IDX], ki_lo[:, :D_IDX]
    if past is None:
        pa_k = pa_v = pb_k = pb_v = pi_hi = pi_lo = None
    else:
        pa_k, pa_v, pb_k, pb_v = (past[k].reshape(nb, n_past, width) for k in ("a_k", "a_v", "b_k", "b_v"))
        pi_hi, pi_lo = _split(past["b_kidx"].astype(F32))
    k_hi = _with_past(pi_hi, k_hi, nb, seq, n_keys)
    k_lo = _with_past(pi_lo, k_lo, nb, seq, n_keys)
    ki_cat = jnp.concatenate([k_hi, k_lo, k_hi, jnp.zeros_like(k_hi)], axis=1)

    o_a = _sb(qa, _with_past(pa_k, ka16, nb, seq, n_keys), _with_past(pa_v, va16, nb, seq, n_keys),
              nb, seq, n_keys, n_past, nh)
    seq_q = max(seq, LANES)
    pad_q = lambda a: a if seq_q == seq else jnp.pad(
        a.reshape(nb, seq, -1), ((0, 0), (0, seq_q - seq), (0, 0))).reshape(nb * seq_q, -1)
    vt_all = _with_past(pb_v, vb16, nb, seq, n_keys).reshape(nb, n_keys, width)
    vt_all = jnp.swapaxes(vt_all, 1, 2).reshape(nb * width, n_keys)
    o_b = _dsat(pad_q(qb), _with_past(pb_k, kb16, nb, seq, n_keys), vt_all, pad_q(qi_cat), ki_cat,
                pad_q(kw).T, nb, seq_q, n_keys, n_valid, n_past, nh)
    if seq_q != seq:
        o_b = o_b.reshape(nb, seq_q, width)[:, :seq].reshape(n, width)

    merged = _merge(h_hi, o_a, o_b, p["w_gate"], p["b_gate"], p["w_pa"], p["w_pb"])
    x1, h2 = _outproj(merged, p["w_out"], x2, mod, p["g_post_mix"], p["g_pre_ffn"], seq)
    y, new_conv = _ffn(h2, x1, mod, conv_state, p["w_ffn_gate"], p["w_ffn_up"], p["w_ffn_down"],
                       p["w_conv"], p["b_conv"], p["g_post_ffn"], seq)

    heads = lambda a: a.reshape(nb, seq, nh, HEAD_DIM)
    new = (heads(ka32), heads(va32), heads(kb32), heads(vb32), kw[:, :D_IDX].reshape(nb, seq, D_IDX), new_conv)
    return y.reshape(nb, seq, d), new


def kernel(x_prompt, x_sample, c_prompt, c_sample, cache_a_k, cache_a_v, cache_b_k, cache_b_v, cache_b_kidx,
           state_ffn_conv, w_mod, b_mod, g_pre_mix, w_in, w_merge_gate, b_merge_gate, w_proj_a, w_proj_b, w_out,
           g_post_mix, g_pre_ffn, w_ffn_gate, w_ffn_up, w_conv, b_conv, w_ffn_down, g_post_ffn):
    depth = w_mod.shape[0]
    nbp, nbs = x_prompt.shape[0], x_sample.shape[0]
    d = x_prompt.shape[2]
    width = w_proj_a.shape[1]
    dff = w_ffn_gate.shape[2]
    assert w_in.shape[2] == 6 * width + H_IDX * D_IDX + D_IDX + H_IDX
    assert cache_b_kidx.shape[-1] == D_IDX and cache_a_k.shape[-1] == HEAD_DIM

    y_p, y_s = x_prompt, x_sample
    st_p = [[] for _ in range(6)]
    st_s = [[] for _ in range(6)]
    for l in range(depth):
        w_idx = jnp.pad(w_in[l][:, 6 * width:], ((0, 0), (0, IDX_COLS - (w_in.shape[2] - 6 * width))))
        w_idx_hi, w_idx_lo = _split(w_idx)
        p = dict(
            g_pre_mix=g_pre_mix[l], w_main=w_in[l][:, :6 * width].astype(BF16), w_idx_hi=w_idx_hi, w_idx_lo=w_idx_lo,
            w_gate=w_merge_gate[l].astype(BF16), b_gate=b_merge_gate[l],
            w_pa=w_proj_a[l].astype(BF16), w_pb=w_proj_b[l].astype(BF16), w_out=w_out[l].astype(BF16),
            g_post_mix=g_post_mix[l], g_pre_ffn=g_pre_ffn[l],
            w_ffn_gate=w_ffn_gate[l].astype(BF16), w_ffn_up=w_ffn_up[l].astype(BF16),
            w_ffn_down=w_ffn_down[l].astype(BF16), w_conv=w_conv[l], b_conv=b_conv[l], g_post_ffn=g_post_ffn[l])

        c_all = jnp.concatenate([c_prompt, c_sample], axis=0)
        pad = -c_all.shape[0] % 16
        mod = _mod(jnp.pad(c_all, ((0, pad), (0, 0))), w_mod[l], b_mod[l]).reshape(-1, N_MOD, d)
        past = dict(a_k=cache_a_k[l], a_v=cache_a_v[l], b_k=cache_b_k[l], b_v=cache_b_v[l], b_kidx=cache_b_kidx[l])
        y_p, new_p = _layer(y_p, mod[:nbp], None, jnp.zeros((nbp, CONV_W - 1, dff), F32), p)
        y_s, new_s = _layer(y_s, mod[nbp:nbp + nbs], past, state_ffn_conv[l], p)
        for i in range(6):
            st_p[i].append(new_p[i])
            st_s[i].append(new_s[i])
    sp = [jnp.stack(s, axis=0) for s in st_p]
    ss = [jnp.stack(s, axis=0) for s in st_s]
    return (y_p, y_s, sp[0], sp[1], sp[2], sp[3], sp[4], sp[5], ss[0], ss[1], ss[2], ss[3], ss[4], ss[5])
```

```python
import functools

import jax
import jax.numpy as jnp
from jax import lax
from jax.experimental import pallas as pl
from jax.experimental.pallas import tpu as pltpu

F32 = jnp.float32
BF16 = jnp.bfloat16
I32 = jnp.int32

CHUNK = 64
HEAD_DIM = 128
H_IDX = 4
D_IDX = 64
TOPK_MAX = 256
CONV_W = 3
ROPE_THETA = 10000.0
RMS_EPS = 1e-6
N_MOD = 6
IDX_COLS = 384
IDX_K = 256
LANES = 128
HALO = 16
ONES_ROWS = 16
LOG2E = 1.4426950408889634
UNDERFLOW_LOG2 = 160.0
NEG = -1e30
INT_MIN = -2 ** 31
VMEM_LIMIT = 56 * 1024 * 1024


def _tile(n, pref, mult):
    t = min(pref, n)
    t -= t % mult
    while t >= mult:
        if n % t == 0:
            return t
        t -= mult
    return n


def _row_tile(n, seq, pref):
    if seq >= pref or n == seq:
        return _tile(seq, pref, 16)
    return _tile(n, pref, seq)


def _params(sem):
    return pltpu.CompilerParams(dimension_semantics=sem, vmem_limit_bytes=VMEM_LIMIT)


def _split(x):
    hi = x.astype(BF16)
    lo = (x - hi.astype(F32)).astype(BF16)
    return hi, lo


def _dot(a, b):
    return jnp.dot(a, b, preferred_element_type=F32)


def _dot_t(a, b):
    return lax.dot_general(a, b, (((1,), (1,)), ((), ())), preferred_element_type=F32)


def _rms(x, g):
    ms = jnp.mean(x * x, axis=-1, keepdims=True)
    return x * lax.rsqrt(ms + RMS_EPS) * g


def _mod_kernel(c_ref, w_ref, b_ref, o_ref):
    c = c_ref[...]
    s = c / (1.0 + jnp.exp(-c))
    s_hi, s_lo = _split(s)
    w_hi, w_lo = _split(w_ref[...])
    o_ref[...] = _dot(s_hi, w_hi) + _dot(s_hi, w_lo) + _dot(s_lo, w_hi) + b_ref[...]


def _mod(c, w, b):
    bc, d = c.shape
    n = w.shape[1]
    tn = _tile(n, 1024, LANES)
    return pl.pallas_call(
        _mod_kernel,
        grid=(n // tn,),
        in_specs=[pl.BlockSpec((bc, d), lambda j: (0, 0)),
                  pl.BlockSpec((d, tn), lambda j: (0, j)),
                  pl.BlockSpec((1, tn), lambda j: (0, j))],
        out_specs=pl.BlockSpec((bc, tn), lambda j: (0, j)),
        out_shape=jax.ShapeDtypeStruct((bc, n), F32),
        compiler_params=_params(("arbitrary",)),
        name="mod",
    )(c, w, b.reshape(1, n))


def _norm_kernel(x_ref, g_ref, mod_ref, hi_ref, lo_ref):
    m = mod_ref[0]
    h = _rms(x_ref[...], g_ref[...]) * (1.0 + m[1:2, :]) + m[0:1, :]
    hi, lo = _split(h)
    hi_ref[...] = hi
    lo_ref[...] = lo


def _norm(x2, g, mod, seq):
    n, d = x2.shape
    tr = _tile(seq, 256, 16)
    tps = seq // tr
    return pl.pallas_call(
        _norm_kernel,
        grid=(n // tr,),
        in_specs=[pl.BlockSpec((tr, d), lambda i: (i, 0)),
                  pl.BlockSpec((1, d), lambda i: (0, 0)),
                  pl.BlockSpec((1, N_MOD, d), lambda i: (i // tps, 0, 0))],
        out_specs=[pl.BlockSpec((tr, d), lambda i: (i, 0))] * 2,
        out_shape=[jax.ShapeDtypeStruct((n, d), BF16)] * 2,
        compiler_params=_params(("arbitrary",)),
        name="norm",
    )(x2, g.reshape(1, d), mod)


def _inproj_kernel(a_ref, w_ref, cos_ref, sin_ref,
                   qa_ref, ka32_ref, ka16_ref, va32_ref, va16_ref,
                   qb_ref, kb32_ref, kb16_ref, vb32_ref, vb16_ref, *, npj, tn, scale):
    grp = pl.program_id(1) // npj
    acc = _dot(a_ref[...], w_ref[...])

    def rope(x):
        parts = []
        for s in range(tn // HEAD_DIM):
            xs = x[:, s * HEAD_DIM:(s + 1) * HEAD_DIM]
            parts.append(xs * cos_ref[...] + pltpu.roll(xs, HEAD_DIM // 2, 1) * sin_ref[...])
        return parts[0] if len(parts) == 1 else jnp.concatenate(parts, axis=1)

    @pl.when(grp == 0)
    def _():
        qa_ref[...] = (acc * scale).astype(BF16)

    @pl.when(grp == 1)
    def _():
        ka32_ref[...] = acc
        ka16_ref[...] = acc.astype(BF16)

    @pl.when(grp == 2)
    def _():
        va32_ref[...] = acc
        va16_ref[...] = acc.astype(BF16)

    @pl.when(grp == 3)
    def _():
        qb_ref[...] = (rope(acc) * scale).astype(BF16)

    @pl.when(grp == 4)
    def _():
        r = rope(acc)
        kb32_ref[...] = r
        kb16_ref[...] = r.astype(BF16)

    @pl.when(grp == 5)
    def _():
        vb32_ref[...] = acc
        vb16_ref[...] = acc.astype(BF16)


def _inproj(h_hi, w_main, cos_t, sin_t, seq, width):
    n, d = h_hi.shape
    tm = _row_tile(n, seq, 1024)
    tn = _tile(width, 512, LANES)
    npj = width // tn
    if tm <= seq:
        tps = seq // tm
        tab_map = lambda i, j: (i % tps, 0)
    else:
        cos_t = jnp.tile(cos_t, (tm // seq, 1))
        sin_t = jnp.tile(sin_t, (tm // seq, 1))
        tab_map = lambda i, j: (0, 0)

    def omap(g):
        return lambda i, j: (i, jnp.clip(j - g * npj, 0, npj - 1))

    dts = [BF16, F32, BF16, F32, BF16, BF16, F32, BF16, F32, BF16]
    grps = [0, 1, 1, 2, 2, 3, 4, 4, 5, 5]
    return pl.pallas_call(
        functools.partial(_inproj_kernel, npj=npj, tn=tn, scale=LOG2E * HEAD_DIM ** -0.5),
        grid=(n // tm, 6 * npj),
        in_specs=[pl.BlockSpec((tm, d), lambda i, j: (i, 0)),
                  pl.BlockSpec((d, tn), lambda i, j: (0, j)),
                  pl.BlockSpec((tm, HEAD_DIM), tab_map),
                  pl.BlockSpec((tm, HEAD_DIM), tab_map)],
        out_specs=[pl.BlockSpec((tm, tn), omap(g)) for g in grps],
        out_shape=[jax.ShapeDtypeStruct((n, width), dt) for dt in dts],
        compiler_params=_params(("arbitrary", "arbitrary")),
        name="inproj",
    )(h_hi, w_main, cos_t, sin_t)


def _idxproj_kernel(hi_ref, lo_ref, whi_ref, wlo_ref, c_ref, s1_ref, s2_ref,
                    qhi_ref, qlo_ref, kw_ref, khi_ref, klo_ref):
    a_hi = hi_ref[...]
    w_hi = whi_ref[...]
    acc = _dot(a_hi, w_hi) + _dot(a_hi, wlo_ref[...]) + _dot(lo_ref[...], w_hi)
    for s in range(IDX_COLS // LANES):
        sl = slice(s * LANES, (s + 1) * LANES)
        xs = acc[:, sl]
        r = (xs * c_ref[:, sl] + pltpu.roll(xs, LANES - D_IDX // 2, 1) * s1_ref[:, sl]
             + pltpu.roll(xs, D_IDX // 2, 1) * s2_ref[:, sl])
        hi, lo = _split(r)
        if s < 2:
            qhi_ref[:, sl] = hi
            qlo_ref[:, sl] = lo
        else:
            kw_ref[...] = r
            khi_ref[...] = hi
            klo_ref[...] = lo


def _idxproj(h_hi, h_lo, w_hi, w_lo, c_t, s1_t, s2_t, seq):
    n, d = h_hi.shape
    tm = _row_tile(n, seq, 512)
    if tm <= seq:
        tps = seq // tm
        tab_map = lambda i: (i % tps, 0)
    else:
        c_t, s1_t, s2_t = (jnp.tile(t, (tm // seq, 1)) for t in (c_t, s1_t, s2_t))
        tab_map = lambda i: (0, 0)
    row = lambda i: (i, 0)
    return pl.pallas_call(
        _idxproj_kernel,
        grid=(n // tm,),
        in_specs=[pl.BlockSpec((tm, d), row), pl.BlockSpec((tm, d), row),
                  pl.BlockSpec((d, IDX_COLS), lambda i: (0, 0)),
                  pl.BlockSpec((d, IDX_COLS), lambda i: (0, 0)),
                  pl.BlockSpec((tm, IDX_COLS), tab_map),
                  pl.BlockSpec((tm, IDX_COLS), tab_map),
                  pl.BlockSpec((tm, IDX_COLS), tab_map)],
        out_specs=[pl.BlockSpec((tm, 2 * LANES), row), pl.BlockSpec((tm, 2 * LANES), row),
                   pl.BlockSpec((tm, LANES), row), pl.BlockSpec((tm, LANES), row),
                   pl.BlockSpec((tm, LANES), row)],
        out_shape=[jax.ShapeDtypeStruct((n, 2 * LANES), BF16), jax.ShapeDtypeStruct((n, 2 * LANES), BF16),
                   jax.ShapeDtypeStruct((n, LANES), F32), jax.ShapeDtypeStruct((n, LANES), BF16),
                   jax.ShapeDtypeStruct((n, LANES), BF16)],
        compiler_params=_params(("arbitrary",)),
        name="idxproj",
    )(h_hi, h_lo, w_hi, w_lo, c_t, s1_t, s2_t)


def _sb_kernel(q_ref, k_ref, v_ref, o_ref, acc_scr, z_scr, i_scr, *, tq, tk, past, group):
    q0 = past + pl.program_id(2) * tq
    n_kt = (q0 + tq - 2) // tk + 1
    n_clear = q0 // tk
    row_pos = q0 + lax.broadcasted_iota(I32, (tq, 1), 0)
    col = lax.broadcasted_iota(I32, (1, tk), 1)
    suffix = jnp.where(lax.broadcasted_iota(I32, (tk, tk), 0) >= lax.broadcasted_iota(I32, (tk, tk), 1),
                       1.0, 0.0).astype(BF16)
    suffix2 = jnp.concatenate([suffix, suffix], axis=0)
    acc_scr[...] = jnp.zeros_like(acc_scr)

    heads = [slice(h * HEAD_DIM, (h + 1) * HEAD_DIM) for h in range(group)]

    def tiles(js, carries, masked):
        offs = [pl.multiple_of(j * tk, tk) for j in js]
        if masked:
            valids = [(off + col) < row_pos for off in offs]
        for t, off in enumerate(offs):
            for h, hs in enumerate(heads):
                z_scr[t * group + h] = _dot_t(q_ref[:, hs], k_ref[pl.ds(off, tk), hs])
        for t in range(len(js)):
            for h in range(group):
                z = z_scr[t * group + h]
                sp = jnp.maximum(z, 0.0) + jnp.log2(1.0 + jnp.exp2(-jnp.abs(z)))
                if masked:
                    sp = jnp.where(valids[t], sp, 0.0)
                hi, lo = _split(sp)
                i_scr[t * group + h] = _dot(jnp.concatenate([hi, lo], axis=1), suffix2)
        carries = list(carries)
        for t, off in enumerate(offs):
            for h, hs in enumerate(heads):
                w = jnp.exp2(z_scr[t * group + h] - i_scr[t * group + h] - carries[h])
                if masked:
                    w = jnp.where(valids[t], w, 0.0)
                acc_scr[h] += _dot(w.astype(BF16), v_ref[pl.ds(off, tk), hs])
                carries[h] = carries[h] + i_scr[t * group + h, :, 0:1]
        return tuple(carries)

    def least(carries):
        m = carries[0]
        for c in carries[1:]:
            m = jnp.minimum(m, c)
        return jnp.min(m)

    st = tuple(jnp.zeros((tq, 1), F32) for _ in range(group))
    st = lax.fori_loop(0, n_kt - n_clear, lambda t, c: tiles([n_kt - 1 - t], c, True), st)

    def more(state):
        t, low, _ = state
        return jnp.logical_and(t < n_clear, low < UNDERFLOW_LOG2)

    def step(state):
        t, _, carries = state
        carries = tiles([n_clear - 1 - t], carries, False)
        return t + 1, least(carries), carries

    lax.while_loop(more, step, (jnp.int32(0), least(st), st))
    for h in range(group):
        o_ref[:, h * HEAD_DIM:(h + 1) * HEAD_DIM] = acc_scr[h].astype(o_ref.dtype)


def _sb(q, k_all, v_all, nb, seq, n_keys, past, nh):
    tq = _tile(seq, 256, 16)
    tk = 256
    nq = seq // tq
    group = 4 if nh % 4 == 0 else (2 if nh % 2 == 0 else 1)
    gw = group * HEAD_DIM
    return pl.pallas_call(
        functools.partial(_sb_kernel, tq=tq, tk=tk, past=past, group=group),
        grid=(nb, nh // group, nq),
        in_specs=[pl.BlockSpec((tq, gw), lambda b, h, i: (b * nq + i, h)),
                  pl.BlockSpec((n_keys, gw), lambda b, h, i: (b, h)),
                  pl.BlockSpec((n_keys, gw), lambda b, h, i: (b, h))],
        out_specs=pl.BlockSpec((tq, gw), lambda b, h, i: (b * nq + i, h)),
        out_shape=jax.ShapeDtypeStruct(q.shape, BF16),
        scratch_shapes=[pltpu.VMEM((group, tq, HEAD_DIM), F32), pltpu.VMEM((group, tq, tk), F32),
                        pltpu.VMEM((group, tq, tk), F32)],
        compiler_params=_params(("arbitrary", "arbitrary", "arbitrary")),
        name="sb",
    )(q, k_all, v_all)


def _dsa_kernel(q_ref, k_ref, v_ref, qi_ref, ki_ref, kw_ref, o_ref, key_scr, acc_scr, s_scr, p_scr,
                *, tq, tk, past, n_valid, n_sel, nh):
    q0 = past + pl.program_id(1) * tq
    row_pos = q0 + lax.broadcasted_iota(I32, (tq, 1), 0)
    row_lim = jnp.minimum((row_pos // CHUNK + 1) * CHUNK, n_valid)
    n_kt = (jnp.minimum(((q0 + tq - 1) // CHUNK + 1) * CHUNK, n_valid) + tk - 1) // tk
    col = lax.broadcasted_iota(I32, (1, tk), 1)
    ksel = float(n_sel)

    wrow = kw_ref[:, D_IDX:D_IDX + H_IDX] * ((H_IDX ** -0.5) * (D_IDX ** -0.5))

    def score_tile(j, _):
        off = pl.multiple_of(j * tk, tk)
        kc = ki_ref[pl.ds(off, tk), :]
        score = jnp.zeros((tq, tk), F32)
        for h in range(H_IDX):
            s = _dot_t(qi_ref[:, h * IDX_K:(h + 1) * IDX_K], kc)
            score = score + wrow[:, h:h + 1] * jnp.maximum(s, 0.0)
        bits = pltpu.bitcast(score, I32)
        key = jnp.where(bits < 0, INT_MIN - bits, bits)
        key_scr[:, pl.ds(off, tk)] = jnp.where((off + col) < row_lim, key, INT_MIN)
        return 0

    lax.fori_loop(0, n_kt, score_tile, 0)

    rb = min(tq, 128)

    def count_ge(thr_b):
        outs = []
        for r in range(tq // rb):
            rows = slice(r * rb, (r + 1) * rb)
            thr_r = thr_b[rows]

            def body(j, cnt):
                off = pl.multiple_of(j * tk, tk)
                key = key_scr[rows, pl.ds(off, tk)]
                for s in range(tk // LANES):
                    cnt = cnt + jnp.where(key[:, s * LANES:(s + 1) * LANES] >= thr_r, 1.0, 0.0)
                return cnt

            cnt = lax.fori_loop(0, n_kt, body, jnp.zeros((rb, LANES), F32))
            outs.append(jnp.broadcast_to(jnp.sum(cnt, axis=1, keepdims=True), (rb, LANES)))
        return outs[0] if len(outs) == 1 else jnp.concatenate(outs, axis=0)

    def bit_step(it, cand):
        trial = cand | lax.shift_left(jnp.int32(1), 30 - it)
        return jnp.where(count_ge(trial) >= ksel, trial, cand)

    zero_b = jnp.zeros((tq, LANES), I32)
    thr_b = lax.fori_loop(0, 31, bit_step, jnp.where(count_ge(zero_b) >= ksel, zero_b, INT_MIN))
    no_thr_b = thr_b == INT_MIN
    cnt_gt = count_ge(jnp.where(no_thr_b, thr_b, thr_b + 1))[:, 0:1]
    thr = thr_b[:, 0:1]
    no_thr = thr == INT_MIN
    need = jnp.where(no_thr, 0.0, ksel - cnt_gt)

    prefix = jnp.where(lax.broadcasted_iota(I32, (tk, tk), 0) <= lax.broadcasted_iota(I32, (tk, tk), 1),
                       1.0, 0.0).astype(BF16)

    def bias_tile(j, carry):
        off = pl.multiple_of(j * tk, tk)
        key = key_scr[:, pl.ds(off, tk)]
        eq = key == thr
        eqf = jnp.where(eq, 1.0, 0.0)
        rank_incl = _dot(eqf.astype(BF16), prefix) + carry
        tie_ok = (rank_incl - eqf) < need
        bias = jnp.where(key > thr, 0.0, jnp.where(eq, jnp.where(tie_ok, 0.0, NEG), NEG))
        key_scr[:, pl.ds(off, tk)] = pltpu.bitcast(bias, I32)
        return rank_incl[:, tk - 1:tk]

    lax.fori_loop(0, n_kt, bias_tile, jnp.zeros((tq, 1), F32))

    acc_scr[...] = jnp.zeros_like(acc_scr)
    ones_v = jnp.ones((tk, HEAD_DIM), BF16)
    heads = [slice(h * HEAD_DIM, (h + 1) * HEAD_DIM) for h in range(nh)]

    def attn_tile(j, ms):
        off = pl.multiple_of(j * tk, tk)
        bias = pltpu.bitcast(key_scr[:, pl.ds(off, tk)], F32)
        for h, hs in enumerate(heads):
            s_scr[h] = _dot_t(q_ref[:, hs], k_ref[pl.ds(off, tk), hs]) + bias
        new_m = [jnp.maximum(m, jnp.max(s_scr[h], axis=1, keepdims=True)) for h, m in enumerate(ms)]
        for h in range(nh):
            p_scr[h] = jnp.exp2(s_scr[h] - new_m[h]).astype(BF16)
        for h, hs in enumerate(heads):
            v_ext = jnp.concatenate([v_ref[pl.ds(off, tk), hs], ones_v], axis=1)
            acc_scr[h] = jnp.exp2(ms[h] - new_m[h]) * acc_scr[h] + _dot(p_scr[h], v_ext)
        return tuple(new_m)

    lax.fori_loop(0, n_kt, attn_tile, tuple(jnp.full((tq, 1), NEG, F32) for _ in range(nh)))
    for h, hs in enumerate(heads):
        o_ref[:, hs] = (acc_scr[h, :, :HEAD_DIM] / acc_scr[h, :, HEAD_DIM:]).astype(o_ref.dtype)


def _dsa(q, k_all, v_all, qi_cat, ki_cat, kw, nb, seq, n_keys, n_valid, past, nh):
    tq = _tile(seq, 256, 16)
    tk = 256
    nq = seq // tq
    width = nh * HEAD_DIM
    n_sel = min(TOPK_MAX, n_valid // 4)
    once = pl.Buffered(1)
    return pl.pallas_call(
        functools.partial(_dsa_kernel, tq=tq, tk=tk, past=past, n_valid=n_valid, n_sel=n_sel, nh=nh),
        grid=(nb, nq),
        in_specs=[pl.BlockSpec((tq, width), lambda b, i: (b * nq + i, 0)),
                  pl.BlockSpec((n_keys, width), lambda b, i: (b, 0), pipeline_mode=once),
                  pl.BlockSpec((n_keys, width), lambda b, i: (b, 0), pipeline_mode=once),
                  pl.BlockSpec((tq, H_IDX * IDX_K), lambda b, i: (b * nq + i, 0)),
                  pl.BlockSpec((n_keys, IDX_K), lambda b, i: (b, 0), pipeline_mode=once),
                  pl.BlockSpec((tq, LANES), lambda b, i: (b * nq + i, 0))],
        out_specs=pl.BlockSpec((tq, width), lambda b, i: (b * nq + i, 0)),
        out_shape=jax.ShapeDtypeStruct(q.shape, BF16),
        scratch_shapes=[pltpu.VMEM((tq, n_keys), I32), pltpu.VMEM((nh, tq, 2 * HEAD_DIM), F32),
                        pltpu.VMEM((nh, tq, tk), F32), pltpu.VMEM((nh, tq, tk), BF16)],
        compiler_params=_params(("arbitrary", "arbitrary")),
        name="dsa",
    )(q, k_all, v_all, qi_cat, ki_cat, kw)


def _fold8(parts, op):
    while len(parts) > 1:
        parts = [op(parts[i], parts[i + 1]) if i + 1 < len(parts) else parts[i] for i in range(0, len(parts), 2)]
    return parts[0]


def _dsat_kernel(q_ref, k_ref, vt_ref, qi_ref, ki_ref, kwt_ref, o_ref, key_scr, acc_scr, s_scr, p_scr,
                 *, tq, tk, past, n_valid, n_sel, nh):
    q0 = past + pl.program_id(1) * tq
    col_pos = q0 + lax.broadcasted_iota(I32, (1, tq), 1)
    col_lim = jnp.minimum((col_pos // CHUNK + 1) * CHUNK, n_valid)
    n_kt = (jnp.minimum(((q0 + tq - 1) // CHUNK + 1) * CHUNK, n_valid) + tk - 1) // tk
    row = lax.broadcasted_iota(I32, (tk, 1), 0)
    ksel = float(n_sel)
    groups = [slice(8 * r, 8 * r + 8) for r in range(tk // 8)]

    wrow = kwt_ref[D_IDX:D_IDX + H_IDX, :] * ((H_IDX ** -0.5) * (D_IDX ** -0.5))

    def score_pair(t, _):
        offs = [pl.multiple_of(jnp.minimum(2 * t + u, n_kt - 1) * tk, tk) for u in range(2)]
        raw = [[_dot_t(ki_ref[pl.ds(off, tk), :], qi_ref[:, h * IDX_K:(h + 1) * IDX_K]) for h in range(H_IDX)]
               for off in offs]
        for off, dots in zip(offs, raw):
            score = jnp.zeros((tk, tq), F32)
            for h, s in enumerate(dots):
                score = score + wrow[h:h + 1, :] * jnp.maximum(s, 0.0)
            bits = pltpu.bitcast(score, I32)
            key = jnp.where(bits < 0, INT_MIN - bits, bits)
            key_scr[pl.ds(off, tk), :] = jnp.where((off + row) < col_lim, key, INT_MIN)
        return 0

    lax.fori_loop(0, (n_kt + 1) // 2, score_pair, 0)

    def count_ge(thr8):
        def body(t, cnt):
            for u in range(2):
                j = 2 * t + u
                off = pl.multiple_of(jnp.minimum(j, n_kt - 1) * tk, tk)
                key = key_scr[pl.ds(off, tk), :]
                part = _fold8([jnp.where(key[g, :] >= thr8, 1.0, 0.0) for g in groups], jnp.add)
                cnt = cnt + (part if u == 0 else jnp.where(j < n_kt, part, 0.0))
            return cnt

        cnt = lax.fori_loop(0, (n_kt + 1) // 2, body, jnp.zeros((8, tq), F32))
        return jnp.broadcast_to(jnp.sum(cnt, axis=0, keepdims=True), (8, tq))

    def bit_step(it, cand):
        trial = cand | lax.shift_left(jnp.int32(1), 30 - it)
        return jnp.where(count_ge(trial) >= ksel, trial, cand)

    zero8 = jnp.zeros((8, tq), I32)
    thr8 = lax.fori_loop(0, 31, bit_step, jnp.where(count_ge(zero8) >= ksel, zero8, INT_MIN))
    no_thr8 = thr8 == INT_MIN
    cnt_gt = count_ge(jnp.where(no_thr8, thr8, thr8 + 1))
    need = jnp.where(no_thr8, 0.0, ksel - cnt_gt)[0:1, :]
    thr = thr8[0:1, :]

    prefix = jnp.where(lax.broadcasted_iota(I32, (tk, tk), 0) >= lax.broadcasted_iota(I32, (tk, tk), 1),
                       1.0, 0.0).astype(BF16)

    def bias_tile(j, carry):
        off = pl.multiple_of(j * tk, tk)
        key = key_scr[pl.ds(off, tk), :]
        eq = key == thr
        eqf = jnp.where(eq, 1.0, 0.0)
        rank_incl = _dot(prefix, eqf.astype(BF16)) + carry
        tie_ok = (rank_incl - eqf) < need
        bias = jnp.where(key > thr, 0.0, jnp.where(eq, jnp.where(tie_ok, 0.0, NEG), NEG))
        key_scr[pl.ds(off, tk), :] = pltpu.bitcast(bias, I32)
        return rank_incl[tk - 1:tk, :]

    lax.fori_loop(0, n_kt, bias_tile, jnp.zeros((1, tq), F32))

    acc_scr[...] = jnp.zeros_like(acc_scr)
    ones_v = jnp.ones((ONES_ROWS, tk), BF16)
    heads = [slice(h * HEAD_DIM, (h + 1) * HEAD_DIM) for h in range(nh)]

    def attn_tile(j, ms):
        off = pl.multiple_of(j * tk, tk)
        bias = pltpu.bitcast(key_scr[pl.ds(off, tk), :], F32)
        for h, hs in enumerate(heads):
            s_scr[h] = _dot_t(k_ref[pl.ds(off, tk), hs], q_ref[:, hs]) + bias
        new_m = []
        for h, m in enumerate(ms):
            s = s_scr[h]
            top = jnp.max(_fold8([s[g, :] for g in groups], jnp.maximum), axis=0, keepdims=True)
            new_m.append(jnp.maximum(m, top))
        for h in range(nh):
            p_scr[h] = jnp.exp2(s_scr[h] - new_m[h]).astype(BF16)
        for h, hs in enumerate(heads):
            vt_ext = jnp.concatenate([vt_ref[hs, pl.ds(off, tk)], ones_v], axis=0)
            acc_scr[h] = jnp.exp2(ms[h] - new_m[h]) * acc_scr[h] + _dot(vt_ext, p_scr[h])
        return tuple(new_m)

    lax.fori_loop(0, n_kt, attn_tile, tuple(jnp.full((1, tq), NEG, F32) for _ in range(nh)))
    for h, hs in enumerate(heads):
        out_t = acc_scr[h, :HEAD_DIM, :] / acc_scr[h, HEAD_DIM:HEAD_DIM + 1, :]
        o_ref[:, hs] = out_t.T.astype(o_ref.dtype)


def _dsat(q, k_all, vt_all, qi_cat, ki_cat, kwt, nb, seq, n_keys, n_valid, past, nh):
    tq = _tile(seq, 256, LANES)
    tk = 256
    nq = seq // tq
    width = nh * HEAD_DIM
    n_sel = min(TOPK_MAX, n_valid // 4)
    once = pl.Buffered(1)
    return pl.pallas_call(
        functools.partial(_dsat_kernel, tq=tq, tk=tk, past=past, n_valid=n_valid, n_sel=n_sel, nh=nh),
        grid=(nb, nq),
        in_specs=[pl.BlockSpec((tq, width), lambda b, i: (b * nq + i, 0)),
                  pl.BlockSpec((n_keys, width), lambda b, i: (b, 0), pipeline_mode=once),
                  pl.BlockSpec((width, n_keys), lambda b, i: (b, 0), pipeline_mode=once),
                  pl.BlockSpec((tq, H_IDX * IDX_K), lambda b, i: (b * nq + i, 0)),
                  pl.BlockSpec((n_keys, IDX_K), lambda b, i: (b, 0), pipeline_mode=once),
                  pl.BlockSpec((LANES, tq), lambda b, i: (0, b * nq + i))],
        out_specs=pl.BlockSpec((tq, width), lambda b, i: (b * nq + i, 0)),
        out_shape=jax.ShapeDtypeStruct(q.shape, BF16),
        scratch_shapes=[pltpu.VMEM((n_keys, tq), I32), pltpu.VMEM((nh, HEAD_DIM + ONES_ROWS, tq), F32),
                        pltpu.VMEM((nh, tk, tq), F32), pltpu.VMEM((nh, tk, tq), BF16)],
        compiler_params=_params(("arbitrary", "arbitrary")),
        name="dsa",
    )(q, k_all, vt_all, qi_cat, ki_cat, kwt)


def _merge_kernel(h_ref, oa_ref, ob_ref, wg1_ref, wg2_ref, b1_ref, b2_ref, wa_ref, wb_ref, o_ref):
    h = h_ref[...]
    g1 = 1.0 / (1.0 + jnp.exp(-(_dot(h, wg1_ref[...]) + b1_ref[...])))
    g2 = 1.0 / (1.0 + jnp.exp(-(_dot(h, wg2_ref[...]) + b2_ref[...])))
    o_ref[...] = (g1 * _dot(oa_ref[...], wa_ref[...]) + g2 * _dot(ob_ref[...], wb_ref[...])).astype(BF16)


def _merge(h_hi, o_a, o_b, w_gate, b_gate, w_pa, w_pb):
    n, d = h_hi.shape
    width = o_a.shape[1]
    tm = _tile(n, 1024, 16)
    tn = _tile(d, 512, LANES)
    nj = d // tn
    b_gate = b_gate.reshape(1, 2 * d)
    row = lambda i, j: (i, 0)
    return pl.pallas_call(
        _merge_kernel,
        grid=(n // tm, nj),
        in_specs=[pl.BlockSpec((tm, d), row), pl.BlockSpec((tm, width), row), pl.BlockSpec((tm, width), row),
                  pl.BlockSpec((d, tn), lambda i, j: (0, j)),
                  pl.BlockSpec((d, tn), lambda i, j: (0, j + nj)),
                  pl.BlockSpec((1, tn), lambda i, j: (0, j)),
                  pl.BlockSpec((1, tn), lambda i, j: (0, j + nj)),
                  pl.BlockSpec((width, tn), lambda i, j: (0, j)),
                  pl.BlockSpec((width, tn), lambda i, j: (0, j))],
        out_specs=pl.BlockSpec((tm, tn), lambda i, j: (i, j)),
        out_shape=jax.ShapeDtypeStruct((n, d), BF16),
        compiler_params=_params(("arbitrary", "arbitrary")),
        name="merge",
    )(h_hi, o_a, o_b, w_gate, w_gate, b_gate, b_gate, w_pa, w_pb)


def _outproj_kernel(m_ref, w_ref, x_ref, mod_ref, gm_ref, gf_ref, x1_ref, h2_ref):
    mod = mod_ref[0]
    x1 = x_ref[...] + mod[2:3, :] * _rms(_dot(m_ref[...], w_ref[...]), gm_ref[...])
    x1_ref[...] = x1
    h2_ref[...] = (_rms(x1, gf_ref[...]) * (1.0 + mod[4:5, :]) + mod[3:4, :]).astype(BF16)


def _outproj(merged, w_out, x2, mod, g_post_mix, g_pre_ffn, seq):
    n, d = x2.shape
    tm = _tile(seq, 256, 16)
    tps = seq // tm
    row = lambda i: (i, 0)
    fix = lambda i: (0, 0)
    return pl.pallas_call(
        _outproj_kernel,
        grid=(n // tm,),
        in_specs=[pl.BlockSpec((tm, d), row), pl.BlockSpec((d, d), fix), pl.BlockSpec((tm, d), row),
                  pl.BlockSpec((1, N_MOD, d), lambda i: (i // tps, 0, 0)),
                  pl.BlockSpec((1, d), fix), pl.BlockSpec((1, d), fix)],
        out_specs=[pl.BlockSpec((tm, d), row), pl.BlockSpec((tm, d), row)],
        out_shape=[jax.ShapeDtypeStruct((n, d), F32), jax.ShapeDtypeStruct((n, d), BF16)],
        compiler_params=_params(("arbitrary",)),
        name="outproj",
    )(merged, w_out, x2, mod, g_post_mix.reshape(1, d), g_pre_ffn.reshape(1, d))


def _gelu_tanh(x):
    return 0.5 * x * (1.0 + jnp.tanh(0.7978845608028654 * (x + 0.044715 * (x * x * x))))


def _ffn_kernel(h_ref, halo_ref, x1_ref, mod_ref, st_ref, wg_ref, wu_ref, wd_ref, wc_ref, bc_ref, gp_ref,
                y_ref, nc_ref, g_scr, act_scr, acc_scr, *, tm, nsub, tps):
    i = pl.program_id(0)
    c = pl.program_id(1)
    sub = tm // nsub
    cur = c % 2

    @pl.when(c == 0)
    def _():
        acc_scr[...] = jnp.zeros_like(acc_scr)
        act_scr[1] = jnp.zeros((tm, act_scr.shape[2]), BF16)

    h = h_ref[...]
    wg = wg_ref[...]
    g = _dot(h, wg)
    u = _dot(h, wu_ref[...])
    wc = wc_ref[...]
    bc = bc_ref[...]
    g_scr[HALO:HALO + tm, :] = g

    acc_scr[...] += _dot(act_scr[1 - cur], wd_ref[...])

    def conv(lo, rows):
        return (bc + wc[0:1, :] * g_scr[lo - 2:lo - 2 + rows, :] + wc[1:2, :] * g_scr[lo - 1:lo - 1 + rows, :]
                + wc[2:3, :] * g_scr[lo:lo + rows, :])

    if nsub == 1:
        g_halo = _dot(halo_ref[...], wg)
        g_scr[0:HALO, :] = g_halo
        g_scr[HALO - 2:HALO, :] = jnp.where(i % tps == 0, st_ref[0], g_halo[HALO - 2:HALO, :])
        nc_ref[0] = g_scr[HALO + tm - 2:HALO + tm, :]
        act_scr[cur] = (_gelu_tanh(conv(HALO, tm)) * u).astype(BF16)
    else:
        for s in range(nsub):
            lo = HALO + s * sub
            nc_ref[s] = g_scr[lo + sub - 2:lo + sub, :]
            g_scr[lo - 2:lo, :] = st_ref[s]
            act_scr[cur, s * sub:(s + 1) * sub, :] = (
                _gelu_tanh(conv(lo, sub)) * u[s * sub:(s + 1) * sub, :]).astype(BF16)

    @pl.when(c == pl.num_programs(1) - 1)
    def _():
        for s in range(nsub):
            rows = slice(s * sub, (s + 1) * sub)
            y_ref[rows, :] = x1_ref[rows, :] + mod_ref[s][5:6, :] * _rms(acc_scr[rows, :], gp_ref[...])


def _ffn(h2, x1, mod, state, w_gate, w_up, w_down, w_conv, b_conv, g_post, seq):
    n, d = x1.shape
    dff = w_gate.shape[1]
    nb = n // seq
    tm = _row_tile(n, seq, 512)
    tf = _tile(dff, 512, LANES)
    n_chunks = dff // tf
    up = lambda c: jnp.minimum(c, n_chunks - 1)
    down = lambda c: jnp.maximum(c - 1, 0)
    per_tile_c = lambda i, c: (i, 0, up(c))
    if tm <= seq:
        nsub, tps = 1, seq // tm
        per_seq = lambda i, c: (i // tps, 0, 0)
        per_seq_c = lambda i, c: (i // tps, 0, up(c))
    else:
        nsub, tps = tm // seq, 1
        per_seq = lambda i, c: (i, 0, 0)
        per_seq_c = per_tile_c
    halo_blocks = tm // HALO
    row = lambda i, c: (i, 0)
    y, new_conv = pl.pallas_call(
        functools.partial(_ffn_kernel, tm=tm, nsub=nsub, tps=tps),
        grid=(n // tm, n_chunks + 1),
        in_specs=[pl.BlockSpec((tm, d), row),
                  pl.BlockSpec((HALO, d), lambda i, c: (jnp.maximum(i * halo_blocks - 1, 0), 0)),
                  pl.BlockSpec((tm, d), row),
                  pl.BlockSpec((nsub, N_MOD, d), per_seq),
                  pl.BlockSpec((nsub, CONV_W - 1, tf), per_seq_c),
                  pl.BlockSpec((d, tf), lambda i, c: (0, up(c))),
                  pl.BlockSpec((d, tf), lambda i, c: (0, up(c))),
                  pl.BlockSpec((tf, d), lambda i, c: (down(c), 0)),
                  pl.BlockSpec((CONV_W, tf), lambda i, c: (0, up(c))),
                  pl.BlockSpec((1, tf), lambda i, c: (0, up(c))),
                  pl.BlockSpec((1, d), lambda i, c: (0, 0))],
        out_specs=[pl.BlockSpec((tm, d), row),
                   pl.BlockSpec((nsub, CONV_W - 1, tf), per_tile_c)],
        out_shape=[jax.ShapeDtypeStruct((n, d), F32),
                   jax.ShapeDtypeStruct((nb * tps, CONV_W - 1, dff), F32)],
        scratch_shapes=[pltpu.VMEM((HALO + tm, tf), F32), pltpu.VMEM((2, tm, tf), BF16),
                        pltpu.VMEM((tm, d), F32)],
        compiler_params=_params(("arbitrary", "arbitrary")),
        name="ffn",
    )(h2, h2, x1, mod, state, w_gate, w_up, w_down, w_conv, b_conv.reshape(1, dff), g_post.reshape(1, d))
    return y, new_conv.reshape(nb, tps, CONV_W - 1, dff)[:, tps - 1]


def _rope_tables(pos):
    def angles(dim):
        half = dim // 2
        inv = ROPE_THETA ** (-jnp.arange(half, dtype=F32) * 2.0 / dim)
        ang = pos.astype(F32)[:, None] * inv[None, :]
        return jnp.cos(ang), jnp.sin(ang)

    cos, sin = angles(HEAD_DIM)
    cos_h = jnp.concatenate([cos, cos], axis=1)
    sin_h = jnp.concatenate([-sin, sin], axis=1)

    cos, sin = angles(D_IDX)
    zero = jnp.zeros_like(sin)
    n_rot = H_IDX + 1
    tail = IDX_COLS - n_rot * D_IDX
    t = pos.shape[0]
    c_i = jnp.concatenate([jnp.tile(jnp.concatenate([cos, cos], 1), (1, n_rot)), jnp.ones((t, tail), F32)], 1)
    s1_i = jnp.concatenate([jnp.tile(jnp.concatenate([-sin, zero], 1), (1, n_rot)), jnp.zeros((t, tail), F32)], 1)
    s2_i = jnp.concatenate([jnp.tile(jnp.concatenate([zero, sin], 1), (1, n_rot)), jnp.zeros((t, tail), F32)], 1)
    return cos_h, sin_h, c_i, s1_i, s2_i


def _with_past(past, new, nb, seq, n_keys):
    w = new.shape[1]
    if past is None and n_keys == seq:
        return new
    parts = [] if past is None else [past.astype(new.dtype).reshape(nb, -1, w)]
    parts.append(new.reshape(nb, seq, w))
    have = sum(p.shape[1] for p in parts)
    if n_keys > have:
        parts.append(jnp.zeros((nb, n_keys - have, w), new.dtype))
    return jnp.concatenate(parts, axis=1).reshape(nb * n_keys, w)


def _layer(x, mod, past, conv_state, p):
    nb, seq, d = x.shape
    n = nb * seq
    nh = p["w_pa"].shape[0] // HEAD_DIM
    width = nh * HEAD_DIM
    n_past = 0 if past is None else past["a_k"].shape[1]
    n_valid = n_past + seq
    n_keys = -(-n_valid // 256) * 256
    x2 = x.reshape(n, d)

    h_hi, h_lo = _norm(x2, p["g_pre_mix"], mod, seq)
    cos_h, sin_h, c_i, s1_i, s2_i = _rope_tables(n_past + jnp.arange(seq, dtype=I32))
    (qa, ka32, ka16, va32, va16, qb, kb32, kb16, vb32, vb16) = _inproj(h_hi, p["w_main"], cos_h, sin_h, seq, width)
    qi_hi, qi_lo, kw, ki_hi, ki_lo = _idxproj(h_hi, h_lo, p["w_idx_hi"], p["w_idx_lo"], c_i, s1_i, s2_i, seq)

    q_hi = qi_hi.reshape(n, H_IDX, D_IDX)
    q_lo = qi_lo.reshape(n, H_IDX, D_IDX)
    qi_cat = jnp.concatenate([q_hi, q_hi, q_lo, jnp.zeros_like(q_hi)], axis=2).reshape(n, H_IDX * IDX_K)
    k_hi, k_lo = ki_hi[:, :D_IDX], ki_lo[:, :D_IDX]
    if past is None:
        pa_k = pa_v = pb_k = pb_v = pi_hi = pi_lo = None
    else:
        pa_k, pa_v, pb_k, pb_v = (past[k].reshape(nb, n_past, width) for k in ("a_k", "a_v", "b_k", "b_v"))
        pi_hi, pi_lo = _split(past["b_kidx"].astype(F32))
    k_hi = _with_past(pi_hi, k_hi, nb, seq, n_keys)
    k_lo = _with_past(pi_lo, k_lo, nb, seq, n_keys)
    ki_cat = jnp.concatenate([k_hi, k_lo, k_hi, jnp.zeros_like(k_hi)], axis=1)

    o_a = _sb(qa, _with_past(pa_k, ka16, nb, seq, n_keys), _with_past(pa_v, va16, nb, seq, n_keys),
              nb, seq, n_keys, n_past, nh)
    seq_q = max(seq, LANES)
    pad_q = lambda a: a if seq_q == seq else jnp.pad(
        a.reshape(nb, seq, -1), ((0, 0), (0, seq_q - seq), (0, 0))).reshape(nb * seq_q, -1)
    vt_all = _with_past(pb_v, vb16, nb, seq, n_keys).reshape(nb, n_keys, width)
    vt_all = jnp.swapaxes(vt_all, 1, 2).reshape(nb * width, n_keys)
    o_b = _dsat(pad_q(qb), _with_past(pb_k, kb16, nb, seq, n_keys), vt_all, pad_q(qi_cat), ki_cat,
                pad_q(kw).T, nb, seq_q, n_keys, n_valid, n_past, nh)
    if seq_q != seq:
        o_b = o_b.reshape(nb, seq_q, width)[:, :seq].reshape(n, width)

    merged = _merge(h_hi, o_a, o_b, p["w_gate"], p["b_gate"], p["w_pa"], p["w_pb"])
    x1, h2 = _outproj(merged, p["w_out"], x2, mod, p["g_post_mix"], p["g_pre_ffn"], seq)
    y, new_conv = _ffn(h2, x1, mod, conv_state, p["w_ffn_gate"], p["w_ffn_up"], p["w_ffn_down"],
                       p["w_conv"], p["b_conv"], p["g_post_ffn"], seq)

    heads = lambda a: a.reshape(nb, seq, nh, HEAD_DIM)
    new = (heads(ka32), heads(va32), heads(kb32), heads(vb32), kw[:, :D_IDX].reshape(nb, seq, D_IDX), new_conv)
    return y.reshape(nb, seq, d), new


def kernel(x_prompt, x_sample, c_prompt, c_sample, cache_a_k, cache_a_v, cache_b_k, cache_b_v, cache_b_kidx,
           state_ffn_conv, w_mod, b_mod, g_pre_mix, w_in, w_merge_gate, b_merge_gate, w_proj_a, w_proj_b, w_out,
           g_post_mix, g_pre_ffn, w_ffn_gate, w_ffn_up, w_conv, b_conv, w_ffn_down, g_post_ffn):
    depth = w_mod.shape[0]
    nbp, nbs = x_prompt.shape[0], x_sample.shape[0]
    d = x_prompt.shape[2]
    width = w_proj_a.shape[1]
    dff = w_ffn_gate.shape[2]
    assert w_in.shape[2] == 6 * width + H_IDX * D_IDX + D_IDX + H_IDX
    assert cache_b_kidx.shape[-1] == D_IDX and cache_a_k.shape[-1] == HEAD_DIM

    y_p, y_s = x_prompt, x_sample
    st_p = [[] for _ in range(6)]
    st_s = [[] for _ in range(6)]
    for l in range(depth):
        w_idx = jnp.pad(w_in[l][:, 6 * width:], ((0, 0), (0, IDX_COLS - (w_in.shape[2] - 6 * width))))
        w_idx_hi, w_idx_lo = _split(w_idx)
        p = dict(
            g_pre_mix=g_pre_mix[l], w_main=w_in[l][:, :6 * width].astype(BF16), w_idx_hi=w_idx_hi, w_idx_lo=w_idx_lo,
            w_gate=w_merge_gate[l].astype(BF16), b_gate=b_merge_gate[l],
            w_pa=w_proj_a[l].astype(BF16), w_pb=w_proj_b[l].astype(BF16), w_out=w_out[l].astype(BF16),
            g_post_mix=g_post_mix[l], g_pre_ffn=g_pre_ffn[l],
            w_ffn_gate=w_ffn_gate[l].astype(BF16), w_ffn_up=w_ffn_up[l].astype(BF16),
            w_ffn_down=w_ffn_down[l].astype(BF16), w_conv=w_conv[l], b_conv=b_conv[l], g_post_ffn=g_post_ffn[l])

        c_all = jnp.concatenate([c_prompt, c_sample], axis=0)
        pad = -c_all.shape[0] % 16
        mod = _mod(jnp.pad(c_all, ((0, pad), (0, 0))), w_mod[l], b_mod[l]).reshape(-1, N_MOD, d)
        past = dict(a_k=cache_a_k[l], a_v=cache_a_v[l], b_k=cache_b_k[l], b_v=cache_b_v[l], b_kidx=cache_b_kidx[l])
        y_p, new_p = _layer(y_p, mod[:nbp], None, jnp.zeros((nbp, CONV_W - 1, dff), F32), p)
        y_s, new_s = _layer(y_s, mod[nbp:nbp + nbs], past, state_ffn_conv[l], p)
        for i in range(6):
            st_p[i].append(new_p[i])
            st_s[i].append(new_s[i])
    sp = [jnp.stack(s, axis=0) for s in st_p]
    ss = [jnp.stack(s, axis=0) for s in st_s]
    return (y_p, y_s, sp[0], sp[1], sp[2], sp[3], sp[4], sp[5], ss[0], ss[1], ss[2], ss[3], ss[4], ss[5])
```

```python
import functools

import jax
import jax.numpy as jnp
from jax import lax
from jax.experimental import pallas as pl
from jax.experimental.pallas import tpu as pltpu

F32 = jnp.float32
BF16 = jnp.bfloat16
I32 = jnp.int32

CHUNK = 64
HEAD_DIM = 128
H_IDX = 4
D_IDX = 64
TOPK_MAX = 256
CONV_W = 3
ROPE_THETA = 10000.0
RMS_EPS = 1e-6
N_MOD = 6
IDX_COLS = 384
IDX_K = 256
LANES = 128
HALO = 16
ONES_ROWS = 16
LOG2E = 1.4426950408889634
UNDERFLOW_LOG2 = 160.0
NEG = -1e30
INT_MIN = -2 ** 31
VMEM_LIMIT = 56 * 1024 * 1024


def _tile(n, pref, mult):
    t = min(pref, n)
    t -= t % mult
    while t >= mult:
        if n % t == 0:
            return t
        t -= mult
    return n


def _row_tile(n, seq, pref):
    if seq >= pref or n == seq:
        return _tile(seq, pref, 16)
    return _tile(n, pref, seq)


def _params(sem):
    return pltpu.CompilerParams(dimension_semantics=sem, vmem_limit_bytes=VMEM_LIMIT)


def _split(x):
    hi = x.astype(BF16)
    lo = (x - hi.astype(F32)).astype(BF16)
    return hi, lo


def _dot(a, b):
    return jnp.dot(a, b, preferred_element_type=F32)


def _dot_t(a, b):
    return lax.dot_general(a, b, (((1,), (1,)), ((), ())), preferred_element_type=F32)


def _rms(x, g):
    ms = jnp.mean(x * x, axis=-1, keepdims=True)
    return x * lax.rsqrt(ms + RMS_EPS) * g


def _mod_kernel(c_ref, w_ref, b_ref, o_ref):
    c = c_ref[...]
    s = c / (1.0 + jnp.exp(-c))
    s_hi, s_lo = _split(s)
    w_hi, w_lo = _split(w_ref[...])
    o_ref[...] = _dot(s_hi, w_hi) + _dot(s_hi, w_lo) + _dot(s_lo, w_hi) + b_ref[...]


def _mod(c, w, b):
    bc, d = c.shape
    n = w.shape[1]
    tn = _tile(n, 1024, LANES)
    return pl.pallas_call(
        _mod_kernel,
        grid=(n // tn,),
        in_specs=[pl.BlockSpec((bc, d), lambda j: (0, 0)),
                  pl.BlockSpec((d, tn), lambda j: (0, j)),
                  pl.BlockSpec((1, tn), lambda j: (0, j))],
        out_specs=pl.BlockSpec((bc, tn), lambda j: (0, j)),
        out_shape=jax.ShapeDtypeStruct((bc, n), F32),
        compiler_params=_params(("arbitrary",)),
        name="mod",
    )(c, w, b.reshape(1, n))


def _norm_kernel(x_ref, g_ref, mod_ref, hi_ref, lo_ref):
    m = mod_ref[0]
    h = _rms(x_ref[...], g_ref[...]) * (1.0 + m[1:2, :]) + m[0:1, :]
    hi, lo = _split(h)
    hi_ref[...] = hi
    lo_ref[...] = lo


def _norm(x2, g, mod, seq):
    n, d = x2.shape
    tr = _tile(seq, 256, 16)
    tps = seq // tr
    return pl.pallas_call(
        _norm_kernel,
        grid=(n // tr,),
        in_specs=[pl.BlockSpec((tr, d), lambda i: (i, 0)),
                  pl.BlockSpec((1, d), lambda i: (0, 0)),
                  pl.BlockSpec((1, N_MOD, d), lambda i: (i // tps, 0, 0))],
        out_specs=[pl.BlockSpec((tr, d), lambda i: (i, 0))] * 2,
        out_shape=[jax.ShapeDtypeStruct((n, d), BF16)] * 2,
        compiler_params=_params(("arbitrary",)),
        name="norm",
    )(x2, g.reshape(1, d), mod)


def _inproj_kernel(a_ref, w_ref, cos_ref, sin_ref,
                   qa_ref, ka32_ref, ka16_ref, va32_ref, va16_ref,
                   qb_ref, kb32_ref, kb16_ref, vb32_ref, vb16_ref, *, npj, tn, scale):
    grp = pl.program_id(1) // npj
    acc = _dot(a_ref[...], w_ref[...])

    def rope(x):
        parts = []
        for s in range(tn // HEAD_DIM):
            xs = x[:, s * HEAD_DIM:(s + 1) * HEAD_DIM]
            parts.append(xs * cos_ref[...] + pltpu.roll(xs, HEAD_DIM // 2, 1) * sin_ref[...])
        return parts[0] if len(parts) == 1 else jnp.concatenate(parts, axis=1)

    @pl.when(grp == 0)
    def _():
        qa_ref[...] = (acc * scale).astype(BF16)

    @pl.when(grp == 1)
    def _():
        ka32_ref[...] = acc
        ka16_ref[...] = acc.astype(BF16)

    @pl.when(grp == 2)
    def _():
        va32_ref[...] = acc
        va16_ref[...] = acc.astype(BF16)

    @pl.when(grp == 3)
    def _():
        qb_ref[...] = (rope(acc) * scale).astype(BF16)

    @pl.when(grp == 4)
    def _():
        r = rope(acc)
        kb32_ref[...] = r
        kb16_ref[...] = r.astype(BF16)

    @pl.when(grp == 5)
    def _():
        vb32_ref[...] = acc
        vb16_ref[...] = acc.astype(BF16)


def _inproj(h_hi, w_main, cos_t, sin_t, seq, width):
    n, d = h_hi.shape
    tm = _row_tile(n, seq, 1024)
    tn = _tile(width, 512, LANES)
    npj = width // tn
    if tm <= seq:
        tps = seq // tm
        tab_map = lambda i, j: (i % tps, 0)
    else:
        cos_t = jnp.tile(cos_t, (tm // seq, 1))
        sin_t = jnp.tile(sin_t, (tm // seq, 1))
        tab_map = lambda i, j: (0, 0)

    def omap(g):
        return lambda i, j: (i, jnp.clip(j - g * npj, 0, npj - 1))

    dts = [BF16, F32, BF16, F32, BF16, BF16, F32, BF16, F32, BF16]
    grps = [0, 1, 1, 2, 2, 3, 4, 4, 5, 5]
    return pl.pallas_call(
        functools.partial(_inproj_kernel, npj=npj, tn=tn, scale=LOG2E * HEAD_DIM ** -0.5),
        grid=(n // tm, 6 * npj),
        in_specs=[pl.BlockSpec((tm, d), lambda i, j: (i, 0)),
                  pl.BlockSpec((d, tn), lambda i, j: (0, j)),
                  pl.BlockSpec((tm, HEAD_DIM), tab_map),
                  pl.BlockSpec((tm, HEAD_DIM), tab_map)],
        out_specs=[pl.BlockSpec((tm, tn), omap(g)) for g in grps],
        out_shape=[jax.ShapeDtypeStruct((n, width), dt) for dt in dts],
        compiler_params=_params(("arbitrary", "arbitrary")),
        name="inproj",
    )(h_hi, w_main, cos_t, sin_t)


def _idxproj_kernel(hi_ref, lo_ref, whi_ref, wlo_ref, c_ref, s1_ref, s2_ref,
                    qhi_ref, qlo_ref, kw_ref, khi_ref, klo_ref):
    a_hi = hi_ref[...]
    w_hi = whi_ref[...]
    acc = _dot(a_hi, w_hi) + _dot(a_hi, wlo_ref[...]) + _dot(lo_ref[...], w_hi)
    for s in range(IDX_COLS // LANES):
        sl = slice(s * LANES, (s + 1) * LANES)
        xs = acc[:, sl]
        r = (xs * c_ref[:, sl] + pltpu.roll(xs, LANES - D_IDX // 2, 1) * s1_ref[:, sl]
             + pltpu.roll(xs, D_IDX // 2, 1) * s2_ref[:, sl])
        hi, lo = _split(r)
        if s < 2:
            qhi_ref[:, sl] = hi
            qlo_ref[:, sl] = lo
        else:
            kw_ref[...] = r
            khi_ref[...] = hi
            klo_ref[...] = lo


def _idxproj(h_hi, h_lo, w_hi, w_lo, c_t, s1_t, s2_t, seq):
    n, d = h_hi.shape
    tm = _row_tile(n, seq, 512)
    if tm <= seq:
        tps = seq // tm
        tab_map = lambda i: (i % tps, 0)
    else:
        c_t, s1_t, s2_t = (jnp.tile(t, (tm // seq, 1)) for t in (c_t, s1_t, s2_t))
        tab_map = lambda i: (0, 0)
    row = lambda i: (i, 0)
    return pl.pallas_call(
        _idxproj_kernel,
        grid=(n // tm,),
        in_specs=[pl.BlockSpec((tm, d), row), pl.BlockSpec((tm, d), row),
                  pl.BlockSpec((d, IDX_COLS), lambda i: (0, 0)),
                  pl.BlockSpec((d, IDX_COLS), lambda i: (0, 0)),
                  pl.BlockSpec((tm, IDX_COLS), tab_map),
                  pl.BlockSpec((tm, IDX_COLS), tab_map),
                  pl.BlockSpec((tm, IDX_COLS), tab_map)],
        out_specs=[pl.BlockSpec((tm, 2 * LANES), row), pl.BlockSpec((tm, 2 * LANES), row),
                   pl.BlockSpec((tm, LANES), row), pl.BlockSpec((tm, LANES), row),
                   pl.BlockSpec((tm, LANES), row)],
        out_shape=[jax.ShapeDtypeStruct((n, 2 * LANES), BF16), jax.ShapeDtypeStruct((n, 2 * LANES), BF16),
                   jax.ShapeDtypeStruct((n, LANES), F32), jax.ShapeDtypeStruct((n, LANES), BF16),
                   jax.ShapeDtypeStruct((n, LANES), BF16)],
        compiler_params=_params(("arbitrary",)),
        name="idxproj",
    )(h_hi, h_lo, w_hi, w_lo, c_t, s1_t, s2_t)


def _sb_kernel(q_ref, k_ref, v_ref, o_ref, acc_scr, z_scr, i_scr, *, tq, tk, past, group):
    q0 = past + pl.program_id(2) * tq
    n_kt = (q0 + tq - 2) // tk + 1
    n_clear = q0 // tk
    row_pos = q0 + lax.broadcasted_iota(I32, (tq, 1), 0)
    col = lax.broadcasted_iota(I32, (1, tk), 1)
    suffix = jnp.where(lax.broadcasted_iota(I32, (tk, tk), 0) >= lax.broadcasted_iota(I32, (tk, tk), 1),
                       1.0, 0.0).astype(BF16)
    suffix2 = jnp.concatenate([suffix, suffix], axis=0)
    acc_scr[...] = jnp.zeros_like(acc_scr)

    heads = [slice(h * HEAD_DIM, (h + 1) * HEAD_DIM) for h in range(group)]

    def tiles(js, carries, masked):
        offs = [pl.multiple_of(j * tk, tk) for j in js]
        if masked:
            valids = [(off + col) < row_pos for off in offs]
        for t, off in enumerate(offs):
            for h, hs in enumerate(heads):
                z_scr[t * group + h] = _dot_t(q_ref[:, hs], k_ref[pl.ds(off, tk), hs])
        for t in range(len(js)):
            for h in range(group):
                z = z_scr[t * group + h]
                sp = jnp.maximum(z, 0.0) + jnp.log2(1.0 + jnp.exp2(-jnp.abs(z)))
                if masked:
                    sp = jnp.where(valids[t], sp, 0.0)
                hi, lo = _split(sp)
                i_scr[t * group + h] = _dot(jnp.concatenate([hi, lo], axis=1), suffix2)
        carries = list(carries)
        for t, off in enumerate(offs):
            for h, hs in enumerate(heads):
                w = jnp.exp2(z_scr[t * group + h] - i_scr[t * group + h] - carries[h])
                if masked:
                    w = jnp.where(valids[t], w, 0.0)
                acc_scr[h] += _dot(w.astype(BF16), v_ref[pl.ds(off, tk), hs])
                carries[h] = carries[h] + i_scr[t * group + h, :, 0:1]
        return tuple(carries)

    def least(carries):
        m = carries[0]
        for c in carries[1:]:
            m = jnp.minimum(m, c)
        return jnp.min(m)

    st = tuple(jnp.zeros((tq, 1), F32) for _ in range(group))
    st = lax.fori_loop(0, n_kt - n_clear, lambda t, c: tiles([n_kt - 1 - t], c, True), st)

    def more(state):
        t, low, _ = state
        return jnp.logical_and(t < n_clear, low < UNDERFLOW_LOG2)

    def step(state):
        t, _, carries = state
        carries = tiles([n_clear - 1 - t], carries, False)
        return t + 1, least(carries), carries

    lax.while_loop(more, step, (jnp.int32(0), least(st), st))
    for h in range(group):
        o_ref[:, h * HEAD_DIM:(h + 1) * HEAD_DIM] = acc_scr[h].astype(o_ref.dtype)


def _sb(q, k_all, v_all, nb, seq, n_keys, past, nh):
    tq = _tile(seq, 256, 16)
    tk = 256
    nq = seq // tq
    group = 4 if nh % 4 == 0 else (2 if nh % 2 == 0 else 1)
    gw = group * HEAD_DIM
    return pl.pallas_call(
        functools.partial(_sb_kernel, tq=tq, tk=tk, past=past, group=group),
        grid=(nb, nh // group, nq),
        in_specs=[pl.BlockSpec((tq, gw), lambda b, h, i: (b * nq + i, h)),
                  pl.BlockSpec((n_keys, gw), lambda b, h, i: (b, h)),
                  pl.BlockSpec((n_keys, gw), lambda b, h, i: (b, h))],
        out_specs=pl.BlockSpec((tq, gw), lambda b, h, i: (b * nq + i, h)),
        out_shape=jax.ShapeDtypeStruct(q.shape, BF16),
        scratch_shapes=[pltpu.VMEM((group, tq, HEAD_DIM), F32), pltpu.VMEM((group, tq, tk), F32),
                        pltpu.VMEM((group, tq, tk), F32)],
        compiler_params=_params(("arbitrary", "arbitrary", "arbitrary")),
        name="sb",
    )(q, k_all, v_all)


def _dsa_kernel(q_ref, k_ref, v_ref, qi_ref, ki_ref, kw_ref, o_ref, key_scr, acc_scr, s_scr, p_scr,
                *, tq, tk, past, n_valid, n_sel, nh):
    q0 = past + pl.program_id(1) * tq
    row_pos = q0 + lax.broadcasted_iota(I32, (tq, 1), 0)
    row_lim = jnp.minimum((row_pos // CHUNK + 1) * CHUNK, n_valid)
    n_kt = (jnp.minimum(((q0 + tq - 1) // CHUNK + 1) * CHUNK, n_valid) + tk - 1) // tk
    col = lax.broadcasted_iota(I32, (1, tk), 1)
    ksel = float(n_sel)

    wrow = kw_ref[:, D_IDX:D_IDX + H_IDX] * ((H_IDX ** -0.5) * (D_IDX ** -0.5))

    def score_tile(j, _):
        off = pl.multiple_of(j * tk, tk)
        kc = ki_ref[pl.ds(off, tk), :]
        score = jnp.zeros((tq, tk), F32)
        for h in range(H_IDX):
            s = _dot_t(qi_ref[:, h * IDX_K:(h + 1) * IDX_K], kc)
            score = score + wrow[:, h:h + 1] * jnp.maximum(s, 0.0)
        bits = pltpu.bitcast(score, I32)
        key = jnp.where(bits < 0, INT_MIN - bits, bits)
        key_scr[:, pl.ds(off, tk)] = jnp.where((off + col) < row_lim, key, INT_MIN)
        return 0

    lax.fori_loop(0, n_kt, score_tile, 0)

    rb = min(tq, 128)

    def count_ge(thr_b):
        outs = []
        for r in range(tq // rb):
            rows = slice(r * rb, (r + 1) * rb)
            thr_r = thr_b[rows]

            def body(j, cnt):
                off = pl.multiple_of(j * tk, tk)
                key = key_scr[rows, pl.ds(off, tk)]
                for s in range(tk // LANES):
                    cnt = cnt + jnp.where(key[:, s * LANES:(s + 1) * LANES] >= thr_r, 1.0, 0.0)
                return cnt

            cnt = lax.fori_loop(0, n_kt, body, jnp.zeros((rb, LANES), F32))
            outs.append(jnp.broadcast_to(jnp.sum(cnt, axis=1, keepdims=True), (rb, LANES)))
        return outs[0] if len(outs) == 1 else jnp.concatenate(outs, axis=0)

    def bit_step(it, cand):
        trial = cand | lax.shift_left(jnp.int32(1), 30 - it)
        return jnp.where(count_ge(trial) >= ksel, trial, cand)

    zero_b = jnp.zeros((tq, LANES), I32)
    thr_b = lax.fori_loop(0, 31, bit_step, jnp.where(count_ge(zero_b) >= ksel, zero_b, INT_MIN))
    no_thr_b = thr_b == INT_MIN
    cnt_gt = count_ge(jnp.where(no_thr_b, thr_b, thr_b + 1))[:, 0:1]
    thr = thr_b[:, 0:1]
    no_thr = thr == INT_MIN
    need = jnp.where(no_thr, 0.0, ksel - cnt_gt)

    prefix = jnp.where(lax.broadcasted_iota(I32, (tk, tk), 0) <= lax.broadcasted_iota(I32, (tk, tk), 1),
                       1.0, 0.0).astype(BF16)

    def bias_tile(j, carry):
        off = pl.multiple_of(j * tk, tk)
        key = key_scr[:, pl.ds(off, tk)]
        eq = key == thr
        eqf = jnp.where(eq, 1.0, 0.0)
        rank_incl = _dot(eqf.astype(BF16), prefix) + carry
        tie_ok = (rank_incl - eqf) < need
        bias = jnp.where(key > thr, 0.0, jnp.where(eq, jnp.where(tie_ok, 0.0, NEG), NEG))
        key_scr[:, pl.ds(off, tk)] = pltpu.bitcast(bias, I32)
        return rank_incl[:, tk - 1:tk]

    lax.fori_loop(0, n_kt, bias_tile, jnp.zeros((tq, 1), F32))

    acc_scr[...] = jnp.zeros_like(acc_scr)
    ones_v = jnp.ones((tk, HEAD_DIM), BF16)
    heads = [slice(h * HEAD_DIM, (h + 1) * HEAD_DIM) for h in range(nh)]

    def attn_tile(j, ms):
        off = pl.multiple_of(j * tk, tk)
        bias = pltpu.bitcast(key_scr[:, pl.ds(off, tk)], F32)
        for h, hs in enumerate(heads):
            s_scr[h] = _dot_t(q_ref[:, hs], k_ref[pl.ds(off, tk), hs]) + bias
        new_m = [jnp.maximum(m, jnp.max(s_scr[h], axis=1, keepdims=True)) for h, m in enumerate(ms)]
        for h in range(nh):
            p_scr[h] = jnp.exp2(s_scr[h] - new_m[h]).astype(BF16)
        for h, hs in enumerate(heads):
            v_ext = jnp.concatenate([v_ref[pl.ds(off, tk), hs], ones_v], axis=1)
            acc_scr[h] = jnp.exp2(ms[h] - new_m[h]) * acc_scr[h] + _dot(p_scr[h], v_ext)
        return tuple(new_m)

    lax.fori_loop(0, n_kt, attn_tile, tuple(jnp.full((tq, 1), NEG, F32) for _ in range(nh)))
    for h, hs in enumerate(heads):
        o_ref[:, hs] = (acc_scr[h, :, :HEAD_DIM] / acc_scr[h, :, HEAD_DIM:]).astype(o_ref.dtype)


def _dsa(q, k_all, v_all, qi_cat, ki_cat, kw, nb, seq, n_keys, n_valid, past, nh):
    tq = _tile(seq, 256, 16)
    tk = 256
    nq = seq // tq
    width = nh * HEAD_DIM
    n_sel = min(TOPK_MAX, n_valid // 4)
    once = pl.Buffered(1)
    return pl.pallas_call(
        functools.partial(_dsa_kernel, tq=tq, tk=tk, past=past, n_valid=n_valid, n_sel=n_sel, nh=nh),
        grid=(nb, nq),
        in_specs=[pl.BlockSpec((tq, width), lambda b, i: (b * nq + i, 0)),
                  pl.BlockSpec((n_keys, width), lambda b, i: (b, 0), pipeline_mode=once),
                  pl.BlockSpec((n_keys, width), lambda b, i: (b, 0), pipeline_mode=once),
                  pl.BlockSpec((tq, H_IDX * IDX_K), lambda b, i: (b * nq + i, 0)),
                  pl.BlockSpec((n_keys, IDX_K), lambda b, i: (b, 0), pipeline_mode=once),
                  pl.BlockSpec((tq, LANES), lambda b, i: (b * nq + i, 0))],
        out_specs=pl.BlockSpec((tq, width), lambda b, i: (b * nq + i, 0)),
        out_shape=jax.ShapeDtypeStruct(q.shape, BF16),
        scratch_shapes=[pltpu.VMEM((tq, n_keys), I32), pltpu.VMEM((nh, tq, 2 * HEAD_DIM), F32),
                        pltpu.VMEM((nh, tq, tk), F32), pltpu.VMEM((nh, tq, tk), BF16)],
        compiler_params=_params(("arbitrary", "arbitrary")),
        name="dsa",
    )(q, k_all, v_all, qi_cat, ki_cat, kw)


def _fold8(parts, op):
    while len(parts) > 1:
        parts = [op(parts[i], parts[i + 1]) if i + 1 < len(parts) else parts[i] for i in range(0, len(parts), 2)]
    return parts[0]


def _dsat_kernel(q_ref, k_ref, vt_ref, qi_ref, ki_ref, kwt_ref, o_ref, key_scr, acc_scr, s_scr, p_scr,
                 *, tq, tk, past, n_valid, n_sel, nh):
    q0 = past + pl.program_id(1) * tq
    col_pos = q0 + lax.broadcasted_iota(I32, (1, tq), 1)
    col_lim = jnp.minimum((col_pos // CHUNK + 1) * CHUNK, n_valid)
    n_kt = (jnp.minimum(((q0 + tq - 1) // CHUNK + 1) * CHUNK, n_valid) + tk - 1) // tk
    row = lax.broadcasted_iota(I32, (tk, 1), 0)
    ksel = float(n_sel)
    groups = [slice(8 * r, 8 * r + 8) for r in range(tk // 8)]

    wrow = kwt_ref[D_IDX:D_IDX + H_IDX, :] * ((H_IDX ** -0.5) * (D_IDX ** -0.5))

    def score_pair(t, _):
        offs = [pl.multiple_of(jnp.minimum(2 * t + u, n_kt - 1) * tk, tk) for u in range(2)]
        raw = [[_dot_t(ki_ref[pl.ds(off, tk), :], qi_ref[:, h * IDX_K:(h + 1) * IDX_K]) for h in range(H_IDX)]
               for off in offs]
        for off, dots in zip(offs, raw):
            score = jnp.zeros((tk, tq), F32)
            for h, s in enumerate(dots):
                score = score + wrow[h:h + 1, :] * jnp.maximum(s, 0.0)
            bits = pltpu.bitcast(score, I32)
            key = jnp.where(bits < 0, INT_MIN - bits, bits)
            key_scr[pl.ds(off, tk), :] = jnp.where((off + row) < col_lim, key, INT_MIN)
        return 0

    lax.fori_loop(0, (n_kt + 1) // 2, score_pair, 0)

    def count_ge(thr8):
        def body(t, cnt):
            for u in range(2):
                j = 2 * t + u
                off = pl.multiple_of(jnp.minimum(j, n_kt - 1) * tk, tk)
                key = key_scr[pl.ds(off, tk), :]
                part = _fold8([jnp.where(key[g, :] >= thr8, 1.0, 0.0) for g in groups], jnp.add)
                cnt = cnt + (part if u == 0 else jnp.where(j < n_kt, part, 0.0))
            return cnt

        cnt = lax.fori_loop(0, (n_kt + 1) // 2, body, jnp.zeros((8, tq), F32))
        return jnp.broadcast_to(jnp.sum(cnt, axis=0, keepdims=True), (8, tq))

    def bit_step(it, st):
        cand, have = st
        trial = cand | lax.shift_left(jnp.int32(1), 30 - it)
        cnt = count_ge(trial)
        take = cnt >= ksel
        return jnp.where(take, trial, cand), jnp.where(take, cnt, have)

    zero8 = jnp.zeros((8, tq), I32)
    cnt0 = count_ge(zero8)
    start = (jnp.where(cnt0 >= ksel, zero8, INT_MIN), jnp.where(cnt0 >= ksel, cnt0, float(2 ** 30)))
    thr8, have8 = lax.fori_loop(0, 31, bit_step, start)
    thr = thr8[0:1, :]
    overfull = jnp.max(jnp.where(have8 > ksel, 1.0, 0.0)) > 0.0

    @pl.when(jnp.logical_not(overfull))
    def _():
        def bias_tile(j, _):
            off = pl.multiple_of(j * tk, tk)
            bias = jnp.where(key_scr[pl.ds(off, tk), :] >= thr, 0.0, NEG)
            key_scr[pl.ds(off, tk), :] = pltpu.bitcast(bias, I32)
            return 0

        lax.fori_loop(0, n_kt, bias_tile, 0)

    @pl.when(overfull)
    def _():
        no_thr8 = thr8 == INT_MIN
        cnt_gt = count_ge(jnp.where(no_thr8, thr8, thr8 + 1))
        need = jnp.where(no_thr8, 0.0, ksel - cnt_gt)[0:1, :]
        prefix = jnp.where(lax.broadcasted_iota(I32, (tk, tk), 0) >= lax.broadcasted_iota(I32, (tk, tk), 1),
                           1.0, 0.0).astype(BF16)

        def bias_tile(j, carry):
            off = pl.multiple_of(j * tk, tk)
            key = key_scr[pl.ds(off, tk), :]
            eq = key == thr
            eqf = jnp.where(eq, 1.0, 0.0)
            rank_incl = _dot(prefix, eqf.astype(BF16)) + carry
            tie_ok = (rank_incl - eqf) < need
            bias = jnp.where(key > thr, 0.0, jnp.where(eq, jnp.where(tie_ok, 0.0, NEG), NEG))
            key_scr[pl.ds(off, tk), :] = pltpu.bitcast(bias, I32)
            return rank_incl[tk - 1:tk, :]

        lax.fori_loop(0, n_kt, bias_tile, jnp.zeros((1, tq), F32))

    acc_scr[...] = jnp.zeros_like(acc_scr)
    ones_v = jnp.ones((ONES_ROWS, tk), BF16)
    heads = [slice(h * HEAD_DIM, (h + 1) * HEAD_DIM) for h in range(nh)]

    def logits(j, slot, ms):
        off = pl.multiple_of(j * tk, tk)
        bias = pltpu.bitcast(key_scr[pl.ds(off, tk), :], F32)
        for h, hs in enumerate(heads):
            s_scr[slot, h] = _dot_t(k_ref[pl.ds(off, tk), hs], q_ref[:, hs]) + bias
        out = []
        for h, m in enumerate(ms):
            s = s_scr[slot, h]
            top = jnp.max(_fold8([s[g, :] for g in groups], jnp.maximum), axis=0, keepdims=True)
            out.append(jnp.maximum(m, top))
        return tuple(out)

    def accumulate(j, slot, m_old, m_new):
        off = pl.multiple_of(j * tk, tk)
        for h in range(nh):
            p_scr[h] = jnp.exp2(s_scr[slot, h] - m_new[h]).astype(BF16)
        for h, hs in enumerate(heads):
            vt_ext = jnp.concatenate([vt_ref[hs, pl.ds(off, tk)], ones_v], axis=0)
            acc_scr[h] = jnp.exp2(m_old[h] - m_new[h]) * acc_scr[h] + _dot(vt_ext, p_scr[h])

    m_none = tuple(jnp.full((1, tq), NEG, F32) for _ in range(nh))
    m_first = logits(0, 0, m_none)

    def attn_step(j, st):
        m_old, m_cur = st
        slot = j % 2
        accumulate(j - 1, 1 - slot, m_old, m_cur)
        return m_cur, logits(j, slot, m_cur)

    m_old, m_cur = lax.fori_loop(1, n_kt, attn_step, (m_none, m_first))
    accumulate(n_kt - 1, (n_kt - 1) % 2, m_old, m_cur)
    for h, hs in enumerate(heads):
        out_t = acc_scr[h, :HEAD_DIM, :] / acc_scr[h, HEAD_DIM:HEAD_DIM + 1, :]
        o_ref[:, hs] = out_t.T.astype(o_ref.dtype)


def _dsat(q, k_all, vt_all, qi_cat, ki_cat, kwt, nb, seq, n_keys, n_valid, past, nh):
    tq = _tile(seq, 256, LANES)
    tk = 256
    nq = seq // tq
    width = nh * HEAD_DIM
    n_sel = min(TOPK_MAX, n_valid // 4)
    once = pl.Buffered(1)
    return pl.pallas_call(
        functools.partial(_dsat_kernel, tq=tq, tk=tk, past=past, n_valid=n_valid, n_sel=n_sel, nh=nh),
        grid=(nb, nq),
        in_specs=[pl.BlockSpec((tq, width), lambda b, i: (b * nq + i, 0)),
                  pl.BlockSpec((n_keys, width), lambda b, i: (b, 0), pipeline_mode=once),
                  pl.BlockSpec((width, n_keys), lambda b, i: (b, 0), pipeline_mode=once),
                  pl.BlockSpec((tq, H_IDX * IDX_K), lambda b, i: (b * nq + i, 0)),
                  pl.BlockSpec((n_keys, IDX_K), lambda b, i: (b, 0), pipeline_mode=once),
                  pl.BlockSpec((LANES, tq), lambda b, i: (0, b * nq + i))],
        out_specs=pl.BlockSpec((tq, width), lambda b, i: (b * nq + i, 0)),
        out_shape=jax.ShapeDtypeStruct(q.shape, BF16),
        scratch_shapes=[pltpu.VMEM((n_keys, tq), I32), pltpu.VMEM((nh, HEAD_DIM + ONES_ROWS, tq), F32),
                        pltpu.VMEM((2, nh, tk, tq), F32), pltpu.VMEM((nh, tk, tq), BF16)],
        compiler_params=_params(("arbitrary", "arbitrary")),
        name="dsa",
    )(q, k_all, vt_all, qi_cat, ki_cat, kwt)


def _merge_kernel(h_ref, oa_ref, ob_ref, wg1_ref, wg2_ref, b1_ref, b2_ref, wa_ref, wb_ref, o_ref):
    h = h_ref[...]
    g1 = 1.0 / (1.0 + jnp.exp(-(_dot(h, wg1_ref[...]) + b1_ref[...])))
    g2 = 1.0 / (1.0 + jnp.exp(-(_dot(h, wg2_ref[...]) + b2_ref[...])))
    o_ref[...] = (g1 * _dot(oa_ref[...], wa_ref[...]) + g2 * _dot(ob_ref[...], wb_ref[...])).astype(BF16)


def _merge(h_hi, o_a, o_b, w_gate, b_gate, w_pa, w_pb):
    n, d = h_hi.shape
    width = o_a.shape[1]
    tm = _tile(n, 1024, 16)
    tn = _tile(d, 512, LANES)
    nj = d // tn
    b_gate = b_gate.reshape(1, 2 * d)
    row = lambda i, j: (i, 0)
    return pl.pallas_call(
        _merge_kernel,
        grid=(n // tm, nj),
        in_specs=[pl.BlockSpec((tm, d), row), pl.BlockSpec((tm, width), row), pl.BlockSpec((tm, width), row),
                  pl.BlockSpec((d, tn), lambda i, j: (0, j)),
                  pl.BlockSpec((d, tn), lambda i, j: (0, j + nj)),
                  pl.BlockSpec((1, tn), lambda i, j: (0, j)),
                  pl.BlockSpec((1, tn), lambda i, j: (0, j + nj)),
                  pl.BlockSpec((width, tn), lambda i, j: (0, j)),
                  pl.BlockSpec((width, tn), lambda i, j: (0, j))],
        out_specs=pl.BlockSpec((tm, tn), lambda i, j: (i, j)),
        out_shape=jax.ShapeDtypeStruct((n, d), BF16),
        compiler_params=_params(("arbitrary", "arbitrary")),
        name="merge",
    )(h_hi, o_a, o_b, w_gate, w_gate, b_gate, b_gate, w_pa, w_pb)


def _outproj_kernel(m_ref, w_ref, x_ref, mod_ref, gm_ref, gf_ref, x1_ref, h2_ref):
    mod = mod_ref[0]
    x1 = x_ref[...] + mod[2:3, :] * _rms(_dot(m_ref[...], w_ref[...]), gm_ref[...])
    x1_ref[...] = x1
    h2_ref[...] = (_rms(x1, gf_ref[...]) * (1.0 + mod[4:5, :]) + mod[3:4, :]).astype(BF16)


def _outproj(merged, w_out, x2, mod, g_post_mix, g_pre_ffn, seq):
    n, d = x2.shape
    tm = _tile(seq, 256, 16)
    tps = seq // tm
    row = lambda i: (i, 0)
    fix = lambda i: (0, 0)
    return pl.pallas_call(
        _outproj_kernel,
        grid=(n // tm,),
        in_specs=[pl.BlockSpec((tm, d), row), pl.BlockSpec((d, d), fix), pl.BlockSpec((tm, d), row),
                  pl.BlockSpec((1, N_MOD, d), lambda i: (i // tps, 0, 0)),
                  pl.BlockSpec((1, d), fix), pl.BlockSpec((1, d), fix)],
        out_specs=[pl.BlockSpec((tm, d), row), pl.BlockSpec((tm, d), row)],
        out_shape=[jax.ShapeDtypeStruct((n, d), F32), jax.ShapeDtypeStruct((n, d), BF16)],
        compiler_params=_params(("arbitrary",)),
        name="outproj",
    )(merged, w_out, x2, mod, g_post_mix.reshape(1, d), g_pre_ffn.reshape(1, d))


def _gelu_tanh(x):
    return 0.5 * x * (1.0 + jnp.tanh(0.7978845608028654 * (x + 0.044715 * (x * x * x))))


def _ffn_kernel(h_ref, halo_ref, x1_ref, mod_ref, st_ref, wg_ref, wu_ref, wd_ref, wc_ref, bc_ref, gp_ref,
                y_ref, nc_ref, g_scr, act_scr, *, tm, nsub, tps):
    i = pl.program_id(0)
    c = pl.program_id(1)
    last = pl.num_programs(1) - 1
    sub = tm // nsub
    cur = c % 2

    def step(down, gate_up):
        if gate_up:
            h = h_ref[...]
            wg = wg_ref[...]
            g = _dot(h, wg)
            u = _dot(h, wu_ref[...])
            wc = wc_ref[...]
            bc = bc_ref[...]
            g_scr[HALO:HALO + tm, :] = g
        if down == "set":
            y_ref[...] = _dot(act_scr[1 - cur], wd_ref[...])
        elif down == "add":
            y_ref[...] += _dot(act_scr[1 - cur], wd_ref[...])
        if not gate_up:
            return

        def conv(lo, rows):
            return (bc + wc[0:1, :] * g_scr[lo - 2:lo - 2 + rows, :]
                    + wc[1:2, :] * g_scr[lo - 1:lo - 1 + rows, :] + wc[2:3, :] * g_scr[lo:lo + rows, :])

        if nsub == 1:
            g_halo = _dot(halo_ref[...], wg)
            g_scr[0:HALO, :] = g_halo
            g_scr[HALO - 2:HALO, :] = jnp.where(i % tps == 0, st_ref[0], g_halo[HALO - 2:HALO, :])
            nc_ref[0] = g_scr[HALO + tm - 2:HALO + tm, :]
            act_scr[cur] = (_gelu_tanh(conv(HALO, tm)) * u).astype(BF16)
        else:
            for s in range(nsub):
                lo = HALO + s * sub
                nc_ref[s] = g_scr[lo + sub - 2:lo + sub, :]
                g_scr[lo - 2:lo, :] = st_ref[s]
                act_scr[cur, s * sub:(s + 1) * sub, :] = (
                    _gelu_tanh(conv(lo, sub)) * u[s * sub:(s + 1) * sub, :]).astype(BF16)

    pl.when(c == 0)(lambda: step(None, True))
    pl.when(c == 1)(lambda: step("set", True))
    pl.when(jnp.logical_and(c > 1, c < last))(lambda: step("add", True))

    @pl.when(c == last)
    def _():
        step("add", False)
        for s in range(nsub):
            rows = slice(s * sub, (s + 1) * sub)
            y_ref[rows, :] = x1_ref[rows, :] + mod_ref[s][5:6, :] * _rms(y_ref[rows, :], gp_ref[...])


def _ffn(h2, x1, mod, state, w_gate, w_up, w_down, w_conv, b_conv, g_post, seq):
    n, d = x1.shape
    dff = w_gate.shape[1]
    nb = n // seq
    tm = _row_tile(n, seq, 512)
    tf = _tile(dff, 512, LANES)
    n_chunks = dff // tf
    assert n_chunks >= 2, "the lagged down projection needs at least two d_ff chunks"
    up = lambda c: jnp.minimum(c, n_chunks - 1)
    down = lambda c: jnp.maximum(c - 1, 0)
    per_tile_c = lambda i, c: (i, 0, up(c))
    if tm <= seq:
        nsub, tps = 1, seq // tm
        per_seq = lambda i, c: (i // tps, 0, 0)
        per_seq_c = lambda i, c: (i // tps, 0, up(c))
    else:
        nsub, tps = tm // seq, 1
        per_seq = lambda i, c: (i, 0, 0)
        per_seq_c = per_tile_c
    halo_blocks = tm // HALO
    row = lambda i, c: (i, 0)
    y, new_conv = pl.pallas_call(
        functools.partial(_ffn_kernel, tm=tm, nsub=nsub, tps=tps),
        grid=(n // tm, n_chunks + 1),
        in_specs=[pl.BlockSpec((tm, d), row),
                  pl.BlockSpec((HALO, d), lambda i, c: (jnp.maximum(i * halo_blocks - 1, 0), 0)),
                  pl.BlockSpec((tm, d), row),
                  pl.BlockSpec((nsub, N_MOD, d), per_seq),
                  pl.BlockSpec((nsub, CONV_W - 1, tf), per_seq_c),
                  pl.BlockSpec((d, tf), lambda i, c: (0, up(c))),
                  pl.BlockSpec((d, tf), lambda i, c: (0, up(c))),
                  pl.BlockSpec((tf, d), lambda i, c: (down(c), 0)),
                  pl.BlockSpec((CONV_W, tf), lambda i, c: (0, up(c))),
                  pl.BlockSpec((1, tf), lambda i, c: (0, up(c))),
                  pl.BlockSpec((1, d), lambda i, c: (0, 0))],
        out_specs=[pl.BlockSpec((tm, d), row),
                   pl.BlockSpec((nsub, CONV_W - 1, tf), per_tile_c)],
        out_shape=[jax.ShapeDtypeStruct((n, d), F32),
                   jax.ShapeDtypeStruct((nb * tps, CONV_W - 1, dff), F32)],
        scratch_shapes=[pltpu.VMEM((HALO + tm, tf), F32), pltpu.VMEM((2, tm, tf), BF16)],
        compiler_params=_params(("arbitrary", "arbitrary")),
        name="ffn",
    )(h2, h2, x1, mod, state, w_gate, w_up, w_down, w_conv, b_conv.reshape(1, dff), g_post.reshape(1, d))
    return y, new_conv.reshape(nb, tps, CONV_W - 1, dff)[:, tps - 1]


def _rope_tables(pos):
    def angles(dim):
        half = dim // 2
        inv = ROPE_THETA ** (-jnp.arange(half, dtype=F32) * 2.0 / dim)
        ang = pos.astype(F32)[:, None] * inv[None, :]
        return jnp.cos(ang), jnp.sin(ang)

    cos, sin = angles(HEAD_DIM)
    cos_h = jnp.concatenate([cos, cos], axis=1)
    sin_h = jnp.concatenate([-sin, sin], axis=1)

    cos, sin = angles(D_IDX)
    zero = jnp.zeros_like(sin)
    n_rot = H_IDX + 1
    tail = IDX_COLS - n_rot * D_IDX
    t = pos.shape[0]
    c_i = jnp.concatenate([jnp.tile(jnp.concatenate([cos, cos], 1), (1, n_rot)), jnp.ones((t, tail), F32)], 1)
    s1_i = jnp.concatenate([jnp.tile(jnp.concatenate([-sin, zero], 1), (1, n_rot)), jnp.zeros((t, tail), F32)], 1)
    s2_i = jnp.concatenate([jnp.tile(jnp.concatenate([zero, sin], 1), (1, n_rot)), jnp.zeros((t, tail), F32)], 1)
    return cos_h, sin_h, c_i, s1_i, s2_i


def _with_past(past, new, nb, seq, n_keys):
    w = new.shape[1]
    if past is None and n_keys == seq:
        return new
    parts = [] if past is None else [past.astype(new.dtype).reshape(nb, -1, w)]
    parts.append(new.reshape(nb, seq, w))
    have = sum(p.shape[1] for p in parts)
    if n_keys > have:
        parts.append(jnp.zeros((nb, n_keys - have, w), new.dtype))
    return jnp.concatenate(parts, axis=1).reshape(nb * n_keys, w)


def _layer(x, mod, past, conv_state, p):
    nb, seq, d = x.shape
    n = nb * seq
    nh = p["w_pa"].shape[0] // HEAD_DIM
    width = nh * HEAD_DIM
    n_past = 0 if past is None else past["a_k"].shape[1]
    n_valid = n_past + seq
    n_keys = -(-n_valid // 256) * 256
    x2 = x.reshape(n, d)

    h_hi, h_lo = _norm(x2, p["g_pre_mix"], mod, seq)
    cos_h, sin_h, c_i, s1_i, s2_i = _rope_tables(n_past + jnp.arange(seq, dtype=I32))
    (qa, ka32, ka16, va32, va16, qb, kb32, kb16, vb32, vb16) = _inproj(h_hi, p["w_main"], cos_h, sin_h, seq, width)
    qi_hi, qi_lo, kw, ki_hi, ki_lo = _idxproj(h_hi, h_lo, p["w_idx_hi"], p["w_idx_lo"], c_i, s1_i, s2_i, seq)

    q_hi = qi_hi.reshape(n, H_IDX, D_IDX)
    q_lo = qi_lo.reshape(n, H_IDX, D_IDX)
    qi_cat = jnp.concatenate([q_hi, q_hi, q_lo, jnp.zeros_like(q_hi)], axis=2).reshape(n, H_IDX * IDX_K)
    k_hi, k_lo = ki_hi[:, :D_IDX], ki_lo[:, :D_IDX]
    if past is None:
        pa_k = pa_v = pb_k = pb_v = pi_hi = pi_lo = None
    else:
        pa_k, pa_v, pb_k, pb_v = (past[k].reshape(nb, n_past, width) for k in ("a_k", "a_v", "b_k", "b_v"))
        pi_hi, pi_lo = _split(past["b_kidx"].astype(F32))
    k_hi = _with_past(pi_hi, k_hi, nb, seq, n_keys)
    k_lo = _with_past(pi_lo, k_lo, nb, seq, n_keys)
    ki_cat = jnp.concatenate([k_hi, k_lo, k_hi, jnp.zeros_like(k_hi)], axis=1)

    o_a = _sb(qa, _with_past(pa_k, ka16, nb, seq, n_keys), _with_past(pa_v, va16, nb, seq, n_keys),
              nb, seq, n_keys, n_past, nh)
    seq_q = max(seq, LANES)
    pad_q = lambda a: a if seq_q == seq else jnp.pad(
        a.reshape(nb, seq, -1), ((0, 0), (0, seq_q - seq), (0, 0))).reshape(nb * seq_q, -1)
    vt_all = _with_past(pb_v, vb16, nb, seq, n_keys).reshape(nb, n_keys, width)
    vt_all = jnp.swapaxes(vt_all, 1, 2).reshape(nb * width, n_keys)
    o_b = _dsat(pad_q(qb), _with_past(pb_k, kb16, nb, seq, n_keys), vt_all, pad_q(qi_cat), ki_cat,
                pad_q(kw).T, nb, seq_q, n_keys, n_valid, n_past, nh)
    if seq_q != seq:
        o_b = o_b.reshape(nb, seq_q, width)[:, :seq].reshape(n, width)

    merged = _merge(h_hi, o_a, o_b, p["w_gate"], p["b_gate"], p["w_pa"], p["w_pb"])
    x1, h2 = _outproj(merged, p["w_out"], x2, mod, p["g_post_mix"], p["g_pre_ffn"], seq)
    y, new_conv = _ffn(h2, x1, mod, conv_state, p["w_ffn_gate"], p["w_ffn_up"], p["w_ffn_down"],
                       p["w_conv"], p["b_conv"], p["g_post_ffn"], seq)

    heads = lambda a: a.reshape(nb, seq, nh, HEAD_DIM)
    new = (heads(ka32), heads(va32), heads(kb32), heads(vb32), kw[:, :D_IDX].reshape(nb, seq, D_IDX), new_conv)
    return y.reshape(nb, seq, d), new


def kernel(x_prompt, x_sample, c_prompt, c_sample, cache_a_k, cache_a_v, cache_b_k, cache_b_v, cache_b_kidx,
           state_ffn_conv, w_mod, b_mod, g_pre_mix, w_in, w_merge_gate, b_merge_gate, w_proj_a, w_proj_b, w_out,
           g_post_mix, g_pre_ffn, w_ffn_gate, w_ffn_up, w_conv, b_conv, w_ffn_down, g_post_ffn):
    depth = w_mod.shape[0]
    nbp, nbs = x_prompt.shape[0], x_sample.shape[0]
    d = x_prompt.shape[2]
    width = w_proj_a.shape[1]
    dff = w_ffn_gate.shape[2]
    assert w_in.shape[2] == 6 * width + H_IDX * D_IDX + D_IDX + H_IDX
    assert cache_b_kidx.shape[-1] == D_IDX and cache_a_k.shape[-1] == HEAD_DIM

    y_p, y_s = x_prompt, x_sample
    st_p = [[] for _ in range(6)]
    st_s = [[] for _ in range(6)]
    for l in range(depth):
        w_idx = jnp.pad(w_in[l][:, 6 * width:], ((0, 0), (0, IDX_COLS - (w_in.shape[2] - 6 * width))))
        w_idx_hi, w_idx_lo = _split(w_idx)
        p = dict(
            g_pre_mix=g_pre_mix[l], w_main=w_in[l][:, :6 * width].astype(BF16), w_idx_hi=w_idx_hi, w_idx_lo=w_idx_lo,
            w_gate=w_merge_gate[l].astype(BF16), b_gate=b_merge_gate[l],
            w_pa=w_proj_a[l].astype(BF16), w_pb=w_proj_b[l].astype(BF16), w_out=w_out[l].astype(BF16),
            g_post_mix=g_post_mix[l], g_pre_ffn=g_pre_ffn[l],
            w_ffn_gate=w_ffn_gate[l].astype(BF16), w_ffn_up=w_ffn_up[l].astype(BF16),
            w_ffn_down=w_ffn_down[l].astype(BF16), w_conv=w_conv[l], b_conv=b_conv[l], g_post_ffn=g_post_ffn[l])

        c_all = jnp.concatenate([c_prompt, c_sample], axis=0)
        pad = -c_all.shape[0] % 16
        mod = _mod(jnp.pad(c_all, ((0, pad), (0, 0))), w_mod[l], b_mod[l]).reshape(-1, N_MOD, d)
        past = dict(a_k=cache_a_k[l], a_v=cache_a_v[l], b_k=cache_b_k[l], b_v=cache_b_v[l], b_kidx=cache_b_kidx[l])
        y_p, new_p = _layer(y_p, mod[:nbp], None, jnp.zeros((nbp, CONV_W - 1, dff), F32), p)
        y_s, new_s = _layer(y_s, mod[nbp:nbp + nbs], past, state_ffn_conv[l], p)
        for i in range(6):
            st_p[i].append(new_p[i])
            st_s[i].append(new_s[i])
    sp = [jnp.stack(s, axis=0) for s in st_p]
    ss = [jnp.stack(s, axis=0) for s in st_s]
    return (y_p, y_s, sp[0], sp[1], sp[2], sp[3], sp[4], sp[5], ss[0], ss[1], ss[2], ss[3], ss[4], ss[5])
```

```python
import functools

import jax
import jax.numpy as jnp
from jax import lax
from jax.experimental import pallas as pl
from jax.experimental.pallas import tpu as pltpu

F32 = jnp.float32
BF16 = jnp.bfloat16
I32 = jnp.int32

CHUNK = 64
HEAD_DIM = 128
H_IDX = 4
D_IDX = 64
TOPK_MAX = 256
CONV_W = 3
ROPE_THETA = 10000.0
RMS_EPS = 1e-6
N_MOD = 6
IDX_COLS = 384
IDX_K = 256
LANES = 128
HALO = 16
ONES_ROWS = 16
LOG2E = 1.4426950408889634
UNDERFLOW_LOG2 = 160.0
NEG = -1e30
INT_MIN = -2 ** 31
VMEM_LIMIT = 56 * 1024 * 1024


def _tile(n, pref, mult):
    t = min(pref, n)
    t -= t % mult
    while t >= mult:
        if n % t == 0:
            return t
        t -= mult
    return n


def _row_tile(n, seq, pref):
    if seq >= pref or n == seq:
        return _tile(seq, pref, 16)
    return _tile(n, pref, seq)


def _params(sem):
    return pltpu.CompilerParams(dimension_semantics=sem, vmem_limit_bytes=VMEM_LIMIT)


def _split(x):
    hi = x.astype(BF16)
    lo = (x - hi.astype(F32)).astype(BF16)
    return hi, lo


def _dot(a, b):
    return jnp.dot(a, b, preferred_element_type=F32)


def _dot_t(a, b):
    return lax.dot_general(a, b, (((1,), (1,)), ((), ())), preferred_element_type=F32)


def _rms(x, g):
    ms = jnp.mean(x * x, axis=-1, keepdims=True)
    return x * lax.rsqrt(ms + RMS_EPS) * g


def _mod_kernel(c_ref, w_ref, b_ref, o_ref):
    c = c_ref[...]
    s = c / (1.0 + jnp.exp(-c))
    s_hi, s_lo = _split(s)
    w_hi, w_lo = _split(w_ref[...])
    o_ref[...] = _dot(s_hi, w_hi) + _dot(s_hi, w_lo) + _dot(s_lo, w_hi) + b_ref[...]


def _mod(c, w, b):
    bc, d = c.shape
    n = w.shape[1]
    tn = _tile(n, 1024, LANES)
    return pl.pallas_call(
        _mod_kernel,
        grid=(n // tn,),
        in_specs=[pl.BlockSpec((bc, d), lambda j: (0, 0)),
                  pl.BlockSpec((d, tn), lambda j: (0, j)),
                  pl.BlockSpec((1, tn), lambda j: (0, j))],
        out_specs=pl.BlockSpec((bc, tn), lambda j: (0, j)),
        out_shape=jax.ShapeDtypeStruct((bc, n), F32),
        compiler_params=_params(("arbitrary",)),
        name="mod",
    )(c, w, b.reshape(1, n))


def _norm_kernel(x_ref, g_ref, mod_ref, hi_ref, lo_ref):
    m = mod_ref[0]
    h = _rms(x_ref[...], g_ref[...]) * (1.0 + m[1:2, :]) + m[0:1, :]
    hi, lo = _split(h)
    hi_ref[...] = hi
    lo_ref[...] = lo


def _norm(x2, g, mod, seq):
    n, d = x2.shape
    tr = _tile(seq, 256, 16)
    tps = seq // tr
    return pl.pallas_call(
        _norm_kernel,
        grid=(n // tr,),
        in_specs=[pl.BlockSpec((tr, d), lambda i: (i, 0)),
                  pl.BlockSpec((1, d), lambda i: (0, 0)),
                  pl.BlockSpec((1, N_MOD, d), lambda i: (i // tps, 0, 0))],
        out_specs=[pl.BlockSpec((tr, d), lambda i: (i, 0))] * 2,
        out_shape=[jax.ShapeDtypeStruct((n, d), BF16)] * 2,
        compiler_params=_params(("arbitrary",)),
        name="norm",
    )(x2, g.reshape(1, d), mod)


def _inproj_kernel(a_ref, w_ref, cos_ref, sin_ref,
                   qa_ref, ka32_ref, ka16_ref, va32_ref, va16_ref,
                   qb_ref, kb32_ref, kb16_ref, vb32_ref, vb16_ref, *, npj, tn, scale):
    grp = pl.program_id(1) // npj
    acc = _dot(a_ref[...], w_ref[...])

    def rope(x):
        parts = []
        for s in range(tn // HEAD_DIM):
            xs = x[:, s * HEAD_DIM:(s + 1) * HEAD_DIM]
            parts.append(xs * cos_ref[...] + pltpu.roll(xs, HEAD_DIM // 2, 1) * sin_ref[...])
        return parts[0] if len(parts) == 1 else jnp.concatenate(parts, axis=1)

    @pl.when(grp == 0)
    def _():
        qa_ref[...] = (acc * scale).astype(BF16)

    @pl.when(grp == 1)
    def _():
        ka32_ref[...] = acc
        ka16_ref[...] = acc.astype(BF16)

    @pl.when(grp == 2)
    def _():
        va32_ref[...] = acc
        va16_ref[...] = acc.astype(BF16)

    @pl.when(grp == 3)
    def _():
        qb_ref[...] = (rope(acc) * scale).astype(BF16)

    @pl.when(grp == 4)
    def _():
        r = rope(acc)
        kb32_ref[...] = r
        kb16_ref[...] = r.astype(BF16)

    @pl.when(grp == 5)
    def _():
        vb32_ref[...] = acc
        vb16_ref[...] = acc.astype(BF16)


def _inproj(h_hi, w_main, cos_t, sin_t, seq, width):
    n, d = h_hi.shape
    tm = _row_tile(n, seq, 1024)
    tn = _tile(width, 512, LANES)
    npj = width // tn
    if tm <= seq:
        tps = seq // tm
        tab_map = lambda i, j: (i % tps, 0)
    else:
        cos_t = jnp.tile(cos_t, (tm // seq, 1))
        sin_t = jnp.tile(sin_t, (tm // seq, 1))
        tab_map = lambda i, j: (0, 0)

    def omap(g):
        return lambda i, j: (i, jnp.clip(j - g * npj, 0, npj - 1))

    dts = [BF16, F32, BF16, F32, BF16, BF16, F32, BF16, F32, BF16]
    grps = [0, 1, 1, 2, 2, 3, 4, 4, 5, 5]
    return pl.pallas_call(
        functools.partial(_inproj_kernel, npj=npj, tn=tn, scale=LOG2E * HEAD_DIM ** -0.5),
        grid=(n // tm, 6 * npj),
        in_specs=[pl.BlockSpec((tm, d), lambda i, j: (i, 0)),
                  pl.BlockSpec((d, tn), lambda i, j: (0, j)),
                  pl.BlockSpec((tm, HEAD_DIM), tab_map),
                  pl.BlockSpec((tm, HEAD_DIM), tab_map)],
        out_specs=[pl.BlockSpec((tm, tn), omap(g)) for g in grps],
        out_shape=[jax.ShapeDtypeStruct((n, width), dt) for dt in dts],
        compiler_params=_params(("arbitrary", "arbitrary")),
        name="inproj",
    )(h_hi, w_main, cos_t, sin_t)


def _idxproj_kernel(hi_ref, lo_ref, whi_ref, wlo_ref, c_ref, s1_ref, s2_ref,
                    qhi_ref, qlo_ref, kw_ref, khi_ref, klo_ref):
    a_hi = hi_ref[...]
    w_hi = whi_ref[...]
    acc = _dot(a_hi, w_hi) + _dot(a_hi, wlo_ref[...]) + _dot(lo_ref[...], w_hi)
    for s in range(IDX_COLS // LANES):
        sl = slice(s * LANES, (s + 1) * LANES)
        xs = acc[:, sl]
        r = (xs * c_ref[:, sl] + pltpu.roll(xs, LANES - D_IDX // 2, 1) * s1_ref[:, sl]
             + pltpu.roll(xs, D_IDX // 2, 1) * s2_ref[:, sl])
        hi, lo = _split(r)
        if s < 2:
            qhi_ref[:, sl] = hi
            qlo_ref[:, sl] = lo
        else:
            kw_ref[...] = r
            khi_ref[...] = hi
            klo_ref[...] = lo


def _idxproj(h_hi, h_lo, w_hi, w_lo, c_t, s1_t, s2_t, seq):
    n, d = h_hi.shape
    tm = _row_tile(n, seq, 512)
    if tm <= seq:
        tps = seq // tm
        tab_map = lambda i: (i % tps, 0)
    else:
        c_t, s1_t, s2_t = (jnp.tile(t, (tm // seq, 1)) for t in (c_t, s1_t, s2_t))
        tab_map = lambda i: (0, 0)
    row = lambda i: (i, 0)
    return pl.pallas_call(
        _idxproj_kernel,
        grid=(n // tm,),
        in_specs=[pl.BlockSpec((tm, d), row), pl.BlockSpec((tm, d), row),
                  pl.BlockSpec((d, IDX_COLS), lambda i: (0, 0)),
                  pl.BlockSpec((d, IDX_COLS), lambda i: (0, 0)),
                  pl.BlockSpec((tm, IDX_COLS), tab_map),
                  pl.BlockSpec((tm, IDX_COLS), tab_map),
                  pl.BlockSpec((tm, IDX_COLS), tab_map)],
        out_specs=[pl.BlockSpec((tm, 2 * LANES), row), pl.BlockSpec((tm, 2 * LANES), row),
                   pl.BlockSpec((tm, LANES), row), pl.BlockSpec((tm, LANES), row),
                   pl.BlockSpec((tm, LANES), row)],
        out_shape=[jax.ShapeDtypeStruct((n, 2 * LANES), BF16), jax.ShapeDtypeStruct((n, 2 * LANES), BF16),
                   jax.ShapeDtypeStruct((n, LANES), F32), jax.ShapeDtypeStruct((n, LANES), BF16),
                   jax.ShapeDtypeStruct((n, LANES), BF16)],
        compiler_params=_params(("arbitrary",)),
        name="idxproj",
    )(h_hi, h_lo, w_hi, w_lo, c_t, s1_t, s2_t)


def _sb_kernel(q_ref, k_ref, v_ref, o_ref, acc_scr, z_scr, i_scr, *, tq, tk, past, group):
    q0 = past + pl.program_id(2) * tq
    n_kt = (q0 + tq - 2) // tk + 1
    n_clear = q0 // tk
    row_pos = q0 + lax.broadcasted_iota(I32, (tq, 1), 0)
    col = lax.broadcasted_iota(I32, (1, tk), 1)
    suffix = jnp.where(lax.broadcasted_iota(I32, (tk, tk), 0) >= lax.broadcasted_iota(I32, (tk, tk), 1),
                       1.0, 0.0).astype(BF16)
    suffix2 = jnp.concatenate([suffix, suffix], axis=0)
    acc_scr[...] = jnp.zeros_like(acc_scr)

    heads = [slice(h * HEAD_DIM, (h + 1) * HEAD_DIM) for h in range(group)]

    def tiles(js, carries, masked):
        offs = [pl.multiple_of(j * tk, tk) for j in js]
        if masked:
            valids = [(off + col) < row_pos for off in offs]
        for t, off in enumerate(offs):
            for h, hs in enumerate(heads):
                z_scr[t * group + h] = _dot_t(q_ref[:, hs], k_ref[pl.ds(off, tk), hs])
        for t in range(len(js)):
            for h in range(group):
                z = z_scr[t * group + h]
                sp = jnp.maximum(z, 0.0) + jnp.log2(1.0 + jnp.exp2(-jnp.abs(z)))
                if masked:
                    sp = jnp.where(valids[t], sp, 0.0)
                hi, lo = _split(sp)
                i_scr[t * group + h] = _dot(jnp.concatenate([hi, lo], axis=1), suffix2)
        carries = list(carries)
        for t, off in enumerate(offs):
            for h, hs in enumerate(heads):
                w = jnp.exp2(z_scr[t * group + h] - i_scr[t * group + h] - carries[h])
                if masked:
                    w = jnp.where(valids[t], w, 0.0)
                acc_scr[h] += _dot(w.astype(BF16), v_ref[pl.ds(off, tk), hs])
                carries[h] = carries[h] + i_scr[t * group + h, :, 0:1]
        return tuple(carries)

    def least(carries):
        m = carries[0]
        for c in carries[1:]:
            m = jnp.minimum(m, c)
        return jnp.min(m)

    st = tuple(jnp.zeros((tq, 1), F32) for _ in range(group))
    st = lax.fori_loop(0, n_kt - n_clear, lambda t, c: tiles([n_kt - 1 - t], c, True), st)

    def more(state):
        t, low, _ = state
        return jnp.logical_and(t < n_clear, low < UNDERFLOW_LOG2)

    def step(state):
        t, _, carries = state
        carries = tiles([n_clear - 1 - t], carries, False)
        return t + 1, least(carries), carries

    lax.while_loop(more, step, (jnp.int32(0), least(st), st))
    for h in range(group):
        o_ref[:, h * HEAD_DIM:(h + 1) * HEAD_DIM] = acc_scr[h].astype(o_ref.dtype)


def _sb(q, k_all, v_all, nb, seq, n_keys, past, nh):
    tq = _tile(seq, 256, 16)
    tk = 256
    nq = seq // tq
    group = 4 if nh % 4 == 0 else (2 if nh % 2 == 0 else 1)
    gw = group * HEAD_DIM
    return pl.pallas_call(
        functools.partial(_sb_kernel, tq=tq, tk=tk, past=past, group=group),
        grid=(nb, nh // group, nq),
        in_specs=[pl.BlockSpec((tq, gw), lambda b, h, i: (b * nq + i, h)),
                  pl.BlockSpec((n_keys, gw), lambda b, h, i: (b, h)),
                  pl.BlockSpec((n_keys, gw), lambda b, h, i: (b, h))],
        out_specs=pl.BlockSpec((tq, gw), lambda b, h, i: (b * nq + i, h)),
        out_shape=jax.ShapeDtypeStruct(q.shape, BF16),
        scratch_shapes=[pltpu.VMEM((group, tq, HEAD_DIM), F32), pltpu.VMEM((group, tq, tk), F32),
                        pltpu.VMEM((group, tq, tk), F32)],
        compiler_params=_params(("arbitrary", "arbitrary", "arbitrary")),
        name="sb",
    )(q, k_all, v_all)


def _dsa_kernel(q_ref, k_ref, v_ref, qi_ref, ki_ref, kw_ref, o_ref, key_scr, acc_scr, s_scr, p_scr,
                *, tq, tk, past, n_valid, n_sel, nh):
    q0 = past + pl.program_id(1) * tq
    row_pos = q0 + lax.broadcasted_iota(I32, (tq, 1), 0)
    row_lim = jnp.minimum((row_pos // CHUNK + 1) * CHUNK, n_valid)
    n_kt = (jnp.minimum(((q0 + tq - 1) // CHUNK + 1) * CHUNK, n_valid) + tk - 1) // tk
    col = lax.broadcasted_iota(I32, (1, tk), 1)
    ksel = float(n_sel)

    wrow = kw_ref[:, D_IDX:D_IDX + H_IDX] * ((H_IDX ** -0.5) * (D_IDX ** -0.5))

    def score_tile(j, _):
        off = pl.multiple_of(j * tk, tk)
        kc = ki_ref[pl.ds(off, tk), :]
        score = jnp.zeros((tq, tk), F32)
        for h in range(H_IDX):
            s = _dot_t(qi_ref[:, h * IDX_K:(h + 1) * IDX_K], kc)
            score = score + wrow[:, h:h + 1] * jnp.maximum(s, 0.0)
        bits = pltpu.bitcast(score, I32)
        key = jnp.where(bits < 0, INT_MIN - bits, bits)
        key_scr[:, pl.ds(off, tk)] = jnp.where((off + col) < row_lim, key, INT_MIN)
        return 0

    lax.fori_loop(0, n_kt, score_tile, 0)

    rb = min(tq, 128)

    def count_ge(thr_b):
        outs = []
        for r in range(tq // rb):
            rows = slice(r * rb, (r + 1) * rb)
            thr_r = thr_b[rows]

            def body(j, cnt):
                off = pl.multiple_of(j * tk, tk)
                key = key_scr[rows, pl.ds(off, tk)]
                for s in range(tk // LANES):
                    cnt = cnt + jnp.where(key[:, s * LANES:(s + 1) * LANES] >= thr_r, 1.0, 0.0)
                return cnt

            cnt = lax.fori_loop(0, n_kt, body, jnp.zeros((rb, LANES), F32))
            outs.append(jnp.broadcast_to(jnp.sum(cnt, axis=1, keepdims=True), (rb, LANES)))
        return outs[0] if len(outs) == 1 else jnp.concatenate(outs, axis=0)

    def bit_step(it, cand):
        trial = cand | lax.shift_left(jnp.int32(1), 30 - it)
        return jnp.where(count_ge(trial) >= ksel, trial, cand)

    zero_b = jnp.zeros((tq, LANES), I32)
    thr_b = lax.fori_loop(0, 31, bit_step, jnp.where(count_ge(zero_b) >= ksel, zero_b, INT_MIN))
    no_thr_b = thr_b == INT_MIN
    cnt_gt = count_ge(jnp.where(no_thr_b, thr_b, thr_b + 1))[:, 0:1]
    thr = thr_b[:, 0:1]
    no_thr = thr == INT_MIN
    need = jnp.where(no_thr, 0.0, ksel - cnt_gt)

    prefix = jnp.where(lax.broadcasted_iota(I32, (tk, tk), 0) <= lax.broadcasted_iota(I32, (tk, tk), 1),
                       1.0, 0.0).astype(BF16)

    def bias_tile(j, carry):
        off = pl.multiple_of(j * tk, tk)
        key = key_scr[:, pl.ds(off, tk)]
        eq = key == thr
        eqf = jnp.where(eq, 1.0, 0.0)
        rank_incl = _dot(eqf.astype(BF16), prefix) + carry
        tie_ok = (rank_incl - eqf) < need
        bias = jnp.where(key > thr, 0.0, jnp.where(eq, jnp.where(tie_ok, 0.0, NEG), NEG))
        key_scr[:, pl.ds(off, tk)] = pltpu.bitcast(bias, I32)
        return rank_incl[:, tk - 1:tk]

    lax.fori_loop(0, n_kt, bias_tile, jnp.zeros((tq, 1), F32))

    acc_scr[...] = jnp.zeros_like(acc_scr)
    ones_v = jnp.ones((tk, HEAD_DIM), BF16)
    heads = [slice(h * HEAD_DIM, (h + 1) * HEAD_DIM) for h in range(nh)]

    def attn_tile(j, ms):
        off = pl.multiple_of(j * tk, tk)
        bias = pltpu.bitcast(key_scr[:, pl.ds(off, tk)], F32)
        for h, hs in enumerate(heads):
            s_scr[h] = _dot_t(q_ref[:, hs], k_ref[pl.ds(off, tk), hs]) + bias
        new_m = [jnp.maximum(m, jnp.max(s_scr[h], axis=1, keepdims=True)) for h, m in enumerate(ms)]
        for h in range(nh):
            p_scr[h] = jnp.exp2(s_scr[h] - new_m[h]).astype(BF16)
        for h, hs in enumerate(heads):
            v_ext = jnp.concatenate([v_ref[pl.ds(off, tk), hs], ones_v], axis=1)
            acc_scr[h] = jnp.exp2(ms[h] - new_m[h]) * acc_scr[h] + _dot(p_scr[h], v_ext)
        return tuple(new_m)

    lax.fori_loop(0, n_kt, attn_tile, tuple(jnp.full((tq, 1), NEG, F32) for _ in range(nh)))
    for h, hs in enumerate(heads):
        o_ref[:, hs] = (acc_scr[h, :, :HEAD_DIM] / acc_scr[h, :, HEAD_DIM:]).astype(o_ref.dtype)


def _dsa(q, k_all, v_all, qi_cat, ki_cat, kw, nb, seq, n_keys, n_valid, past, nh):
    tq = _tile(seq, 256, 16)
    tk = 256
    nq = seq // tq
    width = nh * HEAD_DIM
    n_sel = min(TOPK_MAX, n_valid // 4)
    once = pl.Buffered(1)
    return pl.pallas_call(
        functools.partial(_dsa_kernel, tq=tq, tk=tk, past=past, n_valid=n_valid, n_sel=n_sel, nh=nh),
        grid=(nb, nq),
        in_specs=[pl.BlockSpec((tq, width), lambda b, i: (b * nq + i, 0)),
                  pl.BlockSpec((n_keys, width), lambda b, i: (b, 0), pipeline_mode=once),
                  pl.BlockSpec((n_keys, width), lambda b, i: (b, 0), pipeline_mode=once),
                  pl.BlockSpec((tq, H_IDX * IDX_K), lambda b, i: (b * nq + i, 0)),
                  pl.BlockSpec((n_keys, IDX_K), lambda b, i: (b, 0), pipeline_mode=once),
                  pl.BlockSpec((tq, LANES), lambda b, i: (b * nq + i, 0))],
        out_specs=pl.BlockSpec((tq, width), lambda b, i: (b * nq + i, 0)),
        out_shape=jax.ShapeDtypeStruct(q.shape, BF16),
        scratch_shapes=[pltpu.VMEM((tq, n_keys), I32), pltpu.VMEM((nh, tq, 2 * HEAD_DIM), F32),
                        pltpu.VMEM((nh, tq, tk), F32), pltpu.VMEM((nh, tq, tk), BF16)],
        compiler_params=_params(("arbitrary", "arbitrary")),
        name="dsa",
    )(q, k_all, v_all, qi_cat, ki_cat, kw)


def _fold8(parts, op):
    while len(parts) > 1:
        parts = [op(parts[i], parts[i + 1]) if i + 1 < len(parts) else parts[i] for i in range(0, len(parts), 2)]
    return parts[0]


def _dsat_kernel(q_ref, k_ref, vt_ref, qi_ref, ki_ref, kwt_ref, o_ref, key_scr, acc_scr, s_scr, p_scr,
                 *, tq, tk, past, n_valid, n_sel, nh):
    q0 = past + pl.program_id(1) * tq
    col_pos = q0 + lax.broadcasted_iota(I32, (1, tq), 1)
    col_lim = jnp.minimum((col_pos // CHUNK + 1) * CHUNK, n_valid)
    n_kt = (jnp.minimum(((q0 + tq - 1) // CHUNK + 1) * CHUNK, n_valid) + tk - 1) // tk
    row = lax.broadcasted_iota(I32, (tk, 1), 0)
    ksel = float(n_sel)
    groups = [slice(8 * r, 8 * r + 8) for r in range(tk // 8)]

    wrow = kwt_ref[D_IDX:D_IDX + H_IDX, :] * ((H_IDX ** -0.5) * (D_IDX ** -0.5))

    def score_pair(t, _):
        offs = [pl.multiple_of(jnp.minimum(2 * t + u, n_kt - 1) * tk, tk) for u in range(2)]
        raw = [[_dot_t(ki_ref[pl.ds(off, tk), :], qi_ref[:, h * IDX_K:(h + 1) * IDX_K]) for h in range(H_IDX)]
               for off in offs]
        for off, dots in zip(offs, raw):
            score = jnp.zeros((tk, tq), F32)
            for h, s in enumerate(dots):
                score = score + wrow[h:h + 1, :] * jnp.maximum(s, 0.0)
            bits = pltpu.bitcast(score, I32)
            key = jnp.where(bits < 0, INT_MIN - bits, bits)
            key_scr[pl.ds(off, tk), :] = jnp.where((off + row) < col_lim, key, INT_MIN)
        return 0

    lax.fori_loop(0, (n_kt + 1) // 2, score_pair, 0)

    def count_ge(thr8):
        def body(t, cnt):
            for u in range(2):
                j = 2 * t + u
                off = pl.multiple_of(jnp.minimum(j, n_kt - 1) * tk, tk)
                key = key_scr[pl.ds(off, tk), :]
                part = _fold8([jnp.where(key[g, :] >= thr8, 1.0, 0.0) for g in groups], jnp.add)
                cnt = cnt + (part if u == 0 else jnp.where(j < n_kt, part, 0.0))
            return cnt

        cnt = lax.fori_loop(0, (n_kt + 1) // 2, body, jnp.zeros((8, tq), F32))
        return jnp.broadcast_to(jnp.sum(cnt, axis=0, keepdims=True), (8, tq))

    def bit_step(it, st):
        cand, have = st
        trial = cand | lax.shift_left(jnp.int32(1), 30 - it)
        cnt = count_ge(trial)
        take = cnt >= ksel
        return jnp.where(take, trial, cand), jnp.where(take, cnt, have)

    zero8 = jnp.zeros((8, tq), I32)
    cnt0 = count_ge(zero8)
    start = (jnp.where(cnt0 >= ksel, zero8, INT_MIN), jnp.where(cnt0 >= ksel, cnt0, float(2 ** 30)))
    thr8, have8 = lax.fori_loop(0, 31, bit_step, start)
    thr = thr8[0:1, :]
    overfull = jnp.max(jnp.where(have8 > ksel, 1.0, 0.0)) > 0.0

    @pl.when(jnp.logical_not(overfull))
    def _():
        def bias_tile(j, _):
            off = pl.multiple_of(j * tk, tk)
            bias = jnp.where(key_scr[pl.ds(off, tk), :] >= thr, 0.0, NEG)
            key_scr[pl.ds(off, tk), :] = pltpu.bitcast(bias, I32)
            return 0

        lax.fori_loop(0, n_kt, bias_tile, 0)

    @pl.when(overfull)
    def _():
        no_thr8 = thr8 == INT_MIN
        cnt_gt = count_ge(jnp.where(no_thr8, thr8, thr8 + 1))
        need = jnp.where(no_thr8, 0.0, ksel - cnt_gt)[0:1, :]
        prefix = jnp.where(lax.broadcasted_iota(I32, (tk, tk), 0) >= lax.broadcasted_iota(I32, (tk, tk), 1),
                           1.0, 0.0).astype(BF16)

        def bias_tile(j, carry):
            off = pl.multiple_of(j * tk, tk)
            key = key_scr[pl.ds(off, tk), :]
            eq = key == thr
            eqf = jnp.where(eq, 1.0, 0.0)
            rank_incl = _dot(prefix, eqf.astype(BF16)) + carry
            tie_ok = (rank_incl - eqf) < need
            bias = jnp.where(key > thr, 0.0, jnp.where(eq, jnp.where(tie_ok, 0.0, NEG), NEG))
            key_scr[pl.ds(off, tk), :] = pltpu.bitcast(bias, I32)
            return rank_incl[tk - 1:tk, :]

        lax.fori_loop(0, n_kt, bias_tile, jnp.zeros((1, tq), F32))

    acc_scr[...] = jnp.zeros_like(acc_scr)
    ones_v = jnp.ones((ONES_ROWS, tk), BF16)
    heads = [slice(h * HEAD_DIM, (h + 1) * HEAD_DIM) for h in range(nh)]

    def logits(j, slot, ms):
        off = pl.multiple_of(j * tk, tk)
        bias = pltpu.bitcast(key_scr[pl.ds(off, tk), :], F32)
        for h, hs in enumerate(heads):
            s_scr[slot, h] = _dot_t(k_ref[pl.ds(off, tk), hs], q_ref[:, hs]) + bias
        out = []
        for h, m in enumerate(ms):
            s = s_scr[slot, h]
            top = jnp.max(_fold8([s[g, :] for g in groups], jnp.maximum), axis=0, keepdims=True)
            out.append(jnp.maximum(m, top))
        return tuple(out)

    def accumulate(j, slot, m_old, m_new):
        off = pl.multiple_of(j * tk, tk)
        for h in range(nh):
            p_scr[h] = jnp.exp2(s_scr[slot, h] - m_new[h]).astype(BF16)
        for h, hs in enumerate(heads):
            vt_ext = jnp.concatenate([vt_ref[hs, pl.ds(off, tk)], ones_v], axis=0)
            acc_scr[h] = jnp.exp2(m_old[h] - m_new[h]) * acc_scr[h] + _dot(vt_ext, p_scr[h])

    m_none = tuple(jnp.full((1, tq), NEG, F32) for _ in range(nh))
    m_first = logits(0, 0, m_none)

    def attn_step(j, st):
        m_old, m_cur = st
        slot = j % 2
        accumulate(j - 1, 1 - slot, m_old, m_cur)
        return m_cur, logits(j, slot, m_cur)

    m_old, m_cur = lax.fori_loop(1, n_kt, attn_step, (m_none, m_first))
    accumulate(n_kt - 1, (n_kt - 1) % 2, m_old, m_cur)
    for h, hs in enumerate(heads):
        out_t = acc_scr[h, :HEAD_DIM, :] / acc_scr[h, HEAD_DIM:HEAD_DIM + 1, :]
        o_ref[:, hs] = out_t.T.astype(o_ref.dtype)


def _dsat(q, k_all, vt_all, qi_cat, ki_cat, kwt, nb, seq, n_keys, n_valid, past, nh):
    tq = _tile(seq, 256, LANES)
    tk = 256
    nq = seq // tq
    width = nh * HEAD_DIM
    n_sel = min(TOPK_MAX, n_valid // 4)
    once = pl.Buffered(1)
    return pl.pallas_call(
        functools.partial(_dsat_kernel, tq=tq, tk=tk, past=past, n_valid=n_valid, n_sel=n_sel, nh=nh),
        grid=(nb, nq),
        in_specs=[pl.BlockSpec((tq, width), lambda b, i: (b * nq + i, 0)),
                  pl.BlockSpec((n_keys, width), lambda b, i: (b, 0), pipeline_mode=once),
                  pl.BlockSpec((width, n_keys), lambda b, i: (b, 0), pipeline_mode=once),
                  pl.BlockSpec((tq, H_IDX * IDX_K), lambda b, i: (b * nq + i, 0)),
                  pl.BlockSpec((n_keys, IDX_K), lambda b, i: (b, 0), pipeline_mode=once),
                  pl.BlockSpec((LANES, tq), lambda b, i: (0, b * nq + i))],
        out_specs=pl.BlockSpec((tq, width), lambda b, i: (b * nq + i, 0)),
        out_shape=jax.ShapeDtypeStruct(q.shape, BF16),
        scratch_shapes=[pltpu.VMEM((n_keys, tq), I32), pltpu.VMEM((nh, HEAD_DIM + ONES_ROWS, tq), F32),
                        pltpu.VMEM((2, nh, tk, tq), F32), pltpu.VMEM((nh, tk, tq), BF16)],
        compiler_params=_params(("arbitrary", "arbitrary")),
        name="dsa",
    )(q, k_all, vt_all, qi_cat, ki_cat, kwt)


def _merge_kernel(h_ref, oa_ref, ob_ref, wg1_ref, wg2_ref, b1_ref, b2_ref, wa_ref, wb_ref, o_ref):
    h = h_ref[...]
    g1 = 1.0 / (1.0 + jnp.exp(-(_dot(h, wg1_ref[...]) + b1_ref[...])))
    g2 = 1.0 / (1.0 + jnp.exp(-(_dot(h, wg2_ref[...]) + b2_ref[...])))
    o_ref[...] = (g1 * _dot(oa_ref[...], wa_ref[...]) + g2 * _dot(ob_ref[...], wb_ref[...])).astype(BF16)


def _merge(h_hi, o_a, o_b, w_gate, b_gate, w_pa, w_pb):
    n, d = h_hi.shape
    width = o_a.shape[1]
    tm = _tile(n, 1024, 16)
    tn = _tile(d, 512, LANES)
    nj = d // tn
    b_gate = b_gate.reshape(1, 2 * d)
    row = lambda i, j: (i, 0)
    return pl.pallas_call(
        _merge_kernel,
        grid=(n // tm, nj),
        in_specs=[pl.BlockSpec((tm, d), row), pl.BlockSpec((tm, width), row), pl.BlockSpec((tm, width), row),
                  pl.BlockSpec((d, tn), lambda i, j: (0, j)),
                  pl.BlockSpec((d, tn), lambda i, j: (0, j + nj)),
                  pl.BlockSpec((1, tn), lambda i, j: (0, j)),
                  pl.BlockSpec((1, tn), lambda i, j: (0, j + nj)),
                  pl.BlockSpec((width, tn), lambda i, j: (0, j)),
                  pl.BlockSpec((width, tn), lambda i, j: (0, j))],
        out_specs=pl.BlockSpec((tm, tn), lambda i, j: (i, j)),
        out_shape=jax.ShapeDtypeStruct((n, d), BF16),
        compiler_params=_params(("arbitrary", "arbitrary")),
        name="merge",
    )(h_hi, o_a, o_b, w_gate, w_gate, b_gate, b_gate, w_pa, w_pb)


def _outproj_kernel(m_ref, w_ref, x_ref, mod_ref, gm_ref, gf_ref, x1_ref, h2_ref):
    mod = mod_ref[0]
    x1 = x_ref[...] + mod[2:3, :] * _rms(_dot(m_ref[...], w_ref[...]), gm_ref[...])
    x1_ref[...] = x1
    h2_ref[...] = (_rms(x1, gf_ref[...]) * (1.0 + mod[4:5, :]) + mod[3:4, :]).astype(BF16)


def _outproj(merged, w_out, x2, mod, g_post_mix, g_pre_ffn, seq):
    n, d = x2.shape
    tm = _tile(seq, 256, 16)
    tps = seq // tm
    row = lambda i: (i, 0)
    fix = lambda i: (0, 0)
    return pl.pallas_call(
        _outproj_kernel,
        grid=(n // tm,),
        in_specs=[pl.BlockSpec((tm, d), row), pl.BlockSpec((d, d), fix), pl.BlockSpec((tm, d), row),
                  pl.BlockSpec((1, N_MOD, d), lambda i: (i // tps, 0, 0)),
                  pl.BlockSpec((1, d), fix), pl.BlockSpec((1, d), fix)],
        out_specs=[pl.BlockSpec((tm, d), row), pl.BlockSpec((tm, d), row)],
        out_shape=[jax.ShapeDtypeStruct((n, d), F32), jax.ShapeDtypeStruct((n, d), BF16)],
        compiler_params=_params(("arbitrary",)),
        name="outproj",
    )(merged, w_out, x2, mod, g_post_mix.reshape(1, d), g_pre_ffn.reshape(1, d))


def _gelu_tanh(x):
    return 0.5 * x * (1.0 + jnp.tanh(0.7978845608028654 * (x + 0.044715 * (x * x * x))))


def _ffn_kernel(h_ref, halo_ref, x1_ref, mod_ref, st_ref, wg_ref, wu_ref, wd_ref, wc_ref, bc_ref, gp_ref,
                y_ref, nc_ref, g_scr, act_scr, *, tm, nsub, tps):
    i = pl.program_id(0)
    c = pl.program_id(1)
    last = pl.num_programs(1) - 1
    sub = tm // nsub
    cur = c % 2

    def step(down, gate_up):
        if gate_up:
            h = h_ref[...]
            wg = wg_ref[...]
            g = _dot(h, wg)
            u = _dot(h, wu_ref[...])
            wc = wc_ref[...]
            bc = bc_ref[...]
            g_scr[HALO:HALO + tm, :] = g
        if down == "set":
            y_ref[...] = _dot(act_scr[1 - cur], wd_ref[...])
        elif down == "add":
            y_ref[...] += _dot(act_scr[1 - cur], wd_ref[...])
        if not gate_up:
            return

        def conv(lo, rows):
            return (bc + wc[0:1, :] * g_scr[lo - 2:lo - 2 + rows, :]
                    + wc[1:2, :] * g_scr[lo - 1:lo - 1 + rows, :] + wc[2:3, :] * g_scr[lo:lo + rows, :])

        if nsub == 1:
            g_halo = _dot(halo_ref[...], wg)
            g_scr[0:HALO, :] = g_halo
            g_scr[HALO - 2:HALO, :] = jnp.where(i % tps == 0, st_ref[0], g_halo[HALO - 2:HALO, :])
            nc_ref[0] = g_scr[HALO + tm - 2:HALO + tm, :]
            act_scr[cur] = (_gelu_tanh(conv(HALO, tm)) * u).astype(BF16)
        else:
            for s in range(nsub):
                lo = HALO + s * sub
                nc_ref[s] = g_scr[lo + sub - 2:lo + sub, :]
                g_scr[lo - 2:lo, :] = st_ref[s]
                act_scr[cur, s * sub:(s + 1) * sub, :] = (
                    _gelu_tanh(conv(lo, sub)) * u[s * sub:(s + 1) * sub, :]).astype(BF16)

    pl.when(c == 0)(lambda: step(None, True))
    pl.when(c == 1)(lambda: step("set", True))
    pl.when(jnp.logical_and(c > 1, c < last))(lambda: step("add", True))

    @pl.when(c == last)
    def _():
        step("add", False)
        for s in range(nsub):
            rows = slice(s * sub, (s + 1) * sub)
            y_ref[rows, :] = x1_ref[rows, :] + mod_ref[s][5:6, :] * _rms(y_ref[rows, :], gp_ref[...])


def _ffn(h2, x1, mod, state, w_gate, w_up, w_down, w_conv, b_conv, g_post, seq):
    n, d = x1.shape
    dff = w_gate.shape[1]
    nb = n // seq
    tm = _row_tile(n, seq, 1024)
    tf = _tile(dff, 512, LANES)
    n_chunks = dff // tf
    assert n_chunks >= 2, "the lagged down projection needs at least two d_ff chunks"
    up = lambda c: jnp.minimum(c, n_chunks - 1)
    down = lambda c: jnp.maximum(c - 1, 0)
    per_tile_c = lambda i, c: (i, 0, up(c))
    if tm <= seq:
        nsub, tps = 1, seq // tm
        per_seq = lambda i, c: (i // tps, 0, 0)
        per_seq_c = lambda i, c: (i // tps, 0, up(c))
    else:
        nsub, tps = tm // seq, 1
        per_seq = lambda i, c: (i, 0, 0)
        per_seq_c = per_tile_c
    halo_blocks = tm // HALO
    row = lambda i, c: (i, 0)
    y, new_conv = pl.pallas_call(
        functools.partial(_ffn_kernel, tm=tm, nsub=nsub, tps=tps),
        grid=(n // tm, n_chunks + 1),
        in_specs=[pl.BlockSpec((tm, d), row, pipeline_mode=pl.Buffered(1)),
                  pl.BlockSpec((HALO, d), lambda i, c: (jnp.maximum(i * halo_blocks - 1, 0), 0)),
                  pl.BlockSpec((tm, d), row, pipeline_mode=pl.Buffered(1)),
                  pl.BlockSpec((nsub, N_MOD, d), per_seq),
                  pl.BlockSpec((nsub, CONV_W - 1, tf), per_seq_c),
                  pl.BlockSpec((d, tf), lambda i, c: (0, up(c))),
                  pl.BlockSpec((d, tf), lambda i, c: (0, up(c))),
                  pl.BlockSpec((tf, d), lambda i, c: (down(c), 0)),
                  pl.BlockSpec((CONV_W, tf), lambda i, c: (0, up(c))),
                  pl.BlockSpec((1, tf), lambda i, c: (0, up(c))),
                  pl.BlockSpec((1, d), lambda i, c: (0, 0))],
        out_specs=[pl.BlockSpec((tm, d), row, pipeline_mode=pl.Buffered(1)),
                   pl.BlockSpec((nsub, CONV_W - 1, tf), per_tile_c)],
        out_shape=[jax.ShapeDtypeStruct((n, d), F32),
                   jax.ShapeDtypeStruct((nb * tps, CONV_W - 1, dff), F32)],
        scratch_shapes=[pltpu.VMEM((HALO + tm, tf), F32), pltpu.VMEM((2, tm, tf), BF16)],
        compiler_params=_params(("arbitrary", "arbitrary")),
        name="ffn",
    )(h2, h2, x1, mod, state, w_gate, w_up, w_down, w_conv, b_conv.reshape(1, dff), g_post.reshape(1, d))
    return y, new_conv.reshape(nb, tps, CONV_W - 1, dff)[:, tps - 1]


def _rope_tables(pos):
    def angles(dim):
        half = dim // 2
        inv = ROPE_THETA ** (-jnp.arange(half, dtype=F32) * 2.0 / dim)
        ang = pos.astype(F32)[:, None] * inv[None, :]
        return jnp.cos(ang), jnp.sin(ang)

    cos, sin = angles(HEAD_DIM)
    cos_h = jnp.concatenate([cos, cos], axis=1)
    sin_h = jnp.concatenate([-sin, sin], axis=1)

    cos, sin = angles(D_IDX)
    zero = jnp.zeros_like(sin)
    n_rot = H_IDX + 1
    tail = IDX_COLS - n_rot * D_IDX
    t = pos.shape[0]
    c_i = jnp.concatenate([jnp.tile(jnp.concatenate([cos, cos], 1), (1, n_rot)), jnp.ones((t, tail), F32)], 1)
    s1_i = jnp.concatenate([jnp.tile(jnp.concatenate([-sin, zero], 1), (1, n_rot)), jnp.zeros((t, tail), F32)], 1)
    s2_i = jnp.concatenate([jnp.tile(jnp.concatenate([zero, sin], 1), (1, n_rot)), jnp.zeros((t, tail), F32)], 1)
    return cos_h, sin_h, c_i, s1_i, s2_i


def _with_past(past, new, nb, seq, n_keys):
    w = new.shape[1]
    if past is None and n_keys == seq:
        return new
    parts = [] if past is None else [past.astype(new.dtype).reshape(nb, -1, w)]
    parts.append(new.reshape(nb, seq, w))
    have = sum(p.shape[1] for p in parts)
    if n_keys > have:
        parts.append(jnp.zeros((nb, n_keys - have, w), new.dtype))
    return jnp.concatenate(parts, axis=1).reshape(nb * n_keys, w)


def _layer(x, mod, past, conv_state, p):
    nb, seq, d = x.shape
    n = nb * seq
    nh = p["w_pa"].shape[0] // HEAD_DIM
    width = nh * HEAD_DIM
    n_past = 0 if past is None else past["a_k"].shape[1]
    n_valid = n_past + seq
    n_keys = -(-n_valid // 256) * 256
    x2 = x.reshape(n, d)

    h_hi, h_lo = _norm(x2, p["g_pre_mix"], mod, seq)
    cos_h, sin_h, c_i, s1_i, s2_i = _rope_tables(n_past + jnp.arange(seq, dtype=I32))
    (qa, ka32, ka16, va32, va16, qb, kb32, kb16, vb32, vb16) = _inproj(h_hi, p["w_main"], cos_h, sin_h, seq, width)
    qi_hi, qi_lo, kw, ki_hi, ki_lo = _idxproj(h_hi, h_lo, p["w_idx_hi"], p["w_idx_lo"], c_i, s1_i, s2_i, seq)

    q_hi = qi_hi.reshape(n, H_IDX, D_IDX)
    q_lo = qi_lo.reshape(n, H_IDX, D_IDX)
    qi_cat = jnp.concatenate([q_hi, q_hi, q_lo, jnp.zeros_like(q_hi)], axis=2).reshape(n, H_IDX * IDX_K)
    k_hi, k_lo = ki_hi[:, :D_IDX], ki_lo[:, :D_IDX]
    if past is None:
        pa_k = pa_v = pb_k = pb_v = pi_hi = pi_lo = None
    else:
        pa_k, pa_v, pb_k, pb_v = (past[k].reshape(nb, n_past, width) for k in ("a_k", "a_v", "b_k", "b_v"))
        pi_hi, pi_lo = _split(past["b_kidx"].astype(F32))
    k_hi = _with_past(pi_hi, k_hi, nb, seq, n_keys)
    k_lo = _with_past(pi_lo, k_lo, nb, seq, n_keys)
    ki_cat = jnp.concatenate([k_hi, k_lo, k_hi, jnp.zeros_like(k_hi)], axis=1)

    o_a = _sb(qa, _with_past(pa_k, ka16, nb, seq, n_keys), _with_past(pa_v, va16, nb, seq, n_keys),
              nb, seq, n_keys, n_past, nh)
    seq_q = max(seq, LANES)
    pad_q = lambda a: a if seq_q == seq else jnp.pad(
        a.reshape(nb, seq, -1), ((0, 0), (0, seq_q - seq), (0, 0))).reshape(nb * seq_q, -1)
    vt_all = _with_past(pb_v, vb16, nb, seq, n_keys).reshape(nb, n_keys, width)
    vt_all = jnp.swapaxes(vt_all, 1, 2).reshape(nb * width, n_keys)
    o_b = _dsat(pad_q(qb), _with_past(pb_k, kb16, nb, seq, n_keys), vt_all, pad_q(qi_cat), ki_cat,
                pad_q(kw).T, nb, seq_q, n_keys, n_valid, n_past, nh)
    if seq_q != seq:
        o_b = o_b.reshape(nb, seq_q, width)[:, :seq].reshape(n, width)

    merged = _merge(h_hi, o_a, o_b, p["w_gate"], p["b_gate"], p["w_pa"], p["w_pb"])
    x1, h2 = _outproj(merged, p["w_out"], x2, mod, p["g_post_mix"], p["g_pre_ffn"], seq)
    y, new_conv = _ffn(h2, x1, mod, conv_state, p["w_ffn_gate"], p["w_ffn_up"], p["w_ffn_down"],
                       p["w_conv"], p["b_conv"], p["g_post_ffn"], seq)

    heads = lambda a: a.reshape(nb, seq, nh, HEAD_DIM)
    new = (heads(ka32), heads(va32), heads(kb32), heads(vb32), kw[:, :D_IDX].reshape(nb, seq, D_IDX), new_conv)
    return y.reshape(nb, seq, d), new


def kernel(x_prompt, x_sample, c_prompt, c_sample, cache_a_k, cache_a_v, cache_b_k, cache_b_v, cache_b_kidx,
           state_ffn_conv, w_mod, b_mod, g_pre_mix, w_in, w_merge_gate, b_merge_gate, w_proj_a, w_proj_b, w_out,
           g_post_mix, g_pre_ffn, w_ffn_gate, w_ffn_up, w_conv, b_conv, w_ffn_down, g_post_ffn):
    depth = w_mod.shape[0]
    nbp, nbs = x_prompt.shape[0], x_sample.shape[0]
    d = x_prompt.shape[2]
    width = w_proj_a.shape[1]
    dff = w_ffn_gate.shape[2]
    assert w_in.shape[2] == 6 * width + H_IDX * D_IDX + D_IDX + H_IDX
    assert cache_b_kidx.shape[-1] == D_IDX and cache_a_k.shape[-1] == HEAD_DIM

    y_p, y_s = x_prompt, x_sample
    st_p = [[] for _ in range(6)]
    st_s = [[] for _ in range(6)]
    for l in range(depth):
        w_idx = jnp.pad(w_in[l][:, 6 * width:], ((0, 0), (0, IDX_COLS - (w_in.shape[2] - 6 * width))))
        w_idx_hi, w_idx_lo = _split(w_idx)
        p = dict(
            g_pre_mix=g_pre_mix[l], w_main=w_in[l][:, :6 * width].astype(BF16), w_idx_hi=w_idx_hi, w_idx_lo=w_idx_lo,
            w_gate=w_merge_gate[l].astype(BF16), b_gate=b_merge_gate[l],
            w_pa=w_proj_a[l].astype(BF16), w_pb=w_proj_b[l].astype(BF16), w_out=w_out[l].astype(BF16),
            g_post_mix=g_post_mix[l], g_pre_ffn=g_pre_ffn[l],
            w_ffn_gate=w_ffn_gate[l].astype(BF16), w_ffn_up=w_ffn_up[l].astype(BF16),
            w_ffn_down=w_ffn_down[l].astype(BF16), w_conv=w_conv[l], b_conv=b_conv[l], g_post_ffn=g_post_ffn[l])

        c_all = jnp.concatenate([c_prompt, c_sample], axis=0)
        pad = -c_all.shape[0] % 16
        mod = _mod(jnp.pad(c_all, ((0, pad), (0, 0))), w_mod[l], b_mod[l]).reshape(-1, N_MOD, d)
        past = dict(a_k=cache_a_k[l], a_v=cache_a_v[l], b_k=cache_b_k[l], b_v=cache_b_v[l], b_kidx=cache_b_kidx[l])
        y_p, new_p = _layer(y_p, mod[:nbp], None, jnp.zeros((nbp, CONV_W - 1, dff), F32), p)
        y_s, new_s = _layer(y_s, mod[nbp:nbp + nbs], past, state_ffn_conv[l], p)
        for i in range(6):
            st_p[i].append(new_p[i])
            st_s[i].append(new_s[i])
    sp = [jnp.stack(s, axis=0) for s in st_p]
    ss = [jnp.stack(s, axis=0) for s in st_s]
    return (y_p, y_s, sp[0], sp[1], sp[2], sp[3], sp[4], sp[5], ss[0], ss[1], ss[2], ss[3], ss[4], ss[5])
```

```python
import functools

import jax
import jax.numpy as jnp
from jax import lax
from jax.experimental import pallas as pl
from jax.experimental.pallas import tpu as pltpu

F32 = jnp.float32
BF16 = jnp.bfloat16
I32 = jnp.int32

CHUNK = 64
HEAD_DIM = 128
H_IDX = 4
D_IDX = 64
TOPK_MAX = 256
CONV_W = 3
ROPE_THETA = 10000.0
RMS_EPS = 1e-6
N_MOD = 6
IDX_COLS = 384
IDX_K = 256
LANES = 128
HALO = 16
ONES_ROWS = 16
LOG2E = 1.4426950408889634
UNDERFLOW_LOG2 = 160.0
NEG = -1e30
INT_MIN = -2 ** 31
VMEM_LIMIT = 56 * 1024 * 1024


def _tile(n, pref, mult):
    t = min(pref, n)
    t -= t % mult
    while t >= mult:
        if n % t == 0:
            return t
        t -= mult
    return n


def _row_tile(n, seq, pref):
    if seq >= pref or n == seq:
        return _tile(seq, pref, 16)
    return _tile(n, pref, seq)


def _params(sem):
    return pltpu.CompilerParams(dimension_semantics=sem, vmem_limit_bytes=VMEM_LIMIT)


def _split(x):
    hi = x.astype(BF16)
    lo = (x - hi.astype(F32)).astype(BF16)
    return hi, lo


def _dot(a, b):
    return jnp.dot(a, b, preferred_element_type=F32)


def _dot_t(a, b):
    return lax.dot_general(a, b, (((1,), (1,)), ((), ())), preferred_element_type=F32)


def _rms(x, g):
    ms = jnp.mean(x * x, axis=-1, keepdims=True)
    return x * lax.rsqrt(ms + RMS_EPS) * g


def _mod_kernel(c_ref, w_ref, b_ref, o_ref):
    c = c_ref[...]
    s = c / (1.0 + jnp.exp(-c))
    s_hi, s_lo = _split(s)
    w_hi, w_lo = _split(w_ref[...])
    o_ref[...] = _dot(s_hi, w_hi) + _dot(s_hi, w_lo) + _dot(s_lo, w_hi) + b_ref[...]


def _mod(c, w, b):
    bc, d = c.shape
    n = w.shape[1]
    tn = _tile(n, 1024, LANES)
    return pl.pallas_call(
        _mod_kernel,
        grid=(n // tn,),
        in_specs=[pl.BlockSpec((bc, d), lambda j: (0, 0)),
                  pl.BlockSpec((d, tn), lambda j: (0, j)),
                  pl.BlockSpec((1, tn), lambda j: (0, j))],
        out_specs=pl.BlockSpec((bc, tn), lambda j: (0, j)),
        out_shape=jax.ShapeDtypeStruct((bc, n), F32),
        compiler_params=_params(("arbitrary",)),
        name="mod",
    )(c, w, b.reshape(1, n))


def _norm_kernel(x_ref, g_ref, mod_ref, hi_ref, lo_ref):
    m = mod_ref[0]
    h = _rms(x_ref[...], g_ref[...]) * (1.0 + m[1:2, :]) + m[0:1, :]
    hi, lo = _split(h)
    hi_ref[...] = hi
    lo_ref[...] = lo


def _norm(x2, g, mod, seq):
    n, d = x2.shape
    tr = _tile(seq, 512, 16)
    tps = seq // tr
    return pl.pallas_call(
        _norm_kernel,
        grid=(n // tr,),
        in_specs=[pl.BlockSpec((tr, d), lambda i: (i, 0)),
                  pl.BlockSpec((1, d), lambda i: (0, 0)),
                  pl.BlockSpec((1, N_MOD, d), lambda i: (i // tps, 0, 0))],
        out_specs=[pl.BlockSpec((tr, d), lambda i: (i, 0))] * 2,
        out_shape=[jax.ShapeDtypeStruct((n, d), BF16)] * 2,
        compiler_params=_params(("arbitrary",)),
        name="norm",
    )(x2, g.reshape(1, d), mod)


N_PROJ = 6


def _inproj_kernel(a_ref, w_ref, c_ref, s_ref, p32_ref, p16_ref, *, tn):
    acc = _dot(a_ref[...], w_ref[...])
    c = c_ref[0]
    s = s_ref[0]
    for k in range(tn // HEAD_DIM):
        sl = slice(k * HEAD_DIM, (k + 1) * HEAD_DIM)
        xs = acc[:, sl]
        r = xs * c + pltpu.roll(xs, HEAD_DIM // 2, 1) * s
        p32_ref[:, sl] = r
        p16_ref[:, sl] = r.astype(BF16)


def _inproj(h_hi, w_main, cos_t, sin_t, seq, width):
    n, d = h_hi.shape
    tm = _row_tile(n, seq, 1024)
    tn = _tile(width, 1024, LANES)
    npj = width // tn
    scale = LOG2E * HEAD_DIM ** -0.5
    one, zero = jnp.ones_like(cos_t), jnp.zeros_like(cos_t)
    c_t = jnp.stack([scale * one, one, one, scale * cos_t, cos_t, one])
    s_t = jnp.stack([zero, zero, zero, scale * sin_t, sin_t, zero])
    if tm <= seq:
        tps = seq // tm
        tab_map = lambda i, j: (j // npj, i % tps, 0)
    else:
        c_t = jnp.tile(c_t, (1, tm // seq, 1))
        s_t = jnp.tile(s_t, (1, tm // seq, 1))
        tab_map = lambda i, j: (j // npj, 0, 0)
    return pl.pallas_call(
        functools.partial(_inproj_kernel, tn=tn),
        grid=(n // tm, N_PROJ * npj),
        in_specs=[pl.BlockSpec((tm, d), lambda i, j: (i, 0)),
                  pl.BlockSpec((d, tn), lambda i, j: (0, j)),
                  pl.BlockSpec((1, tm, HEAD_DIM), tab_map),
                  pl.BlockSpec((1, tm, HEAD_DIM), tab_map)],
        out_specs=[pl.BlockSpec((tm, tn), lambda i, j: (i, j))] * 2,
        out_shape=[jax.ShapeDtypeStruct((n, N_PROJ * width), F32),
                   jax.ShapeDtypeStruct((n, N_PROJ * width), BF16)],
        compiler_params=_params(("arbitrary", "arbitrary")),
        name="inproj",
    )(h_hi, w_main, c_t, s_t)


def _idxproj_kernel(hi_ref, lo_ref, whi_ref, wlo_ref, c_ref, s1_ref, s2_ref,
                    qhi_ref, qlo_ref, kw_ref, khi_ref, klo_ref):
    a_hi = hi_ref[...]
    w_hi = whi_ref[...]
    acc = _dot(a_hi, w_hi) + _dot(a_hi, wlo_ref[...]) + _dot(lo_ref[...], w_hi)
    for s in range(IDX_COLS // LANES):
        sl = slice(s * LANES, (s + 1) * LANES)
        xs = acc[:, sl]
        r = (xs * c_ref[:, sl] + pltpu.roll(xs, LANES - D_IDX // 2, 1) * s1_ref[:, sl]
             + pltpu.roll(xs, D_IDX // 2, 1) * s2_ref[:, sl])
        hi, lo = _split(r)
        if s < 2:
            qhi_ref[:, sl] = hi
            qlo_ref[:, sl] = lo
        else:
            kw_ref[...] = r
            khi_ref[...] = hi
            klo_ref[...] = lo


def _idxproj(h_hi, h_lo, w_hi, w_lo, c_t, s1_t, s2_t, seq):
    n, d = h_hi.shape
    tm = _row_tile(n, seq, 512)
    if tm <= seq:
        tps = seq // tm
        tab_map = lambda i: (i % tps, 0)
    else:
        c_t, s1_t, s2_t = (jnp.tile(t, (tm // seq, 1)) for t in (c_t, s1_t, s2_t))
        tab_map = lambda i: (0, 0)
    row = lambda i: (i, 0)
    return pl.pallas_call(
        _idxproj_kernel,
        grid=(n // tm,),
        in_specs=[pl.BlockSpec((tm, d), row), pl.BlockSpec((tm, d), row),
                  pl.BlockSpec((d, IDX_COLS), lambda i: (0, 0)),
                  pl.BlockSpec((d, IDX_COLS), lambda i: (0, 0)),
                  pl.BlockSpec((tm, IDX_COLS), tab_map),
                  pl.BlockSpec((tm, IDX_COLS), tab_map),
                  pl.BlockSpec((tm, IDX_COLS), tab_map)],
        out_specs=[pl.BlockSpec((tm, 2 * LANES), row), pl.BlockSpec((tm, 2 * LANES), row),
                   pl.BlockSpec((tm, LANES), row), pl.BlockSpec((tm, LANES), row),
                   pl.BlockSpec((tm, LANES), row)],
        out_shape=[jax.ShapeDtypeStruct((n, 2 * LANES), BF16), jax.ShapeDtypeStruct((n, 2 * LANES), BF16),
                   jax.ShapeDtypeStruct((n, LANES), F32), jax.ShapeDtypeStruct((n, LANES), BF16),
                   jax.ShapeDtypeStruct((n, LANES), BF16)],
        compiler_params=_params(("arbitrary",)),
        name="idxproj",
    )(h_hi, h_lo, w_hi, w_lo, c_t, s1_t, s2_t)


def _sb_kernel(q_ref, k_ref, v_ref, o_ref, acc_scr, z_scr, i_scr, *, tq, tk, past, group):
    q0 = past + pl.program_id(2) * tq
    n_kt = (q0 + tq - 2) // tk + 1
    n_clear = q0 // tk
    row_pos = q0 + lax.broadcasted_iota(I32, (tq, 1), 0)
    col = lax.broadcasted_iota(I32, (1, tk), 1)
    suffix = jnp.where(lax.broadcasted_iota(I32, (tk, tk), 0) >= lax.broadcasted_iota(I32, (tk, tk), 1),
                       1.0, 0.0).astype(BF16)
    suffix2 = jnp.concatenate([suffix, suffix], axis=0)
    acc_scr[...] = jnp.zeros_like(acc_scr)

    heads = [slice(h * HEAD_DIM, (h + 1) * HEAD_DIM) for h in range(group)]

    def tiles(js, carries, masked):
        offs = [pl.multiple_of(j * tk, tk) for j in js]
        if masked:
            valids = [(off + col) < row_pos for off in offs]
        for t, off in enumerate(offs):
            for h, hs in enumerate(heads):
                z_scr[t * group + h] = _dot_t(q_ref[:, hs], k_ref[pl.ds(off, tk), hs])
        for t in range(len(js)):
            for h in range(group):
                z = z_scr[t * group + h]
                sp = jnp.maximum(z, 0.0) + jnp.log2(1.0 + jnp.exp2(-jnp.abs(z)))
                if masked:
                    sp = jnp.where(valids[t], sp, 0.0)
                hi, lo = _split(sp)
                i_scr[t * group + h] = _dot(jnp.concatenate([hi, lo], axis=1), suffix2)
        carries = list(carries)
        for t, off in enumerate(offs):
            for h, hs in enumerate(heads):
                w = jnp.exp2(z_scr[t * group + h] - i_scr[t * group + h] - carries[h])
                if masked:
                    w = jnp.where(valids[t], w, 0.0)
                acc_scr[h] += _dot(w.astype(BF16), v_ref[pl.ds(off, tk), hs])
                carries[h] = carries[h] + i_scr[t * group + h, :, 0:1]
        return tuple(carries)

    def least(carries):
        m = carries[0]
        for c in carries[1:]:
            m = jnp.minimum(m, c)
        return jnp.min(m)

    st = tuple(jnp.zeros((tq, 1), F32) for _ in range(group))
    st = lax.fori_loop(0, n_kt - n_clear, lambda t, c: tiles([n_kt - 1 - t], c, True), st)

    def more(state):
        t, low, _ = state
        return jnp.logical_and(t < n_clear, low < UNDERFLOW_LOG2)

    def step(state):
        t, _, carries = state
        carries = tiles([n_clear - 1 - t], carries, False)
        return t + 1, least(carries), carries

    lax.while_loop(more, step, (jnp.int32(0), least(st), st))
    for h in range(group):
        o_ref[:, h * HEAD_DIM:(h + 1) * HEAD_DIM] = acc_scr[h].astype(o_ref.dtype)


def _sb(q, k_all, v_all, nb, seq, n_keys, past, nh):
    tq = _tile(seq, 256, 16)
    tk = 256
    nq = seq // tq
    group = 4 if nh % 4 == 0 else (2 if nh % 2 == 0 else 1)
    gw = group * HEAD_DIM
    (q, q0), (k_all, k0), (v_all, v0) = ((a, c // gw) for a, c in (q, k_all, v_all))
    return pl.pallas_call(
        functools.partial(_sb_kernel, tq=tq, tk=tk, past=past, group=group),
        grid=(nb, nh // group, nq),
        in_specs=[pl.BlockSpec((tq, gw), lambda b, h, i: (b * nq + i, q0 + h)),
                  pl.BlockSpec((n_keys, gw), lambda b, h, i: (b, k0 + h)),
                  pl.BlockSpec((n_keys, gw), lambda b, h, i: (b, v0 + h))],
        out_specs=pl.BlockSpec((tq, gw), lambda b, h, i: (b * nq + i, h)),
        out_shape=jax.ShapeDtypeStruct((q.shape[0], nh * HEAD_DIM), BF16),
        scratch_shapes=[pltpu.VMEM((group, tq, HEAD_DIM), F32), pltpu.VMEM((group, tq, tk), F32),
                        pltpu.VMEM((group, tq, tk), F32)],
        compiler_params=_params(("arbitrary", "arbitrary", "arbitrary")),
        name="sb",
    )(q, k_all, v_all)


def _fold8(parts, op):
    while len(parts) > 1:
        parts = [op(parts[i], parts[i + 1]) if i + 1 < len(parts) else parts[i] for i in range(0, len(parts), 2)]
    return parts[0]


def _dsat_kernel(q_ref, k_ref, vt_ref, qi_ref, ki_ref, kwt_ref, o_ref, key_scr, acc_scr, s_scr, p_scr,
                 *, tq, tk, past, n_valid, n_sel, nh):
    q0 = past + pl.program_id(1) * tq
    col_pos = q0 + lax.broadcasted_iota(I32, (1, tq), 1)
    col_lim = jnp.minimum((col_pos // CHUNK + 1) * CHUNK, n_valid)
    n_kt = (jnp.minimum(((q0 + tq - 1) // CHUNK + 1) * CHUNK, n_valid) + tk - 1) // tk
    row = lax.broadcasted_iota(I32, (tk, 1), 0)
    ksel = float(n_sel)
    groups = [slice(8 * r, 8 * r + 8) for r in range(tk // 8)]

    wrow = kwt_ref[D_IDX:D_IDX + H_IDX, :] * ((H_IDX ** -0.5) * (D_IDX ** -0.5))

    def score_pair(t, _):
        offs = [pl.multiple_of(jnp.minimum(2 * t + u, n_kt - 1) * tk, tk) for u in range(2)]
        raw = [[_dot_t(ki_ref[pl.ds(off, tk), :], qi_ref[:, h * IDX_K:(h + 1) * IDX_K]) for h in range(H_IDX)]
               for off in offs]
        for off, dots in zip(offs, raw):
            score = jnp.zeros((tk, tq), F32)
            for h, s in enumerate(dots):
                score = score + wrow[h:h + 1, :] * jnp.maximum(s, 0.0)
            bits = pltpu.bitcast(score, I32)
            key = jnp.where(bits < 0, INT_MIN - bits, bits)
            key_scr[pl.ds(off, tk), :] = jnp.where((off + row) < col_lim, key, INT_MIN)
        return 0

    lax.fori_loop(0, (n_kt + 1) // 2, score_pair, 0)

    def count_ge(thr8):
        def body(t, cnt):
            for u in range(2):
                j = 2 * t + u
                off = pl.multiple_of(jnp.minimum(j, n_kt - 1) * tk, tk)
                key = key_scr[pl.ds(off, tk), :]
                part = _fold8([jnp.where(key[g, :] >= thr8, 1.0, 0.0) for g in groups], jnp.add)
                cnt = cnt + (part if u == 0 else jnp.where(j < n_kt, part, 0.0))
            return cnt

        cnt = lax.fori_loop(0, (n_kt + 1) // 2, body, jnp.zeros((8, tq), F32))
        return jnp.broadcast_to(jnp.sum(cnt, axis=0, keepdims=True), (8, tq))

    def bit_step(it, st):
        cand, have = st
        trial = cand | lax.shift_left(jnp.int32(1), 30 - it)
        cnt = count_ge(trial)
        take = cnt >= ksel
        return jnp.where(take, trial, cand), jnp.where(take, cnt, have)

    zero8 = jnp.zeros((8, tq), I32)
    cnt0 = count_ge(zero8)
    start = (jnp.where(cnt0 >= ksel, zero8, INT_MIN), jnp.where(cnt0 >= ksel, cnt0, float(2 ** 30)))
    thr8, have8 = lax.fori_loop(0, 31, bit_step, start)
    thr = thr8[0:1, :]
    overfull = jnp.max(jnp.where(have8 > ksel, 1.0, 0.0)) > 0.0

    @pl.when(jnp.logical_not(overfull))
    def _():
        def bias_tile(j, _):
            off = pl.multiple_of(j * tk, tk)
            bias = jnp.where(key_scr[pl.ds(off, tk), :] >= thr, 0.0, NEG)
            key_scr[pl.ds(off, tk), :] = pltpu.bitcast(bias, I32)
            return 0

        lax.fori_loop(0, n_kt, bias_tile, 0)

    @pl.when(overfull)
    def _():
        no_thr8 = thr8 == INT_MIN
        cnt_gt = count_ge(jnp.where(no_thr8, thr8, thr8 + 1))
        need = jnp.where(no_thr8, 0.0, ksel - cnt_gt)[0:1, :]
        prefix = jnp.where(lax.broadcasted_iota(I32, (tk, tk), 0) >= lax.broadcasted_iota(I32, (tk, tk), 1),
                           1.0, 0.0).astype(BF16)

        def bias_tile(j, carry):
            off = pl.multiple_of(j * tk, tk)
            key = key_scr[pl.ds(off, tk), :]
            eq = key == thr
            eqf = jnp.where(eq, 1.0, 0.0)
            rank_incl = _dot(prefix, eqf.astype(BF16)) + carry
            tie_ok = (rank_incl - eqf) < need
            bias = jnp.where(key > thr, 0.0, jnp.where(eq, jnp.where(tie_ok, 0.0, NEG), NEG))
            key_scr[pl.ds(off, tk), :] = pltpu.bitcast(bias, I32)
            return rank_incl[tk - 1:tk, :]

        lax.fori_loop(0, n_kt, bias_tile, jnp.zeros((1, tq), F32))

    acc_scr[...] = jnp.zeros_like(acc_scr)
    ones_v = jnp.ones((ONES_ROWS, tk), BF16)
    heads = [slice(h * HEAD_DIM, (h + 1) * HEAD_DIM) for h in range(nh)]

    def logits(j, slot, ms):
        off = pl.multiple_of(j * tk, tk)
        bias = pltpu.bitcast(key_scr[pl.ds(off, tk), :], F32)
        for h, hs in enumerate(heads):
            s_scr[slot, h] = _dot_t(k_ref[pl.ds(off, tk), hs], q_ref[:, hs]) + bias
        out = []
        for h, m in enumerate(ms):
            s = s_scr[slot, h]
            top = jnp.max(_fold8([s[g, :] for g in groups], jnp.maximum), axis=0, keepdims=True)
            out.append(jnp.maximum(m, top))
        return tuple(out)

    def accumulate(j, slot, m_old, m_new):
        off = pl.multiple_of(j * tk, tk)
        for h in range(nh):
            p_scr[h] = jnp.exp2(s_scr[slot, h] - m_new[h]).astype(BF16)
        for h, hs in enumerate(heads):
            vt_ext = jnp.concatenate([vt_ref[hs, pl.ds(off, tk)], ones_v], axis=0)
            acc_scr[h] = jnp.exp2(m_old[h] - m_new[h]) * acc_scr[h] + _dot(vt_ext, p_scr[h])

    m_none = tuple(jnp.full((1, tq), NEG, F32) for _ in range(nh))
    m_first = logits(0, 0, m_none)

    def attn_step(j, st):
        m_old, m_cur = st
        slot = j % 2
        accumulate(j - 1, 1 - slot, m_old, m_cur)
        return m_cur, logits(j, slot, m_cur)

    m_old, m_cur = lax.fori_loop(1, n_kt, attn_step, (m_none, m_first))
    accumulate(n_kt - 1, (n_kt - 1) % 2, m_old, m_cur)
    for h, hs in enumerate(heads):
        out_t = acc_scr[h, :HEAD_DIM, :] / acc_scr[h, HEAD_DIM:HEAD_DIM + 1, :]
        o_ref[:, hs] = out_t.T.astype(o_ref.dtype)


def _dsat(q, k_all, vt_all, qi_cat, ki_cat, kwt, nb, seq, n_keys, n_valid, past, nh):
    tq = _tile(seq, 256, LANES)
    tk = 256
    nq = seq // tq
    width = nh * HEAD_DIM
    n_sel = min(TOPK_MAX, n_valid // 4)
    once = pl.Buffered(1)
    (q, q0), (k_all, k0) = ((a, c // width) for a, c in (q, k_all))
    return pl.pallas_call(
        functools.partial(_dsat_kernel, tq=tq, tk=tk, past=past, n_valid=n_valid, n_sel=n_sel, nh=nh),
        grid=(nb, nq),
        in_specs=[pl.BlockSpec((tq, width), lambda b, i: (b * nq + i, q0)),
                  pl.BlockSpec((n_keys, width), lambda b, i: (b, k0), pipeline_mode=once),
                  pl.BlockSpec((width, n_keys), lambda b, i: (b, 0), pipeline_mode=once),
                  pl.BlockSpec((tq, H_IDX * IDX_K), lambda b, i: (b * nq + i, 0)),
                  pl.BlockSpec((n_keys, IDX_K), lambda b, i: (b, 0), pipeline_mode=once),
                  pl.BlockSpec((LANES, tq), lambda b, i: (0, b * nq + i))],
        out_specs=pl.BlockSpec((tq, width), lambda b, i: (b * nq + i, 0)),
        out_shape=jax.ShapeDtypeStruct((q.shape[0], width), BF16),
        scratch_shapes=[pltpu.VMEM((n_keys, tq), I32), pltpu.VMEM((nh, HEAD_DIM + ONES_ROWS, tq), F32),
                        pltpu.VMEM((2, nh, tk, tq), F32), pltpu.VMEM((nh, tk, tq), BF16)],
        compiler_params=_params(("arbitrary", "arbitrary")),
        name="dsa",
    )(q, k_all, vt_all, qi_cat, ki_cat, kwt)


def _merge_kernel(h_ref, oa_ref, ob_ref, wg1_ref, wg2_ref, b1_ref, b2_ref, wa_ref, wb_ref, o_ref):
    h = h_ref[...]
    g1 = 1.0 / (1.0 + jnp.exp(-(_dot(h, wg1_ref[...]) + b1_ref[...])))
    g2 = 1.0 / (1.0 + jnp.exp(-(_dot(h, wg2_ref[...]) + b2_ref[...])))
    o_ref[...] = (g1 * _dot(oa_ref[...], wa_ref[...]) + g2 * _dot(ob_ref[...], wb_ref[...])).astype(BF16)


def _merge(h_hi, o_a, o_b, w_gate, b_gate, w_pa, w_pb):
    n, d = h_hi.shape
    width = o_a.shape[1]
    tm = _tile(n, 1024, 16)
    tn = _tile(d, 512, LANES)
    nj = d // tn
    b_gate = b_gate.reshape(1, 2 * d)
    row = lambda i, j: (i, 0)
    return pl.pallas_call(
        _merge_kernel,
        grid=(n // tm, nj),
        in_specs=[pl.BlockSpec((tm, d), row), pl.BlockSpec((tm, width), row), pl.BlockSpec((tm, width), row),
                  pl.BlockSpec((d, tn), lambda i, j: (0, j)),
                  pl.BlockSpec((d, tn), lambda i, j: (0, j + nj)),
                  pl.BlockSpec((1, tn), lambda i, j: (0, j)),
                  pl.BlockSpec((1, tn), lambda i, j: (0, j + nj)),
                  pl.BlockSpec((width, tn), lambda i, j: (0, j)),
                  pl.BlockSpec((width, tn), lambda i, j: (0, j))],
        out_specs=pl.BlockSpec((tm, tn), lambda i, j: (i, j)),
        out_shape=jax.ShapeDtypeStruct((n, d), BF16),
        compiler_params=_params(("arbitrary", "arbitrary")),
        name="merge",
    )(h_hi, o_a, o_b, w_gate, w_gate, b_gate, b_gate, w_pa, w_pb)


def _outproj_kernel(m_ref, w_ref, x_ref, mod_ref, gm_ref, gf_ref, x1_ref, h2_ref):
    mod = mod_ref[0]
    x1 = x_ref[...] + mod[2:3, :] * _rms(_dot(m_ref[...], w_ref[...]), gm_ref[...])
    x1_ref[...] = x1
    h2_ref[...] = (_rms(x1, gf_ref[...]) * (1.0 + mod[4:5, :]) + mod[3:4, :]).astype(BF16)


def _outproj(merged, w_out, x2, mod, g_post_mix, g_pre_ffn, seq):
    n, d = x2.shape
    tm = _tile(seq, 512, 16)
    tps = seq // tm
    row = lambda i: (i, 0)
    fix = lambda i: (0, 0)
    return pl.pallas_call(
        _outproj_kernel,
        grid=(n // tm,),
        in_specs=[pl.BlockSpec((tm, d), row), pl.BlockSpec((d, d), fix, pipeline_mode=pl.Buffered(1)),
                  pl.BlockSpec((tm, d), row),
                  pl.BlockSpec((1, N_MOD, d), lambda i: (i // tps, 0, 0)),
                  pl.BlockSpec((1, d), fix), pl.BlockSpec((1, d), fix)],
        out_specs=[pl.BlockSpec((tm, d), row), pl.BlockSpec((tm, d), row)],
        out_shape=[jax.ShapeDtypeStruct((n, d), F32), jax.ShapeDtypeStruct((n, d), BF16)],
        compiler_params=_params(("arbitrary",)),
        name="outproj",
    )(merged, w_out, x2, mod, g_post_mix.reshape(1, d), g_pre_ffn.reshape(1, d))


def _gelu_tanh(x):
    return 0.5 * x * (1.0 + jnp.tanh(0.7978845608028654 * (x + 0.044715 * (x * x * x))))


def _ffn_kernel(h_ref, halo_ref, x1_ref, mod_ref, st_ref, wg_ref, wu_ref, wd_ref, wc_ref, bc_ref, gp_ref,
                y_ref, nc_ref, g_scr, act_scr, *, tm, nsub, tps):
    i = pl.program_id(0)
    c = pl.program_id(1)
    last = pl.num_programs(1) - 1
    sub = tm // nsub
    cur = c % 2

    def step(down, gate_up):
        if gate_up:
            h = h_ref[...]
            wg = wg_ref[...]
            g = _dot(h, wg)
            u = _dot(h, wu_ref[...])
            wc = wc_ref[...]
            bc = bc_ref[...]
            g_scr[HALO:HALO + tm, :] = g
        if down == "set":
            y_ref[...] = _dot(act_scr[1 - cur], wd_ref[...])
        elif down == "add":
            y_ref[...] += _dot(act_scr[1 - cur], wd_ref[...])
        if not gate_up:
            return

        def conv(lo, rows):
            return (bc + wc[0:1, :] * g_scr[lo - 2:lo - 2 + rows, :]
                    + wc[1:2, :] * g_scr[lo - 1:lo - 1 + rows, :] + wc[2:3, :] * g_scr[lo:lo + rows, :])

        if nsub == 1:
            g_halo = _dot(halo_ref[...], wg)
            g_scr[0:HALO, :] = g_halo
            g_scr[HALO - 2:HALO, :] = jnp.where(i % tps == 0, st_ref[0], g_halo[HALO - 2:HALO, :])
            nc_ref[0] = g_scr[HALO + tm - 2:HALO + tm, :]
            act_scr[cur] = (_gelu_tanh(conv(HALO, tm)) * u).astype(BF16)
        else:
            for s in range(nsub):
                lo = HALO + s * sub
                nc_ref[s] = g_scr[lo + sub - 2:lo + sub, :]
                g_scr[lo - 2:lo, :] = st_ref[s]
                act_scr[cur, s * sub:(s + 1) * sub, :] = (
                    _gelu_tanh(conv(lo, sub)) * u[s * sub:(s + 1) * sub, :]).astype(BF16)

    pl.when(c == 0)(lambda: step(None, True))
    pl.when(c == 1)(lambda: step("set", True))
    pl.when(jnp.logical_and(c > 1, c < last))(lambda: step("add", True))

    @pl.when(c == last)
    def _():
        step("add", False)
        for s in range(nsub):
            rows = slice(s * sub, (s + 1) * sub)
            y_ref[rows, :] = x1_ref[rows, :] + mod_ref[s][5:6, :] * _rms(y_ref[rows, :], gp_ref[...])


def _ffn(h2, x1, mod, state, w_gate, w_up, w_down, w_conv, b_conv, g_post, seq):
    n, d = x1.shape
    dff = w_gate.shape[1]
    nb = n // seq
    tm = _row_tile(n, seq, 1024)
    tf = _tile(dff, 512, LANES)
    n_chunks = dff // tf
    assert n_chunks >= 2, "the lagged down projection needs at least two d_ff chunks"
    up = lambda c: jnp.minimum(c, n_chunks - 1)
    down = lambda c: jnp.maximum(c - 1, 0)
    per_tile_c = lambda i, c: (i, 0, up(c))
    if tm <= seq:
        nsub, tps = 1, seq // tm
        per_seq = lambda i, c: (i // tps, 0, 0)
        per_seq_c = lambda i, c: (i // tps, 0, up(c))
    else:
        nsub, tps = tm // seq, 1
        per_seq = lambda i, c: (i, 0, 0)
        per_seq_c = per_tile_c
    halo_blocks = tm // HALO
    row = lambda i, c: (i, 0)
    y, new_conv = pl.pallas_call(
        functools.partial(_ffn_kernel, tm=tm, nsub=nsub, tps=tps),
        grid=(n // tm, n_chunks + 1),
        in_specs=[pl.BlockSpec((tm, d), row, pipeline_mode=pl.Buffered(1)),
                  pl.BlockSpec((HALO, d), lambda i, c: (jnp.maximum(i * halo_blocks - 1, 0), 0)),
                  pl.BlockSpec((tm, d), row, pipeline_mode=pl.Buffered(1)),
                  pl.BlockSpec((nsub, N_MOD, d), per_seq),
                  pl.BlockSpec((nsub, CONV_W - 1, tf), per_seq_c),
                  pl.BlockSpec((d, tf), lambda i, c: (0, up(c))),
                  pl.BlockSpec((d, tf), lambda i, c: (0, up(c))),
                  pl.BlockSpec((tf, d), lambda i, c: (down(c), 0)),
                  pl.BlockSpec((CONV_W, tf), lambda i, c: (0, up(c))),
                  pl.BlockSpec((1, tf), lambda i, c: (0, up(c))),
                  pl.BlockSpec((1, d), lambda i, c: (0, 0))],
        out_specs=[pl.BlockSpec((tm, d), row, pipeline_mode=pl.Buffered(1)),
                   pl.BlockSpec((nsub, CONV_W - 1, tf), per_tile_c)],
        out_shape=[jax.ShapeDtypeStruct((n, d), F32),
                   jax.ShapeDtypeStruct((nb * tps, CONV_W - 1, dff), F32)],
        scratch_shapes=[pltpu.VMEM((HALO + tm, tf), F32), pltpu.VMEM((2, tm, tf), BF16)],
        compiler_params=_params(("arbitrary", "arbitrary")),
        name="ffn",
    )(h2, h2, x1, mod, state, w_gate, w_up, w_down, w_conv, b_conv.reshape(1, dff), g_post.reshape(1, d))
    return y, new_conv.reshape(nb, tps, CONV_W - 1, dff)[:, tps - 1]


def _rope_tables(pos):
    def angles(dim):
        half = dim // 2
        inv = ROPE_THETA ** (-jnp.arange(half, dtype=F32) * 2.0 / dim)
        ang = pos.astype(F32)[:, None] * inv[None, :]
        return jnp.cos(ang), jnp.sin(ang)

    cos, sin = angles(HEAD_DIM)
    cos_h = jnp.concatenate([cos, cos], axis=1)
    sin_h = jnp.concatenate([-sin, sin], axis=1)

    cos, sin = angles(D_IDX)
    zero = jnp.zeros_like(sin)
    n_rot = H_IDX + 1
    tail = IDX_COLS - n_rot * D_IDX
    t = pos.shape[0]
    c_i = jnp.concatenate([jnp.tile(jnp.concatenate([cos, cos], 1), (1, n_rot)), jnp.ones((t, tail), F32)], 1)
    s1_i = jnp.concatenate([jnp.tile(jnp.concatenate([-sin, zero], 1), (1, n_rot)), jnp.zeros((t, tail), F32)], 1)
    s2_i = jnp.concatenate([jnp.tile(jnp.concatenate([zero, sin], 1), (1, n_rot)), jnp.zeros((t, tail), F32)], 1)
    return cos_h, sin_h, c_i, s1_i, s2_i


def _with_past(past, new, nb, seq, n_keys):
    w = new.shape[1]
    if past is None and n_keys == seq:
        return new
    parts = [] if past is None else [past.astype(new.dtype).reshape(nb, -1, w)]
    parts.append(new.reshape(nb, seq, w))
    have = sum(p.shape[1] for p in parts)
    if n_keys > have:
        parts.append(jnp.zeros((nb, n_keys - have, w), new.dtype))
    return jnp.concatenate(parts, axis=1).reshape(nb * n_keys, w)


def _layer(x, mod, past, conv_state, p):
    nb, seq, d = x.shape
    n = nb * seq
    nh = p["w_pa"].shape[0] // HEAD_DIM
    width = nh * HEAD_DIM
    n_past = 0 if past is None else past["a_k"].shape[1]
    n_valid = n_past + seq
    n_keys = -(-n_valid // 256) * 256
    x2 = x.reshape(n, d)

    h_hi, h_lo = _norm(x2, p["g_pre_mix"], mod, seq)
    cos_h, sin_h, c_i, s1_i, s2_i = _rope_tables(n_past + jnp.arange(seq, dtype=I32))
    p32, p16 = _inproj(h_hi, p["w_main"], cos_h, sin_h, seq, width)
    col = lambda a, g: a[:, g * width:(g + 1) * width]
    qi_hi, qi_lo, kw, ki_hi, ki_lo = _idxproj(h_hi, h_lo, p["w_idx_hi"], p["w_idx_lo"], c_i, s1_i, s2_i, seq)

    def keys(g, past_rows):
        if past_rows is None and n_keys == seq:
            return p16, g * width
        return _with_past(past_rows, col(p16, g), nb, seq, n_keys), 0

    q_hi = qi_hi.reshape(n, H_IDX, D_IDX)
    q_lo = qi_lo.reshape(n, H_IDX, D_IDX)
    qi_cat = jnp.concatenate([q_hi, q_hi, q_lo, jnp.zeros_like(q_hi)], axis=2).reshape(n, H_IDX * IDX_K)
    k_hi, k_lo = ki_hi[:, :D_IDX], ki_lo[:, :D_IDX]
    if past is None:
        pa_k = pa_v = pb_k = pb_v = pi_hi = pi_lo = None
    else:
        pa_k, pa_v, pb_k, pb_v = (past[k].reshape(nb, n_past, width) for k in ("a_k", "a_v", "b_k", "b_v"))
        pi_hi, pi_lo = _split(past["b_kidx"].astype(F32))
    k_hi = _with_past(pi_hi, k_hi, nb, seq, n_keys)
    k_lo = _with_past(pi_lo, k_lo, nb, seq, n_keys)
    ki_cat = jnp.concatenate([k_hi, k_lo, k_hi, jnp.zeros_like(k_hi)], axis=1)

    o_a = _sb((p16, 0), keys(1, pa_k), keys(2, pa_v), nb, seq, n_keys, n_past, nh)
    seq_q = max(seq, LANES)
    pad_q = lambda a: a if seq_q == seq else jnp.pad(
        a.reshape(nb, seq, -1), ((0, 0), (0, seq_q - seq), (0, 0))).reshape(nb * seq_q, -1)
    vt_all = _with_past(pb_v, col(p16, 5), nb, seq, n_keys).reshape(nb, n_keys, width)
    vt_all = jnp.swapaxes(vt_all, 1, 2).reshape(nb * width, n_keys)
    q_b = (p16, 3 * width) if seq_q == seq else (pad_q(col(p16, 3)), 0)
    o_b = _dsat(q_b, keys(4, pb_k), vt_all, pad_q(qi_cat), ki_cat,
                pad_q(kw).T, nb, seq_q, n_keys, n_valid, n_past, nh)
    if seq_q != seq:
        o_b = o_b.reshape(nb, seq_q, width)[:, :seq].reshape(n, width)

    merged = _merge(h_hi, o_a, o_b, p["w_gate"], p["b_gate"], p["w_pa"], p["w_pb"])
    x1, h2 = _outproj(merged, p["w_out"], x2, mod, p["g_post_mix"], p["g_pre_ffn"], seq)
    y, new_conv = _ffn(h2, x1, mod, conv_state, p["w_ffn_gate"], p["w_ffn_up"], p["w_ffn_down"],
                       p["w_conv"], p["b_conv"], p["g_post_ffn"], seq)

    heads = lambda g: col(p32, g).reshape(nb, seq, nh, HEAD_DIM)
    new = (heads(1), heads(2), heads(4), heads(5), kw[:, :D_IDX].reshape(nb, seq, D_IDX), new_conv)
    return y.reshape(nb, seq, d), new


def kernel(x_prompt, x_sample, c_prompt, c_sample, cache_a_k, cache_a_v, cache_b_k, cache_b_v, cache_b_kidx,
           state_ffn_conv, w_mod, b_mod, g_pre_mix, w_in, w_merge_gate, b_merge_gate, w_proj_a, w_proj_b, w_out,
           g_post_mix, g_pre_ffn, w_ffn_gate, w_ffn_up, w_conv, b_conv, w_ffn_down, g_post_ffn):
    depth = w_mod.shape[0]
    nbp, nbs = x_prompt.shape[0], x_sample.shape[0]
    d = x_prompt.shape[2]
    width = w_proj_a.shape[1]
    dff = w_ffn_gate.shape[2]
    assert w_in.shape[2] == 6 * width + H_IDX * D_IDX + D_IDX + H_IDX
    assert cache_b_kidx.shape[-1] == D_IDX and cache_a_k.shape[-1] == HEAD_DIM

    y_p, y_s = x_prompt, x_sample
    st_p = [[] for _ in range(6)]
    st_s = [[] for _ in range(6)]
    for l in range(depth):
        w_idx = jnp.pad(w_in[l][:, 6 * width:], ((0, 0), (0, IDX_COLS - (w_in.shape[2] - 6 * width))))
        w_idx_hi, w_idx_lo = _split(w_idx)
        p = dict(
            g_pre_mix=g_pre_mix[l], w_main=w_in[l][:, :6 * width].astype(BF16), w_idx_hi=w_idx_hi, w_idx_lo=w_idx_lo,
            w_gate=w_merge_gate[l].astype(BF16), b_gate=b_merge_gate[l],
            w_pa=w_proj_a[l].astype(BF16), w_pb=w_proj_b[l].astype(BF16), w_out=w_out[l].astype(BF16),
            g_post_mix=g_post_mix[l], g_pre_ffn=g_pre_ffn[l],
            w_ffn_gate=w_ffn_gate[l].astype(BF16), w_ffn_up=w_ffn_up[l].astype(BF16),
            w_ffn_down=w_ffn_down[l].astype(BF16), w_conv=w_conv[l], b_conv=b_conv[l], g_post_ffn=g_post_ffn[l])

        c_all = jnp.concatenate([c_prompt, c_sample], axis=0)
        pad = -c_all.shape[0] % 16
        mod = _mod(jnp.pad(c_all, ((0, pad), (0, 0))), w_mod[l], b_mod[l]).reshape(-1, N_MOD, d)
        past = dict(a_k=cache_a_k[l], a_v=cache_a_v[l], b_k=cache_b_k[l], b_v=cache_b_v[l], b_kidx=cache_b_kidx[l])
        y_p, new_p = _layer(y_p, mod[:nbp], None, jnp.zeros((nbp, CONV_W - 1, dff), F32), p)
        y_s, new_s = _layer(y_s, mod[nbp:nbp + nbs], past, state_ffn_conv[l], p)
        for i in range(6):
            st_p[i].append(new_p[i])
            st_s[i].append(new_s[i])
    sp = [jnp.stack(s, axis=0) for s in st_p]
    ss = [jnp.stack(s, axis=0) for s in st_s]
    return (y_p, y_s, sp[0], sp[1], sp[2], sp[3], sp[4], sp[5], ss[0], ss[1], ss[2], ss[3], ss[4], ss[5])
```

```python
import functools

import jax
import jax.numpy as jnp
from jax import lax
from jax.experimental import pallas as pl
from jax.experimental.pallas import tpu as pltpu

F32 = jnp.float32
BF16 = jnp.bfloat16
I32 = jnp.int32

CHUNK = 64
HEAD_DIM = 128
H_IDX = 4
D_IDX = 64
TOPK_MAX = 256
CONV_W = 3
ROPE_THETA = 10000.0
RMS_EPS = 1e-6
N_MOD = 6
IDX_COLS = 384
IDX_K = 256
LANES = 128
HALO = 16
ONES_ROWS = 16
LOG2E = 1.4426950408889634
UNDERFLOW_LOG2 = 160.0
NEG = -1e30
INT_MIN = -2 ** 31
VMEM_LIMIT = 56 * 1024 * 1024


def _tile(n, pref, mult):
    t = min(pref, n)
    t -= t % mult
    while t >= mult:
        if n % t == 0:
            return t
        t -= mult
    return n


def _row_tile(n, seq, pref):
    if seq >= pref or n == seq:
        return _tile(seq, pref, 16)
    return _tile(n, pref, seq)


def _params(sem):
    return pltpu.CompilerParams(dimension_semantics=sem, vmem_limit_bytes=VMEM_LIMIT)


def _split(x):
    hi = x.astype(BF16)
    lo = (x - hi.astype(F32)).astype(BF16)
    return hi, lo


def _dot(a, b):
    return jnp.dot(a, b, preferred_element_type=F32)


def _dot_t(a, b):
    return lax.dot_general(a, b, (((1,), (1,)), ((), ())), preferred_element_type=F32)


def _rms(x, g):
    ms = jnp.mean(x * x, axis=-1, keepdims=True)
    return x * lax.rsqrt(ms + RMS_EPS) * g


def _mod_kernel(c_ref, w_ref, b_ref, o_ref):
    c = c_ref[...]
    s = c / (1.0 + jnp.exp(-c))
    s_hi, s_lo = _split(s)
    w_hi, w_lo = _split(w_ref[...])
    o_ref[...] = _dot(s_hi, w_hi) + _dot(s_hi, w_lo) + _dot(s_lo, w_hi) + b_ref[...]


def _mod(c, w, b):
    bc, d = c.shape
    n = w.shape[1]
    tn = _tile(n, 1024, LANES)
    return pl.pallas_call(
        _mod_kernel,
        grid=(n // tn,),
        in_specs=[pl.BlockSpec((bc, d), lambda j: (0, 0)),
                  pl.BlockSpec((d, tn), lambda j: (0, j)),
                  pl.BlockSpec((1, tn), lambda j: (0, j))],
        out_specs=pl.BlockSpec((bc, tn), lambda j: (0, j)),
        out_shape=jax.ShapeDtypeStruct((bc, n), F32),
        compiler_params=_params(("arbitrary",)),
        name="mod",
    )(c, w, b.reshape(1, n))


def _norm_kernel(x_ref, g_ref, mod_ref, hi_ref, lo_ref):
    m = mod_ref[0]
    h = _rms(x_ref[...], g_ref[...]) * (1.0 + m[1:2, :]) + m[0:1, :]
    hi, lo = _split(h)
    hi_ref[...] = hi
    lo_ref[...] = lo


def _norm(x2, g, mod, seq):
    n, d = x2.shape
    tr = _tile(seq, 512, 16)
    tps = seq // tr
    return pl.pallas_call(
        _norm_kernel,
        grid=(n // tr,),
        in_specs=[pl.BlockSpec((tr, d), lambda i: (i, 0)),
                  pl.BlockSpec((1, d), lambda i: (0, 0)),
                  pl.BlockSpec((1, N_MOD, d), lambda i: (i // tps, 0, 0))],
        out_specs=[pl.BlockSpec((tr, d), lambda i: (i, 0))] * 2,
        out_shape=[jax.ShapeDtypeStruct((n, d), BF16)] * 2,
        compiler_params=_params(("arbitrary",)),
        name="norm",
    )(x2, g.reshape(1, d), mod)


N_PROJ = 6


KV_GROUPS = (1, 2, 4, 5)


def _inproj_kernel(a_ref, w_ref, c_ref, s_ref, p16_ref, ka_ref, va_ref, kb_ref, vb_ref, r_scr, *, tm, nh):
    acc = _dot(a_ref[...], w_ref[...])
    c = c_ref[0]
    s = s_ref[0]
    for k in range(nh):
        sl = slice(k * HEAD_DIM, (k + 1) * HEAD_DIM)
        xs = acc[:, sl]
        r = xs * c + pltpu.roll(xs, HEAD_DIM // 2, 1) * s
        r_scr[:, sl] = r
        p16_ref[:, sl] = r.astype(BF16)
    for grp, ref in zip(KV_GROUPS, (ka_ref, va_ref, kb_ref, vb_ref)):
        @pl.when(pl.program_id(1) == grp)
        def _(ref=ref):
            for k in range(nh):
                ref[pl.ds(k, tm, stride=nh), :] = r_scr[:, k * HEAD_DIM:(k + 1) * HEAD_DIM]


def _inproj(h_hi, w_main, cos_t, sin_t, seq, width):
    n, d = h_hi.shape
    nh = width // HEAD_DIM
    tm = _row_tile(n, seq, 512)
    tn = width
    npj = 1
    scale = LOG2E * HEAD_DIM ** -0.5
    one, zero = jnp.ones_like(cos_t), jnp.zeros_like(cos_t)
    c_t = jnp.stack([scale * one, one, one, scale * cos_t, cos_t, one])
    s_t = jnp.stack([zero, zero, zero, scale * sin_t, sin_t, zero])
    if tm <= seq:
        tps = seq // tm
        tab_map = lambda i, j: (j // npj, i % tps, 0)
    else:
        c_t = jnp.tile(c_t, (1, tm // seq, 1))
        s_t = jnp.tile(s_t, (1, tm // seq, 1))
        tab_map = lambda i, j: (j // npj, 0, 0)
    out = pl.pallas_call(
        functools.partial(_inproj_kernel, tm=tm, nh=nh),
        grid=(n // tm, N_PROJ * npj),
        in_specs=[pl.BlockSpec((tm, d), lambda i, j: (i, 0)),
                  pl.BlockSpec((d, tn), lambda i, j: (0, j)),
                  pl.BlockSpec((1, tm, HEAD_DIM), tab_map),
                  pl.BlockSpec((1, tm, HEAD_DIM), tab_map)],
        out_specs=[pl.BlockSpec((tm, tn), lambda i, j: (i, j))]
        + [pl.BlockSpec((tm * nh, HEAD_DIM), lambda i, j: (i, 0))] * len(KV_GROUPS),
        out_shape=[jax.ShapeDtypeStruct((n, N_PROJ * width), BF16)]
        + [jax.ShapeDtypeStruct((n * nh, HEAD_DIM), F32)] * len(KV_GROUPS),
        scratch_shapes=[pltpu.VMEM((tm, tn), F32)],
        compiler_params=_params(("arbitrary", "arbitrary")),
        name="inproj",
    )(h_hi, w_main, c_t, s_t)
    return out[0], dict(zip(KV_GROUPS, out[1:]))


def _idxproj_kernel(hi_ref, lo_ref, whi_ref, wlo_ref, c_ref, s1_ref, s2_ref,
                    qhi_ref, qlo_ref, kw_ref, khi_ref, klo_ref):
    a_hi = hi_ref[...]
    w_hi = whi_ref[...]
    acc = _dot(a_hi, w_hi) + _dot(a_hi, wlo_ref[...]) + _dot(lo_ref[...], w_hi)
    for s in range(IDX_COLS // LANES):
        sl = slice(s * LANES, (s + 1) * LANES)
        xs = acc[:, sl]
        r = (xs * c_ref[:, sl] + pltpu.roll(xs, LANES - D_IDX // 2, 1) * s1_ref[:, sl]
             + pltpu.roll(xs, D_IDX // 2, 1) * s2_ref[:, sl])
        hi, lo = _split(r)
        if s < 2:
            qhi_ref[:, sl] = hi
            qlo_ref[:, sl] = lo
        else:
            kw_ref[...] = r
            khi_ref[...] = hi
            klo_ref[...] = lo


def _idxproj(h_hi, h_lo, w_hi, w_lo, c_t, s1_t, s2_t, seq):
    n, d = h_hi.shape
    tm = _row_tile(n, seq, 512)
    if tm <= seq:
        tps = seq // tm
        tab_map = lambda i: (i % tps, 0)
    else:
        c_t, s1_t, s2_t = (jnp.tile(t, (tm // seq, 1)) for t in (c_t, s1_t, s2_t))
        tab_map = lambda i: (0, 0)
    row = lambda i: (i, 0)
    return pl.pallas_call(
        _idxproj_kernel,
        grid=(n // tm,),
        in_specs=[pl.BlockSpec((tm, d), row), pl.BlockSpec((tm, d), row),
                  pl.BlockSpec((d, IDX_COLS), lambda i: (0, 0)),
                  pl.BlockSpec((d, IDX_COLS), lambda i: (0, 0)),
                  pl.BlockSpec((tm, IDX_COLS), tab_map),
                  pl.BlockSpec((tm, IDX_COLS), tab_map),
                  pl.BlockSpec((tm, IDX_COLS), tab_map)],
        out_specs=[pl.BlockSpec((tm, 2 * LANES), row), pl.BlockSpec((tm, 2 * LANES), row),
                   pl.BlockSpec((tm, LANES), row), pl.BlockSpec((tm, LANES), row),
                   pl.BlockSpec((tm, LANES), row)],
        out_shape=[jax.ShapeDtypeStruct((n, 2 * LANES), BF16), jax.ShapeDtypeStruct((n, 2 * LANES), BF16),
                   jax.ShapeDtypeStruct((n, LANES), F32), jax.ShapeDtypeStruct((n, LANES), BF16),
                   jax.ShapeDtypeStruct((n, LANES), BF16)],
        compiler_params=_params(("arbitrary",)),
        name="idxproj",
    )(h_hi, h_lo, w_hi, w_lo, c_t, s1_t, s2_t)


def _sb_kernel(q_ref, k_ref, v_ref, o_ref, acc_scr, z_scr, i_scr, *, tq, tk, past, group):
    q0 = past + pl.program_id(2) * tq
    n_kt = (q0 + tq - 2) // tk + 1
    n_clear = q0 // tk
    row_pos = q0 + lax.broadcasted_iota(I32, (tq, 1), 0)
    col = lax.broadcasted_iota(I32, (1, tk), 1)
    suffix = jnp.where(lax.broadcasted_iota(I32, (tk, tk), 0) >= lax.broadcasted_iota(I32, (tk, tk), 1),
                       1.0, 0.0).astype(BF16)
    suffix2 = jnp.concatenate([suffix, suffix], axis=0)
    acc_scr[...] = jnp.zeros_like(acc_scr)

    heads = [slice(h * HEAD_DIM, (h + 1) * HEAD_DIM) for h in range(group)]

    def tiles(js, carries, masked):
        offs = [pl.multiple_of(j * tk, tk) for j in js]
        if masked:
            valids = [(off + col) < row_pos for off in offs]
        for t, off in enumerate(offs):
            for h, hs in enumerate(heads):
                z_scr[t * group + h] = _dot_t(q_ref[:, hs], k_ref[pl.ds(off, tk), hs])
        for t in range(len(js)):
            for h in range(group):
                z = z_scr[t * group + h]
                sp = jnp.maximum(z, 0.0) + jnp.log2(1.0 + jnp.exp2(-jnp.abs(z)))
                if masked:
                    sp = jnp.where(valids[t], sp, 0.0)
                hi, lo = _split(sp)
                i_scr[t * group + h] = _dot(jnp.concatenate([hi, lo], axis=1), suffix2)
        carries = list(carries)
        for t, off in enumerate(offs):
            for h, hs in enumerate(heads):
                w = jnp.exp2(z_scr[t * group + h] - i_scr[t * group + h] - carries[h])
                if masked:
                    w = jnp.where(valids[t], w, 0.0)
                acc_scr[h] += _dot(w.astype(BF16), v_ref[pl.ds(off, tk), hs])
                carries[h] = carries[h] + i_scr[t * group + h, :, 0:1]
        return tuple(carries)

    def least(carries):
        m = carries[0]
        for c in carries[1:]:
            m = jnp.minimum(m, c)
        return jnp.min(m)

    st = tuple(jnp.zeros((tq, 1), F32) for _ in range(group))
    st = lax.fori_loop(0, n_kt - n_clear, lambda t, c: tiles([n_kt - 1 - t], c, True), st)

    def more(state):
        t, low, _ = state
        return jnp.logical_and(t < n_clear, low < UNDERFLOW_LOG2)

    def step(state):
        t, _, carries = state
        carries = tiles([n_clear - 1 - t], carries, False)
        return t + 1, least(carries), carries

    lax.while_loop(more, step, (jnp.int32(0), least(st), st))
    for h in range(group):
        o_ref[:, h * HEAD_DIM:(h + 1) * HEAD_DIM] = acc_scr[h].astype(o_ref.dtype)


def _sb(q, k_all, v_all, nb, seq, n_keys, past, nh):
    tq = _tile(seq, 256, 16)
    tk = 256
    nq = seq // tq
    group = 4 if nh % 4 == 0 else (2 if nh % 2 == 0 else 1)
    gw = group * HEAD_DIM
    (q, q0), (k_all, k0), (v_all, v0) = ((a, c // gw) for a, c in (q, k_all, v_all))
    return pl.pallas_call(
        functools.partial(_sb_kernel, tq=tq, tk=tk, past=past, group=group),
        grid=(nb, nh // group, nq),
        in_specs=[pl.BlockSpec((tq, gw), lambda b, h, i: (b * nq + i, q0 + h)),
                  pl.BlockSpec((n_keys, gw), lambda b, h, i: (b, k0 + h)),
                  pl.BlockSpec((n_keys, gw), lambda b, h, i: (b, v0 + h))],
        out_specs=pl.BlockSpec((tq, gw), lambda b, h, i: (b * nq + i, h)),
        out_shape=jax.ShapeDtypeStruct((q.shape[0], nh * HEAD_DIM), BF16),
        scratch_shapes=[pltpu.VMEM((group, tq, HEAD_DIM), F32), pltpu.VMEM((group, tq, tk), F32),
                        pltpu.VMEM((group, tq, tk), F32)],
        compiler_params=_params(("arbitrary", "arbitrary", "arbitrary")),
        name="sb",
    )(q, k_all, v_all)


def _fold8(parts, op):
    while len(parts) > 1:
        parts = [op(parts[i], parts[i + 1]) if i + 1 < len(parts) else parts[i] for i in range(0, len(parts), 2)]
    return parts[0]


def _dsat_kernel(q_ref, k_ref, vt_ref, qi_ref, ki_ref, kwt_ref, o_ref, key_scr, acc_scr, s_scr, p_scr,
                 *, tq, tk, past, n_valid, n_sel, nh):
    q0 = past + pl.program_id(1) * tq
    col_pos = q0 + lax.broadcasted_iota(I32, (1, tq), 1)
    col_lim = jnp.minimum((col_pos // CHUNK + 1) * CHUNK, n_valid)
    n_kt = (jnp.minimum(((q0 + tq - 1) // CHUNK + 1) * CHUNK, n_valid) + tk - 1) // tk
    row = lax.broadcasted_iota(I32, (tk, 1), 0)
    ksel = float(n_sel)
    groups = [slice(8 * r, 8 * r + 8) for r in range(tk // 8)]

    wrow = kwt_ref[D_IDX:D_IDX + H_IDX, :] * ((H_IDX ** -0.5) * (D_IDX ** -0.5))

    def score_pair(t, _):
        offs = [pl.multiple_of(jnp.minimum(2 * t + u, n_kt - 1) * tk, tk) for u in range(2)]
        raw = [[_dot_t(ki_ref[pl.ds(off, tk), :], qi_ref[:, h * IDX_K:(h + 1) * IDX_K]) for h in range(H_IDX)]
               for off in offs]
        for off, dots in zip(offs, raw):
            score = jnp.zeros((tk, tq), F32)
            for h, s in enumerate(dots):
                score = score + wrow[h:h + 1, :] * jnp.maximum(s, 0.0)
            bits = pltpu.bitcast(score, I32)
            key = jnp.where(bits < 0, INT_MIN - bits, bits)
            key_scr[pl.ds(off, tk), :] = jnp.where((off + row) < col_lim, key, INT_MIN)
        return 0

    lax.fori_loop(0, (n_kt + 1) // 2, score_pair, 0)

    def count_ge(thr8):
        def body(t, cnt):
            for u in range(2):
                j = 2 * t + u
                off = pl.multiple_of(jnp.minimum(j, n_kt - 1) * tk, tk)
                key = key_scr[pl.ds(off, tk), :]
                part = _fold8([jnp.where(key[g, :] >= thr8, 1.0, 0.0) for g in groups], jnp.add)
                cnt = cnt + (part if u == 0 else jnp.where(j < n_kt, part, 0.0))
            return cnt

        cnt = lax.fori_loop(0, (n_kt + 1) // 2, body, jnp.zeros((8, tq), F32))
        return jnp.broadcast_to(jnp.sum(cnt, axis=0, keepdims=True), (8, tq))

    def bit_step(it, st):
        cand, have = st
        trial = cand | lax.shift_left(jnp.int32(1), 30 - it)
        cnt = count_ge(trial)
        take = cnt >= ksel
        return jnp.where(take, trial, cand), jnp.where(take, cnt, have)

    zero8 = jnp.zeros((8, tq), I32)
    cnt0 = count_ge(zero8)
    start = (jnp.where(cnt0 >= ksel, zero8, INT_MIN), jnp.where(cnt0 >= ksel, cnt0, float(2 ** 30)))
    thr8, have8 = lax.fori_loop(0, 31, bit_step, start)
    thr = thr8[0:1, :]
    overfull = jnp.max(jnp.where(have8 > ksel, 1.0, 0.0)) > 0.0

    @pl.when(jnp.logical_not(overfull))
    def _():
        def bias_tile(j, _):
            off = pl.multiple_of(j * tk, tk)
            bias = jnp.where(key_scr[pl.ds(off, tk), :] >= thr, 0.0, NEG)
            key_scr[pl.ds(off, tk), :] = pltpu.bitcast(bias, I32)
            return 0

        lax.fori_loop(0, n_kt, bias_tile, 0)

    @pl.when(overfull)
    def _():
        no_thr8 = thr8 == INT_MIN
        cnt_gt = count_ge(jnp.where(no_thr8, thr8, thr8 + 1))
        need = jnp.where(no_thr8, 0.0, ksel - cnt_gt)[0:1, :]
        prefix = jnp.where(lax.broadcasted_iota(I32, (tk, tk), 0) >= lax.broadcasted_iota(I32, (tk, tk), 1),
                           1.0, 0.0).astype(BF16)

        def bias_tile(j, carry):
            off = pl.multiple_of(j * tk, tk)
            key = key_scr[pl.ds(off, tk), :]
            eq = key == thr
            eqf = jnp.where(eq, 1.0, 0.0)
            rank_incl = _dot(prefix, eqf.astype(BF16)) + carry
            tie_ok = (rank_incl - eqf) < need
            bias = jnp.where(key > thr, 0.0, jnp.where(eq, jnp.where(tie_ok, 0.0, NEG), NEG))
            key_scr[pl.ds(off, tk), :] = pltpu.bitcast(bias, I32)
            return rank_incl[tk - 1:tk, :]

        lax.fori_loop(0, n_kt, bias_tile, jnp.zeros((1, tq), F32))

    acc_scr[...] = jnp.zeros_like(acc_scr)
    ones_v = jnp.ones((ONES_ROWS, tk), BF16)
    heads = [slice(h * HEAD_DIM, (h + 1) * HEAD_DIM) for h in range(nh)]

    def logits(j, slot, ms):
        off = pl.multiple_of(j * tk, tk)
        bias = pltpu.bitcast(key_scr[pl.ds(off, tk), :], F32)
        for h, hs in enumerate(heads):
            s_scr[slot, h] = _dot_t(k_ref[pl.ds(off, tk), hs], q_ref[:, hs]) + bias
        out = []
        for h, m in enumerate(ms):
            s = s_scr[slot, h]
            top = jnp.max(_fold8([s[g, :] for g in groups], jnp.maximum), axis=0, keepdims=True)
            out.append(jnp.maximum(m, top))
        return tuple(out)

    def accumulate(j, slot, m_old, m_new):
        off = pl.multiple_of(j * tk, tk)
        for h in range(nh):
            p_scr[h] = jnp.exp2(s_scr[slot, h] - m_new[h]).astype(BF16)
        for h, hs in enumerate(heads):
            vt_ext = jnp.concatenate([vt_ref[hs, pl.ds(off, tk)], ones_v], axis=0)
            acc_scr[h] = jnp.exp2(m_old[h] - m_new[h]) * acc_scr[h] + _dot(vt_ext, p_scr[h])

    m_none = tuple(jnp.full((1, tq), NEG, F32) for _ in range(nh))
    m_first = logits(0, 0, m_none)

    def attn_step(j, st):
        m_old, m_cur = st
        slot = j % 2
        accumulate(j - 1, 1 - slot, m_old, m_cur)
        return m_cur, logits(j, slot, m_cur)

    m_old, m_cur = lax.fori_loop(1, n_kt, attn_step, (m_none, m_first))
    accumulate(n_kt - 1, (n_kt - 1) % 2, m_old, m_cur)
    for h, hs in enumerate(heads):
        out_t = acc_scr[h, :HEAD_DIM, :] / acc_scr[h, HEAD_DIM:HEAD_DIM + 1, :]
        o_ref[:, hs] = out_t.T.astype(o_ref.dtype)


def _dsat(q, k_all, vt_all, qi_cat, ki_cat, kwt, nb, seq, n_keys, n_valid, past, nh):
    tq = _tile(seq, 256, LANES)
    tk = 256
    nq = seq // tq
    width = nh * HEAD_DIM
    n_sel = min(TOPK_MAX, n_valid // 4)
    once = pl.Buffered(1)
    (q, q0), (k_all, k0) = ((a, c // width) for a, c in (q, k_all))
    return pl.pallas_call(
        functools.partial(_dsat_kernel, tq=tq, tk=tk, past=past, n_valid=n_valid, n_sel=n_sel, nh=nh),
        grid=(nb, nq),
        in_specs=[pl.BlockSpec((tq, width), lambda b, i: (b * nq + i, q0)),
                  pl.BlockSpec((n_keys, width), lambda b, i: (b, k0), pipeline_mode=once),
                  pl.BlockSpec((width, n_keys), lambda b, i: (b, 0), pipeline_mode=once),
                  pl.BlockSpec((tq, H_IDX * IDX_K), lambda b, i: (b * nq + i, 0)),
                  pl.BlockSpec((n_keys, IDX_K), lambda b, i: (b, 0), pipeline_mode=once),
                  pl.BlockSpec((LANES, tq), lambda b, i: (0, b * nq + i))],
        out_specs=pl.BlockSpec((tq, width), lambda b, i: (b * nq + i, 0)),
        out_shape=jax.ShapeDtypeStruct((q.shape[0], width), BF16),
        scratch_shapes=[pltpu.VMEM((n_keys, tq), I32), pltpu.VMEM((nh, HEAD_DIM + ONES_ROWS, tq), F32),
                        pltpu.VMEM((2, nh, tk, tq), F32), pltpu.VMEM((nh, tk, tq), BF16)],
        compiler_params=_params(("arbitrary", "arbitrary")),
        name="dsa",
    )(q, k_all, vt_all, qi_cat, ki_cat, kwt)


def _merge_kernel(h_ref, oa_ref, ob_ref, wg1_ref, wg2_ref, b1_ref, b2_ref, wa_ref, wb_ref, o_ref):
    h = h_ref[...]
    g1 = 1.0 / (1.0 + jnp.exp(-(_dot(h, wg1_ref[...]) + b1_ref[...])))
    g2 = 1.0 / (1.0 + jnp.exp(-(_dot(h, wg2_ref[...]) + b2_ref[...])))
    o_ref[...] = (g1 * _dot(oa_ref[...], wa_ref[...]) + g2 * _dot(ob_ref[...], wb_ref[...])).astype(BF16)


def _merge(h_hi, o_a, o_b, w_gate, b_gate, w_pa, w_pb):
    n, d = h_hi.shape
    width = o_a.shape[1]
    tm = _tile(n, 1024, 16)
    tn = _tile(d, 512, LANES)
    nj = d // tn
    b_gate = b_gate.reshape(1, 2 * d)
    row = lambda i, j: (i, 0)
    return pl.pallas_call(
        _merge_kernel,
        grid=(n // tm, nj),
        in_specs=[pl.BlockSpec((tm, d), row), pl.BlockSpec((tm, width), row), pl.BlockSpec((tm, width), row),
                  pl.BlockSpec((d, tn), lambda i, j: (0, j)),
                  pl.BlockSpec((d, tn), lambda i, j: (0, j + nj)),
                  pl.BlockSpec((1, tn), lambda i, j: (0, j)),
                  pl.BlockSpec((1, tn), lambda i, j: (0, j + nj)),
                  pl.BlockSpec((width, tn), lambda i, j: (0, j)),
                  pl.BlockSpec((width, tn), lambda i, j: (0, j))],
        out_specs=pl.BlockSpec((tm, tn), lambda i, j: (i, j)),
        out_shape=jax.ShapeDtypeStruct((n, d), BF16),
        compiler_params=_params(("arbitrary", "arbitrary")),
        name="merge",
    )(h_hi, o_a, o_b, w_gate, w_gate, b_gate, b_gate, w_pa, w_pb)


def _outproj_kernel(m_ref, w_ref, x_ref, mod_ref, gm_ref, gf_ref, x1_ref, h2_ref):
    mod = mod_ref[0]
    x1 = x_ref[...] + mod[2:3, :] * _rms(_dot(m_ref[...], w_ref[...]), gm_ref[...])
    x1_ref[...] = x1
    h2_ref[...] = (_rms(x1, gf_ref[...]) * (1.0 + mod[4:5, :]) + mod[3:4, :]).astype(BF16)


def _outproj(merged, w_out, x2, mod, g_post_mix, g_pre_ffn, seq):
    n, d = x2.shape
    tm = _tile(seq, 512, 16)
    tps = seq // tm
    row = lambda i: (i, 0)
    fix = lambda i: (0, 0)
    return pl.pallas_call(
        _outproj_kernel,
        grid=(n // tm,),
        in_specs=[pl.BlockSpec((tm, d), row), pl.BlockSpec((d, d), fix, pipeline_mode=pl.Buffered(1)),
                  pl.BlockSpec((tm, d), row),
                  pl.BlockSpec((1, N_MOD, d), lambda i: (i // tps, 0, 0)),
                  pl.BlockSpec((1, d), fix), pl.BlockSpec((1, d), fix)],
        out_specs=[pl.BlockSpec((tm, d), row), pl.BlockSpec((tm, d), row)],
        out_shape=[jax.ShapeDtypeStruct((n, d), F32), jax.ShapeDtypeStruct((n, d), BF16)],
        compiler_params=_params(("arbitrary",)),
        name="outproj",
    )(merged, w_out, x2, mod, g_post_mix.reshape(1, d), g_pre_ffn.reshape(1, d))


def _gelu_tanh(x):
    return 0.5 * x * (1.0 + jnp.tanh(0.7978845608028654 * (x + 0.044715 * (x * x * x))))


def _ffn_kernel(h_ref, halo_ref, x1_ref, mod_ref, st_ref, wg_ref, wu_ref, wd_ref, wc_ref, bc_ref, gp_ref,
                y_ref, nc_ref, g_scr, act_scr, *, tm, nsub, tps):
    i = pl.program_id(0)
    c = pl.program_id(1)
    last = pl.num_programs(1) - 1
    sub = tm // nsub
    cur = c % 2

    def step(down, gate_up):
        if gate_up:
            h = h_ref[...]
            wg = wg_ref[...]
            g = _dot(h, wg)
            u = _dot(h, wu_ref[...])
            wc = wc_ref[...]
            bc = bc_ref[...]
            g_scr[HALO:HALO + tm, :] = g
        if down == "set":
            y_ref[...] = _dot(act_scr[1 - cur], wd_ref[...])
        elif down == "add":
            y_ref[...] += _dot(act_scr[1 - cur], wd_ref[...])
        if not gate_up:
            return

        def conv(lo, rows):
            return (bc + wc[0:1, :] * g_scr[lo - 2:lo - 2 + rows, :]
                    + wc[1:2, :] * g_scr[lo - 1:lo - 1 + rows, :] + wc[2:3, :] * g_scr[lo:lo + rows, :])

        if nsub == 1:
            g_halo = _dot(halo_ref[...], wg)
            g_scr[0:HALO, :] = g_halo
            g_scr[HALO - 2:HALO, :] = jnp.where(i % tps == 0, st_ref[0], g_halo[HALO - 2:HALO, :])
            nc_ref[0] = g_scr[HALO + tm - 2:HALO + tm, :]
            act_scr[cur] = (_gelu_tanh(conv(HALO, tm)) * u).astype(BF16)
        else:
            for s in range(nsub):
                lo = HALO + s * sub
                nc_ref[s] = g_scr[lo + sub - 2:lo + sub, :]
                g_scr[lo - 2:lo, :] = st_ref[s]
                act_scr[cur, s * sub:(s + 1) * sub, :] = (
                    _gelu_tanh(conv(lo, sub)) * u[s * sub:(s + 1) * sub, :]).astype(BF16)

    pl.when(c == 0)(lambda: step(None, True))
    pl.when(c == 1)(lambda: step("set", True))
    pl.when(jnp.logical_and(c > 1, c < last))(lambda: step("add", True))

    @pl.when(c == last)
    def _():
        step("add", False)
        for s in range(nsub):
            rows = slice(s * sub, (s + 1) * sub)
            y_ref[rows, :] = x1_ref[rows, :] + mod_ref[s][5:6, :] * _rms(y_ref[rows, :], gp_ref[...])


def _ffn(h2, x1, mod, state, w_gate, w_up, w_down, w_conv, b_conv, g_post, seq):
    n, d = x1.shape
    dff = w_gate.shape[1]
    nb = n // seq
    tm = _row_tile(n, seq, 1024)
    tf = _tile(dff, 512, LANES)
    n_chunks = dff // tf
    assert n_chunks >= 2, "the lagged down projection needs at least two d_ff chunks"
    up = lambda c: jnp.minimum(c, n_chunks - 1)
    down = lambda c: jnp.maximum(c - 1, 0)
    per_tile_c = lambda i, c: (i, 0, up(c))
    if tm <= seq:
        nsub, tps = 1, seq // tm
        per_seq = lambda i, c: (i // tps, 0, 0)
        per_seq_c = lambda i, c: (i // tps, 0, up(c))
    else:
        nsub, tps = tm // seq, 1
        per_seq = lambda i, c: (i, 0, 0)
        per_seq_c = per_tile_c
    halo_blocks = tm // HALO
    row = lambda i, c: (i, 0)
    y, new_conv = pl.pallas_call(
        functools.partial(_ffn_kernel, tm=tm, nsub=nsub, tps=tps),
        grid=(n // tm, n_chunks + 1),
        in_specs=[pl.BlockSpec((tm, d), row, pipeline_mode=pl.Buffered(1)),
                  pl.BlockSpec((HALO, d), lambda i, c: (jnp.maximum(i * halo_blocks - 1, 0), 0)),
                  pl.BlockSpec((tm, d), row, pipeline_mode=pl.Buffered(1)),
                  pl.BlockSpec((nsub, N_MOD, d), per_seq),
                  pl.BlockSpec((nsub, CONV_W - 1, tf), per_seq_c),
                  pl.BlockSpec((d, tf), lambda i, c: (0, up(c))),
                  pl.BlockSpec((d, tf), lambda i, c: (0, up(c))),
                  pl.BlockSpec((tf, d), lambda i, c: (down(c), 0)),
                  pl.BlockSpec((CONV_W, tf), lambda i, c: (0, up(c))),
                  pl.BlockSpec((1, tf), lambda i, c: (0, up(c))),
                  pl.BlockSpec((1, d), lambda i, c: (0, 0))],
        out_specs=[pl.BlockSpec((tm, d), row, pipeline_mode=pl.Buffered(1)),
                   pl.BlockSpec((nsub, CONV_W - 1, tf), per_tile_c)],
        out_shape=[jax.ShapeDtypeStruct((n, d), F32),
                   jax.ShapeDtypeStruct((nb * tps, CONV_W - 1, dff), F32)],
        scratch_shapes=[pltpu.VMEM((HALO + tm, tf), F32), pltpu.VMEM((2, tm, tf), BF16)],
        compiler_params=_params(("arbitrary", "arbitrary")),
        name="ffn",
    )(h2, h2, x1, mod, state, w_gate, w_up, w_down, w_conv, b_conv.reshape(1, dff), g_post.reshape(1, d))
    return y, new_conv.reshape(nb, tps, CONV_W - 1, dff)[:, tps - 1]


def _rope_tables(pos):
    def angles(dim):
        half = dim // 2
        inv = ROPE_THETA ** (-jnp.arange(half, dtype=F32) * 2.0 / dim)
        ang = pos.astype(F32)[:, None] * inv[None, :]
        return jnp.cos(ang), jnp.sin(ang)

    cos, sin = angles(HEAD_DIM)
    cos_h = jnp.concatenate([cos, cos], axis=1)
    sin_h = jnp.concatenate([-sin, sin], axis=1)

    cos, sin = angles(D_IDX)
    zero = jnp.zeros_like(sin)
    n_rot = H_IDX + 1
    tail = IDX_COLS - n_rot * D_IDX
    t = pos.shape[0]
    c_i = jnp.concatenate([jnp.tile(jnp.concatenate([cos, cos], 1), (1, n_rot)), jnp.ones((t, tail), F32)], 1)
    s1_i = jnp.concatenate([jnp.tile(jnp.concatenate([-sin, zero], 1), (1, n_rot)), jnp.zeros((t, tail), F32)], 1)
    s2_i = jnp.concatenate([jnp.tile(jnp.concatenate([zero, sin], 1), (1, n_rot)), jnp.zeros((t, tail), F32)], 1)
    return cos_h, sin_h, c_i, s1_i, s2_i


def _with_past(past, new, nb, seq, n_keys):
    w = new.shape[1]
    if past is None and n_keys == seq:
        return new
    parts = [] if past is None else [past.astype(new.dtype).reshape(nb, -1, w)]
    parts.append(new.reshape(nb, seq, w))
    have = sum(p.shape[1] for p in parts)
    if n_keys > have:
        parts.append(jnp.zeros((nb, n_keys - have, w), new.dtype))
    return jnp.concatenate(parts, axis=1).reshape(nb * n_keys, w)


def _layer(x, mod, past, conv_state, p):
    nb, seq, d = x.shape
    n = nb * seq
    nh = p["w_pa"].shape[0] // HEAD_DIM
    width = nh * HEAD_DIM
    n_past = 0 if past is None else past["a_k"].shape[1]
    n_valid = n_past + seq
    n_keys = -(-n_valid // 256) * 256
    x2 = x.reshape(n, d)

    h_hi, h_lo = _norm(x2, p["g_pre_mix"], mod, seq)
    cos_h, sin_h, c_i, s1_i, s2_i = _rope_tables(n_past + jnp.arange(seq, dtype=I32))
    p16, kv32 = _inproj(h_hi, p["w_main"], cos_h, sin_h, seq, width)
    col = lambda a, g: a[:, g * width:(g + 1) * width]
    qi_hi, qi_lo, kw, ki_hi, ki_lo = _idxproj(h_hi, h_lo, p["w_idx_hi"], p["w_idx_lo"], c_i, s1_i, s2_i, seq)

    def keys(g, past_rows):
        if past_rows is None and n_keys == seq:
            return p16, g * width
        return _with_past(past_rows, col(p16, g), nb, seq, n_keys), 0

    q_hi = qi_hi.reshape(n, H_IDX, D_IDX)
    q_lo = qi_lo.reshape(n, H_IDX, D_IDX)
    qi_cat = jnp.concatenate([q_hi, q_hi, q_lo, jnp.zeros_like(q_hi)], axis=2).reshape(n, H_IDX * IDX_K)
    k_hi, k_lo = ki_hi[:, :D_IDX], ki_lo[:, :D_IDX]
    if past is None:
        pa_k = pa_v = pb_k = pb_v = pi_hi = pi_lo = None
    else:
        pa_k, pa_v, pb_k, pb_v = (past[k].reshape(nb, n_past, width) for k in ("a_k", "a_v", "b_k", "b_v"))
        pi_hi, pi_lo = _split(past["b_kidx"].astype(F32))
    k_hi = _with_past(pi_hi, k_hi, nb, seq, n_keys)
    k_lo = _with_past(pi_lo, k_lo, nb, seq, n_keys)
    ki_cat = jnp.concatenate([k_hi, k_lo, k_hi, jnp.zeros_like(k_hi)], axis=1)

    o_a = _sb((p16, 0), keys(1, pa_k), keys(2, pa_v), nb, seq, n_keys, n_past, nh)
    seq_q = max(seq, LANES)
    pad_q = lambda a: a if seq_q == seq else jnp.pad(
        a.reshape(nb, seq, -1), ((0, 0), (0, seq_q - seq), (0, 0))).reshape(nb * seq_q, -1)
    vt_all = _with_past(pb_v, col(p16, 5), nb, seq, n_keys).reshape(nb, n_keys, width)
    vt_all = jnp.swapaxes(vt_all, 1, 2).reshape(nb * width, n_keys)
    q_b = (p16, 3 * width) if seq_q == seq else (pad_q(col(p16, 3)), 0)
    o_b = _dsat(q_b, keys(4, pb_k), vt_all, pad_q(qi_cat), ki_cat,
                pad_q(kw).T, nb, seq_q, n_keys, n_valid, n_past, nh)
    if seq_q != seq:
        o_b = o_b.reshape(nb, seq_q, width)[:, :seq].reshape(n, width)

    merged = _merge(h_hi, o_a, o_b, p["w_gate"], p["b_gate"], p["w_pa"], p["w_pb"])
    x1, h2 = _outproj(merged, p["w_out"], x2, mod, p["g_post_mix"], p["g_pre_ffn"], seq)
    y, new_conv = _ffn(h2, x1, mod, conv_state, p["w_ffn_gate"], p["w_ffn_up"], p["w_ffn_down"],
                       p["w_conv"], p["b_conv"], p["g_post_ffn"], seq)

    heads = lambda g: kv32[g].reshape(nb, seq, nh, HEAD_DIM)
    new = (heads(1), heads(2), heads(4), heads(5), kw[:, :D_IDX].reshape(nb, seq, D_IDX), new_conv)
    return y.reshape(nb, seq, d), new


def kernel(x_prompt, x_sample, c_prompt, c_sample, cache_a_k, cache_a_v, cache_b_k, cache_b_v, cache_b_kidx,
           state_ffn_conv, w_mod, b_mod, g_pre_mix, w_in, w_merge_gate, b_merge_gate, w_proj_a, w_proj_b, w_out,
           g_post_mix, g_pre_ffn, w_ffn_gate, w_ffn_up, w_conv, b_conv, w_ffn_down, g_post_ffn):
    depth = w_mod.shape[0]
    nbp, nbs = x_prompt.shape[0], x_sample.shape[0]
    d = x_prompt.shape[2]
    width = w_proj_a.shape[1]
    dff = w_ffn_gate.shape[2]
    assert w_in.shape[2] == 6 * width + H_IDX * D_IDX + D_IDX + H_IDX
    assert cache_b_kidx.shape[-1] == D_IDX and cache_a_k.shape[-1] == HEAD_DIM

    y_p, y_s = x_prompt, x_sample
    st_p = [[] for _ in range(6)]
    st_s = [[] for _ in range(6)]
    for l in range(depth):
        w_idx = jnp.pad(w_in[l][:, 6 * width:], ((0, 0), (0, IDX_COLS - (w_in.shape[2] - 6 * width))))
        w_idx_hi, w_idx_lo = _split(w_idx)
        p = dict(
            g_pre_mix=g_pre_mix[l], w_main=w_in[l][:, :6 * width].astype(BF16), w_idx_hi=w_idx_hi, w_idx_lo=w_idx_lo,
            w_gate=w_merge_gate[l].astype(BF16), b_gate=b_merge_gate[l],
            w_pa=w_proj_a[l].astype(BF16), w_pb=w_proj_b[l].astype(BF16), w_out=w_out[l].astype(BF16),
            g_post_mix=g_post_mix[l], g_pre_ffn=g_pre_ffn[l],
            w_ffn_gate=w_ffn_gate[l].astype(BF16), w_ffn_up=w_ffn_up[l].astype(BF16),
            w_ffn_down=w_ffn_down[l].astype(BF16), w_conv=w_conv[l], b_conv=b_conv[l], g_post_ffn=g_post_ffn[l])

        c_all = jnp.concatenate([c_prompt, c_sample], axis=0)
        pad = -c_all.shape[0] % 16
        mod = _mod(jnp.pad(c_all, ((0, pad), (0, 0))), w_mod[l], b_mod[l]).reshape(-1, N_MOD, d)
        past = dict(a_k=cache_a_k[l], a_v=cache_a_v[l], b_k=cache_b_k[l], b_v=cache_b_v[l], b_kidx=cache_b_kidx[l])
        y_p, new_p = _layer(y_p, mod[:nbp], None, jnp.zeros((nbp, CONV_W - 1, dff), F32), p)
        y_s, new_s = _layer(y_s, mod[nbp:nbp + nbs], past, state_ffn_conv[l], p)
        for i in range(6):
            st_p[i].append(new_p[i])
            st_s[i].append(new_s[i])
    sp = [jnp.stack(s, axis=0) for s in st_p]
    ss = [jnp.stack(s, axis=0) for s in st_s]
    return (y_p, y_s, sp[0], sp[1], sp[2], sp[3], sp[4], sp[5], ss[0], ss[1], ss[2], ss[3], ss[4], ss[5])
```

```python
import functools

import jax
import jax.numpy as jnp
from jax import lax
from jax.experimental import pallas as pl
from jax.experimental.pallas import tpu as pltpu

F32 = jnp.float32
BF16 = jnp.bfloat16
I32 = jnp.int32

CHUNK = 64
HEAD_DIM = 128
H_IDX = 4
D_IDX = 64
TOPK_MAX = 256
CONV_W = 3
ROPE_THETA = 10000.0
RMS_EPS = 1e-6
N_MOD = 6
IDX_COLS = 384
IDX_K = 256
LANES = 128
HALO = 16
ONES_ROWS = 16
LOG2E = 1.4426950408889634
UNDERFLOW_LOG2 = 160.0
NEG = -1e30
INT_MIN = -2 ** 31
VMEM_LIMIT = 56 * 1024 * 1024


def _tile(n, pref, mult):
    t = min(pref, n)
    t -= t % mult
    while t >= mult:
        if n % t == 0:
            return t
        t -= mult
    return n


def _row_tile(n, seq, pref):
    if seq >= pref or n == seq:
        return _tile(seq, pref, 16)
    return _tile(n, pref, seq)


def _params(sem):
    return pltpu.CompilerParams(dimension_semantics=sem, vmem_limit_bytes=VMEM_LIMIT)


def _split(x):
    hi = x.astype(BF16)
    lo = (x - hi.astype(F32)).astype(BF16)
    return hi, lo


def _dot(a, b):
    return jnp.dot(a, b, preferred_element_type=F32)


def _dot_t(a, b):
    return lax.dot_general(a, b, (((1,), (1,)), ((), ())), preferred_element_type=F32)


def _rms(x, g):
    ms = jnp.mean(x * x, axis=-1, keepdims=True)
    return x * lax.rsqrt(ms + RMS_EPS) * g


def _mod_kernel(c_ref, w_ref, b_ref, o_ref):
    c = c_ref[...]
    s = c / (1.0 + jnp.exp(-c))
    s_hi, s_lo = _split(s)
    w_hi, w_lo = _split(w_ref[...])
    o_ref[...] = _dot(s_hi, w_hi) + _dot(s_hi, w_lo) + _dot(s_lo, w_hi) + b_ref[...]


def _mod(c, w, b):
    bc, d = c.shape
    n = w.shape[1]
    tn = _tile(n, 1024, LANES)
    return pl.pallas_call(
        _mod_kernel,
        grid=(n // tn,),
        in_specs=[pl.BlockSpec((bc, d), lambda j: (0, 0)),
                  pl.BlockSpec((d, tn), lambda j: (0, j)),
                  pl.BlockSpec((1, tn), lambda j: (0, j))],
        out_specs=pl.BlockSpec((bc, tn), lambda j: (0, j)),
        out_shape=jax.ShapeDtypeStruct((bc, n), F32),
        compiler_params=_params(("arbitrary",)),
        name="mod",
    )(c, w, b.reshape(1, n))


def _norm_kernel(x_ref, g_ref, mod_ref, hi_ref, lo_ref):
    m = mod_ref[0]
    h = _rms(x_ref[...], g_ref[...]) * (1.0 + m[1:2, :]) + m[0:1, :]
    hi, lo = _split(h)
    hi_ref[...] = hi
    lo_ref[...] = lo


def _norm(x2, g, mod, seq):
    n, d = x2.shape
    tr = _tile(seq, 512, 16)
    tps = seq // tr
    return pl.pallas_call(
        _norm_kernel,
        grid=(n // tr,),
        in_specs=[pl.BlockSpec((tr, d), lambda i: (i, 0)),
                  pl.BlockSpec((1, d), lambda i: (0, 0)),
                  pl.BlockSpec((1, N_MOD, d), lambda i: (i // tps, 0, 0))],
        out_specs=[pl.BlockSpec((tr, d), lambda i: (i, 0))] * 2,
        out_shape=[jax.ShapeDtypeStruct((n, d), BF16)] * 2,
        compiler_params=_params(("arbitrary",)),
        name="norm",
    )(x2, g.reshape(1, d), mod)


N_PROJ = 6


KV_GROUPS = (1, 2, 4, 5)


def _inproj_kernel(a_ref, w_ref, c_ref, s_ref, p16_ref, ka_ref, va_ref, kb_ref, vb_ref, r_scr, *, tm, nh):
    acc = _dot(a_ref[...], w_ref[...])
    c = c_ref[0]
    s = s_ref[0]
    for k in range(nh):
        sl = slice(k * HEAD_DIM, (k + 1) * HEAD_DIM)
        xs = acc[:, sl]
        r = xs * c + pltpu.roll(xs, HEAD_DIM // 2, 1) * s
        r_scr[:, sl] = r
        p16_ref[:, sl] = r.astype(BF16)
    for grp, ref in zip(KV_GROUPS, (ka_ref, va_ref, kb_ref, vb_ref)):
        @pl.when(pl.program_id(1) == grp)
        def _(ref=ref):
            for k in range(nh):
                ref[pl.ds(k, tm, stride=nh), :] = r_scr[:, k * HEAD_DIM:(k + 1) * HEAD_DIM]


def _inproj(h_hi, w_main, cos_t, sin_t, seq, width):
    n, d = h_hi.shape
    nh = width // HEAD_DIM
    tm = _row_tile(n, seq, 1024)
    tn = width
    npj = 1
    scale = LOG2E * HEAD_DIM ** -0.5
    one, zero = jnp.ones_like(cos_t), jnp.zeros_like(cos_t)
    c_t = jnp.stack([scale * one, one, one, scale * cos_t, cos_t, one])
    s_t = jnp.stack([zero, zero, zero, scale * sin_t, sin_t, zero])
    if tm <= seq:
        tps = seq // tm
        tab_map = lambda i, j: (j // npj, i % tps, 0)
    else:
        c_t = jnp.tile(c_t, (1, tm // seq, 1))
        s_t = jnp.tile(s_t, (1, tm // seq, 1))
        tab_map = lambda i, j: (j // npj, 0, 0)
    out = pl.pallas_call(
        functools.partial(_inproj_kernel, tm=tm, nh=nh),
        grid=(n // tm, N_PROJ * npj),
        in_specs=[pl.BlockSpec((tm, d), lambda i, j: (i, 0)),
                  pl.BlockSpec((d, tn), lambda i, j: (0, j)),
                  pl.BlockSpec((1, tm, HEAD_DIM), tab_map),
                  pl.BlockSpec((1, tm, HEAD_DIM), tab_map)],
        out_specs=[pl.BlockSpec((tm, tn), lambda i, j: (i, j))]
        + [pl.BlockSpec((tm * nh, HEAD_DIM), lambda i, j: (i, 0), pipeline_mode=pl.Buffered(1))] * len(KV_GROUPS),
        out_shape=[jax.ShapeDtypeStruct((n, N_PROJ * width), BF16)]
        + [jax.ShapeDtypeStruct((n * nh, HEAD_DIM), F32)] * len(KV_GROUPS),
        scratch_shapes=[pltpu.VMEM((tm, tn), F32)],
        compiler_params=_params(("arbitrary", "arbitrary")),
        name="inproj",
    )(h_hi, w_main, c_t, s_t)
    return out[0], dict(zip(KV_GROUPS, out[1:]))


def _idxproj_kernel(hi_ref, lo_ref, whi_ref, wlo_ref, c_ref, s1_ref, s2_ref,
                    qhi_ref, qlo_ref, kw_ref, khi_ref, klo_ref):
    a_hi = hi_ref[...]
    w_hi = whi_ref[...]
    acc = _dot(a_hi, w_hi) + _dot(a_hi, wlo_ref[...]) + _dot(lo_ref[...], w_hi)
    for s in range(IDX_COLS // LANES):
        sl = slice(s * LANES, (s + 1) * LANES)
        xs = acc[:, sl]
        r = (xs * c_ref[:, sl] + pltpu.roll(xs, LANES - D_IDX // 2, 1) * s1_ref[:, sl]
             + pltpu.roll(xs, D_IDX // 2, 1) * s2_ref[:, sl])
        hi, lo = _split(r)
        if s < 2:
            qhi_ref[:, sl] = hi
            qlo_ref[:, sl] = lo
        else:
            kw_ref[...] = r
            khi_ref[...] = hi
            klo_ref[...] = lo


def _idxproj(h_hi, h_lo, w_hi, w_lo, c_t, s1_t, s2_t, seq):
    n, d = h_hi.shape
    tm = _row_tile(n, seq, 512)
    if tm <= seq:
        tps = seq // tm
        tab_map = lambda i: (i % tps, 0)
    else:
        c_t, s1_t, s2_t = (jnp.tile(t, (tm // seq, 1)) for t in (c_t, s1_t, s2_t))
        tab_map = lambda i: (0, 0)
    row = lambda i: (i, 0)
    return pl.pallas_call(
        _idxproj_kernel,
        grid=(n // tm,),
        in_specs=[pl.BlockSpec((tm, d), row), pl.BlockSpec((tm, d), row),
                  pl.BlockSpec((d, IDX_COLS), lambda i: (0, 0)),
                  pl.BlockSpec((d, IDX_COLS), lambda i: (0, 0)),
                  pl.BlockSpec((tm, IDX_COLS), tab_map),
                  pl.BlockSpec((tm, IDX_COLS), tab_map),
                  pl.BlockSpec((tm, IDX_COLS), tab_map)],
        out_specs=[pl.BlockSpec((tm, 2 * LANES), row), pl.BlockSpec((tm, 2 * LANES), row),
                   pl.BlockSpec((tm, LANES), row), pl.BlockSpec((tm, LANES), row),
                   pl.BlockSpec((tm, LANES), row)],
        out_shape=[jax.ShapeDtypeStruct((n, 2 * LANES), BF16), jax.ShapeDtypeStruct((n, 2 * LANES), BF16),
                   jax.ShapeDtypeStruct((n, LANES), F32), jax.ShapeDtypeStruct((n, LANES), BF16),
                   jax.ShapeDtypeStruct((n, LANES), BF16)],
        compiler_params=_params(("arbitrary",)),
        name="idxproj",
    )(h_hi, h_lo, w_hi, w_lo, c_t, s1_t, s2_t)


def _sb_kernel(q_ref, k_ref, v_ref, o_ref, acc_scr, z_scr, i_scr, *, tq, tk, past, group):
    q0 = past + pl.program_id(2) * tq
    n_kt = (q0 + tq - 2) // tk + 1
    n_clear = q0 // tk
    row_pos = q0 + lax.broadcasted_iota(I32, (tq, 1), 0)
    col = lax.broadcasted_iota(I32, (1, tk), 1)
    suffix = jnp.where(lax.broadcasted_iota(I32, (tk, tk), 0) >= lax.broadcasted_iota(I32, (tk, tk), 1),
                       1.0, 0.0).astype(BF16)
    suffix2 = jnp.concatenate([suffix, suffix], axis=0)
    acc_scr[...] = jnp.zeros_like(acc_scr)

    heads = [slice(h * HEAD_DIM, (h + 1) * HEAD_DIM) for h in range(group)]

    def tiles(js, carries, masked):
        offs = [pl.multiple_of(j * tk, tk) for j in js]
        if masked:
            valids = [(off + col) < row_pos for off in offs]
        for t, off in enumerate(offs):
            for h, hs in enumerate(heads):
                z_scr[t * group + h] = _dot_t(q_ref[:, hs], k_ref[pl.ds(off, tk), hs])
        for t in range(len(js)):
            for h in range(group):
                z = z_scr[t * group + h]
                sp = jnp.maximum(z, 0.0) + jnp.log2(1.0 + jnp.exp2(-jnp.abs(z)))
                if masked:
                    sp = jnp.where(valids[t], sp, 0.0)
                hi, lo = _split(sp)
                i_scr[t * group + h] = _dot(jnp.concatenate([hi, lo], axis=1), suffix2)
        carries = list(carries)
        for t, off in enumerate(offs):
            for h, hs in enumerate(heads):
                w = jnp.exp2(z_scr[t * group + h] - i_scr[t * group + h] - carries[h])
                if masked:
                    w = jnp.where(valids[t], w, 0.0)
                acc_scr[h] += _dot(w.astype(BF16), v_ref[pl.ds(off, tk), hs])
                carries[h] = carries[h] + i_scr[t * group + h, :, 0:1]
        return tuple(carries)

    def least(carries):
        m = carries[0]
        for c in carries[1:]:
            m = jnp.minimum(m, c)
        return jnp.min(m)

    st = tuple(jnp.zeros((tq, 1), F32) for _ in range(group))
    st = lax.fori_loop(0, n_kt - n_clear, lambda t, c: tiles([n_kt - 1 - t], c, True), st)

    def more(state):
        t, low, _ = state
        return jnp.logical_and(t < n_clear, low < UNDERFLOW_LOG2)

    def step(state):
        t, _, carries = state
        carries = tiles([n_clear - 1 - t], carries, False)
        return t + 1, least(carries), carries

    lax.while_loop(more, step, (jnp.int32(0), least(st), st))
    for h in range(group):
        o_ref[:, h * HEAD_DIM:(h + 1) * HEAD_DIM] = acc_scr[h].astype(o_ref.dtype)


def _sb(q, k_all, v_all, nb, seq, n_keys, past, nh):
    tq = _tile(seq, 256, 16)
    tk = 256
    nq = seq // tq
    group = 4 if nh % 4 == 0 else (2 if nh % 2 == 0 else 1)
    gw = group * HEAD_DIM
    (q, q0), (k_all, k0), (v_all, v0) = ((a, c // gw) for a, c in (q, k_all, v_all))
    return pl.pallas_call(
        functools.partial(_sb_kernel, tq=tq, tk=tk, past=past, group=group),
        grid=(nb, nh // group, nq),
        in_specs=[pl.BlockSpec((tq, gw), lambda b, h, i: (b * nq + i, q0 + h)),
                  pl.BlockSpec((n_keys, gw), lambda b, h, i: (b, k0 + h)),
                  pl.BlockSpec((n_keys, gw), lambda b, h, i: (b, v0 + h))],
        out_specs=pl.BlockSpec((tq, gw), lambda b, h, i: (b * nq + i, h)),
        out_shape=jax.ShapeDtypeStruct((q.shape[0], nh * HEAD_DIM), BF16),
        scratch_shapes=[pltpu.VMEM((group, tq, HEAD_DIM), F32), pltpu.VMEM((group, tq, tk), F32),
                        pltpu.VMEM((group, tq, tk), F32)],
        compiler_params=_params(("arbitrary", "arbitrary", "arbitrary")),
        name="sb",
    )(q, k_all, v_all)


def _fold8(parts, op):
    while len(parts) > 1:
        parts = [op(parts[i], parts[i + 1]) if i + 1 < len(parts) else parts[i] for i in range(0, len(parts), 2)]
    return parts[0]


def _dsat_kernel(q_ref, k_ref, vt_ref, qi_ref, ki_ref, kwt_ref, o_ref, key_scr, acc_scr, s_scr, p_scr,
                 *, tq, tk, past, n_valid, n_sel, nh):
    q0 = past + pl.program_id(1) * tq
    col_pos = q0 + lax.broadcasted_iota(I32, (1, tq), 1)
    col_lim = jnp.minimum((col_pos // CHUNK + 1) * CHUNK, n_valid)
    n_kt = (jnp.minimum(((q0 + tq - 1) // CHUNK + 1) * CHUNK, n_valid) + tk - 1) // tk
    row = lax.broadcasted_iota(I32, (tk, 1), 0)
    ksel = float(n_sel)
    groups = [slice(8 * r, 8 * r + 8) for r in range(tk // 8)]

    wrow = kwt_ref[D_IDX:D_IDX + H_IDX, :] * ((H_IDX ** -0.5) * (D_IDX ** -0.5))

    def score_pair(t, _):
        offs = [pl.multiple_of(jnp.minimum(2 * t + u, n_kt - 1) * tk, tk) for u in range(2)]
        raw = [[_dot_t(ki_ref[pl.ds(off, tk), :], qi_ref[:, h * IDX_K:(h + 1) * IDX_K]) for h in range(H_IDX)]
               for off in offs]
        for off, dots in zip(offs, raw):
            score = jnp.zeros((tk, tq), F32)
            for h, s in enumerate(dots):
                score = score + wrow[h:h + 1, :] * jnp.maximum(s, 0.0)
            bits = pltpu.bitcast(score, I32)
            key = jnp.where(bits < 0, INT_MIN - bits, bits)
            key_scr[pl.ds(off, tk), :] = jnp.where((off + row) < col_lim, key, INT_MIN)
        return 0

    lax.fori_loop(0, (n_kt + 1) // 2, score_pair, 0)

    def count_ge(thr8):
        def body(t, cnt):
            for u in range(2):
                j = 2 * t + u
                off = pl.multiple_of(jnp.minimum(j, n_kt - 1) * tk, tk)
                key = key_scr[pl.ds(off, tk), :]
                part = _fold8([jnp.where(key[g, :] >= thr8, 1.0, 0.0) for g in groups], jnp.add)
                cnt = cnt + (part if u == 0 else jnp.where(j < n_kt, part, 0.0))
            return cnt

        cnt = lax.fori_loop(0, (n_kt + 1) // 2, body, jnp.zeros((8, tq), F32))
        return jnp.broadcast_to(jnp.sum(cnt, axis=0, keepdims=True), (8, tq))

    def bit_step(it, st):
        cand, have = st
        trial = cand | lax.shift_left(jnp.int32(1), 30 - it)
        cnt = count_ge(trial)
        take = cnt >= ksel
        return jnp.where(take, trial, cand), jnp.where(take, cnt, have)

    zero8 = jnp.zeros((8, tq), I32)
    cnt0 = count_ge(zero8)
    start = (jnp.where(cnt0 >= ksel, zero8, INT_MIN), jnp.where(cnt0 >= ksel, cnt0, float(2 ** 30)))
    thr8, have8 = lax.fori_loop(0, 31, bit_step, start)
    thr = thr8[0:1, :]
    overfull = jnp.max(jnp.where(have8 > ksel, 1.0, 0.0)) > 0.0

    @pl.when(jnp.logical_not(overfull))
    def _():
        def bias_tile(j, _):
            off = pl.multiple_of(j * tk, tk)
            bias = jnp.where(key_scr[pl.ds(off, tk), :] >= thr, 0.0, NEG)
            key_scr[pl.ds(off, tk), :] = pltpu.bitcast(bias, I32)
            return 0

        lax.fori_loop(0, n_kt, bias_tile, 0)

    @pl.when(overfull)
    def _():
        no_thr8 = thr8 == INT_MIN
        cnt_gt = count_ge(jnp.where(no_thr8, thr8, thr8 + 1))
        need = jnp.where(no_thr8, 0.0, ksel - cnt_gt)[0:1, :]
        prefix = jnp.where(lax.broadcasted_iota(I32, (tk, tk), 0) >= lax.broadcasted_iota(I32, (tk, tk), 1),
                           1.0, 0.0).astype(BF16)

        def bias_tile(j, carry):
            off = pl.multiple_of(j * tk, tk)
            key = key_scr[pl.ds(off, tk), :]
            eq = key == thr
            eqf = jnp.where(eq, 1.0, 0.0)
            rank_incl = _dot(prefix, eqf.astype(BF16)) + carry
            tie_ok = (rank_incl - eqf) < need
            bias = jnp.where(key > thr, 0.0, jnp.where(eq, jnp.where(tie_ok, 0.0, NEG), NEG))
            key_scr[pl.ds(off, tk), :] = pltpu.bitcast(bias, I32)
            return rank_incl[tk - 1:tk, :]

        lax.fori_loop(0, n_kt, bias_tile, jnp.zeros((1, tq), F32))

    acc_scr[...] = jnp.zeros_like(acc_scr)
    ones_v = jnp.ones((ONES_ROWS, tk), BF16)
    heads = [slice(h * HEAD_DIM, (h + 1) * HEAD_DIM) for h in range(nh)]

    def logits(j, slot, ms):
        off = pl.multiple_of(j * tk, tk)
        bias = pltpu.bitcast(key_scr[pl.ds(off, tk), :], F32)
        for h, hs in enumerate(heads):
            s_scr[slot, h] = _dot_t(k_ref[pl.ds(off, tk), hs], q_ref[:, hs]) + bias
        out = []
        for h, m in enumerate(ms):
            s = s_scr[slot, h]
            top = jnp.max(_fold8([s[g, :] for g in groups], jnp.maximum), axis=0, keepdims=True)
            out.append(jnp.maximum(m, top))
        return tuple(out)

    def accumulate(j, slot, m_old, m_new):
        off = pl.multiple_of(j * tk, tk)
        for h in range(nh):
            p_scr[h] = jnp.exp2(s_scr[slot, h] - m_new[h]).astype(BF16)
        for h, hs in enumerate(heads):
            vt_ext = jnp.concatenate([vt_ref[hs, pl.ds(off, tk)], ones_v], axis=0)
            acc_scr[h] = jnp.exp2(m_old[h] - m_new[h]) * acc_scr[h] + _dot(vt_ext, p_scr[h])

    m_none = tuple(jnp.full((1, tq), NEG, F32) for _ in range(nh))
    m_first = logits(0, 0, m_none)

    def attn_step(j, st):
        m_old, m_cur = st
        slot = j % 2
        accumulate(j - 1, 1 - slot, m_old, m_cur)
        return m_cur, logits(j, slot, m_cur)

    m_old, m_cur = lax.fori_loop(1, n_kt, attn_step, (m_none, m_first))
    accumulate(n_kt - 1, (n_kt - 1) % 2, m_old, m_cur)
    for h, hs in enumerate(heads):
        out_t = acc_scr[h, :HEAD_DIM, :] / acc_scr[h, HEAD_DIM:HEAD_DIM + 1, :]
        o_ref[:, hs] = out_t.T.astype(o_ref.dtype)


def _dsat(q, k_all, vt_all, qi_cat, ki_cat, kwt, nb, seq, n_keys, n_valid, past, nh):
    tq = _tile(seq, 256, LANES)
    tk = 256
    nq = seq // tq
    width = nh * HEAD_DIM
    n_sel = min(TOPK_MAX, n_valid // 4)
    once = pl.Buffered(1)
    (q, q0), (k_all, k0) = ((a, c // width) for a, c in (q, k_all))
    return pl.pallas_call(
        functools.partial(_dsat_kernel, tq=tq, tk=tk, past=past, n_valid=n_valid, n_sel=n_sel, nh=nh),
        grid=(nb, nq),
        in_specs=[pl.BlockSpec((tq, width), lambda b, i: (b * nq + i, q0)),
                  pl.BlockSpec((n_keys, width), lambda b, i: (b, k0), pipeline_mode=once),
                  pl.BlockSpec((width, n_keys), lambda b, i: (b, 0), pipeline_mode=once),
                  pl.BlockSpec((tq, H_IDX * IDX_K), lambda b, i: (b * nq + i, 0)),
                  pl.BlockSpec((n_keys, IDX_K), lambda b, i: (b, 0), pipeline_mode=once),
                  pl.BlockSpec((LANES, tq), lambda b, i: (0, b * nq + i))],
        out_specs=pl.BlockSpec((tq, width), lambda b, i: (b * nq + i, 0)),
        out_shape=jax.ShapeDtypeStruct((q.shape[0], width), BF16),
        scratch_shapes=[pltpu.VMEM((n_keys, tq), I32), pltpu.VMEM((nh, HEAD_DIM + ONES_ROWS, tq), F32),
                        pltpu.VMEM((2, nh, tk, tq), F32), pltpu.VMEM((nh, tk, tq), BF16)],
        compiler_params=_params(("arbitrary", "arbitrary")),
        name="dsa",
    )(q, k_all, vt_all, qi_cat, ki_cat, kwt)


def _merge_kernel(h_ref, oa_ref, ob_ref, wg1_ref, wg2_ref, b1_ref, b2_ref, wa_ref, wb_ref, o_ref):
    h = h_ref[...]
    g1 = 1.0 / (1.0 + jnp.exp(-(_dot(h, wg1_ref[...]) + b1_ref[...])))
    g2 = 1.0 / (1.0 + jnp.exp(-(_dot(h, wg2_ref[...]) + b2_ref[...])))
    o_ref[...] = (g1 * _dot(oa_ref[...], wa_ref[...]) + g2 * _dot(ob_ref[...], wb_ref[...])).astype(BF16)


def _merge(h_hi, o_a, o_b, w_gate, b_gate, w_pa, w_pb):
    n, d = h_hi.shape
    width = o_a.shape[1]
    tm = _tile(n, 1024, 16)
    tn = _tile(d, 512, LANES)
    nj = d // tn
    b_gate = b_gate.reshape(1, 2 * d)
    row = lambda i, j: (i, 0)
    return pl.pallas_call(
        _merge_kernel,
        grid=(n // tm, nj),
        in_specs=[pl.BlockSpec((tm, d), row), pl.BlockSpec((tm, width), row), pl.BlockSpec((tm, width), row),
                  pl.BlockSpec((d, tn), lambda i, j: (0, j)),
                  pl.BlockSpec((d, tn), lambda i, j: (0, j + nj)),
                  pl.BlockSpec((1, tn), lambda i, j: (0, j)),
                  pl.BlockSpec((1, tn), lambda i, j: (0, j + nj)),
                  pl.BlockSpec((width, tn), lambda i, j: (0, j)),
                  pl.BlockSpec((width, tn), lambda i, j: (0, j))],
        out_specs=pl.BlockSpec((tm, tn), lambda i, j: (i, j)),
        out_shape=jax.ShapeDtypeStruct((n, d), BF16),
        compiler_params=_params(("arbitrary", "arbitrary")),
        name="merge",
    )(h_hi, o_a, o_b, w_gate, w_gate, b_gate, b_gate, w_pa, w_pb)


def _outproj_kernel(m_ref, w_ref, x_ref, mod_ref, gm_ref, gf_ref, x1_ref, h2_ref):
    mod = mod_ref[0]
    x1 = x_ref[...] + mod[2:3, :] * _rms(_dot(m_ref[...], w_ref[...]), gm_ref[...])
    x1_ref[...] = x1
    h2_ref[...] = (_rms(x1, gf_ref[...]) * (1.0 + mod[4:5, :]) + mod[3:4, :]).astype(BF16)


def _outproj(merged, w_out, x2, mod, g_post_mix, g_pre_ffn, seq):
    n, d = x2.shape
    tm = _tile(seq, 512, 16)
    tps = seq // tm
    row = lambda i: (i, 0)
    fix = lambda i: (0, 0)
    return pl.pallas_call(
        _outproj_kernel,
        grid=(n // tm,),
        in_specs=[pl.BlockSpec((tm, d), row), pl.BlockSpec((d, d), fix, pipeline_mode=pl.Buffered(1)),
                  pl.BlockSpec((tm, d), row),
                  pl.BlockSpec((1, N_MOD, d), lambda i: (i // tps, 0, 0)),
                  pl.BlockSpec((1, d), fix), pl.BlockSpec((1, d), fix)],
        out_specs=[pl.BlockSpec((tm, d), row), pl.BlockSpec((tm, d), row)],
        out_shape=[jax.ShapeDtypeStruct((n, d), F32), jax.ShapeDtypeStruct((n, d), BF16)],
        compiler_params=_params(("arbitrary",)),
        name="outproj",
    )(merged, w_out, x2, mod, g_post_mix.reshape(1, d), g_pre_ffn.reshape(1, d))


def _gelu_tanh(x):
    return 0.5 * x * (1.0 + jnp.tanh(0.7978845608028654 * (x + 0.044715 * (x * x * x))))


def _ffn_kernel(h_ref, halo_ref, x1_ref, mod_ref, st_ref, wg_ref, wu_ref, wd_ref, wc_ref, bc_ref, gp_ref,
                y_ref, nc_ref, g_scr, act_scr, *, tm, nsub, tps):
    i = pl.program_id(0)
    c = pl.program_id(1)
    last = pl.num_programs(1) - 1
    sub = tm // nsub
    cur = c % 2

    def step(down, gate_up):
        if gate_up:
            h = h_ref[...]
            wg = wg_ref[...]
            g = _dot(h, wg)
            u = _dot(h, wu_ref[...])
            wc = wc_ref[...]
            bc = bc_ref[...]
            g_scr[HALO:HALO + tm, :] = g
        if down == "set":
            y_ref[...] = _dot(act_scr[1 - cur], wd_ref[...])
        elif down == "add":
            y_ref[...] += _dot(act_scr[1 - cur], wd_ref[...])
        if not gate_up:
            return

        def conv(lo, rows):
            return (bc + wc[0:1, :] * g_scr[lo - 2:lo - 2 + rows, :]
                    + wc[1:2, :] * g_scr[lo - 1:lo - 1 + rows, :] + wc[2:3, :] * g_scr[lo:lo + rows, :])

        if nsub == 1:
            g_halo = _dot(halo_ref[...], wg)
            g_scr[0:HALO, :] = g_halo
            g_scr[HALO - 2:HALO, :] = jnp.where(i % tps == 0, st_ref[0], g_halo[HALO - 2:HALO, :])
            nc_ref[0] = g_scr[HALO + tm - 2:HALO + tm, :]
            act_scr[cur] = (_gelu_tanh(conv(HALO, tm)) * u).astype(BF16)
        else:
            for s in range(nsub):
                lo = HALO + s * sub
                nc_ref[s] = g_scr[lo + sub - 2:lo + sub, :]
                g_scr[lo - 2:lo, :] = st_ref[s]
                act_scr[cur, s * sub:(s + 1) * sub, :] = (
                    _gelu_tanh(conv(lo, sub)) * u[s * sub:(s + 1) * sub, :]).astype(BF16)

    pl.when(c == 0)(lambda: step(None, True))
    pl.when(c == 1)(lambda: step("set", True))
    pl.when(jnp.logical_and(c > 1, c < last))(lambda: step("add", True))

    @pl.when(c == last)
    def _():
        step("add", False)
        for s in range(nsub):
            rows = slice(s * sub, (s + 1) * sub)
            y_ref[rows, :] = x1_ref[rows, :] + mod_ref[s][5:6, :] * _rms(y_ref[rows, :], gp_ref[...])


def _ffn(h2, x1, mod, state, w_gate, w_up, w_down, w_conv, b_conv, g_post, seq):
    n, d = x1.shape
    dff = w_gate.shape[1]
    nb = n // seq
    tm = _row_tile(n, seq, 1024)
    tf = _tile(dff, 512, LANES)
    n_chunks = dff // tf
    assert n_chunks >= 2, "the lagged down projection needs at least two d_ff chunks"
    up = lambda c: jnp.minimum(c, n_chunks - 1)
    down = lambda c: jnp.maximum(c - 1, 0)
    per_tile_c = lambda i, c: (i, 0, up(c))
    if tm <= seq:
        nsub, tps = 1, seq // tm
        per_seq = lambda i, c: (i // tps, 0, 0)
        per_seq_c = lambda i, c: (i // tps, 0, up(c))
    else:
        nsub, tps = tm // seq, 1
        per_seq = lambda i, c: (i, 0, 0)
        per_seq_c = per_tile_c
    halo_blocks = tm // HALO
    row = lambda i, c: (i, 0)
    y, new_conv = pl.pallas_call(
        functools.partial(_ffn_kernel, tm=tm, nsub=nsub, tps=tps),
        grid=(n // tm, n_chunks + 1),
        in_specs=[pl.BlockSpec((tm, d), row, pipeline_mode=pl.Buffered(1)),
                  pl.BlockSpec((HALO, d), lambda i, c: (jnp.maximum(i * halo_blocks - 1, 0), 0)),
                  pl.BlockSpec((tm, d), row, pipeline_mode=pl.Buffered(1)),
                  pl.BlockSpec((nsub, N_MOD, d), per_seq),
                  pl.BlockSpec((nsub, CONV_W - 1, tf), per_seq_c),
                  pl.BlockSpec((d, tf), lambda i, c: (0, up(c))),
                  pl.BlockSpec((d, tf), lambda i, c: (0, up(c))),
                  pl.BlockSpec((tf, d), lambda i, c: (down(c), 0)),
                  pl.BlockSpec((CONV_W, tf), lambda i, c: (0, up(c))),
                  pl.BlockSpec((1, tf), lambda i, c: (0, up(c))),
                  pl.BlockSpec((1, d), lambda i, c: (0, 0))],
        out_specs=[pl.BlockSpec((tm, d), row, pipeline_mode=pl.Buffered(1)),
                   pl.BlockSpec((nsub, CONV_W - 1, tf), per_tile_c)],
        out_shape=[jax.ShapeDtypeStruct((n, d), F32),
                   jax.ShapeDtypeStruct((nb * tps, CONV_W - 1, dff), F32)],
        scratch_shapes=[pltpu.VMEM((HALO + tm, tf), F32), pltpu.VMEM((2, tm, tf), BF16)],
        compiler_params=_params(("arbitrary", "arbitrary")),
        name="ffn",
    )(h2, h2, x1, mod, state, w_gate, w_up, w_down, w_conv, b_conv.reshape(1, dff), g_post.reshape(1, d))
    return y, new_conv.reshape(nb, tps, CONV_W - 1, dff)[:, tps - 1]


def _rope_tables(pos):
    def angles(dim):
        half = dim // 2
        inv = ROPE_THETA ** (-jnp.arange(half, dtype=F32) * 2.0 / dim)
        ang = pos.astype(F32)[:, None] * inv[None, :]
        return jnp.cos(ang), jnp.sin(ang)

    cos, sin = angles(HEAD_DIM)
    cos_h = jnp.concatenate([cos, cos], axis=1)
    sin_h = jnp.concatenate([-sin, sin], axis=1)

    cos, sin = angles(D_IDX)
    zero = jnp.zeros_like(sin)
    n_rot = H_IDX + 1
    tail = IDX_COLS - n_rot * D_IDX
    t = pos.shape[0]
    c_i = jnp.concatenate([jnp.tile(jnp.concatenate([cos, cos], 1), (1, n_rot)), jnp.ones((t, tail), F32)], 1)
    s1_i = jnp.concatenate([jnp.tile(jnp.concatenate([-sin, zero], 1), (1, n_rot)), jnp.zeros((t, tail), F32)], 1)
    s2_i = jnp.concatenate([jnp.tile(jnp.concatenate([zero, sin], 1), (1, n_rot)), jnp.zeros((t, tail), F32)], 1)
    return cos_h, sin_h, c_i, s1_i, s2_i


def _with_past(past, new, nb, seq, n_keys):
    w = new.shape[1]
    if past is None and n_keys == seq:
        return new
    parts = [] if past is None else [past.astype(new.dtype).reshape(nb, -1, w)]
    parts.append(new.reshape(nb, seq, w))
    have = sum(p.shape[1] for p in parts)
    if n_keys > have:
        parts.append(jnp.zeros((nb, n_keys - have, w), new.dtype))
    return jnp.concatenate(parts, axis=1).reshape(nb * n_keys, w)


def _layer(x, mod, past, conv_state, p):
    nb, seq, d = x.shape
    n = nb * seq
    nh = p["w_pa"].shape[0] // HEAD_DIM
    width = nh * HEAD_DIM
    n_past = 0 if past is None else past["a_k"].shape[1]
    n_valid = n_past + seq
    n_keys = -(-n_valid // 256) * 256
    x2 = x.reshape(n, d)

    h_hi, h_lo = _norm(x2, p["g_pre_mix"], mod, seq)
    cos_h, sin_h, c_i, s1_i, s2_i = _rope_tables(n_past + jnp.arange(seq, dtype=I32))
    p16, kv32 = _inproj(h_hi, p["w_main"], cos_h, sin_h, seq, width)
    col = lambda a, g: a[:, g * width:(g + 1) * width]
    qi_hi, qi_lo, kw, ki_hi, ki_lo = _idxproj(h_hi, h_lo, p["w_idx_hi"], p["w_idx_lo"], c_i, s1_i, s2_i, seq)

    def keys(g, past_rows):
        if past_rows is None and n_keys == seq:
            return p16, g * width
        return _with_past(past_rows, col(p16, g), nb, seq, n_keys), 0

    q_hi = qi_hi.reshape(n, H_IDX, D_IDX)
    q_lo = qi_lo.reshape(n, H_IDX, D_IDX)
    qi_cat = jnp.concatenate([q_hi, q_hi, q_lo, jnp.zeros_like(q_hi)], axis=2).reshape(n, H_IDX * IDX_K)
    k_hi, k_lo = ki_hi[:, :D_IDX], ki_lo[:, :D_IDX]
    if past is None:
        pa_k = pa_v = pb_k = pb_v = pi_hi = pi_lo = None
    else:
        pa_k, pa_v, pb_k, pb_v = (past[k].reshape(nb, n_past, width) for k in ("a_k", "a_v", "b_k", "b_v"))
        pi_hi, pi_lo = _split(past["b_kidx"].astype(F32))
    k_hi = _with_past(pi_hi, k_hi, nb, seq, n_keys)
    k_lo = _with_past(pi_lo, k_lo, nb, seq, n_keys)
    ki_cat = jnp.concatenate([k_hi, k_lo, k_hi, jnp.zeros_like(k_hi)], axis=1)

    o_a = _sb((p16, 0), keys(1, pa_k), keys(2, pa_v), nb, seq, n_keys, n_past, nh)
    seq_q = max(seq, LANES)
    pad_q = lambda a: a if seq_q == seq else jnp.pad(
        a.reshape(nb, seq, -1), ((0, 0), (0, seq_q - seq), (0, 0))).reshape(nb * seq_q, -1)
    vt_all = _with_past(pb_v, col(p16, 5), nb, seq, n_keys).reshape(nb, n_keys, width)
    vt_all = jnp.swapaxes(vt_all, 1, 2).reshape(nb * width, n_keys)
    q_b = (p16, 3 * width) if seq_q == seq else (pad_q(col(p16, 3)), 0)
    o_b = _dsat(q_b, keys(4, pb_k), vt_all, pad_q(qi_cat), ki_cat,
                pad_q(kw).T, nb, seq_q, n_keys, n_valid, n_past, nh)
    if seq_q != seq:
        o_b = o_b.reshape(nb, seq_q, width)[:, :seq].reshape(n, width)

    merged = _merge(h_hi, o_a, o_b, p["w_gate"], p["b_gate"], p["w_pa"], p["w_pb"])
    x1, h2 = _outproj(merged, p["w_out"], x2, mod, p["g_post_mix"], p["g_pre_ffn"], seq)
    y, new_conv = _ffn(h2, x1, mod, conv_state, p["w_ffn_gate"], p["w_ffn_up"], p["w_ffn_down"],
                       p["w_conv"], p["b_conv"], p["g_post_ffn"], seq)

    heads = lambda g: kv32[g].reshape(nb, seq, nh, HEAD_DIM)
    new = (heads(1), heads(2), heads(4), heads(5), kw[:, :D_IDX].reshape(nb, seq, D_IDX), new_conv)
    return y.reshape(nb, seq, d), new


def kernel(x_prompt, x_sample, c_prompt, c_sample, cache_a_k, cache_a_v, cache_b_k, cache_b_v, cache_b_kidx,
           state_ffn_conv, w_mod, b_mod, g_pre_mix, w_in, w_merge_gate, b_merge_gate, w_proj_a, w_proj_b, w_out,
           g_post_mix, g_pre_ffn, w_ffn_gate, w_ffn_up, w_conv, b_conv, w_ffn_down, g_post_ffn):
    depth = w_mod.shape[0]
    nbp, nbs = x_prompt.shape[0], x_sample.shape[0]
    d = x_prompt.shape[2]
    width = w_proj_a.shape[1]
    dff = w_ffn_gate.shape[2]
    assert w_in.shape[2] == 6 * width + H_IDX * D_IDX + D_IDX + H_IDX
    assert cache_b_kidx.shape[-1] == D_IDX and cache_a_k.shape[-1] == HEAD_DIM

    y_p, y_s = x_prompt, x_sample
    st_p = [[] for _ in range(6)]
    st_s = [[] for _ in range(6)]
    for l in range(depth):
        w_idx = jnp.pad(w_in[l][:, 6 * width:], ((0, 0), (0, IDX_COLS - (w_in.shape[2] - 6 * width))))
        w_idx_hi, w_idx_lo = _split(w_idx)
        p = dict(
            g_pre_mix=g_pre_mix[l], w_main=w_in[l][:, :6 * width].astype(BF16), w_idx_hi=w_idx_hi, w_idx_lo=w_idx_lo,
            w_gate=w_merge_gate[l].astype(BF16), b_gate=b_merge_gate[l],
            w_pa=w_proj_a[l].astype(BF16), w_pb=w_proj_b[l].astype(BF16), w_out=w_out[l].astype(BF16),
            g_post_mix=g_post_mix[l], g_pre_ffn=g_pre_ffn[l],
            w_ffn_gate=w_ffn_gate[l].astype(BF16), w_ffn_up=w_ffn_up[l].astype(BF16),
            w_ffn_down=w_ffn_down[l].astype(BF16), w_conv=w_conv[l], b_conv=b_conv[l], g_post_ffn=g_post_ffn[l])

        c_all = jnp.concatenate([c_prompt, c_sample], axis=0)
        pad = -c_all.shape[0] % 16
        mod = _mod(jnp.pad(c_all, ((0, pad), (0, 0))), w_mod[l], b_mod[l]).reshape(-1, N_MOD, d)
        past = dict(a_k=cache_a_k[l], a_v=cache_a_v[l], b_k=cache_b_k[l], b_v=cache_b_v[l], b_kidx=cache_b_kidx[l])
        y_p, new_p = _layer(y_p, mod[:nbp], None, jnp.zeros((nbp, CONV_W - 1, dff), F32), p)
        y_s, new_s = _layer(y_s, mod[nbp:nbp + nbs], past, state_ffn_conv[l], p)
        for i in range(6):
            st_p[i].append(new_p[i])
            st_s[i].append(new_s[i])
    sp = [jnp.stack(s, axis=0) for s in st_p]
    ss = [jnp.stack(s, axis=0) for s in st_s]
    return (y_p, y_s, sp[0], sp[1], sp[2], sp[3], sp[4], sp[5], ss[0], ss[1], ss[2], ss[3], ss[4], ss[5])
```

```python
import functools

import jax
import jax.numpy as jnp
from jax import lax
from jax.experimental import pallas as pl
from jax.experimental.pallas import tpu as pltpu

F32 = jnp.float32
BF16 = jnp.bfloat16
I32 = jnp.int32

CHUNK = 64
HEAD_DIM = 128
H_IDX = 4
D_IDX = 64
TOPK_MAX = 256
CONV_W = 3
ROPE_THETA = 10000.0
RMS_EPS = 1e-6
N_MOD = 6
IDX_COLS = 384
IDX_K = 256
LANES = 128
HALO = 16
ONES_ROWS = 16
LOG2E = 1.4426950408889634
UNDERFLOW_LOG2 = 160.0
NEG = -1e30
INT_MIN = -2 ** 31
VMEM_LIMIT = 56 * 1024 * 1024


def _tile(n, pref, mult):
    t = min(pref, n)
    t -= t % mult
    while t >= mult:
        if n % t == 0:
            return t
        t -= mult
    return n


def _row_tile(n, seq, pref):
    if seq >= pref or n == seq:
        return _tile(seq, pref, 16)
    return _tile(n, pref, seq)


def _params(sem):
    return pltpu.CompilerParams(dimension_semantics=sem, vmem_limit_bytes=VMEM_LIMIT)


def _split(x):
    hi = x.astype(BF16)
    lo = (x - hi.astype(F32)).astype(BF16)
    return hi, lo


def _dot(a, b):
    return jnp.dot(a, b, preferred_element_type=F32)


def _dot_t(a, b):
    return lax.dot_general(a, b, (((1,), (1,)), ((), ())), preferred_element_type=F32)


def _rms(x, g):
    ms = jnp.mean(x * x, axis=-1, keepdims=True)
    return x * lax.rsqrt(ms + RMS_EPS) * g


def _mod_kernel(c_ref, w_ref, b_ref, o_ref):
    c = c_ref[...]
    s = c / (1.0 + jnp.exp(-c))
    s_hi, s_lo = _split(s)
    w_hi, w_lo = _split(w_ref[...])
    o_ref[...] = _dot(s_hi, w_hi) + _dot(s_hi, w_lo) + _dot(s_lo, w_hi) + b_ref[...]


def _mod(c, w, b):
    bc, d = c.shape
    n = w.shape[1]
    tn = _tile(n, 1024, LANES)
    return pl.pallas_call(
        _mod_kernel,
        grid=(n // tn,),
        in_specs=[pl.BlockSpec((bc, d), lambda j: (0, 0)),
                  pl.BlockSpec((d, tn), lambda j: (0, j)),
                  pl.BlockSpec((1, tn), lambda j: (0, j))],
        out_specs=pl.BlockSpec((bc, tn), lambda j: (0, j)),
        out_shape=jax.ShapeDtypeStruct((bc, n), F32),
        compiler_params=_params(("arbitrary",)),
        name="mod",
    )(c, w, b.reshape(1, n))


def _norm_kernel(x_ref, g_ref, mod_ref, hi_ref, lo_ref):
    m = mod_ref[0]
    h = _rms(x_ref[...], g_ref[...]) * (1.0 + m[1:2, :]) + m[0:1, :]
    hi, lo = _split(h)
    hi_ref[...] = hi
    lo_ref[...] = lo


def _norm(x2, g, mod, seq):
    n, d = x2.shape
    tr = _tile(seq, 512, 16)
    tps = seq // tr
    return pl.pallas_call(
        _norm_kernel,
        grid=(n // tr,),
        in_specs=[pl.BlockSpec((tr, d), lambda i: (i, 0)),
                  pl.BlockSpec((1, d), lambda i: (0, 0)),
                  pl.BlockSpec((1, N_MOD, d), lambda i: (i // tps, 0, 0))],
        out_specs=[pl.BlockSpec((tr, d), lambda i: (i, 0))] * 2,
        out_shape=[jax.ShapeDtypeStruct((n, d), BF16)] * 2,
        compiler_params=_params(("arbitrary",)),
        name="norm",
    )(x2, g.reshape(1, d), mod)


N_PROJ = 6


KV_GROUPS = (1, 2, 4, 5)


def _inproj_kernel(a_ref, w_ref, c_ref, s_ref, p16_ref, ka_ref, va_ref, kb_ref, vb_ref, *rest, tm, nh):
    r_scr = rest[-1]
    acc = _dot(a_ref[...], w_ref[...])
    c = c_ref[0]
    s = s_ref[0]
    for k in range(nh):
        sl = slice(k * HEAD_DIM, (k + 1) * HEAD_DIM)
        xs = acc[:, sl]
        r = xs * c + pltpu.roll(xs, HEAD_DIM // 2, 1) * s
        r_scr[:, sl] = r
        p16_ref[:, sl] = r.astype(BF16)
    for grp, ref in zip(KV_GROUPS, (ka_ref, va_ref, kb_ref, vb_ref)):
        @pl.when(pl.program_id(1) == grp)
        def _(ref=ref):
            for k in range(nh):
                ref[pl.ds(k, tm, stride=nh), :] = r_scr[:, k * HEAD_DIM:(k + 1) * HEAD_DIM]

    if len(rest) == 2:
        @pl.when(pl.program_id(1) == N_PROJ - 1)
        def _():
            rest[0][...] = r_scr[...].T.astype(BF16)


def _inproj(h_hi, w_main, cos_t, sin_t, seq, width):
    n, d = h_hi.shape
    nh = width // HEAD_DIM
    tm = _row_tile(n, seq, 1024)
    with_vt = tm <= seq
    tn = width
    npj = 1
    scale = LOG2E * HEAD_DIM ** -0.5
    one, zero = jnp.ones_like(cos_t), jnp.zeros_like(cos_t)
    c_t = jnp.stack([scale * one, one, one, scale * cos_t, cos_t, one])
    s_t = jnp.stack([zero, zero, zero, scale * sin_t, sin_t, zero])
    if tm <= seq:
        tps = seq // tm
        tab_map = lambda i, j: (j // npj, i % tps, 0)
    else:
        c_t = jnp.tile(c_t, (1, tm // seq, 1))
        s_t = jnp.tile(s_t, (1, tm // seq, 1))
        tab_map = lambda i, j: (j // npj, 0, 0)
    out = pl.pallas_call(
        functools.partial(_inproj_kernel, tm=tm, nh=nh),
        grid=(n // tm, N_PROJ * npj),
        in_specs=[pl.BlockSpec((tm, d), lambda i, j: (i, 0)),
                  pl.BlockSpec((d, tn), lambda i, j: (0, j)),
                  pl.BlockSpec((1, tm, HEAD_DIM), tab_map),
                  pl.BlockSpec((1, tm, HEAD_DIM), tab_map)],
        out_specs=[pl.BlockSpec((tm, tn), lambda i, j: (i, j))]
        + [pl.BlockSpec((tm * nh, HEAD_DIM), lambda i, j: (i, 0), pipeline_mode=pl.Buffered(1))] * len(KV_GROUPS)
        + ([pl.BlockSpec((width, tm), lambda i, j: (i // tps, i % tps))] if with_vt else []),
        out_shape=[jax.ShapeDtypeStruct((n, N_PROJ * width), BF16)]
        + [jax.ShapeDtypeStruct((n * nh, HEAD_DIM), F32)] * len(KV_GROUPS)
        + ([jax.ShapeDtypeStruct((n // seq * width, seq), BF16)] if with_vt else []),
        scratch_shapes=[pltpu.VMEM((tm, tn), F32)],
        compiler_params=_params(("arbitrary", "arbitrary")),
        name="inproj",
    )(h_hi, w_main, c_t, s_t)
    return out[0], dict(zip(KV_GROUPS, out[1:1 + len(KV_GROUPS)])), (out[-1] if with_vt else None)


def _idxproj_kernel(hi_ref, lo_ref, whi_ref, wlo_ref, c_ref, s1_ref, s2_ref,
                    qhi_ref, qlo_ref, kw_ref, khi_ref, klo_ref):
    a_hi = hi_ref[...]
    w_hi = whi_ref[...]
    acc = _dot(a_hi, w_hi) + _dot(a_hi, wlo_ref[...]) + _dot(lo_ref[...], w_hi)
    for s in range(IDX_COLS // LANES):
        sl = slice(s * LANES, (s + 1) * LANES)
        xs = acc[:, sl]
        r = (xs * c_ref[:, sl] + pltpu.roll(xs, LANES - D_IDX // 2, 1) * s1_ref[:, sl]
             + pltpu.roll(xs, D_IDX // 2, 1) * s2_ref[:, sl])
        hi, lo = _split(r)
        if s < 2:
            qhi_ref[:, sl] = hi
            qlo_ref[:, sl] = lo
        else:
            kw_ref[...] = r
            khi_ref[...] = hi
            klo_ref[...] = lo


def _idxproj(h_hi, h_lo, w_hi, w_lo, c_t, s1_t, s2_t, seq):
    n, d = h_hi.shape
    tm = _row_tile(n, seq, 512)
    if tm <= seq:
        tps = seq // tm
        tab_map = lambda i: (i % tps, 0)
    else:
        c_t, s1_t, s2_t = (jnp.tile(t, (tm // seq, 1)) for t in (c_t, s1_t, s2_t))
        tab_map = lambda i: (0, 0)
    row = lambda i: (i, 0)
    return pl.pallas_call(
        _idxproj_kernel,
        grid=(n // tm,),
        in_specs=[pl.BlockSpec((tm, d), row), pl.BlockSpec((tm, d), row),
                  pl.BlockSpec((d, IDX_COLS), lambda i: (0, 0)),
                  pl.BlockSpec((d, IDX_COLS), lambda i: (0, 0)),
                  pl.BlockSpec((tm, IDX_COLS), tab_map),
                  pl.BlockSpec((tm, IDX_COLS), tab_map),
                  pl.BlockSpec((tm, IDX_COLS), tab_map)],
        out_specs=[pl.BlockSpec((tm, 2 * LANES), row), pl.BlockSpec((tm, 2 * LANES), row),
                   pl.BlockSpec((tm, LANES), row), pl.BlockSpec((tm, LANES), row),
                   pl.BlockSpec((tm, LANES), row)],
        out_shape=[jax.ShapeDtypeStruct((n, 2 * LANES), BF16), jax.ShapeDtypeStruct((n, 2 * LANES), BF16),
                   jax.ShapeDtypeStruct((n, LANES), F32), jax.ShapeDtypeStruct((n, LANES), BF16),
                   jax.ShapeDtypeStruct((n, LANES), BF16)],
        compiler_params=_params(("arbitrary",)),
        name="idxproj",
    )(h_hi, h_lo, w_hi, w_lo, c_t, s1_t, s2_t)


def _sb_kernel(q_ref, k_ref, v_ref, o_ref, acc_scr, z_scr, i_scr, *, tq, tk, past, group):
    q0 = past + pl.program_id(2) * tq
    n_kt = (q0 + tq - 2) // tk + 1
    n_clear = q0 // tk
    row_pos = q0 + lax.broadcasted_iota(I32, (tq, 1), 0)
    col = lax.broadcasted_iota(I32, (1, tk), 1)
    suffix = jnp.where(lax.broadcasted_iota(I32, (tk, tk), 0) >= lax.broadcasted_iota(I32, (tk, tk), 1),
                       1.0, 0.0).astype(BF16)
    suffix2 = jnp.concatenate([suffix, suffix], axis=0)
    acc_scr[...] = jnp.zeros_like(acc_scr)

    heads = [slice(h * HEAD_DIM, (h + 1) * HEAD_DIM) for h in range(group)]

    def tiles(js, carries, masked):
        offs = [pl.multiple_of(j * tk, tk) for j in js]
        if masked:
            valids = [(off + col) < row_pos for off in offs]
        for t, off in enumerate(offs):
            for h, hs in enumerate(heads):
                z_scr[t * group + h] = _dot_t(q_ref[:, hs], k_ref[pl.ds(off, tk), hs])
        for t in range(len(js)):
            for h in range(group):
                z = z_scr[t * group + h]
                sp = jnp.maximum(z, 0.0) + jnp.log2(1.0 + jnp.exp2(-jnp.abs(z)))
                if masked:
                    sp = jnp.where(valids[t], sp, 0.0)
                hi, lo = _split(sp)
                i_scr[t * group + h] = _dot(jnp.concatenate([hi, lo], axis=1), suffix2)
        carries = list(carries)
        for t, off in enumerate(offs):
            for h, hs in enumerate(heads):
                w = jnp.exp2(z_scr[t * group + h] - i_scr[t * group + h] - carries[h])
                if masked:
                    w = jnp.where(valids[t], w, 0.0)
                acc_scr[h] += _dot(w.astype(BF16), v_ref[pl.ds(off, tk), hs])
                carries[h] = carries[h] + i_scr[t * group + h, :, 0:1]
        return tuple(carries)

    def least(carries):
        m = carries[0]
        for c in carries[1:]:
            m = jnp.minimum(m, c)
        return jnp.min(m)

    st = tuple(jnp.zeros((tq, 1), F32) for _ in range(group))
    st = lax.fori_loop(0, n_kt - n_clear, lambda t, c: tiles([n_kt - 1 - t], c, True), st)

    def more(state):
        t, low, _ = state
        return jnp.logical_and(t < n_clear, low < UNDERFLOW_LOG2)

    def step(state):
        t, _, carries = state
        carries = tiles([n_clear - 1 - t], carries, False)
        return t + 1, least(carries), carries

    lax.while_loop(more, step, (jnp.int32(0), least(st), st))
    for h in range(group):
        o_ref[:, h * HEAD_DIM:(h + 1) * HEAD_DIM] = acc_scr[h].astype(o_ref.dtype)


def _sb(q, k_all, v_all, nb, seq, n_keys, past, nh):
    tq = _tile(seq, 256, 16)
    tk = 256
    nq = seq // tq
    group = 4 if nh % 4 == 0 else (2 if nh % 2 == 0 else 1)
    gw = group * HEAD_DIM
    (q, q0), (k_all, k0), (v_all, v0) = ((a, c // gw) for a, c in (q, k_all, v_all))
    return pl.pallas_call(
        functools.partial(_sb_kernel, tq=tq, tk=tk, past=past, group=group),
        grid=(nb, nh // group, nq),
        in_specs=[pl.BlockSpec((tq, gw), lambda b, h, i: (b * nq + i, q0 + h)),
                  pl.BlockSpec((n_keys, gw), lambda b, h, i: (b, k0 + h)),
                  pl.BlockSpec((n_keys, gw), lambda b, h, i: (b, v0 + h))],
        out_specs=pl.BlockSpec((tq, gw), lambda b, h, i: (b * nq + i, h)),
        out_shape=jax.ShapeDtypeStruct((q.shape[0], nh * HEAD_DIM), BF16),
        scratch_shapes=[pltpu.VMEM((group, tq, HEAD_DIM), F32), pltpu.VMEM((group, tq, tk), F32),
                        pltpu.VMEM((group, tq, tk), F32)],
        compiler_params=_params(("arbitrary", "arbitrary", "arbitrary")),
        name="sb",
    )(q, k_all, v_all)


def _fold8(parts, op):
    while len(parts) > 1:
        parts = [op(parts[i], parts[i + 1]) if i + 1 < len(parts) else parts[i] for i in range(0, len(parts), 2)]
    return parts[0]


def _dsat_kernel(q_ref, k_ref, vt_ref, qi_ref, ki_ref, kwt_ref, o_ref, key_scr, acc_scr, s_scr, p_scr,
                 *, tq, tk, past, n_valid, n_sel, nh):
    q0 = past + pl.program_id(1) * tq
    col_pos = q0 + lax.broadcasted_iota(I32, (1, tq), 1)
    col_lim = jnp.minimum((col_pos // CHUNK + 1) * CHUNK, n_valid)
    n_kt = (jnp.minimum(((q0 + tq - 1) // CHUNK + 1) * CHUNK, n_valid) + tk - 1) // tk
    row = lax.broadcasted_iota(I32, (tk, 1), 0)
    ksel = float(n_sel)
    groups = [slice(8 * r, 8 * r + 8) for r in range(tk // 8)]

    wrow = kwt_ref[D_IDX:D_IDX + H_IDX, :] * ((H_IDX ** -0.5) * (D_IDX ** -0.5))

    def score_pair(t, _):
        offs = [pl.multiple_of(jnp.minimum(2 * t + u, n_kt - 1) * tk, tk) for u in range(2)]
        raw = [[_dot_t(ki_ref[pl.ds(off, tk), :], qi_ref[:, h * IDX_K:(h + 1) * IDX_K]) for h in range(H_IDX)]
               for off in offs]
        for off, dots in zip(offs, raw):
            score = jnp.zeros((tk, tq), F32)
            for h, s in enumerate(dots):
                score = score + wrow[h:h + 1, :] * jnp.maximum(s, 0.0)
            bits = pltpu.bitcast(score, I32)
            key = jnp.where(bits < 0, INT_MIN - bits, bits)
            key_scr[pl.ds(off, tk), :] = jnp.where((off + row) < col_lim, key, INT_MIN)
        return 0

    lax.fori_loop(0, (n_kt + 1) // 2, score_pair, 0)

    def count_ge(thr8):
        def body(t, cnt):
            for u in range(2):
                j = 2 * t + u
                off = pl.multiple_of(jnp.minimum(j, n_kt - 1) * tk, tk)
                key = key_scr[pl.ds(off, tk), :]
                part = _fold8([jnp.where(key[g, :] >= thr8, 1.0, 0.0) for g in groups], jnp.add)
                cnt = cnt + (part if u == 0 else jnp.where(j < n_kt, part, 0.0))
            return cnt

        cnt = lax.fori_loop(0, (n_kt + 1) // 2, body, jnp.zeros((8, tq), F32))
        return jnp.broadcast_to(jnp.sum(cnt, axis=0, keepdims=True), (8, tq))

    def bit_step(it, st):
        cand, have = st
        trial = cand | lax.shift_left(jnp.int32(1), 30 - it)
        cnt = count_ge(trial)
        take = cnt >= ksel
        return jnp.where(take, trial, cand), jnp.where(take, cnt, have)

    zero8 = jnp.zeros((8, tq), I32)
    cnt0 = count_ge(zero8)
    start = (jnp.where(cnt0 >= ksel, zero8, INT_MIN), jnp.where(cnt0 >= ksel, cnt0, float(2 ** 30)))
    thr8, have8 = lax.fori_loop(0, 31, bit_step, start)
    thr = thr8[0:1, :]
    overfull = jnp.max(jnp.where(have8 > ksel, 1.0, 0.0)) > 0.0

    @pl.when(jnp.logical_not(overfull))
    def _():
        def bias_tile(j, _):
            off = pl.multiple_of(j * tk, tk)
            bias = jnp.where(key_scr[pl.ds(off, tk), :] >= thr, 0.0, NEG)
            key_scr[pl.ds(off, tk), :] = pltpu.bitcast(bias, I32)
            return 0

        lax.fori_loop(0, n_kt, bias_tile, 0)

    @pl.when(overfull)
    def _():
        no_thr8 = thr8 == INT_MIN
        cnt_gt = count_ge(jnp.where(no_thr8, thr8, thr8 + 1))
        need = jnp.where(no_thr8, 0.0, ksel - cnt_gt)[0:1, :]
        prefix = jnp.where(lax.broadcasted_iota(I32, (tk, tk), 0) >= lax.broadcasted_iota(I32, (tk, tk), 1),
                           1.0, 0.0).astype(BF16)

        def bias_tile(j, carry):
            off = pl.multiple_of(j * tk, tk)
            key = key_scr[pl.ds(off, tk), :]
            eq = key == thr
            eqf = jnp.where(eq, 1.0, 0.0)
            rank_incl = _dot(prefix, eqf.astype(BF16)) + carry
            tie_ok = (rank_incl - eqf) < need
            bias = jnp.where(key > thr, 0.0, jnp.where(eq, jnp.where(tie_ok, 0.0, NEG), NEG))
            key_scr[pl.ds(off, tk), :] = pltpu.bitcast(bias, I32)
            return rank_incl[tk - 1:tk, :]

        lax.fori_loop(0, n_kt, bias_tile, jnp.zeros((1, tq), F32))

    acc_scr[...] = jnp.zeros_like(acc_scr)
    ones_v = jnp.ones((ONES_ROWS, tk), BF16)
    heads = [slice(h * HEAD_DIM, (h + 1) * HEAD_DIM) for h in range(nh)]

    def logits(j, slot, ms):
        off = pl.multiple_of(j * tk, tk)
        bias = pltpu.bitcast(key_scr[pl.ds(off, tk), :], F32)
        for h, hs in enumerate(heads):
            s_scr[slot, h] = _dot_t(k_ref[pl.ds(off, tk), hs], q_ref[:, hs]) + bias
        out = []
        for h, m in enumerate(ms):
            s = s_scr[slot, h]
            top = jnp.max(_fold8([s[g, :] for g in groups], jnp.maximum), axis=0, keepdims=True)
            out.append(jnp.maximum(m, top))
        return tuple(out)

    def accumulate(j, slot, m_old, m_new):
        off = pl.multiple_of(j * tk, tk)
        for h in range(nh):
            p_scr[h] = jnp.exp2(s_scr[slot, h] - m_new[h]).astype(BF16)
        for h, hs in enumerate(heads):
            vt_ext = jnp.concatenate([vt_ref[hs, pl.ds(off, tk)], ones_v], axis=0)
            acc_scr[h] = jnp.exp2(m_old[h] - m_new[h]) * acc_scr[h] + _dot(vt_ext, p_scr[h])

    m_none = tuple(jnp.full((1, tq), NEG, F32) for _ in range(nh))
    m_first = logits(0, 0, m_none)

    def attn_step(j, st):
        m_old, m_cur = st
        slot = j % 2
        accumulate(j - 1, 1 - slot, m_old, m_cur)
        return m_cur, logits(j, slot, m_cur)

    m_old, m_cur = lax.fori_loop(1, n_kt, attn_step, (m_none, m_first))
    accumulate(n_kt - 1, (n_kt - 1) % 2, m_old, m_cur)
    for h, hs in enumerate(heads):
        out_t = acc_scr[h, :HEAD_DIM, :] / acc_scr[h, HEAD_DIM:HEAD_DIM + 1, :]
        o_ref[:, hs] = out_t.T.astype(o_ref.dtype)


def _dsat(q, k_all, vt_all, qi_cat, ki_cat, kwt, nb, seq, n_keys, n_valid, past, nh):
    tq = _tile(seq, 256, LANES)
    tk = 256
    nq = seq // tq
    width = nh * HEAD_DIM
    n_sel = min(TOPK_MAX, n_valid // 4)
    once = pl.Buffered(1)
    (q, q0), (k_all, k0) = ((a, c // width) for a, c in (q, k_all))
    return pl.pallas_call(
        functools.partial(_dsat_kernel, tq=tq, tk=tk, past=past, n_valid=n_valid, n_sel=n_sel, nh=nh),
        grid=(nb, nq),
        in_specs=[pl.BlockSpec((tq, width), lambda b, i: (b * nq + i, q0)),
                  pl.BlockSpec((n_keys, width), lambda b, i: (b, k0), pipeline_mode=once),
                  pl.BlockSpec((width, n_keys), lambda b, i: (b, 0), pipeline_mode=once),
                  pl.BlockSpec((tq, H_IDX * IDX_K), lambda b, i: (b * nq + i, 0)),
                  pl.BlockSpec((n_keys, IDX_K), lambda b, i: (b, 0), pipeline_mode=once),
                  pl.BlockSpec((LANES, tq), lambda b, i: (0, b * nq + i))],
        out_specs=pl.BlockSpec((tq, width), lambda b, i: (b * nq + i, 0)),
        out_shape=jax.ShapeDtypeStruct((q.shape[0], width), BF16),
        scratch_shapes=[pltpu.VMEM((n_keys, tq), I32), pltpu.VMEM((nh, HEAD_DIM + ONES_ROWS, tq), F32),
                        pltpu.VMEM((2, nh, tk, tq), F32), pltpu.VMEM((nh, tk, tq), BF16)],
        compiler_params=_params(("arbitrary", "arbitrary")),
        name="dsa",
    )(q, k_all, vt_all, qi_cat, ki_cat, kwt)


def _merge_kernel(h_ref, oa_ref, ob_ref, wg1_ref, wg2_ref, b1_ref, b2_ref, wa_ref, wb_ref, o_ref):
    h = h_ref[...]
    g1 = 1.0 / (1.0 + jnp.exp(-(_dot(h, wg1_ref[...]) + b1_ref[...])))
    g2 = 1.0 / (1.0 + jnp.exp(-(_dot(h, wg2_ref[...]) + b2_ref[...])))
    o_ref[...] = (g1 * _dot(oa_ref[...], wa_ref[...]) + g2 * _dot(ob_ref[...], wb_ref[...])).astype(BF16)


def _merge(h_hi, o_a, o_b, w_gate, b_gate, w_pa, w_pb):
    n, d = h_hi.shape
    width = o_a.shape[1]
    tm = _tile(n, 1024, 16)
    tn = _tile(d, 512, LANES)
    nj = d // tn
    b_gate = b_gate.reshape(1, 2 * d)
    row = lambda i, j: (i, 0)
    return pl.pallas_call(
        _merge_kernel,
        grid=(n // tm, nj),
        in_specs=[pl.BlockSpec((tm, d), row), pl.BlockSpec((tm, width), row), pl.BlockSpec((tm, width), row),
                  pl.BlockSpec((d, tn), lambda i, j: (0, j)),
                  pl.BlockSpec((d, tn), lambda i, j: (0, j + nj)),
                  pl.BlockSpec((1, tn), lambda i, j: (0, j)),
                  pl.BlockSpec((1, tn), lambda i, j: (0, j + nj)),
                  pl.BlockSpec((width, tn), lambda i, j: (0, j)),
                  pl.BlockSpec((width, tn), lambda i, j: (0, j))],
        out_specs=pl.BlockSpec((tm, tn), lambda i, j: (i, j)),
        out_shape=jax.ShapeDtypeStruct((n, d), BF16),
        compiler_params=_params(("arbitrary", "arbitrary")),
        name="merge",
    )(h_hi, o_a, o_b, w_gate, w_gate, b_gate, b_gate, w_pa, w_pb)


def _outproj_kernel(m_ref, w_ref, x_ref, mod_ref, gm_ref, gf_ref, x1_ref, h2_ref):
    mod = mod_ref[0]
    x1 = x_ref[...] + mod[2:3, :] * _rms(_dot(m_ref[...], w_ref[...]), gm_ref[...])
    x1_ref[...] = x1
    h2_ref[...] = (_rms(x1, gf_ref[...]) * (1.0 + mod[4:5, :]) + mod[3:4, :]).astype(BF16)


def _outproj(merged, w_out, x2, mod, g_post_mix, g_pre_ffn, seq):
    n, d = x2.shape
    tm = _tile(seq, 512, 16)
    tps = seq // tm
    row = lambda i: (i, 0)
    fix = lambda i: (0, 0)
    return pl.pallas_call(
        _outproj_kernel,
        grid=(n // tm,),
        in_specs=[pl.BlockSpec((tm, d), row), pl.BlockSpec((d, d), fix, pipeline_mode=pl.Buffered(1)),
                  pl.BlockSpec((tm, d), row),
                  pl.BlockSpec((1, N_MOD, d), lambda i: (i // tps, 0, 0)),
                  pl.BlockSpec((1, d), fix), pl.BlockSpec((1, d), fix)],
        out_specs=[pl.BlockSpec((tm, d), row), pl.BlockSpec((tm, d), row)],
        out_shape=[jax.ShapeDtypeStruct((n, d), F32), jax.ShapeDtypeStruct((n, d), BF16)],
        compiler_params=_params(("arbitrary",)),
        name="outproj",
    )(merged, w_out, x2, mod, g_post_mix.reshape(1, d), g_pre_ffn.reshape(1, d))


def _gelu_tanh(x):
    return 0.5 * x * (1.0 + jnp.tanh(0.7978845608028654 * (x + 0.044715 * (x * x * x))))


def _ffn_kernel(h_ref, halo_ref, x1_ref, mod_ref, st_ref, wg_ref, wu_ref, wd_ref, wc_ref, bc_ref, gp_ref,
                y_ref, nc_ref, g_scr, act_scr, *, tm, nsub, tps):
    i = pl.program_id(0)
    c = pl.program_id(1)
    last = pl.num_programs(1) - 1
    sub = tm // nsub
    cur = c % 2

    def step(down, gate_up):
        if gate_up:
            h = h_ref[...]
            wg = wg_ref[...]
            g = _dot(h, wg)
            u = _dot(h, wu_ref[...])
            wc = wc_ref[...]
            bc = bc_ref[...]
            g_scr[HALO:HALO + tm, :] = g
        if down == "set":
            y_ref[...] = _dot(act_scr[1 - cur], wd_ref[...])
        elif down == "add":
            y_ref[...] += _dot(act_scr[1 - cur], wd_ref[...])
        if not gate_up:
            return

        def conv(lo, rows):
            return (bc + wc[0:1, :] * g_scr[lo - 2:lo - 2 + rows, :]
                    + wc[1:2, :] * g_scr[lo - 1:lo - 1 + rows, :] + wc[2:3, :] * g_scr[lo:lo + rows, :])

        if nsub == 1:
            g_halo = _dot(halo_ref[...], wg)
            g_scr[0:HALO, :] = g_halo
            g_scr[HALO - 2:HALO, :] = jnp.where(i % tps == 0, st_ref[0], g_halo[HALO - 2:HALO, :])
            nc_ref[0] = g_scr[HALO + tm - 2:HALO + tm, :]
            act_scr[cur] = (_gelu_tanh(conv(HALO, tm)) * u).astype(BF16)
        else:
            for s in range(nsub):
                lo = HALO + s * sub
                nc_ref[s] = g_scr[lo + sub - 2:lo + sub, :]
                g_scr[lo - 2:lo, :] = st_ref[s]
                act_scr[cur, s * sub:(s + 1) * sub, :] = (
                    _gelu_tanh(conv(lo, sub)) * u[s * sub:(s + 1) * sub, :]).astype(BF16)

    pl.when(c == 0)(lambda: step(None, True))
    pl.when(c == 1)(lambda: step("set", True))
    pl.when(jnp.logical_and(c > 1, c < last))(lambda: step("add", True))

    @pl.when(c == last)
    def _():
        step("add", False)
        for s in range(nsub):
            rows = slice(s * sub, (s + 1) * sub)
            y_ref[rows, :] = x1_ref[rows, :] + mod_ref[s][5:6, :] * _rms(y_ref[rows, :], gp_ref[...])


def _ffn(h2, x1, mod, state, w_gate, w_up, w_down, w_conv, b_conv, g_post, seq):
    n, d = x1.shape
    dff = w_gate.shape[1]
    nb = n // seq
    tm = _row_tile(n, seq, 1024)
    tf = _tile(dff, 512, LANES)
    n_chunks = dff // tf
    assert n_chunks >= 2, "the lagged down projection needs at least two d_ff chunks"
    up = lambda c: jnp.minimum(c, n_chunks - 1)
    down = lambda c: jnp.maximum(c - 1, 0)
    per_tile_c = lambda i, c: (i, 0, up(c))
    if tm <= seq:
        nsub, tps = 1, seq // tm
        per_seq = lambda i, c: (i // tps, 0, 0)
        per_seq_c = lambda i, c: (i // tps, 0, up(c))
    else:
        nsub, tps = tm // seq, 1
        per_seq = lambda i, c: (i, 0, 0)
        per_seq_c = per_tile_c
    halo_blocks = tm // HALO
    row = lambda i, c: (i, 0)
    y, new_conv = pl.pallas_call(
        functools.partial(_ffn_kernel, tm=tm, nsub=nsub, tps=tps),
        grid=(n // tm, n_chunks + 1),
        in_specs=[pl.BlockSpec((tm, d), row, pipeline_mode=pl.Buffered(1)),
                  pl.BlockSpec((HALO, d), lambda i, c: (jnp.maximum(i * halo_blocks - 1, 0), 0)),
                  pl.BlockSpec((tm, d), row, pipeline_mode=pl.Buffered(1)),
                  pl.BlockSpec((nsub, N_MOD, d), per_seq),
                  pl.BlockSpec((nsub, CONV_W - 1, tf), per_seq_c),
                  pl.BlockSpec((d, tf), lambda i, c: (0, up(c))),
                  pl.BlockSpec((d, tf), lambda i, c: (0, up(c))),
                  pl.BlockSpec((tf, d), lambda i, c: (down(c), 0)),
                  pl.BlockSpec((CONV_W, tf), lambda i, c: (0, up(c))),
                  pl.BlockSpec((1, tf), lambda i, c: (0, up(c))),
                  pl.BlockSpec((1, d), lambda i, c: (0, 0))],
        out_specs=[pl.BlockSpec((tm, d), row, pipeline_mode=pl.Buffered(1)),
                   pl.BlockSpec((nsub, CONV_W - 1, tf), per_tile_c)],
        out_shape=[jax.ShapeDtypeStruct((n, d), F32),
                   jax.ShapeDtypeStruct((nb * tps, CONV_W - 1, dff), F32)],
        scratch_shapes=[pltpu.VMEM((HALO + tm, tf), F32), pltpu.VMEM((2, tm, tf), BF16)],
        compiler_params=_params(("arbitrary", "arbitrary")),
        name="ffn",
    )(h2, h2, x1, mod, state, w_gate, w_up, w_down, w_conv, b_conv.reshape(1, dff), g_post.reshape(1, d))
    return y, new_conv.reshape(nb, tps, CONV_W - 1, dff)[:, tps - 1]


def _rope_tables(pos):
    def angles(dim):
        half = dim // 2
        inv = ROPE_THETA ** (-jnp.arange(half, dtype=F32) * 2.0 / dim)
        ang = pos.astype(F32)[:, None] * inv[None, :]
        return jnp.cos(ang), jnp.sin(ang)

    cos, sin = angles(HEAD_DIM)
    cos_h = jnp.concatenate([cos, cos], axis=1)
    sin_h = jnp.concatenate([-sin, sin], axis=1)

    cos, sin = angles(D_IDX)
    zero = jnp.zeros_like(sin)
    n_rot = H_IDX + 1
    tail = IDX_COLS - n_rot * D_IDX
    t = pos.shape[0]
    c_i = jnp.concatenate([jnp.tile(jnp.concatenate([cos, cos], 1), (1, n_rot)), jnp.ones((t, tail), F32)], 1)
    s1_i = jnp.concatenate([jnp.tile(jnp.concatenate([-sin, zero], 1), (1, n_rot)), jnp.zeros((t, tail), F32)], 1)
    s2_i = jnp.concatenate([jnp.tile(jnp.concatenate([zero, sin], 1), (1, n_rot)), jnp.zeros((t, tail), F32)], 1)
    return cos_h, sin_h, c_i, s1_i, s2_i


def _with_past(past, new, nb, seq, n_keys):
    w = new.shape[1]
    if past is None and n_keys == seq:
        return new
    parts = [] if past is None else [past.astype(new.dtype).reshape(nb, -1, w)]
    parts.append(new.reshape(nb, seq, w))
    have = sum(p.shape[1] for p in parts)
    if n_keys > have:
        parts.append(jnp.zeros((nb, n_keys - have, w), new.dtype))
    return jnp.concatenate(parts, axis=1).reshape(nb * n_keys, w)


def _layer(x, mod, past, conv_state, p):
    nb, seq, d = x.shape
    n = nb * seq
    nh = p["w_pa"].shape[0] // HEAD_DIM
    width = nh * HEAD_DIM
    n_past = 0 if past is None else past["a_k"].shape[1]
    n_valid = n_past + seq
    n_keys = -(-n_valid // 256) * 256
    x2 = x.reshape(n, d)

    h_hi, h_lo = _norm(x2, p["g_pre_mix"], mod, seq)
    cos_h, sin_h, c_i, s1_i, s2_i = _rope_tables(n_past + jnp.arange(seq, dtype=I32))
    p16, kv32, vt_new = _inproj(h_hi, p["w_main"], cos_h, sin_h, seq, width)
    col = lambda a, g: a[:, g * width:(g + 1) * width]
    qi_hi, qi_lo, kw, ki_hi, ki_lo = _idxproj(h_hi, h_lo, p["w_idx_hi"], p["w_idx_lo"], c_i, s1_i, s2_i, seq)

    def keys(g, past_rows):
        if past_rows is None and n_keys == seq:
            return p16, g * width
        return _with_past(past_rows, col(p16, g), nb, seq, n_keys), 0

    q_hi = qi_hi.reshape(n, H_IDX, D_IDX)
    q_lo = qi_lo.reshape(n, H_IDX, D_IDX)
    qi_cat = jnp.concatenate([q_hi, q_hi, q_lo, jnp.zeros_like(q_hi)], axis=2).reshape(n, H_IDX * IDX_K)
    k_hi, k_lo = ki_hi[:, :D_IDX], ki_lo[:, :D_IDX]
    if past is None:
        pa_k = pa_v = pb_k = pb_v = pi_hi = pi_lo = None
    else:
        pa_k, pa_v, pb_k, pb_v = (past[k].reshape(nb, n_past, width) for k in ("a_k", "a_v", "b_k", "b_v"))
        pi_hi, pi_lo = _split(past["b_kidx"].astype(F32))
    k_hi = _with_past(pi_hi, k_hi, nb, seq, n_keys)
    k_lo = _with_past(pi_lo, k_lo, nb, seq, n_keys)
    ki_cat = jnp.concatenate([k_hi, k_lo, k_hi, jnp.zeros_like(k_hi)], axis=1)

    o_a = _sb((p16, 0), keys(1, pa_k), keys(2, pa_v), nb, seq, n_keys, n_past, nh)
    seq_q = max(seq, LANES)
    pad_q = lambda a: a if seq_q == seq else jnp.pad(
        a.reshape(nb, seq, -1), ((0, 0), (0, seq_q - seq), (0, 0))).reshape(nb * seq_q, -1)
    if vt_new is not None and past is None and n_keys == seq:
        vt_all = vt_new
    else:
        vt_all = _with_past(pb_v, col(p16, 5), nb, seq, n_keys).reshape(nb, n_keys, width)
        vt_all = jnp.swapaxes(vt_all, 1, 2).reshape(nb * width, n_keys)
    q_b = (p16, 3 * width) if seq_q == seq else (pad_q(col(p16, 3)), 0)
    o_b = _dsat(q_b, keys(4, pb_k), vt_all, pad_q(qi_cat), ki_cat,
                pad_q(kw).T, nb, seq_q, n_keys, n_valid, n_past, nh)
    if seq_q != seq:
        o_b = o_b.reshape(nb, seq_q, width)[:, :seq].reshape(n, width)

    merged = _merge(h_hi, o_a, o_b, p["w_gate"], p["b_gate"], p["w_pa"], p["w_pb"])
    x1, h2 = _outproj(merged, p["w_out"], x2, mod, p["g_post_mix"], p["g_pre_ffn"], seq)
    y, new_conv = _ffn(h2, x1, mod, conv_state, p["w_ffn_gate"], p["w_ffn_up"], p["w_ffn_down"],
                       p["w_conv"], p["b_conv"], p["g_post_ffn"], seq)

    heads = lambda g: kv32[g].reshape(nb, seq, nh, HEAD_DIM)
    new = (heads(1), heads(2), heads(4), heads(5), kw[:, :D_IDX].reshape(nb, seq, D_IDX), new_conv)
    return y.reshape(nb, seq, d), new


def kernel(x_prompt, x_sample, c_prompt, c_sample, cache_a_k, cache_a_v, cache_b_k, cache_b_v, cache_b_kidx,
           state_ffn_conv, w_mod, b_mod, g_pre_mix, w_in, w_merge_gate, b_merge_gate, w_proj_a, w_proj_b, w_out,
           g_post_mix, g_pre_ffn, w_ffn_gate, w_ffn_up, w_conv, b_conv, w_ffn_down, g_post_ffn):
    depth = w_mod.shape[0]
    nbp, nbs = x_prompt.shape[0], x_sample.shape[0]
    d = x_prompt.shape[2]
    width = w_proj_a.shape[1]
    dff = w_ffn_gate.shape[2]
    assert w_in.shape[2] == 6 * width + H_IDX * D_IDX + D_IDX + H_IDX
    assert cache_b_kidx.shape[-1] == D_IDX and cache_a_k.shape[-1] == HEAD_DIM

    y_p, y_s = x_prompt, x_sample
    st_p = [[] for _ in range(6)]
    st_s = [[] for _ in range(6)]
    for l in range(depth):
        w_idx = jnp.pad(w_in[l][:, 6 * width:], ((0, 0), (0, IDX_COLS - (w_in.shape[2] - 6 * width))))
        w_idx_hi, w_idx_lo = _split(w_idx)
        p = dict(
            g_pre_mix=g_pre_mix[l], w_main=w_in[l][:, :6 * width].astype(BF16), w_idx_hi=w_idx_hi, w_idx_lo=w_idx_lo,
            w_gate=w_merge_gate[l].astype(BF16), b_gate=b_merge_gate[l],
            w_pa=w_proj_a[l].astype(BF16), w_pb=w_proj_b[l].astype(BF16), w_out=w_out[l].astype(BF16),
            g_post_mix=g_post_mix[l], g_pre_ffn=g_pre_ffn[l],
            w_ffn_gate=w_ffn_gate[l].astype(BF16), w_ffn_up=w_ffn_up[l].astype(BF16),
            w_ffn_down=w_ffn_down[l].astype(BF16), w_conv=w_conv[l], b_conv=b_conv[l], g_post_ffn=g_post_ffn[l])

        c_all = jnp.concatenate([c_prompt, c_sample], axis=0)
        pad = -c_all.shape[0] % 16
        mod = _mod(jnp.pad(c_all, ((0, pad), (0, 0))), w_mod[l], b_mod[l]).reshape(-1, N_MOD, d)
        past = dict(a_k=cache_a_k[l], a_v=cache_a_v[l], b_k=cache_b_k[l], b_v=cache_b_v[l], b_kidx=cache_b_kidx[l])
        y_p, new_p = _layer(y_p, mod[:nbp], None, jnp.zeros((nbp, CONV_W - 1, dff), F32), p)
        y_s, new_s = _layer(y_s, mod[nbp:nbp + nbs], past, state_ffn_conv[l], p)
        for i in range(6):
            st_p[i].append(new_p[i])
            st_s[i].append(new_s[i])
    sp = [jnp.stack(s, axis=0) for s in st_p]
    ss = [jnp.stack(s, axis=0) for s in st_s]
    return (y_p, y_s, sp[0], sp[1], sp[2], sp[3], sp[4], sp[5], ss[0], ss[1], ss[2], ss[3], ss[4], ss[5])
```

```python
import functools

import jax
import jax.numpy as jnp
from jax import lax
from jax.experimental import pallas as pl
from jax.experimental.pallas import tpu as pltpu

F32 = jnp.float32
BF16 = jnp.bfloat16
I32 = jnp.int32

CHUNK = 64
HEAD_DIM = 128
H_IDX = 4
D_IDX = 64
TOPK_MAX = 256
CONV_W = 3
ROPE_THETA = 10000.0
RMS_EPS = 1e-6
N_MOD = 6
IDX_COLS = 384
IDX_K = 256
LANES = 128
HALO = 16
ONES_ROWS = 16
LOG2E = 1.4426950408889634
UNDERFLOW_LOG2 = 160.0
NEG = -1e30
INT_MIN = -2 ** 31
VMEM_LIMIT = 56 * 1024 * 1024


def _tile(n, pref, mult):
    t = min(pref, n)
    t -= t % mult
    while t >= mult:
        if n % t == 0:
            return t
        t -= mult
    return n


def _row_tile(n, seq, pref):
    if seq >= pref or n == seq:
        return _tile(seq, pref, 16)
    return _tile(n, pref, seq)


def _params(sem):
    return pltpu.CompilerParams(dimension_semantics=sem, vmem_limit_bytes=VMEM_LIMIT)


def _split(x):
    hi = x.astype(BF16)
    lo = (x - hi.astype(F32)).astype(BF16)
    return hi, lo


def _dot(a, b):
    return jnp.dot(a, b, preferred_element_type=F32)


def _dot_t(a, b):
    return lax.dot_general(a, b, (((1,), (1,)), ((), ())), preferred_element_type=F32)


def _rms(x, g):
    ms = jnp.mean(x * x, axis=-1, keepdims=True)
    return x * lax.rsqrt(ms + RMS_EPS) * g


def _mod_kernel(c_ref, w_ref, b_ref, o_ref):
    c = c_ref[...]
    s = c / (1.0 + jnp.exp(-c))
    s_hi, s_lo = _split(s)
    w_hi, w_lo = _split(w_ref[...])
    o_ref[...] = _dot(s_hi, w_hi) + _dot(s_hi, w_lo) + _dot(s_lo, w_hi) + b_ref[...]


def _mod(c, w, b):
    bc, d = c.shape
    n = w.shape[1]
    tn = _tile(n, 1024, LANES)
    return pl.pallas_call(
        _mod_kernel,
        grid=(n // tn,),
        in_specs=[pl.BlockSpec((bc, d), lambda j: (0, 0)),
                  pl.BlockSpec((d, tn), lambda j: (0, j)),
                  pl.BlockSpec((1, tn), lambda j: (0, j))],
        out_specs=pl.BlockSpec((bc, tn), lambda j: (0, j)),
        out_shape=jax.ShapeDtypeStruct((bc, n), F32),
        compiler_params=_params(("arbitrary",)),
        name="mod",
    )(c, w, b.reshape(1, n))


def _norm_kernel(x_ref, g_ref, mod_ref, hi_ref, lo_ref):
    m = mod_ref[0]
    h = _rms(x_ref[...], g_ref[...]) * (1.0 + m[1:2, :]) + m[0:1, :]
    hi, lo = _split(h)
    hi_ref[...] = hi
    lo_ref[...] = lo


def _norm(x2, g, mod, seq):
    n, d = x2.shape
    tr = _tile(seq, 512, 16)
    tps = seq // tr
    return pl.pallas_call(
        _norm_kernel,
        grid=(n // tr,),
        in_specs=[pl.BlockSpec((tr, d), lambda i: (i, 0)),
                  pl.BlockSpec((1, d), lambda i: (0, 0)),
                  pl.BlockSpec((1, N_MOD, d), lambda i: (i // tps, 0, 0))],
        out_specs=[pl.BlockSpec((tr, d), lambda i: (i, 0))] * 2,
        out_shape=[jax.ShapeDtypeStruct((n, d), BF16)] * 2,
        compiler_params=_params(("arbitrary",)),
        name="norm",
    )(x2, g.reshape(1, d), mod)


N_PROJ = 6


KV_GROUPS = (1, 2, 4, 5)


def _inproj_kernel(a_ref, w_ref, c_ref, s_ref, p16_ref, ka_ref, va_ref, kb_ref, vb_ref, *rest, tm, nh):
    r_scr = rest[-1]
    acc = _dot(a_ref[...], w_ref[...])
    c = c_ref[0]
    s = s_ref[0]
    for k in range(nh):
        sl = slice(k * HEAD_DIM, (k + 1) * HEAD_DIM)
        xs = acc[:, sl]
        r = xs * c + pltpu.roll(xs, HEAD_DIM // 2, 1) * s
        r_scr[:, sl] = r
        p16_ref[:, sl] = r.astype(BF16)
    for grp, ref in zip(KV_GROUPS, (ka_ref, va_ref, kb_ref, vb_ref)):
        @pl.when(pl.program_id(1) == grp)
        def _(ref=ref):
            ref[...] = r_scr[...]

    if len(rest) == 2:
        @pl.when(pl.program_id(1) == N_PROJ - 1)
        def _():
            rest[0][...] = r_scr[...].T.astype(BF16)


def _inproj(h_hi, w_main, cos_t, sin_t, seq, width):
    n, d = h_hi.shape
    nh = width // HEAD_DIM
    tm = _row_tile(n, seq, 1024)
    with_vt = tm <= seq
    tn = width
    npj = 1
    scale = LOG2E * HEAD_DIM ** -0.5
    one, zero = jnp.ones_like(cos_t), jnp.zeros_like(cos_t)
    c_t = jnp.stack([scale * one, one, one, scale * cos_t, cos_t, one])
    s_t = jnp.stack([zero, zero, zero, scale * sin_t, sin_t, zero])
    if tm <= seq:
        tps = seq // tm
        tab_map = lambda i, j: (j // npj, i % tps, 0)
    else:
        c_t = jnp.tile(c_t, (1, tm // seq, 1))
        s_t = jnp.tile(s_t, (1, tm // seq, 1))
        tab_map = lambda i, j: (j // npj, 0, 0)
    out = pl.pallas_call(
        functools.partial(_inproj_kernel, tm=tm, nh=nh),
        grid=(n // tm, N_PROJ * npj),
        in_specs=[pl.BlockSpec((tm, d), lambda i, j: (i, 0)),
                  pl.BlockSpec((d, tn), lambda i, j: (0, j)),
                  pl.BlockSpec((1, tm, HEAD_DIM), tab_map),
                  pl.BlockSpec((1, tm, HEAD_DIM), tab_map)],
        out_specs=[pl.BlockSpec((tm, tn), lambda i, j: (i, j))]
        + [pl.BlockSpec((tm, width), lambda i, j: (i, 0), pipeline_mode=pl.Buffered(1))] * len(KV_GROUPS)
        + ([pl.BlockSpec((width, tm), lambda i, j: (i // tps, i % tps))] if with_vt else []),
        out_shape=[jax.ShapeDtypeStruct((n, N_PROJ * width), BF16)]
        + [jax.ShapeDtypeStruct((n, width), F32)] * len(KV_GROUPS)
        + ([jax.ShapeDtypeStruct((n // seq * width, seq), BF16)] if with_vt else []),
        scratch_shapes=[pltpu.VMEM((tm, tn), F32)],
        compiler_params=_params(("arbitrary", "arbitrary")),
        name="inproj",
    )(h_hi, w_main, c_t, s_t)
    return out[0], dict(zip(KV_GROUPS, out[1:1 + len(KV_GROUPS)])), (out[-1] if with_vt else None)


def _idxproj_kernel(hi_ref, lo_ref, whi_ref, wlo_ref, c_ref, s1_ref, s2_ref,
                    qhi_ref, qlo_ref, kw_ref, khi_ref, klo_ref):
    a_hi = hi_ref[...]
    w_hi = whi_ref[...]
    acc = _dot(a_hi, w_hi) + _dot(a_hi, wlo_ref[...]) + _dot(lo_ref[...], w_hi)
    for s in range(IDX_COLS // LANES):
        sl = slice(s * LANES, (s + 1) * LANES)
        xs = acc[:, sl]
        r = (xs * c_ref[:, sl] + pltpu.roll(xs, LANES - D_IDX // 2, 1) * s1_ref[:, sl]
             + pltpu.roll(xs, D_IDX // 2, 1) * s2_ref[:, sl])
        hi, lo = _split(r)
        if s < 2:
            qhi_ref[:, sl] = hi
            qlo_ref[:, sl] = lo
        else:
            kw_ref[...] = r
            khi_ref[...] = hi
            klo_ref[...] = lo


def _idxproj(h_hi, h_lo, w_hi, w_lo, c_t, s1_t, s2_t, seq):
    n, d = h_hi.shape
    tm = _row_tile(n, seq, 512)
    if tm <= seq:
        tps = seq // tm
        tab_map = lambda i: (i % tps, 0)
    else:
        c_t, s1_t, s2_t = (jnp.tile(t, (tm // seq, 1)) for t in (c_t, s1_t, s2_t))
        tab_map = lambda i: (0, 0)
    row = lambda i: (i, 0)
    return pl.pallas_call(
        _idxproj_kernel,
        grid=(n // tm,),
        in_specs=[pl.BlockSpec((tm, d), row), pl.BlockSpec((tm, d), row),
                  pl.BlockSpec((d, IDX_COLS), lambda i: (0, 0)),
                  pl.BlockSpec((d, IDX_COLS), lambda i: (0, 0)),
                  pl.BlockSpec((tm, IDX_COLS), tab_map),
                  pl.BlockSpec((tm, IDX_COLS), tab_map),
                  pl.BlockSpec((tm, IDX_COLS), tab_map)],
        out_specs=[pl.BlockSpec((tm, 2 * LANES), row), pl.BlockSpec((tm, 2 * LANES), row),
                   pl.BlockSpec((tm, LANES), row), pl.BlockSpec((tm, LANES), row),
                   pl.BlockSpec((tm, LANES), row)],
        out_shape=[jax.ShapeDtypeStruct((n, 2 * LANES), BF16), jax.ShapeDtypeStruct((n, 2 * LANES), BF16),
                   jax.ShapeDtypeStruct((n, LANES), F32), jax.ShapeDtypeStruct((n, LANES), BF16),
                   jax.ShapeDtypeStruct((n, LANES), BF16)],
        compiler_params=_params(("arbitrary",)),
        name="idxproj",
    )(h_hi, h_lo, w_hi, w_lo, c_t, s1_t, s2_t)


def _sb_kernel(q_ref, k_ref, v_ref, o_ref, acc_scr, z_scr, i_scr, *, tq, tk, past, group):
    q0 = past + pl.program_id(2) * tq
    n_kt = (q0 + tq - 2) // tk + 1
    n_clear = q0 // tk
    row_pos = q0 + lax.broadcasted_iota(I32, (tq, 1), 0)
    col = lax.broadcasted_iota(I32, (1, tk), 1)
    suffix = jnp.where(lax.broadcasted_iota(I32, (tk, tk), 0) >= lax.broadcasted_iota(I32, (tk, tk), 1),
                       1.0, 0.0).astype(BF16)
    suffix2 = jnp.concatenate([suffix, suffix], axis=0)
    acc_scr[...] = jnp.zeros_like(acc_scr)

    heads = [slice(h * HEAD_DIM, (h + 1) * HEAD_DIM) for h in range(group)]

    def tiles(js, carries, masked):
        offs = [pl.multiple_of(j * tk, tk) for j in js]
        if masked:
            valids = [(off + col) < row_pos for off in offs]
        for t, off in enumerate(offs):
            for h, hs in enumerate(heads):
                z_scr[t * group + h] = _dot_t(q_ref[:, hs], k_ref[pl.ds(off, tk), hs])
        for t in range(len(js)):
            for h in range(group):
                z = z_scr[t * group + h]
                sp = jnp.maximum(z, 0.0) + jnp.log2(1.0 + jnp.exp2(-jnp.abs(z)))
                if masked:
                    sp = jnp.where(valids[t], sp, 0.0)
                hi, lo = _split(sp)
                i_scr[t * group + h] = _dot(jnp.concatenate([hi, lo], axis=1), suffix2)
        carries = list(carries)
        for t, off in enumerate(offs):
            for h, hs in enumerate(heads):
                w = jnp.exp2(z_scr[t * group + h] - i_scr[t * group + h] - carries[h])
                if masked:
                    w = jnp.where(valids[t], w, 0.0)
                acc_scr[h] += _dot(w.astype(BF16), v_ref[pl.ds(off, tk), hs])
                carries[h] = carries[h] + i_scr[t * group + h, :, 0:1]
        return tuple(carries)

    def least(carries):
        m = carries[0]
        for c in carries[1:]:
            m = jnp.minimum(m, c)
        return jnp.min(m)

    st = tuple(jnp.zeros((tq, 1), F32) for _ in range(group))
    st = lax.fori_loop(0, n_kt - n_clear, lambda t, c: tiles([n_kt - 1 - t], c, True), st)

    def more(state):
        t, low, _ = state
        return jnp.logical_and(t < n_clear, low < UNDERFLOW_LOG2)

    def step(state):
        t, _, carries = state
        carries = tiles([n_clear - 1 - t], carries, False)
        return t + 1, least(carries), carries

    lax.while_loop(more, step, (jnp.int32(0), least(st), st))
    for h in range(group):
        o_ref[:, h * HEAD_DIM:(h + 1) * HEAD_DIM] = acc_scr[h].astype(o_ref.dtype)


def _sb(q, k_all, v_all, nb, seq, n_keys, past, nh):
    tq = _tile(seq, 256, 16)
    tk = 256
    nq = seq // tq
    group = 4 if nh % 4 == 0 else (2 if nh % 2 == 0 else 1)
    gw = group * HEAD_DIM
    (q, q0), (k_all, k0), (v_all, v0) = ((a, c // gw) for a, c in (q, k_all, v_all))
    return pl.pallas_call(
        functools.partial(_sb_kernel, tq=tq, tk=tk, past=past, group=group),
        grid=(nb, nh // group, nq),
        in_specs=[pl.BlockSpec((tq, gw), lambda b, h, i: (b * nq + i, q0 + h)),
                  pl.BlockSpec((n_keys, gw), lambda b, h, i: (b, k0 + h)),
                  pl.BlockSpec((n_keys, gw), lambda b, h, i: (b, v0 + h))],
        out_specs=pl.BlockSpec((tq, gw), lambda b, h, i: (b * nq + i, h)),
        out_shape=jax.ShapeDtypeStruct((q.shape[0], nh * HEAD_DIM), BF16),
        scratch_shapes=[pltpu.VMEM((group, tq, HEAD_DIM), F32), pltpu.VMEM((group, tq, tk), F32),
                        pltpu.VMEM((group, tq, tk), F32)],
        compiler_params=_params(("arbitrary", "arbitrary", "arbitrary")),
        name="sb",
    )(q, k_all, v_all)


def _fold8(parts, op):
    while len(parts) > 1:
        parts = [op(parts[i], parts[i + 1]) if i + 1 < len(parts) else parts[i] for i in range(0, len(parts), 2)]
    return parts[0]


def _dsat_kernel(q_ref, k_ref, vt_ref, qi_ref, ki_ref, kwt_ref, o_ref, key_scr, acc_scr, s_scr, p_scr,
                 *, tq, tk, past, n_valid, n_sel, nh):
    q0 = past + pl.program_id(1) * tq
    col_pos = q0 + lax.broadcasted_iota(I32, (1, tq), 1)
    col_lim = jnp.minimum((col_pos // CHUNK + 1) * CHUNK, n_valid)
    n_kt = (jnp.minimum(((q0 + tq - 1) // CHUNK + 1) * CHUNK, n_valid) + tk - 1) // tk
    row = lax.broadcasted_iota(I32, (tk, 1), 0)
    ksel = float(n_sel)
    groups = [slice(8 * r, 8 * r + 8) for r in range(tk // 8)]

    wrow = kwt_ref[D_IDX:D_IDX + H_IDX, :] * ((H_IDX ** -0.5) * (D_IDX ** -0.5))

    def score_pair(t, _):
        offs = [pl.multiple_of(jnp.minimum(2 * t + u, n_kt - 1) * tk, tk) for u in range(2)]
        raw = [[_dot_t(ki_ref[pl.ds(off, tk), :], qi_ref[:, h * IDX_K:(h + 1) * IDX_K]) for h in range(H_IDX)]
               for off in offs]
        for off, dots in zip(offs, raw):
            score = jnp.zeros((tk, tq), F32)
            for h, s in enumerate(dots):
                score = score + wrow[h:h + 1, :] * jnp.maximum(s, 0.0)
            bits = pltpu.bitcast(score, I32)
            key = jnp.where(bits < 0, INT_MIN - bits, bits)
            key_scr[pl.ds(off, tk), :] = jnp.where((off + row) < col_lim, key, INT_MIN)
        return 0

    lax.fori_loop(0, (n_kt + 1) // 2, score_pair, 0)

    def count_ge(thr8):
        def body(t, cnt):
            for u in range(2):
                j = 2 * t + u
                off = pl.multiple_of(jnp.minimum(j, n_kt - 1) * tk, tk)
                key = key_scr[pl.ds(off, tk), :]
                part = _fold8([jnp.where(key[g, :] >= thr8, 1.0, 0.0) for g in groups], jnp.add)
                cnt = cnt + (part if u == 0 else jnp.where(j < n_kt, part, 0.0))
            return cnt

        cnt = lax.fori_loop(0, (n_kt + 1) // 2, body, jnp.zeros((8, tq), F32))
        return jnp.broadcast_to(jnp.sum(cnt, axis=0, keepdims=True), (8, tq))

    def bit_step(it, st):
        cand, have = st
        trial = cand | lax.shift_left(jnp.int32(1), 30 - it)
        cnt = count_ge(trial)
        take = cnt >= ksel
        return jnp.where(take, trial, cand), jnp.where(take, cnt, have)

    zero8 = jnp.zeros((8, tq), I32)
    cnt0 = count_ge(zero8)
    start = (jnp.where(cnt0 >= ksel, zero8, INT_MIN), jnp.where(cnt0 >= ksel, cnt0, float(2 ** 30)))
    thr8, have8 = lax.fori_loop(0, 31, bit_step, start)
    thr = thr8[0:1, :]
    overfull = jnp.max(jnp.where(have8 > ksel, 1.0, 0.0)) > 0.0

    @pl.when(jnp.logical_not(overfull))
    def _():
        def bias_tile(j, _):
            off = pl.multiple_of(j * tk, tk)
            bias = jnp.where(key_scr[pl.ds(off, tk), :] >= thr, 0.0, NEG)
            key_scr[pl.ds(off, tk), :] = pltpu.bitcast(bias, I32)
            return 0

        lax.fori_loop(0, n_kt, bias_tile, 0)

    @pl.when(overfull)
    def _():
        no_thr8 = thr8 == INT_MIN
        cnt_gt = count_ge(jnp.where(no_thr8, thr8, thr8 + 1))
        need = jnp.where(no_thr8, 0.0, ksel - cnt_gt)[0:1, :]
        prefix = jnp.where(lax.broadcasted_iota(I32, (tk, tk), 0) >= lax.broadcasted_iota(I32, (tk, tk), 1),
                           1.0, 0.0).astype(BF16)

        def bias_tile(j, carry):
            off = pl.multiple_of(j * tk, tk)
            key = key_scr[pl.ds(off, tk), :]
            eq = key == thr
            eqf = jnp.where(eq, 1.0, 0.0)
            rank_incl = _dot(prefix, eqf.astype(BF16)) + carry
            tie_ok = (rank_incl - eqf) < need
            bias = jnp.where(key > thr, 0.0, jnp.where(eq, jnp.where(tie_ok, 0.0, NEG), NEG))
            key_scr[pl.ds(off, tk), :] = pltpu.bitcast(bias, I32)
            return rank_incl[tk - 1:tk, :]

        lax.fori_loop(0, n_kt, bias_tile, jnp.zeros((1, tq), F32))

    acc_scr[...] = jnp.zeros_like(acc_scr)
    ones_v = jnp.ones((ONES_ROWS, tk), BF16)
    heads = [slice(h * HEAD_DIM, (h + 1) * HEAD_DIM) for h in range(nh)]

    def logits(j, slot, ms):
        off = pl.multiple_of(j * tk, tk)
        bias = pltpu.bitcast(key_scr[pl.ds(off, tk), :], F32)
        for h, hs in enumerate(heads):
            s_scr[slot, h] = _dot_t(k_ref[pl.ds(off, tk), hs], q_ref[:, hs]) + bias
        out = []
        for h, m in enumerate(ms):
            s = s_scr[slot, h]
            top = jnp.max(_fold8([s[g, :] for g in groups], jnp.maximum), axis=0, keepdims=True)
            out.append(jnp.maximum(m, top))
        return tuple(out)

    def accumulate(j, slot, m_old, m_new):
        off = pl.multiple_of(j * tk, tk)
        for h in range(nh):
            p_scr[h] = jnp.exp2(s_scr[slot, h] - m_new[h]).astype(BF16)
        for h, hs in enumerate(heads):
            vt_ext = jnp.concatenate([vt_ref[hs, pl.ds(off, tk)], ones_v], axis=0)
            acc_scr[h] = jnp.exp2(m_old[h] - m_new[h]) * acc_scr[h] + _dot(vt_ext, p_scr[h])

    m_none = tuple(jnp.full((1, tq), NEG, F32) for _ in range(nh))
    m_first = logits(0, 0, m_none)

    def attn_step(j, st):
        m_old, m_cur = st
        slot = j % 2
        accumulate(j - 1, 1 - slot, m_old, m_cur)
        return m_cur, logits(j, slot, m_cur)

    m_old, m_cur = lax.fori_loop(1, n_kt, attn_step, (m_none, m_first))
    accumulate(n_kt - 1, (n_kt - 1) % 2, m_old, m_cur)
    for h, hs in enumerate(heads):
        out_t = acc_scr[h, :HEAD_DIM, :] / acc_scr[h, HEAD_DIM:HEAD_DIM + 1, :]
        o_ref[:, hs] = out_t.T.astype(o_ref.dtype)


def _dsat(q, k_all, vt_all, qi_cat, ki_cat, kwt, nb, seq, n_keys, n_valid, past, nh):
    tq = _tile(seq, 256, LANES)
    tk = 256
    nq = seq // tq
    width = nh * HEAD_DIM
    n_sel = min(TOPK_MAX, n_valid // 4)
    once = pl.Buffered(1)
    (q, q0), (k_all, k0) = ((a, c // width) for a, c in (q, k_all))
    return pl.pallas_call(
        functools.partial(_dsat_kernel, tq=tq, tk=tk, past=past, n_valid=n_valid, n_sel=n_sel, nh=nh),
        grid=(nb, nq),
        in_specs=[pl.BlockSpec((tq, width), lambda b, i: (b * nq + i, q0)),
                  pl.BlockSpec((n_keys, width), lambda b, i: (b, k0), pipeline_mode=once),
                  pl.BlockSpec((width, n_keys), lambda b, i: (b, 0), pipeline_mode=once),
                  pl.BlockSpec((tq, H_IDX * IDX_K), lambda b, i: (b * nq + i, 0)),
                  pl.BlockSpec((n_keys, IDX_K), lambda b, i: (b, 0), pipeline_mode=once),
                  pl.BlockSpec((LANES, tq), lambda b, i: (0, b * nq + i))],
        out_specs=pl.BlockSpec((tq, width), lambda b, i: (b * nq + i, 0)),
        out_shape=jax.ShapeDtypeStruct((q.shape[0], width), BF16),
        scratch_shapes=[pltpu.VMEM((n_keys, tq), I32), pltpu.VMEM((nh, HEAD_DIM + ONES_ROWS, tq), F32),
                        pltpu.VMEM((2, nh, tk, tq), F32), pltpu.VMEM((nh, tk, tq), BF16)],
        compiler_params=_params(("arbitrary", "arbitrary")),
        name="dsa",
    )(q, k_all, vt_all, qi_cat, ki_cat, kwt)


def _merge_kernel(h_ref, oa_ref, ob_ref, wg1_ref, wg2_ref, b1_ref, b2_ref, wa_ref, wb_ref, o_ref):
    h = h_ref[...]
    g1 = 1.0 / (1.0 + jnp.exp(-(_dot(h, wg1_ref[...]) + b1_ref[...])))
    g2 = 1.0 / (1.0 + jnp.exp(-(_dot(h, wg2_ref[...]) + b2_ref[...])))
    o_ref[...] = (g1 * _dot(oa_ref[...], wa_ref[...]) + g2 * _dot(ob_ref[...], wb_ref[...])).astype(BF16)


def _merge(h_hi, o_a, o_b, w_gate, b_gate, w_pa, w_pb):
    n, d = h_hi.shape
    width = o_a.shape[1]
    tm = _tile(n, 1024, 16)
    tn = _tile(d, 512, LANES)
    nj = d // tn
    b_gate = b_gate.reshape(1, 2 * d)
    row = lambda i, j: (i, 0)
    return pl.pallas_call(
        _merge_kernel,
        grid=(n // tm, nj),
        in_specs=[pl.BlockSpec((tm, d), row), pl.BlockSpec((tm, width), row), pl.BlockSpec((tm, width), row),
                  pl.BlockSpec((d, tn), lambda i, j: (0, j)),
                  pl.BlockSpec((d, tn), lambda i, j: (0, j + nj)),
                  pl.BlockSpec((1, tn), lambda i, j: (0, j)),
                  pl.BlockSpec((1, tn), lambda i, j: (0, j + nj)),
                  pl.BlockSpec((width, tn), lambda i, j: (0, j)),
                  pl.BlockSpec((width, tn), lambda i, j: (0, j))],
        out_specs=pl.BlockSpec((tm, tn), lambda i, j: (i, j)),
        out_shape=jax.ShapeDtypeStruct((n, d), BF16),
        compiler_params=_params(("arbitrary", "arbitrary")),
        name="merge",
    )(h_hi, o_a, o_b, w_gate, w_gate, b_gate, b_gate, w_pa, w_pb)


def _outproj_kernel(m_ref, w_ref, x_ref, mod_ref, gm_ref, gf_ref, x1_ref, h2_ref):
    mod = mod_ref[0]
    x1 = x_ref[...] + mod[2:3, :] * _rms(_dot(m_ref[...], w_ref[...]), gm_ref[...])
    x1_ref[...] = x1
    h2_ref[...] = (_rms(x1, gf_ref[...]) * (1.0 + mod[4:5, :]) + mod[3:4, :]).astype(BF16)


def _outproj(merged, w_out, x2, mod, g_post_mix, g_pre_ffn, seq):
    n, d = x2.shape
    tm = _tile(seq, 512, 16)
    tps = seq // tm
    row = lambda i: (i, 0)
    fix = lambda i: (0, 0)
    return pl.pallas_call(
        _outproj_kernel,
        grid=(n // tm,),
        in_specs=[pl.BlockSpec((tm, d), row), pl.BlockSpec((d, d), fix, pipeline_mode=pl.Buffered(1)),
                  pl.BlockSpec((tm, d), row),
                  pl.BlockSpec((1, N_MOD, d), lambda i: (i // tps, 0, 0)),
                  pl.BlockSpec((1, d), fix), pl.BlockSpec((1, d), fix)],
        out_specs=[pl.BlockSpec((tm, d), row), pl.BlockSpec((tm, d), row)],
        out_shape=[jax.ShapeDtypeStruct((n, d), F32), jax.ShapeDtypeStruct((n, d), BF16)],
        compiler_params=_params(("arbitrary",)),
        name="outproj",
    )(merged, w_out, x2, mod, g_post_mix.reshape(1, d), g_pre_ffn.reshape(1, d))


def _gelu_tanh(x):
    return 0.5 * x * (1.0 + jnp.tanh(0.7978845608028654 * (x + 0.044715 * (x * x * x))))


def _ffn_kernel(h_ref, halo_ref, x1_ref, mod_ref, st_ref, wg_ref, wu_ref, wd_ref, wc_ref, bc_ref, gp_ref,
                y_ref, nc_ref, g_scr, act_scr, *, tm, nsub, tps):
    i = pl.program_id(0)
    c = pl.program_id(1)
    last = pl.num_programs(1) - 1
    sub = tm // nsub
    cur = c % 2

    def step(down, gate_up):
        if gate_up:
            h = h_ref[...]
            wg = wg_ref[...]
            g = _dot(h, wg)
            u = _dot(h, wu_ref[...])
            wc = wc_ref[...]
            bc = bc_ref[...]
            g_scr[HALO:HALO + tm, :] = g
        if down == "set":
            y_ref[...] = _dot(act_scr[1 - cur], wd_ref[...])
        elif down == "add":
            y_ref[...] += _dot(act_scr[1 - cur], wd_ref[...])
        if not gate_up:
            return

        def conv(lo, rows):
            return (bc + wc[0:1, :] * g_scr[lo - 2:lo - 2 + rows, :]
                    + wc[1:2, :] * g_scr[lo - 1:lo - 1 + rows, :] + wc[2:3, :] * g_scr[lo:lo + rows, :])

        if nsub == 1:
            g_halo = _dot(halo_ref[...], wg)
            g_scr[0:HALO, :] = g_halo
            g_scr[HALO - 2:HALO, :] = jnp.where(i % tps == 0, st_ref[0], g_halo[HALO - 2:HALO, :])
            nc_ref[0] = g_scr[HALO + tm - 2:HALO + tm, :]
            act_scr[cur] = (_gelu_tanh(conv(HALO, tm)) * u).astype(BF16)
        else:
            for s in range(nsub):
                lo = HALO + s * sub
                nc_ref[s] = g_scr[lo + sub - 2:lo + sub, :]
                g_scr[lo - 2:lo, :] = st_ref[s]
                act_scr[cur, s * sub:(s + 1) * sub, :] = (
                    _gelu_tanh(conv(lo, sub)) * u[s * sub:(s + 1) * sub, :]).astype(BF16)

    pl.when(c == 0)(lambda: step(None, True))
    pl.when(c == 1)(lambda: step("set", True))
    pl.when(jnp.logical_and(c > 1, c < last))(lambda: step("add", True))

    @pl.when(c == last)
    def _():
        step("add", False)
        for s in range(nsub):
            rows = slice(s * sub, (s + 1) * sub)
            y_ref[rows, :] = x1_ref[rows, :] + mod_ref[s][5:6, :] * _rms(y_ref[rows, :], gp_ref[...])


def _ffn(h2, x1, mod, state, w_gate, w_up, w_down, w_conv, b_conv, g_post, seq):
    n, d = x1.shape
    dff = w_gate.shape[1]
    nb = n // seq
    tm = _row_tile(n, seq, 1024)
    tf = _tile(dff, 512, LANES)
    n_chunks = dff // tf
    assert n_chunks >= 2, "the lagged down projection needs at least two d_ff chunks"
    up = lambda c: jnp.minimum(c, n_chunks - 1)
    down = lambda c: jnp.maximum(c - 1, 0)
    per_tile_c = lambda i, c: (i, 0, up(c))
    if tm <= seq:
        nsub, tps = 1, seq // tm
        per_seq = lambda i, c: (i // tps, 0, 0)
        per_seq_c = lambda i, c: (i // tps, 0, up(c))
    else:
        nsub, tps = tm // seq, 1
        per_seq = lambda i, c: (i, 0, 0)
        per_seq_c = per_tile_c
    halo_blocks = tm // HALO
    row = lambda i, c: (i, 0)
    y, new_conv = pl.pallas_call(
        functools.partial(_ffn_kernel, tm=tm, nsub=nsub, tps=tps),
        grid=(n // tm, n_chunks + 1),
        in_specs=[pl.BlockSpec((tm, d), row, pipeline_mode=pl.Buffered(1)),
                  pl.BlockSpec((HALO, d), lambda i, c: (jnp.maximum(i * halo_blocks - 1, 0), 0)),
                  pl.BlockSpec((tm, d), row, pipeline_mode=pl.Buffered(1)),
                  pl.BlockSpec((nsub, N_MOD, d), per_seq),
                  pl.BlockSpec((nsub, CONV_W - 1, tf), per_seq_c),
                  pl.BlockSpec((d, tf), lambda i, c: (0, up(c))),
                  pl.BlockSpec((d, tf), lambda i, c: (0, up(c))),
                  pl.BlockSpec((tf, d), lambda i, c: (down(c), 0)),
                  pl.BlockSpec((CONV_W, tf), lambda i, c: (0, up(c))),
                  pl.BlockSpec((1, tf), lambda i, c: (0, up(c))),
                  pl.BlockSpec((1, d), lambda i, c: (0, 0))],
        out_specs=[pl.BlockSpec((tm, d), row, pipeline_mode=pl.Buffered(1)),
                   pl.BlockSpec((nsub, CONV_W - 1, tf), per_tile_c)],
        out_shape=[jax.ShapeDtypeStruct((n, d), F32),
                   jax.ShapeDtypeStruct((nb * tps, CONV_W - 1, dff), F32)],
        scratch_shapes=[pltpu.VMEM((HALO + tm, tf), F32), pltpu.VMEM((2, tm, tf), BF16)],
        compiler_params=_params(("arbitrary", "arbitrary")),
        name="ffn",
    )(h2, h2, x1, mod, state, w_gate, w_up, w_down, w_conv, b_conv.reshape(1, dff), g_post.reshape(1, d))
    return y, new_conv.reshape(nb, tps, CONV_W - 1, dff)[:, tps - 1]


def _rope_tables(pos):
    def angles(dim):
        half = dim // 2
        inv = ROPE_THETA ** (-jnp.arange(half, dtype=F32) * 2.0 / dim)
        ang = pos.astype(F32)[:, None] * inv[None, :]
        return jnp.cos(ang), jnp.sin(ang)

    cos, sin = angles(HEAD_DIM)
    cos_h = jnp.concatenate([cos, cos], axis=1)
    sin_h = jnp.concatenate([-sin, sin], axis=1)

    cos, sin = angles(D_IDX)
    zero = jnp.zeros_like(sin)
    n_rot = H_IDX + 1
    tail = IDX_COLS - n_rot * D_IDX
    t = pos.shape[0]
    c_i = jnp.concatenate([jnp.tile(jnp.concatenate([cos, cos], 1), (1, n_rot)), jnp.ones((t, tail), F32)], 1)
    s1_i = jnp.concatenate([jnp.tile(jnp.concatenate([-sin, zero], 1), (1, n_rot)), jnp.zeros((t, tail), F32)], 1)
    s2_i = jnp.concatenate([jnp.tile(jnp.concatenate([zero, sin], 1), (1, n_rot)), jnp.zeros((t, tail), F32)], 1)
    return cos_h, sin_h, c_i, s1_i, s2_i


def _with_past(past, new, nb, seq, n_keys):
    w = new.shape[1]
    if past is None and n_keys == seq:
        return new
    parts = [] if past is None else [past.astype(new.dtype).reshape(nb, -1, w)]
    parts.append(new.reshape(nb, seq, w))
    have = sum(p.shape[1] for p in parts)
    if n_keys > have:
        parts.append(jnp.zeros((nb, n_keys - have, w), new.dtype))
    return jnp.concatenate(parts, axis=1).reshape(nb * n_keys, w)


def _layer(x, mod, past, conv_state, p):
    nb, seq, d = x.shape
    n = nb * seq
    nh = p["w_pa"].shape[0] // HEAD_DIM
    width = nh * HEAD_DIM
    n_past = 0 if past is None else past["a_k"].shape[1]
    n_valid = n_past + seq
    n_keys = -(-n_valid // 256) * 256
    x2 = x.reshape(n, d)

    h_hi, h_lo = _norm(x2, p["g_pre_mix"], mod, seq)
    cos_h, sin_h, c_i, s1_i, s2_i = _rope_tables(n_past + jnp.arange(seq, dtype=I32))
    p16, kv32, vt_new = _inproj(h_hi, p["w_main"], cos_h, sin_h, seq, width)
    col = lambda a, g: a[:, g * width:(g + 1) * width]
    qi_hi, qi_lo, kw, ki_hi, ki_lo = _idxproj(h_hi, h_lo, p["w_idx_hi"], p["w_idx_lo"], c_i, s1_i, s2_i, seq)

    def keys(g, past_rows):
        if past_rows is None and n_keys == seq:
            return p16, g * width
        return _with_past(past_rows, col(p16, g), nb, seq, n_keys), 0

    q_hi = qi_hi.reshape(n, H_IDX, D_IDX)
    q_lo = qi_lo.reshape(n, H_IDX, D_IDX)
    qi_cat = jnp.concatenate([q_hi, q_hi, q_lo, jnp.zeros_like(q_hi)], axis=2).reshape(n, H_IDX * IDX_K)
    k_hi, k_lo = ki_hi[:, :D_IDX], ki_lo[:, :D_IDX]
    if past is None:
        pa_k = pa_v = pb_k = pb_v = pi_hi = pi_lo = None
    else:
        pa_k, pa_v, pb_k, pb_v = (past[k].reshape(nb, n_past, width) for k in ("a_k", "a_v", "b_k", "b_v"))
        pi_hi, pi_lo = _split(past["b_kidx"].astype(F32))
    k_hi = _with_past(pi_hi, k_hi, nb, seq, n_keys)
    k_lo = _with_past(pi_lo, k_lo, nb, seq, n_keys)
    ki_cat = jnp.concatenate([k_hi, k_lo, k_hi, jnp.zeros_like(k_hi)], axis=1)

    o_a = _sb((p16, 0), keys(1, pa_k), keys(2, pa_v), nb, seq, n_keys, n_past, nh)
    seq_q = max(seq, LANES)
    pad_q = lambda a: a if seq_q == seq else jnp.pad(
        a.reshape(nb, seq, -1), ((0, 0), (0, seq_q - seq), (0, 0))).reshape(nb * seq_q, -1)
    if vt_new is not None and past is None and n_keys == seq:
        vt_all = vt_new
    else:
        vt_all = _with_past(pb_v, col(p16, 5), nb, seq, n_keys).reshape(nb, n_keys, width)
        vt_all = jnp.swapaxes(vt_all, 1, 2).reshape(nb * width, n_keys)
    q_b = (p16, 3 * width) if seq_q == seq else (pad_q(col(p16, 3)), 0)
    o_b = _dsat(q_b, keys(4, pb_k), vt_all, pad_q(qi_cat), ki_cat,
                pad_q(kw).T, nb, seq_q, n_keys, n_valid, n_past, nh)
    if seq_q != seq:
        o_b = o_b.reshape(nb, seq_q, width)[:, :seq].reshape(n, width)

    merged = _merge(h_hi, o_a, o_b, p["w_gate"], p["b_gate"], p["w_pa"], p["w_pb"])
    x1, h2 = _outproj(merged, p["w_out"], x2, mod, p["g_post_mix"], p["g_pre_ffn"], seq)
    y, new_conv = _ffn(h2, x1, mod, conv_state, p["w_ffn_gate"], p["w_ffn_up"], p["w_ffn_down"],
                       p["w_conv"], p["b_conv"], p["g_post_ffn"], seq)

    heads = lambda g: kv32[g].reshape(nb, seq, nh, HEAD_DIM)
    new = (heads(1), heads(2), heads(4), heads(5), kw[:, :D_IDX].reshape(nb, seq, D_IDX), new_conv)
    return y.reshape(nb, seq, d), new


def kernel(x_prompt, x_sample, c_prompt, c_sample, cache_a_k, cache_a_v, cache_b_k, cache_b_v, cache_b_kidx,
           state_ffn_conv, w_mod, b_mod, g_pre_mix, w_in, w_merge_gate, b_merge_gate, w_proj_a, w_proj_b, w_out,
           g_post_mix, g_pre_ffn, w_ffn_gate, w_ffn_up, w_conv, b_conv, w_ffn_down, g_post_ffn):
    depth = w_mod.shape[0]
    nbp, nbs = x_prompt.shape[0], x_sample.shape[0]
    d = x_prompt.shape[2]
    width = w_proj_a.shape[1]
    dff = w_ffn_gate.shape[2]
    assert w_in.shape[2] == 6 * width + H_IDX * D_IDX + D_IDX + H_IDX
    assert cache_b_kidx.shape[-1] == D_IDX and cache_a_k.shape[-1] == HEAD_DIM

    y_p, y_s = x_prompt, x_sample
    st_p = [[] for _ in range(6)]
    st_s = [[] for _ in range(6)]
    for l in range(depth):
        w_idx = jnp.pad(w_in[l][:, 6 * width:], ((0, 0), (0, IDX_COLS - (w_in.shape[2] - 6 * width))))
        w_idx_hi, w_idx_lo = _split(w_idx)
        p = dict(
            g_pre_mix=g_pre_mix[l], w_main=w_in[l][:, :6 * width].astype(BF16), w_idx_hi=w_idx_hi, w_idx_lo=w_idx_lo,
            w_gate=w_merge_gate[l].astype(BF16), b_gate=b_merge_gate[l],
            w_pa=w_proj_a[l].astype(BF16), w_pb=w_proj_b[l].astype(BF16), w_out=w_out[l].astype(BF16),
            g_post_mix=g_post_mix[l], g_pre_ffn=g_pre_ffn[l],
            w_ffn_gate=w_ffn_gate[l].astype(BF16), w_ffn_up=w_ffn_up[l].astype(BF16),
            w_ffn_down=w_ffn_down[l].astype(BF16), w_conv=w_conv[l], b_conv=b_conv[l], g_post_ffn=g_post_ffn[l])

        c_all = jnp.concatenate([c_prompt, c_sample], axis=0)
        pad = -c_all.shape[0] % 16
        mod = _mod(jnp.pad(c_all, ((0, pad), (0, 0))), w_mod[l], b_mod[l]).reshape(-1, N_MOD, d)
        past = dict(a_k=cache_a_k[l], a_v=cache_a_v[l], b_k=cache_b_k[l], b_v=cache_b_v[l], b_kidx=cache_b_kidx[l])
        y_p, new_p = _layer(y_p, mod[:nbp], None, jnp.zeros((nbp, CONV_W - 1, dff), F32), p)
        y_s, new_s = _layer(y_s, mod[nbp:nbp + nbs], past, state_ffn_conv[l], p)
        for i in range(6):
            st_p[i].append(new_p[i])
            st_s[i].append(new_s[i])
    sp = [jnp.stack(s, axis=0) for s in st_p]
    ss = [jnp.stack(s, axis=0) for s in st_s]
    return (y_p, y_s, sp[0], sp[1], sp[2], sp[3], sp[4], sp[5], ss[0], ss[1], ss[2], ss[3], ss[4], ss[5])
```

```python
import functools

import jax
import jax.numpy as jnp
from jax import lax
from jax.experimental import pallas as pl
from jax.experimental.pallas import tpu as pltpu

F32 = jnp.float32
BF16 = jnp.bfloat16
I32 = jnp.int32

CHUNK = 64
HEAD_DIM = 128
H_IDX = 4
D_IDX = 64
TOPK_MAX = 256
CONV_W = 3
ROPE_THETA = 10000.0
RMS_EPS = 1e-6
N_MOD = 6
IDX_COLS = 384
IDX_K = 256
LANES = 128
HALO = 16
ONES_ROWS = 16
LOG2E = 1.4426950408889634
UNDERFLOW_LOG2 = 160.0
NEG = -1e30
INT_MIN = -2 ** 31
VMEM_LIMIT = 56 * 1024 * 1024


def _tile(n, pref, mult):
    t = min(pref, n)
    t -= t % mult
    while t >= mult:
        if n % t == 0:
            return t
        t -= mult
    return n


def _row_tile(n, seq, pref):
    if seq >= pref or n == seq:
        return _tile(seq, pref, 16)
    return _tile(n, pref, seq)


def _params(sem):
    return pltpu.CompilerParams(dimension_semantics=sem, vmem_limit_bytes=VMEM_LIMIT)


def _split(x):
    hi = x.astype(BF16)
    lo = (x - hi.astype(F32)).astype(BF16)
    return hi, lo


def _dot(a, b):
    return jnp.dot(a, b, preferred_element_type=F32)


def _dot_t(a, b):
    return lax.dot_general(a, b, (((1,), (1,)), ((), ())), preferred_element_type=F32)


def _rms(x, g):
    ms = jnp.mean(x * x, axis=-1, keepdims=True)
    return x * lax.rsqrt(ms + RMS_EPS) * g


def _mod_kernel(c_ref, w_ref, b_ref, o_ref):
    c = c_ref[...]
    s = c / (1.0 + jnp.exp(-c))
    s_hi, s_lo = _split(s)
    w_hi, w_lo = _split(w_ref[...])
    o_ref[...] = _dot(s_hi, w_hi) + _dot(s_hi, w_lo) + _dot(s_lo, w_hi) + b_ref[...]


def _mod(c, w, b):
    bc, d = c.shape
    n = w.shape[1]
    tn = _tile(n, 1024, LANES)
    return pl.pallas_call(
        _mod_kernel,
        grid=(n // tn,),
        in_specs=[pl.BlockSpec((bc, d), lambda j: (0, 0)),
                  pl.BlockSpec((d, tn), lambda j: (0, j)),
                  pl.BlockSpec((1, tn), lambda j: (0, j))],
        out_specs=pl.BlockSpec((bc, tn), lambda j: (0, j)),
        out_shape=jax.ShapeDtypeStruct((bc, n), F32),
        compiler_params=_params(("arbitrary",)),
        name="mod",
    )(c, w, b.reshape(1, n))


def _norm_kernel(x_ref, g_ref, mod_ref, hi_ref, lo_ref):
    m = mod_ref[0]
    h = _rms(x_ref[...], g_ref[...]) * (1.0 + m[1:2, :]) + m[0:1, :]
    hi, lo = _split(h)
    hi_ref[...] = hi
    lo_ref[...] = lo


def _norm(x2, g, mod, seq):
    n, d = x2.shape
    tr = _tile(seq, 512, 16)
    tps = seq // tr
    return pl.pallas_call(
        _norm_kernel,
        grid=(n // tr,),
        in_specs=[pl.BlockSpec((tr, d), lambda i: (i, 0)),
                  pl.BlockSpec((1, d), lambda i: (0, 0)),
                  pl.BlockSpec((1, N_MOD, d), lambda i: (i // tps, 0, 0))],
        out_specs=[pl.BlockSpec((tr, d), lambda i: (i, 0))] * 2,
        out_shape=[jax.ShapeDtypeStruct((n, d), BF16)] * 2,
        compiler_params=_params(("arbitrary",)),
        name="norm",
    )(x2, g.reshape(1, d), mod)


N_PROJ = 6


KV_GROUPS = (1, 2, 4, 5)


def _inproj_kernel(a_ref, w_ref, c_ref, s_ref, p16_ref, ka_ref, va_ref, kb_ref, vb_ref, *rest, tm, nh):
    r_scr = rest[-1]
    acc = _dot(a_ref[...], w_ref[...])
    c = c_ref[0]
    s = s_ref[0]
    for k in range(nh):
        sl = slice(k * HEAD_DIM, (k + 1) * HEAD_DIM)
        xs = acc[:, sl]
        r = xs * c + pltpu.roll(xs, HEAD_DIM // 2, 1) * s
        r_scr[:, sl] = r
        p16_ref[:, sl] = r.astype(BF16)
    for grp, ref in zip(KV_GROUPS, (ka_ref, va_ref, kb_ref, vb_ref)):
        @pl.when(pl.program_id(1) == grp)
        def _(ref=ref):
            for k in range(nh):
                ref[pl.ds(k, tm, stride=nh), :] = r_scr[:, k * HEAD_DIM:(k + 1) * HEAD_DIM]

    if len(rest) == 2:
        @pl.when(pl.program_id(1) == N_PROJ - 1)
        def _():
            rest[0][...] = r_scr[...].T.astype(BF16)


def _inproj(h_hi, w_main, cos_t, sin_t, seq, width):
    n, d = h_hi.shape
    nh = width // HEAD_DIM
    tm = _row_tile(n, seq, 1024)
    with_vt = tm <= seq
    tn = width
    npj = 1
    scale = LOG2E * HEAD_DIM ** -0.5
    one, zero = jnp.ones_like(cos_t), jnp.zeros_like(cos_t)
    c_t = jnp.stack([scale * one, one, one, scale * cos_t, cos_t, one])
    s_t = jnp.stack([zero, zero, zero, scale * sin_t, sin_t, zero])
    if tm <= seq:
        tps = seq // tm
        tab_map = lambda i, j: (j // npj, i % tps, 0)
    else:
        c_t = jnp.tile(c_t, (1, tm // seq, 1))
        s_t = jnp.tile(s_t, (1, tm // seq, 1))
        tab_map = lambda i, j: (j // npj, 0, 0)
    out = pl.pallas_call(
        functools.partial(_inproj_kernel, tm=tm, nh=nh),
        grid=(n // tm, N_PROJ * npj),
        in_specs=[pl.BlockSpec((tm, d), lambda i, j: (i, 0)),
                  pl.BlockSpec((d, tn), lambda i, j: (0, j)),
                  pl.BlockSpec((1, tm, HEAD_DIM), tab_map),
                  pl.BlockSpec((1, tm, HEAD_DIM), tab_map)],
        out_specs=[pl.BlockSpec((tm, tn), lambda i, j: (i, j))]
        + [pl.BlockSpec((tm * nh, HEAD_DIM), lambda i, j: (i, 0), pipeline_mode=pl.Buffered(1))] * len(KV_GROUPS)
        + ([pl.BlockSpec((width, tm), lambda i, j: (i // tps, i % tps))] if with_vt else []),
        out_shape=[jax.ShapeDtypeStruct((n, N_PROJ * width), BF16)]
        + [jax.ShapeDtypeStruct((n * nh, HEAD_DIM), F32)] * len(KV_GROUPS)
        + ([jax.ShapeDtypeStruct((n // seq * width, seq), BF16)] if with_vt else []),
        scratch_shapes=[pltpu.VMEM((tm, tn), F32)],
        compiler_params=_params(("arbitrary", "arbitrary")),
        name="inproj",
    )(h_hi, w_main, c_t, s_t)
    return out[0], dict(zip(KV_GROUPS, out[1:1 + len(KV_GROUPS)])), (out[-1] if with_vt else None)


def _idxproj_kernel(hi_ref, lo_ref, whi_ref, wlo_ref, c_ref, s1_ref, s2_ref,
                    qhi_ref, qlo_ref, kw_ref, khi_ref, klo_ref):
    a_hi = hi_ref[...]
    w_hi = whi_ref[...]
    acc = _dot(a_hi, w_hi) + _dot(a_hi, wlo_ref[...]) + _dot(lo_ref[...], w_hi)
    for s in range(IDX_COLS // LANES):
        sl = slice(s * LANES, (s + 1) * LANES)
        xs = acc[:, sl]
        r = (xs * c_ref[:, sl] + pltpu.roll(xs, LANES - D_IDX // 2, 1) * s1_ref[:, sl]
             + pltpu.roll(xs, D_IDX // 2, 1) * s2_ref[:, sl])
        hi, lo = _split(r)
        if s < 2:
            qhi_ref[:, sl] = hi
            qlo_ref[:, sl] = lo
        else:
            kw_ref[...] = r
            khi_ref[...] = hi
            klo_ref[...] = lo


def _idxproj(h_hi, h_lo, w_hi, w_lo, c_t, s1_t, s2_t, seq):
    n, d = h_hi.shape
    tm = _row_tile(n, seq, 512)
    if tm <= seq:
        tps = seq // tm
        tab_map = lambda i: (i % tps, 0)
    else:
        c_t, s1_t, s2_t = (jnp.tile(t, (tm // seq, 1)) for t in (c_t, s1_t, s2_t))
        tab_map = lambda i: (0, 0)
    row = lambda i: (i, 0)
    return pl.pallas_call(
        _idxproj_kernel,
        grid=(n // tm,),
        in_specs=[pl.BlockSpec((tm, d), row), pl.BlockSpec((tm, d), row),
                  pl.BlockSpec((d, IDX_COLS), lambda i: (0, 0)),
                  pl.BlockSpec((d, IDX_COLS), lambda i: (0, 0)),
                  pl.BlockSpec((tm, IDX_COLS), tab_map),
                  pl.BlockSpec((tm, IDX_COLS), tab_map),
                  pl.BlockSpec((tm, IDX_COLS), tab_map)],
        out_specs=[pl.BlockSpec((tm, 2 * LANES), row), pl.BlockSpec((tm, 2 * LANES), row),
                   pl.BlockSpec((tm, LANES), row), pl.BlockSpec((tm, LANES), row),
                   pl.BlockSpec((tm, LANES), row)],
        out_shape=[jax.ShapeDtypeStruct((n, 2 * LANES), BF16), jax.ShapeDtypeStruct((n, 2 * LANES), BF16),
                   jax.ShapeDtypeStruct((n, LANES), F32), jax.ShapeDtypeStruct((n, LANES), BF16),
                   jax.ShapeDtypeStruct((n, LANES), BF16)],
        compiler_params=_params(("arbitrary",)),
        name="idxproj",
    )(h_hi, h_lo, w_hi, w_lo, c_t, s1_t, s2_t)


def _sb_kernel(q_ref, k_ref, v_ref, o_ref, acc_scr, z_scr, i_scr, *, tq, tk, past, group):
    q0 = past + pl.program_id(2) * tq
    n_kt = (q0 + tq - 2) // tk + 1
    n_clear = q0 // tk
    row_pos = q0 + lax.broadcasted_iota(I32, (tq, 1), 0)
    col = lax.broadcasted_iota(I32, (1, tk), 1)
    suffix = jnp.where(lax.broadcasted_iota(I32, (tk, tk), 0) >= lax.broadcasted_iota(I32, (tk, tk), 1),
                       1.0, 0.0).astype(BF16)
    suffix2 = jnp.concatenate([suffix, suffix], axis=0)
    acc_scr[...] = jnp.zeros_like(acc_scr)

    heads = [slice(h * HEAD_DIM, (h + 1) * HEAD_DIM) for h in range(group)]

    def tiles(js, carries, masked):
        offs = [pl.multiple_of(j * tk, tk) for j in js]
        if masked:
            valids = [(off + col) < row_pos for off in offs]
        for t, off in enumerate(offs):
            for h, hs in enumerate(heads):
                z_scr[t * group + h] = _dot_t(q_ref[:, hs], k_ref[pl.ds(off, tk), hs])
        for t in range(len(js)):
            for h in range(group):
                z = z_scr[t * group + h]
                sp = jnp.maximum(z, 0.0) + jnp.log2(1.0 + jnp.exp2(-jnp.abs(z)))
                if masked:
                    sp = jnp.where(valids[t], sp, 0.0)
                hi, lo = _split(sp)
                i_scr[t * group + h] = _dot(jnp.concatenate([hi, lo], axis=1), suffix2)
        carries = list(carries)
        for t, off in enumerate(offs):
            for h, hs in enumerate(heads):
                w = jnp.exp2(z_scr[t * group + h] - i_scr[t * group + h] - carries[h])
                if masked:
                    w = jnp.where(valids[t], w, 0.0)
                acc_scr[h] += _dot(w.astype(BF16), v_ref[pl.ds(off, tk), hs])
                carries[h] = carries[h] + i_scr[t * group + h, :, 0:1]
        return tuple(carries)

    def least(carries):
        m = carries[0]
        for c in carries[1:]:
            m = jnp.minimum(m, c)
        return jnp.min(m)

    st = tuple(jnp.zeros((tq, 1), F32) for _ in range(group))
    st = lax.fori_loop(0, n_kt - n_clear, lambda t, c: tiles([n_kt - 1 - t], c, True), st)

    def more(state):
        t, low, _ = state
        return jnp.logical_and(t < n_clear, low < UNDERFLOW_LOG2)

    def step(state):
        t, _, carries = state
        carries = tiles([n_clear - 1 - t], carries, False)
        return t + 1, least(carries), carries

    lax.while_loop(more, step, (jnp.int32(0), least(st), st))
    for h in range(group):
        o_ref[:, h * HEAD_DIM:(h + 1) * HEAD_DIM] = acc_scr[h].astype(o_ref.dtype)


def _sb(q, k_all, v_all, nb, seq, n_keys, past, nh):
    tq = _tile(seq, 256, 16)
    tk = 256
    nq = seq // tq
    group = 4 if nh % 4 == 0 else (2 if nh % 2 == 0 else 1)
    gw = group * HEAD_DIM
    (q, q0), (k_all, k0), (v_all, v0) = ((a, c // gw) for a, c in (q, k_all, v_all))
    return pl.pallas_call(
        functools.partial(_sb_kernel, tq=tq, tk=tk, past=past, group=group),
        grid=(nb, nh // group, nq),
        in_specs=[pl.BlockSpec((tq, gw), lambda b, h, i: (b * nq + i, q0 + h)),
                  pl.BlockSpec((n_keys, gw), lambda b, h, i: (b, k0 + h)),
                  pl.BlockSpec((n_keys, gw), lambda b, h, i: (b, v0 + h))],
        out_specs=pl.BlockSpec((tq, gw), lambda b, h, i: (b * nq + i, h)),
        out_shape=jax.ShapeDtypeStruct((q.shape[0], nh * HEAD_DIM), BF16),
        scratch_shapes=[pltpu.VMEM((group, tq, HEAD_DIM), F32), pltpu.VMEM((group, tq, tk), F32),
                        pltpu.VMEM((group, tq, tk), F32)],
        compiler_params=_params(("arbitrary", "arbitrary", "arbitrary")),
        name="sb",
    )(q, k_all, v_all)


def _fold8(parts, op):
    while len(parts) > 1:
        parts = [op(parts[i], parts[i + 1]) if i + 1 < len(parts) else parts[i] for i in range(0, len(parts), 2)]
    return parts[0]


def _dsat_kernel(q_ref, k_ref, vt_ref, qi_ref, ki_ref, kwt_ref, o_ref, key_scr, acc_scr, s_scr, p_scr,
                 *, tq, tk, past, n_valid, n_sel, nh):
    q0 = past + pl.program_id(1) * tq
    col_pos = q0 + lax.broadcasted_iota(I32, (1, tq), 1)
    col_lim = jnp.minimum((col_pos // CHUNK + 1) * CHUNK, n_valid)
    n_kt = (jnp.minimum(((q0 + tq - 1) // CHUNK + 1) * CHUNK, n_valid) + tk - 1) // tk
    row = lax.broadcasted_iota(I32, (tk, 1), 0)
    ksel = float(n_sel)
    groups = [slice(8 * r, 8 * r + 8) for r in range(tk // 8)]

    wrow = kwt_ref[D_IDX:D_IDX + H_IDX, :] * ((H_IDX ** -0.5) * (D_IDX ** -0.5))

    def score_pair(t, _):
        offs = [pl.multiple_of(jnp.minimum(2 * t + u, n_kt - 1) * tk, tk) for u in range(2)]
        raw = [[_dot_t(ki_ref[pl.ds(off, tk), :], qi_ref[:, h * IDX_K:(h + 1) * IDX_K]) for h in range(H_IDX)]
               for off in offs]
        for off, dots in zip(offs, raw):
            score = jnp.zeros((tk, tq), F32)
            for h, s in enumerate(dots):
                score = score + wrow[h:h + 1, :] * jnp.maximum(s, 0.0)
            bits = pltpu.bitcast(score, I32)
            key = jnp.where(bits < 0, INT_MIN - bits, bits)
            key_scr[pl.ds(off, tk), :] = jnp.where((off + row) < col_lim, key, INT_MIN)
        return 0

    lax.fori_loop(0, (n_kt + 1) // 2, score_pair, 0)

    def count_ge(thr8):
        def body(t, cnt):
            for u in range(2):
                j = 2 * t + u
                off = pl.multiple_of(jnp.minimum(j, n_kt - 1) * tk, tk)
                key = key_scr[pl.ds(off, tk), :]
                part = _fold8([jnp.where(key[g, :] >= thr8, 1.0, 0.0) for g in groups], jnp.add)
                cnt = cnt + (part if u == 0 else jnp.where(j < n_kt, part, 0.0))
            return cnt

        cnt = lax.fori_loop(0, (n_kt + 1) // 2, body, jnp.zeros((8, tq), F32))
        return jnp.broadcast_to(jnp.sum(cnt, axis=0, keepdims=True), (8, tq))

    def bit_step(it, st):
        cand, have = st
        trial = cand | lax.shift_left(jnp.int32(1), 30 - it)
        cnt = count_ge(trial)
        take = cnt >= ksel
        return jnp.where(take, trial, cand), jnp.where(take, cnt, have)

    zero8 = jnp.zeros((8, tq), I32)
    cnt0 = count_ge(zero8)
    start = (jnp.where(cnt0 >= ksel, zero8, INT_MIN), jnp.where(cnt0 >= ksel, cnt0, float(2 ** 30)))
    thr8, have8 = lax.fori_loop(0, 31, bit_step, start)
    thr = thr8[0:1, :]
    overfull = jnp.max(jnp.where(have8 > ksel, 1.0, 0.0)) > 0.0

    @pl.when(jnp.logical_not(overfull))
    def _():
        def bias_tile(j, _):
            off = pl.multiple_of(j * tk, tk)
            bias = jnp.where(key_scr[pl.ds(off, tk), :] >= thr, 0.0, NEG)
            key_scr[pl.ds(off, tk), :] = pltpu.bitcast(bias, I32)
            return 0

        lax.fori_loop(0, n_kt, bias_tile, 0)

    @pl.when(overfull)
    def _():
        no_thr8 = thr8 == INT_MIN
        cnt_gt = count_ge(jnp.where(no_thr8, thr8, thr8 + 1))
        need = jnp.where(no_thr8, 0.0, ksel - cnt_gt)[0:1, :]
        prefix = jnp.where(lax.broadcasted_iota(I32, (tk, tk), 0) >= lax.broadcasted_iota(I32, (tk, tk), 1),
                           1.0, 0.0).astype(BF16)

        def bias_tile(j, carry):
            off = pl.multiple_of(j * tk, tk)
            key = key_scr[pl.ds(off, tk), :]
            eq = key == thr
            eqf = jnp.where(eq, 1.0, 0.0)
            rank_incl = _dot(prefix, eqf.astype(BF16)) + carry
            tie_ok = (rank_incl - eqf) < need
            bias = jnp.where(key > thr, 0.0, jnp.where(eq, jnp.where(tie_ok, 0.0, NEG), NEG))
            key_scr[pl.ds(off, tk), :] = pltpu.bitcast(bias, I32)
            return rank_incl[tk - 1:tk, :]

        lax.fori_loop(0, n_kt, bias_tile, jnp.zeros((1, tq), F32))

    acc_scr[...] = jnp.zeros_like(acc_scr)
    ones_v = jnp.ones((ONES_ROWS, tk), BF16)
    heads = [slice(h * HEAD_DIM, (h + 1) * HEAD_DIM) for h in range(nh)]

    def logits(j, slot, ms):
        off = pl.multiple_of(j * tk, tk)
        bias = pltpu.bitcast(key_scr[pl.ds(off, tk), :], F32)
        for h, hs in enumerate(heads):
            s_scr[slot, h] = _dot_t(k_ref[pl.ds(off, tk), hs], q_ref[:, hs]) + bias
        out = []
        for h, m in enumerate(ms):
            s = s_scr[slot, h]
            top = jnp.max(_fold8([s[g, :] for g in groups], jnp.maximum), axis=0, keepdims=True)
            out.append(jnp.maximum(m, top))
        return tuple(out)

    def accumulate(j, slot, m_old, m_new):
        off = pl.multiple_of(j * tk, tk)
        for h in range(nh):
            p_scr[h] = jnp.exp2(s_scr[slot, h] - m_new[h]).astype(BF16)
        for h, hs in enumerate(heads):
            vt_ext = jnp.concatenate([vt_ref[hs, pl.ds(off, tk)], ones_v], axis=0)
            acc_scr[h] = jnp.exp2(m_old[h] - m_new[h]) * acc_scr[h] + _dot(vt_ext, p_scr[h])

    m_none = tuple(jnp.full((1, tq), NEG, F32) for _ in range(nh))
    m_first = logits(0, 0, m_none)

    def attn_step(j, st):
        m_old, m_cur = st
        slot = j % 2
        accumulate(j - 1, 1 - slot, m_old, m_cur)
        return m_cur, logits(j, slot, m_cur)

    m_old, m_cur = lax.fori_loop(1, n_kt, attn_step, (m_none, m_first))
    accumulate(n_kt - 1, (n_kt - 1) % 2, m_old, m_cur)
    for h, hs in enumerate(heads):
        out_t = acc_scr[h, :HEAD_DIM, :] / acc_scr[h, HEAD_DIM:HEAD_DIM + 1, :]
        o_ref[:, hs] = out_t.T.astype(o_ref.dtype)


def _dsat(q, k_all, vt_all, qi_cat, ki_cat, kwt, nb, seq, n_keys, n_valid, past, nh):
    tq = _tile(seq, 256, LANES)
    tk = 256
    nq = seq // tq
    width = nh * HEAD_DIM
    n_sel = min(TOPK_MAX, n_valid // 4)
    once = pl.Buffered(1)
    (q, q0), (k_all, k0) = ((a, c // width) for a, c in (q, k_all))
    return pl.pallas_call(
        functools.partial(_dsat_kernel, tq=tq, tk=tk, past=past, n_valid=n_valid, n_sel=n_sel, nh=nh),
        grid=(nb, nq),
        in_specs=[pl.BlockSpec((tq, width), lambda b, i: (b * nq + i, q0)),
                  pl.BlockSpec((n_keys, width), lambda b, i: (b, k0), pipeline_mode=once),
                  pl.BlockSpec((width, n_keys), lambda b, i: (b, 0), pipeline_mode=once),
                  pl.BlockSpec((tq, H_IDX * IDX_K), lambda b, i: (b * nq + i, 0)),
                  pl.BlockSpec((n_keys, IDX_K), lambda b, i: (b, 0), pipeline_mode=once),
                  pl.BlockSpec((LANES, tq), lambda b, i: (0, b * nq + i))],
        out_specs=pl.BlockSpec((tq, width), lambda b, i: (b * nq + i, 0)),
        out_shape=jax.ShapeDtypeStruct((q.shape[0], width), BF16),
        scratch_shapes=[pltpu.VMEM((n_keys, tq), I32), pltpu.VMEM((nh, HEAD_DIM + ONES_ROWS, tq), F32),
                        pltpu.VMEM((2, nh, tk, tq), F32), pltpu.VMEM((nh, tk, tq), BF16)],
        compiler_params=_params(("arbitrary", "arbitrary")),
        name="dsa",
    )(q, k_all, vt_all, qi_cat, ki_cat, kwt)


def _merge_kernel(h_ref, oa_ref, ob_ref, wg1_ref, wg2_ref, b1_ref, b2_ref, wa_ref, wb_ref, o_ref):
    h = h_ref[...]
    g1 = 1.0 / (1.0 + jnp.exp(-(_dot(h, wg1_ref[...]) + b1_ref[...])))
    g2 = 1.0 / (1.0 + jnp.exp(-(_dot(h, wg2_ref[...]) + b2_ref[...])))
    o_ref[...] = (g1 * _dot(oa_ref[...], wa_ref[...]) + g2 * _dot(ob_ref[...], wb_ref[...])).astype(BF16)


def _merge(h_hi, o_a, o_b, w_gate, b_gate, w_pa, w_pb):
    n, d = h_hi.shape
    width = o_a.shape[1]
    tm = _tile(n, 1024, 16)
    tn = _tile(d, 512, LANES)
    nj = d // tn
    b_gate = b_gate.reshape(1, 2 * d)
    row = lambda i, j: (i, 0)
    return pl.pallas_call(
        _merge_kernel,
        grid=(n // tm, nj),
        in_specs=[pl.BlockSpec((tm, d), row), pl.BlockSpec((tm, width), row), pl.BlockSpec((tm, width), row),
                  pl.BlockSpec((d, tn), lambda i, j: (0, j)),
                  pl.BlockSpec((d, tn), lambda i, j: (0, j + nj)),
                  pl.BlockSpec((1, tn), lambda i, j: (0, j)),
                  pl.BlockSpec((1, tn), lambda i, j: (0, j + nj)),
                  pl.BlockSpec((width, tn), lambda i, j: (0, j)),
                  pl.BlockSpec((width, tn), lambda i, j: (0, j))],
        out_specs=pl.BlockSpec((tm, tn), lambda i, j: (i, j)),
        out_shape=jax.ShapeDtypeStruct((n, d), BF16),
        compiler_params=_params(("arbitrary", "arbitrary")),
        name="merge",
    )(h_hi, o_a, o_b, w_gate, w_gate, b_gate, b_gate, w_pa, w_pb)


def _outproj_kernel(m_ref, w_ref, x_ref, mod_ref, gm_ref, gf_ref, x1_ref, h2_ref):
    mod = mod_ref[0]
    x1 = x_ref[...] + mod[2:3, :] * _rms(_dot(m_ref[...], w_ref[...]), gm_ref[...])
    x1_ref[...] = x1
    h2_ref[...] = (_rms(x1, gf_ref[...]) * (1.0 + mod[4:5, :]) + mod[3:4, :]).astype(BF16)


def _outproj(merged, w_out, x2, mod, g_post_mix, g_pre_ffn, seq):
    n, d = x2.shape
    tm = _tile(seq, 512, 16)
    tps = seq // tm
    row = lambda i: (i, 0)
    fix = lambda i: (0, 0)
    return pl.pallas_call(
        _outproj_kernel,
        grid=(n // tm,),
        in_specs=[pl.BlockSpec((tm, d), row), pl.BlockSpec((d, d), fix, pipeline_mode=pl.Buffered(1)),
                  pl.BlockSpec((tm, d), row),
                  pl.BlockSpec((1, N_MOD, d), lambda i: (i // tps, 0, 0)),
                  pl.BlockSpec((1, d), fix), pl.BlockSpec((1, d), fix)],
        out_specs=[pl.BlockSpec((tm, d), row), pl.BlockSpec((tm, d), row)],
        out_shape=[jax.ShapeDtypeStruct((n, d), F32), jax.ShapeDtypeStruct((n, d), BF16)],
        compiler_params=_params(("arbitrary",)),
        name="outproj",
    )(merged, w_out, x2, mod, g_post_mix.reshape(1, d), g_pre_ffn.reshape(1, d))


def _gelu_tanh(x):
    return 0.5 * x * (1.0 + jnp.tanh(0.7978845608028654 * (x + 0.044715 * (x * x * x))))


def _ffn_kernel(h_ref, halo_ref, x1_ref, mod_ref, st_ref, wg_ref, wu_ref, wd_ref, wc_ref, bc_ref, gp_ref,
                y_ref, nc_ref, g_scr, act_scr, *, tm, nsub, tps):
    i = pl.program_id(0)
    c = pl.program_id(1)
    last = pl.num_programs(1) - 1
    sub = tm // nsub
    cur = c % 2

    def step(down, gate_up):
        if gate_up:
            h = h_ref[...]
            wg = wg_ref[...]
            g = _dot(h, wg)
            u = _dot(h, wu_ref[...])
            wc = wc_ref[...]
            bc = bc_ref[...]
            g_scr[HALO:HALO + tm, :] = g
        if down == "set":
            y_ref[...] = _dot(act_scr[1 - cur], wd_ref[...])
        elif down == "add":
            y_ref[...] += _dot(act_scr[1 - cur], wd_ref[...])
        if not gate_up:
            return

        def conv(lo, rows):
            return (bc + wc[0:1, :] * g_scr[lo - 2:lo - 2 + rows, :]
                    + wc[1:2, :] * g_scr[lo - 1:lo - 1 + rows, :] + wc[2:3, :] * g_scr[lo:lo + rows, :])

        if nsub == 1:
            g_halo = _dot(halo_ref[...], wg)
            g_scr[0:HALO, :] = g_halo
            g_scr[HALO - 2:HALO, :] = jnp.where(i % tps == 0, st_ref[0], g_halo[HALO - 2:HALO, :])
            nc_ref[0] = g_scr[HALO + tm - 2:HALO + tm, :]
            act_scr[cur] = (_gelu_tanh(conv(HALO, tm)) * u).astype(BF16)
        else:
            for s in range(nsub):
                lo = HALO + s * sub
                nc_ref[s] = g_scr[lo + sub - 2:lo + sub, :]
                g_scr[lo - 2:lo, :] = st_ref[s]
                act_scr[cur, s * sub:(s + 1) * sub, :] = (
                    _gelu_tanh(conv(lo, sub)) * u[s * sub:(s + 1) * sub, :]).astype(BF16)

    pl.when(c == 0)(lambda: step(None, True))
    pl.when(c == 1)(lambda: step("set", True))
    pl.when(jnp.logical_and(c > 1, c < last))(lambda: step("add", True))

    @pl.when(c == last)
    def _():
        step("add", False)
        for s in range(nsub):
            rows = slice(s * sub, (s + 1) * sub)
            y_ref[rows, :] = x1_ref[rows, :] + mod_ref[s][5:6, :] * _rms(y_ref[rows, :], gp_ref[...])


def _ffn(h2, x1, mod, state, w_gate, w_up, w_down, w_conv, b_conv, g_post, seq):
    n, d = x1.shape
    dff = w_gate.shape[1]
    nb = n // seq
    tm = _row_tile(n, seq, 1024)
    tf = _tile(dff, 512, LANES)
    n_chunks = dff // tf
    assert n_chunks >= 2, "the lagged down projection needs at least two d_ff chunks"
    up = lambda c: jnp.minimum(c, n_chunks - 1)
    down = lambda c: jnp.maximum(c - 1, 0)
    per_tile_c = lambda i, c: (i, 0, up(c))
    if tm <= seq:
        nsub, tps = 1, seq // tm
        per_seq = lambda i, c: (i // tps, 0, 0)
        per_seq_c = lambda i, c: (i // tps, 0, up(c))
    else:
        nsub, tps = tm // seq, 1
        per_seq = lambda i, c: (i, 0, 0)
        per_seq_c = per_tile_c
    halo = h2.reshape(n // tm, tm, d)[:, tm - HALO:, :].reshape(n // tm * HALO, d)
    row = lambda i, c: (i, 0)
    y, new_conv = pl.pallas_call(
        functools.partial(_ffn_kernel, tm=tm, nsub=nsub, tps=tps),
        grid=(n // tm, n_chunks + 1),
        in_specs=[pl.BlockSpec((tm, d), row, pipeline_mode=pl.Buffered(1)),
                  pl.BlockSpec((HALO, d), lambda i, c: (jnp.maximum(i - 1, 0), 0)),
                  pl.BlockSpec((tm, d), row, pipeline_mode=pl.Buffered(1)),
                  pl.BlockSpec((nsub, N_MOD, d), per_seq),
                  pl.BlockSpec((nsub, CONV_W - 1, tf), per_seq_c),
                  pl.BlockSpec((d, tf), lambda i, c: (0, up(c))),
                  pl.BlockSpec((d, tf), lambda i, c: (0, up(c))),
                  pl.BlockSpec((tf, d), lambda i, c: (down(c), 0)),
                  pl.BlockSpec((CONV_W, tf), lambda i, c: (0, up(c))),
                  pl.BlockSpec((1, tf), lambda i, c: (0, up(c))),
                  pl.BlockSpec((1, d), lambda i, c: (0, 0))],
        out_specs=[pl.BlockSpec((tm, d), row, pipeline_mode=pl.Buffered(1)),
                   pl.BlockSpec((nsub, CONV_W - 1, tf), per_tile_c)],
        out_shape=[jax.ShapeDtypeStruct((n, d), F32),
                   jax.ShapeDtypeStruct((nb * tps, CONV_W - 1, dff), F32)],
        scratch_shapes=[pltpu.VMEM((HALO + tm, tf), F32), pltpu.VMEM((2, tm, tf), BF16)],
        compiler_params=_params(("arbitrary", "arbitrary")),
        name="ffn",
    )(h2, halo, x1, mod, state, w_gate, w_up, w_down, w_conv, b_conv.reshape(1, dff), g_post.reshape(1, d))
    return y, new_conv.reshape(nb, tps, CONV_W - 1, dff)[:, tps - 1]


def _rope_tables(pos):
    def angles(dim):
        half = dim // 2
        inv = ROPE_THETA ** (-jnp.arange(half, dtype=F32) * 2.0 / dim)
        ang = pos.astype(F32)[:, None] * inv[None, :]
        return jnp.cos(ang), jnp.sin(ang)

    cos, sin = angles(HEAD_DIM)
    cos_h = jnp.concatenate([cos, cos], axis=1)
    sin_h = jnp.concatenate([-sin, sin], axis=1)

    cos, sin = angles(D_IDX)
    zero = jnp.zeros_like(sin)
    n_rot = H_IDX + 1
    tail = IDX_COLS - n_rot * D_IDX
    t = pos.shape[0]
    c_i = jnp.concatenate([jnp.tile(jnp.concatenate([cos, cos], 1), (1, n_rot)), jnp.ones((t, tail), F32)], 1)
    s1_i = jnp.concatenate([jnp.tile(jnp.concatenate([-sin, zero], 1), (1, n_rot)), jnp.zeros((t, tail), F32)], 1)
    s2_i = jnp.concatenate([jnp.tile(jnp.concatenate([zero, sin], 1), (1, n_rot)), jnp.zeros((t, tail), F32)], 1)
    return cos_h, sin_h, c_i, s1_i, s2_i


def _with_past(past, new, nb, seq, n_keys):
    w = new.shape[1]
    if past is None and n_keys == seq:
        return new
    parts = [] if past is None else [past.astype(new.dtype).reshape(nb, -1, w)]
    parts.append(new.reshape(nb, seq, w))
    have = sum(p.shape[1] for p in parts)
    if n_keys > have:
        parts.append(jnp.zeros((nb, n_keys - have, w), new.dtype))
    return jnp.concatenate(parts, axis=1).reshape(nb * n_keys, w)


def _layer(x, mod, past, conv_state, p):
    nb, seq, d = x.shape
    n = nb * seq
    nh = p["w_pa"].shape[0] // HEAD_DIM
    width = nh * HEAD_DIM
    n_past = 0 if past is None else past["a_k"].shape[1]
    n_valid = n_past + seq
    n_keys = -(-n_valid // 256) * 256
    x2 = x.reshape(n, d)

    h_hi, h_lo = _norm(x2, p["g_pre_mix"], mod, seq)
    cos_h, sin_h, c_i, s1_i, s2_i = _rope_tables(n_past + jnp.arange(seq, dtype=I32))
    p16, kv32, vt_new = _inproj(h_hi, p["w_main"], cos_h, sin_h, seq, width)
    col = lambda a, g: a[:, g * width:(g + 1) * width]
    qi_hi, qi_lo, kw, ki_hi, ki_lo = _idxproj(h_hi, h_lo, p["w_idx_hi"], p["w_idx_lo"], c_i, s1_i, s2_i, seq)

    def keys(g, past_rows):
        if past_rows is None and n_keys == seq:
            return p16, g * width
        return _with_past(past_rows, col(p16, g), nb, seq, n_keys), 0

    zeros = jnp.zeros((n, D_IDX), BF16)
    qi_cat = jnp.concatenate(
        [part for h in range(H_IDX) for sl in [slice(h * D_IDX, (h + 1) * D_IDX)]
         for part in (qi_hi[:, sl], qi_hi[:, sl], qi_lo[:, sl], zeros)], axis=1)
    k_hi, k_lo = ki_hi[:, :D_IDX], ki_lo[:, :D_IDX]
    if past is None:
        pa_k = pa_v = pb_k = pb_v = pi_hi = pi_lo = None
    else:
        pa_k, pa_v, pb_k, pb_v = (past[k].reshape(nb, n_past, width) for k in ("a_k", "a_v", "b_k", "b_v"))
        pi_hi, pi_lo = _split(past["b_kidx"].astype(F32))
    k_hi = _with_past(pi_hi, k_hi, nb, seq, n_keys)
    k_lo = _with_past(pi_lo, k_lo, nb, seq, n_keys)
    ki_cat = jnp.concatenate([k_hi, k_lo, k_hi, jnp.zeros_like(k_hi)], axis=1)

    o_a = _sb((p16, 0), keys(1, pa_k), keys(2, pa_v), nb, seq, n_keys, n_past, nh)
    seq_q = max(seq, LANES)
    pad_q = lambda a: a if seq_q == seq else jnp.pad(
        a.reshape(nb, seq, -1), ((0, 0), (0, seq_q - seq), (0, 0))).reshape(nb * seq_q, -1)
    if vt_new is not None and past is None and n_keys == seq:
        vt_all = vt_new
    else:
        vt_all = _with_past(pb_v, col(p16, 5), nb, seq, n_keys).reshape(nb, n_keys, width)
        vt_all = jnp.swapaxes(vt_all, 1, 2).reshape(nb * width, n_keys)
    q_b = (p16, 3 * width) if seq_q == seq else (pad_q(col(p16, 3)), 0)
    o_b = _dsat(q_b, keys(4, pb_k), vt_all, pad_q(qi_cat), ki_cat,
                pad_q(kw).T, nb, seq_q, n_keys, n_valid, n_past, nh)
    if seq_q != seq:
        o_b = o_b.reshape(nb, seq_q, width)[:, :seq].reshape(n, width)

    merged = _merge(h_hi, o_a, o_b, p["w_gate"], p["b_gate"], p["w_pa"], p["w_pb"])
    x1, h2 = _outproj(merged, p["w_out"], x2, mod, p["g_post_mix"], p["g_pre_ffn"], seq)
    y, new_conv = _ffn(h2, x1, mod, conv_state, p["w_ffn_gate"], p["w_ffn_up"], p["w_ffn_down"],
                       p["w_conv"], p["b_conv"], p["g_post_ffn"], seq)

    heads = lambda g: kv32[g].reshape(nb, seq, nh, HEAD_DIM)
    new = (heads(1), heads(2), heads(4), heads(5), kw[:, :D_IDX].reshape(nb, seq, D_IDX), new_conv)
    return y.reshape(nb, seq, d), new


def kernel(x_prompt, x_sample, c_prompt, c_sample, cache_a_k, cache_a_v, cache_b_k, cache_b_v, cache_b_kidx,
           state_ffn_conv, w_mod, b_mod, g_pre_mix, w_in, w_merge_gate, b_merge_gate, w_proj_a, w_proj_b, w_out,
           g_post_mix, g_pre_ffn, w_ffn_gate, w_ffn_up, w_conv, b_conv, w_ffn_down, g_post_ffn):
    depth = w_mod.shape[0]
    nbp, nbs = x_prompt.shape[0], x_sample.shape[0]
    d = x_prompt.shape[2]
    width = w_proj_a.shape[1]
    dff = w_ffn_gate.shape[2]
    assert w_in.shape[2] == 6 * width + H_IDX * D_IDX + D_IDX + H_IDX
    assert cache_b_kidx.shape[-1] == D_IDX and cache_a_k.shape[-1] == HEAD_DIM

    y_p, y_s = x_prompt, x_sample
    st_p = [[] for _ in range(6)]
    st_s = [[] for _ in range(6)]
    for l in range(depth):
        w_idx = jnp.pad(w_in[l][:, 6 * width:], ((0, 0), (0, IDX_COLS - (w_in.shape[2] - 6 * width))))
        w_idx_hi, w_idx_lo = _split(w_idx)
        p = dict(
            g_pre_mix=g_pre_mix[l], w_main=w_in[l][:, :6 * width].astype(BF16), w_idx_hi=w_idx_hi, w_idx_lo=w_idx_lo,
            w_gate=w_merge_gate[l].astype(BF16), b_gate=b_merge_gate[l],
            w_pa=w_proj_a[l].astype(BF16), w_pb=w_proj_b[l].astype(BF16), w_out=w_out[l].astype(BF16),
            g_post_mix=g_post_mix[l], g_pre_ffn=g_pre_ffn[l],
            w_ffn_gate=w_ffn_gate[l].astype(BF16), w_ffn_up=w_ffn_up[l].astype(BF16),
            w_ffn_down=w_ffn_down[l].astype(BF16), w_conv=w_conv[l], b_conv=b_conv[l], g_post_ffn=g_post_ffn[l])

        c_all = jnp.concatenate([c_prompt, c_sample], axis=0)
        pad = -c_all.shape[0] % 16
        mod = _mod(jnp.pad(c_all, ((0, pad), (0, 0))), w_mod[l], b_mod[l]).reshape(-1, N_MOD, d)
        past = dict(a_k=cache_a_k[l], a_v=cache_a_v[l], b_k=cache_b_k[l], b_v=cache_b_v[l], b_kidx=cache_b_kidx[l])
        y_p, new_p = _layer(y_p, mod[:nbp], None, jnp.zeros((nbp, CONV_W - 1, dff), F32), p)
        y_s, new_s = _layer(y_s, mod[nbp:nbp + nbs], past, state_ffn_conv[l], p)
        for i in range(6):
            st_p[i].append(new_p[i])
            st_s[i].append(new_s[i])
    sp = [jnp.stack(s, axis=0) for s in st_p]
    ss = [jnp.stack(s, axis=0) for s in st_s]
    return (y_p, y_s, sp[0], sp[1], sp[2], sp[3], sp[4], sp[5], ss[0], ss[1], ss[2], ss[3], ss[4], ss[5])
```

```python
import functools

import jax
import jax.numpy as jnp
from jax import lax
from jax.experimental import pallas as pl
from jax.experimental.pallas import tpu as pltpu

F32 = jnp.float32
BF16 = jnp.bfloat16
I32 = jnp.int32

CHUNK = 64
HEAD_DIM = 128
H_IDX = 4
D_IDX = 64
TOPK_MAX = 256
CONV_W = 3
ROPE_THETA = 10000.0
RMS_EPS = 1e-6
N_MOD = 6
IDX_COLS = 384
IDX_K = 256
LANES = 128
HALO = 16
ONES_ROWS = 16
LOG2E = 1.4426950408889634
UNDERFLOW_LOG2 = 160.0
NEG = -1e30
INT_MIN = -2 ** 31
VMEM_LIMIT = 56 * 1024 * 1024


def _tile(n, pref, mult):
    t = min(pref, n)
    t -= t % mult
    while t >= mult:
        if n % t == 0:
            return t
        t -= mult
    return n


def _row_tile(n, seq, pref):
    if seq >= pref or n == seq:
        return _tile(seq, pref, 16)
    return _tile(n, pref, seq)


def _params(sem):
    return pltpu.CompilerParams(dimension_semantics=sem, vmem_limit_bytes=VMEM_LIMIT)


def _split(x):
    hi = x.astype(BF16)
    lo = (x - hi.astype(F32)).astype(BF16)
    return hi, lo


def _dot(a, b):
    return jnp.dot(a, b, preferred_element_type=F32)


def _dot_t(a, b):
    return lax.dot_general(a, b, (((1,), (1,)), ((), ())), preferred_element_type=F32)


def _rms(x, g):
    ms = jnp.mean(x * x, axis=-1, keepdims=True)
    return x * lax.rsqrt(ms + RMS_EPS) * g


def _mod_kernel(c_ref, w_ref, b_ref, o_ref):
    c = c_ref[...]
    s = c / (1.0 + jnp.exp(-c))
    s_hi, s_lo = _split(s)
    w_hi, w_lo = _split(w_ref[...])
    o_ref[...] = _dot(s_hi, w_hi) + _dot(s_hi, w_lo) + _dot(s_lo, w_hi) + b_ref[...]


def _mod(c, w, b):
    bc, d = c.shape
    n = w.shape[1]
    tn = _tile(n, 1024, LANES)
    return pl.pallas_call(
        _mod_kernel,
        grid=(n // tn,),
        in_specs=[pl.BlockSpec((bc, d), lambda j: (0, 0)),
                  pl.BlockSpec((d, tn), lambda j: (0, j)),
                  pl.BlockSpec((1, tn), lambda j: (0, j))],
        out_specs=pl.BlockSpec((bc, tn), lambda j: (0, j)),
        out_shape=jax.ShapeDtypeStruct((bc, n), F32),
        compiler_params=_params(("arbitrary",)),
        name="mod",
    )(c, w, b.reshape(1, n))


def _norm_kernel(x_ref, g_ref, mod_ref, hi_ref, lo_ref):
    m = mod_ref[0]
    h = _rms(x_ref[...], g_ref[...]) * (1.0 + m[1:2, :]) + m[0:1, :]
    hi, lo = _split(h)
    hi_ref[...] = hi
    lo_ref[...] = lo


def _norm(x2, g, mod, seq):
    n, d = x2.shape
    tr = _tile(seq, 512, 16)
    tps = seq // tr
    return pl.pallas_call(
        _norm_kernel,
        grid=(n // tr,),
        in_specs=[pl.BlockSpec((tr, d), lambda i: (i, 0)),
                  pl.BlockSpec((1, d), lambda i: (0, 0)),
                  pl.BlockSpec((1, N_MOD, d), lambda i: (i // tps, 0, 0))],
        out_specs=[pl.BlockSpec((tr, d), lambda i: (i, 0))] * 2,
        out_shape=[jax.ShapeDtypeStruct((n, d), BF16)] * 2,
        compiler_params=_params(("arbitrary",)),
        name="norm",
    )(x2, g.reshape(1, d), mod)


N_PROJ = 6


KV_GROUPS = (1, 2, 4, 5)


def _inproj_kernel(a_ref, w_ref, c_ref, s_ref, p16_ref, ka_ref, va_ref, kb_ref, vb_ref, *rest, tm, nh):
    r_scr = rest[-1]
    acc = _dot(a_ref[...], w_ref[...])
    c = c_ref[0]
    s = s_ref[0]
    for k in range(nh):
        sl = slice(k * HEAD_DIM, (k + 1) * HEAD_DIM)
        xs = acc[:, sl]
        r = xs * c + pltpu.roll(xs, HEAD_DIM // 2, 1) * s
        r_scr[:, sl] = r
        p16_ref[:, sl] = r.astype(BF16)
    for grp, ref in zip(KV_GROUPS, (ka_ref, va_ref, kb_ref, vb_ref)):
        @pl.when(pl.program_id(1) == grp)
        def _(ref=ref):
            for k in range(nh):
                ref[pl.ds(k, tm, stride=nh), :] = r_scr[:, k * HEAD_DIM:(k + 1) * HEAD_DIM]

    if len(rest) == 2:
        @pl.when(pl.program_id(1) == N_PROJ - 1)
        def _():
            rest[0][...] = r_scr[...].T.astype(BF16)


def _inproj(h_hi, w_main, cos_t, sin_t, seq, width):
    n, d = h_hi.shape
    nh = width // HEAD_DIM
    tm = _row_tile(n, seq, 1024)
    with_vt = tm <= seq
    tn = width
    npj = 1
    scale = LOG2E * HEAD_DIM ** -0.5
    one, zero = jnp.ones_like(cos_t), jnp.zeros_like(cos_t)
    c_t = jnp.stack([scale * one, one, one, scale * cos_t, cos_t, one])
    s_t = jnp.stack([zero, zero, zero, scale * sin_t, sin_t, zero])
    if tm <= seq:
        tps = seq // tm
        tab_map = lambda i, j: (j // npj, i % tps, 0)
    else:
        c_t = jnp.tile(c_t, (1, tm // seq, 1))
        s_t = jnp.tile(s_t, (1, tm // seq, 1))
        tab_map = lambda i, j: (j // npj, 0, 0)
    out = pl.pallas_call(
        functools.partial(_inproj_kernel, tm=tm, nh=nh),
        grid=(n // tm, N_PROJ * npj),
        in_specs=[pl.BlockSpec((tm, d), lambda i, j: (i, 0)),
                  pl.BlockSpec((d, tn), lambda i, j: (0, j)),
                  pl.BlockSpec((1, tm, HEAD_DIM), tab_map),
                  pl.BlockSpec((1, tm, HEAD_DIM), tab_map)],
        out_specs=[pl.BlockSpec((tm, tn), lambda i, j: (i, j))]
        + [pl.BlockSpec((tm * nh, HEAD_DIM), lambda i, j: (i, 0), pipeline_mode=pl.Buffered(1))] * len(KV_GROUPS)
        + ([pl.BlockSpec((width, tm), lambda i, j: (i // tps, i % tps))] if with_vt else []),
        out_shape=[jax.ShapeDtypeStruct((n, N_PROJ * width), BF16)]
        + [jax.ShapeDtypeStruct((n * nh, HEAD_DIM), F32)] * len(KV_GROUPS)
        + ([jax.ShapeDtypeStruct((n // seq * width, seq), BF16)] if with_vt else []),
        scratch_shapes=[pltpu.VMEM((tm, tn), F32)],
        compiler_params=_params(("arbitrary", "arbitrary")),
        name="inproj",
    )(h_hi, w_main, c_t, s_t)
    return out[0], dict(zip(KV_GROUPS, out[1:1 + len(KV_GROUPS)])), (out[-1] if with_vt else None)


def _idxproj_kernel(hi_ref, lo_ref, whi_ref, wlo_ref, c_ref, s1_ref, s2_ref,
                    qhi_ref, qlo_ref, kw_ref, khi_ref, klo_ref):
    a_hi = hi_ref[...]
    w_hi = whi_ref[...]
    acc = _dot(a_hi, w_hi) + _dot(a_hi, wlo_ref[...]) + _dot(lo_ref[...], w_hi)
    for s in range(IDX_COLS // LANES):
        sl = slice(s * LANES, (s + 1) * LANES)
        xs = acc[:, sl]
        r = (xs * c_ref[:, sl] + pltpu.roll(xs, LANES - D_IDX // 2, 1) * s1_ref[:, sl]
             + pltpu.roll(xs, D_IDX // 2, 1) * s2_ref[:, sl])
        hi, lo = _split(r)
        if s < 2:
            qhi_ref[:, sl] = hi
            qlo_ref[:, sl] = lo
        else:
            kw_ref[...] = r
            khi_ref[...] = hi
            klo_ref[...] = lo


def _idxproj(h_hi, h_lo, w_hi, w_lo, c_t, s1_t, s2_t, seq):
    n, d = h_hi.shape
    tm = _row_tile(n, seq, 512)
    if tm <= seq:
        tps = seq // tm
        tab_map = lambda i: (i % tps, 0)
    else:
        c_t, s1_t, s2_t = (jnp.tile(t, (tm // seq, 1)) for t in (c_t, s1_t, s2_t))
        tab_map = lambda i: (0, 0)
    row = lambda i: (i, 0)
    return pl.pallas_call(
        _idxproj_kernel,
        grid=(n // tm,),
        in_specs=[pl.BlockSpec((tm, d), row), pl.BlockSpec((tm, d), row),
                  pl.BlockSpec((d, IDX_COLS), lambda i: (0, 0)),
                  pl.BlockSpec((d, IDX_COLS), lambda i: (0, 0)),
                  pl.BlockSpec((tm, IDX_COLS), tab_map),
                  pl.BlockSpec((tm, IDX_COLS), tab_map),
                  pl.BlockSpec((tm, IDX_COLS), tab_map)],
        out_specs=[pl.BlockSpec((tm, 2 * LANES), row), pl.BlockSpec((tm, 2 * LANES), row),
                   pl.BlockSpec((tm, LANES), row), pl.BlockSpec((tm, LANES), row),
                   pl.BlockSpec((tm, LANES), row)],
        out_shape=[jax.ShapeDtypeStruct((n, 2 * LANES), BF16), jax.ShapeDtypeStruct((n, 2 * LANES), BF16),
                   jax.ShapeDtypeStruct((n, LANES), F32), jax.ShapeDtypeStruct((n, LANES), BF16),
                   jax.ShapeDtypeStruct((n, LANES), BF16)],
        compiler_params=_params(("arbitrary",)),
        name="idxproj",
    )(h_hi, h_lo, w_hi, w_lo, c_t, s1_t, s2_t)


def _sb_kernel(q_ref, k_ref, v_ref, o_ref, acc_scr, z_scr, i_scr, *, tq, tk, past, group):
    q0 = past + pl.program_id(2) * tq
    n_kt = (q0 + tq - 2) // tk + 1
    n_clear = q0 // tk
    row_pos = q0 + lax.broadcasted_iota(I32, (tq, 1), 0)
    col = lax.broadcasted_iota(I32, (1, tk), 1)
    suffix = jnp.where(lax.broadcasted_iota(I32, (tk, tk), 0) >= lax.broadcasted_iota(I32, (tk, tk), 1),
                       1.0, 0.0).astype(BF16)
    suffix2 = jnp.concatenate([suffix, suffix], axis=0)
    acc_scr[...] = jnp.zeros_like(acc_scr)

    heads = [slice(h * HEAD_DIM, (h + 1) * HEAD_DIM) for h in range(group)]

    def tiles(js, carries, masked):
        offs = [pl.multiple_of(j * tk, tk) for j in js]
        if masked:
            valids = [(off + col) < row_pos for off in offs]
        for t, off in enumerate(offs):
            for h, hs in enumerate(heads):
                z_scr[t * group + h] = _dot_t(q_ref[:, hs], k_ref[pl.ds(off, tk), hs])
        for t in range(len(js)):
            for h in range(group):
                z = z_scr[t * group + h]
                sp = jnp.maximum(z, 0.0) + jnp.log2(1.0 + jnp.exp2(-jnp.abs(z)))
                if masked:
                    sp = jnp.where(valids[t], sp, 0.0)
                hi, lo = _split(sp)
                i_scr[t * group + h] = _dot(jnp.concatenate([hi, lo], axis=1), suffix2)
        carries = list(carries)
        for t, off in enumerate(offs):
            for h, hs in enumerate(heads):
                w = jnp.exp2(z_scr[t * group + h] - i_scr[t * group + h] - carries[h])
                if masked:
                    w = jnp.where(valids[t], w, 0.0)
                acc_scr[h] += _dot(w.astype(BF16), v_ref[pl.ds(off, tk), hs])
                carries[h] = carries[h] + i_scr[t * group + h, :, 0:1]
        return tuple(carries)

    def least(carries):
        m = carries[0]
        for c in carries[1:]:
            m = jnp.minimum(m, c)
        return jnp.min(m)

    st = tuple(jnp.zeros((tq, 1), F32) for _ in range(group))
    st = lax.fori_loop(0, n_kt - n_clear, lambda t, c: tiles([n_kt - 1 - t], c, True), st)

    def more(state):
        t, low, _ = state
        return jnp.logical_and(t < n_clear, low < UNDERFLOW_LOG2)

    def step(state):
        t, _, carries = state
        carries = tiles([n_clear - 1 - t], carries, False)
        return t + 1, least(carries), carries

    lax.while_loop(more, step, (jnp.int32(0), least(st), st))
    for h in range(group):
        o_ref[:, h * HEAD_DIM:(h + 1) * HEAD_DIM] = acc_scr[h].astype(o_ref.dtype)


def _sb(q, k_all, v_all, nb, seq, n_keys, past, nh):
    tq = _tile(seq, 256, 16)
    tk = 256
    nq = seq // tq
    group = next(g for g in (8, 4, 2, 1) if nh % g == 0)
    gw = group * HEAD_DIM
    (q, q0), (k_all, k0), (v_all, v0) = ((a, c // gw) for a, c in (q, k_all, v_all))
    once = pl.Buffered(1)
    return pl.pallas_call(
        functools.partial(_sb_kernel, tq=tq, tk=tk, past=past, group=group),
        grid=(nb, nh // group, nq),
        in_specs=[pl.BlockSpec((tq, gw), lambda b, h, i: (b * nq + i, q0 + h)),
                  pl.BlockSpec((n_keys, gw), lambda b, h, i: (b, k0 + h), pipeline_mode=once),
                  pl.BlockSpec((n_keys, gw), lambda b, h, i: (b, v0 + h), pipeline_mode=once)],
        out_specs=pl.BlockSpec((tq, gw), lambda b, h, i: (b * nq + i, h)),
        out_shape=jax.ShapeDtypeStruct((q.shape[0], nh * HEAD_DIM), BF16),
        scratch_shapes=[pltpu.VMEM((group, tq, HEAD_DIM), F32), pltpu.VMEM((group, tq, tk), F32),
                        pltpu.VMEM((group, tq, tk), F32)],
        compiler_params=_params(("arbitrary", "arbitrary", "arbitrary")),
        name="sb",
    )(q, k_all, v_all)


def _fold8(parts, op):
    while len(parts) > 1:
        parts = [op(parts[i], parts[i + 1]) if i + 1 < len(parts) else parts[i] for i in range(0, len(parts), 2)]
    return parts[0]


def _dsat_kernel(q_ref, k_ref, vt_ref, qi_ref, ki_ref, kwt_ref, o_ref, key_scr, acc_scr, s_scr, p_scr,
                 *, tq, tk, past, n_valid, n_sel, nh):
    q0 = past + pl.program_id(1) * tq
    col_pos = q0 + lax.broadcasted_iota(I32, (1, tq), 1)
    col_lim = jnp.minimum((col_pos // CHUNK + 1) * CHUNK, n_valid)
    n_kt = (jnp.minimum(((q0 + tq - 1) // CHUNK + 1) * CHUNK, n_valid) + tk - 1) // tk
    row = lax.broadcasted_iota(I32, (tk, 1), 0)
    ksel = float(n_sel)
    groups = [slice(8 * r, 8 * r + 8) for r in range(tk // 8)]

    wrow = kwt_ref[D_IDX:D_IDX + H_IDX, :] * ((H_IDX ** -0.5) * (D_IDX ** -0.5))

    def score_pair(t, _):
        offs = [pl.multiple_of(jnp.minimum(2 * t + u, n_kt - 1) * tk, tk) for u in range(2)]
        raw = [[_dot_t(ki_ref[pl.ds(off, tk), :], qi_ref[:, h * IDX_K:(h + 1) * IDX_K]) for h in range(H_IDX)]
               for off in offs]
        for off, dots in zip(offs, raw):
            score = jnp.zeros((tk, tq), F32)
            for h, s in enumerate(dots):
                score = score + wrow[h:h + 1, :] * jnp.maximum(s, 0.0)
            bits = pltpu.bitcast(score, I32)
            key = jnp.where(bits < 0, INT_MIN - bits, bits)
            key_scr[pl.ds(off, tk), :] = jnp.where((off + row) < col_lim, key, INT_MIN)
        return 0

    lax.fori_loop(0, (n_kt + 1) // 2, score_pair, 0)

    def count_ge(thr8):
        def body(t, cnt):
            for u in range(2):
                j = 2 * t + u
                off = pl.multiple_of(jnp.minimum(j, n_kt - 1) * tk, tk)
                key = key_scr[pl.ds(off, tk), :]
                part = _fold8([jnp.where(key[g, :] >= thr8, 1.0, 0.0) for g in groups], jnp.add)
                cnt = cnt + (part if u == 0 else jnp.where(j < n_kt, part, 0.0))
            return cnt

        cnt = lax.fori_loop(0, (n_kt + 1) // 2, body, jnp.zeros((8, tq), F32))
        return jnp.broadcast_to(jnp.sum(cnt, axis=0, keepdims=True), (8, tq))

    def bit_step(it, st):
        cand, have = st
        trial = cand | lax.shift_left(jnp.int32(1), 30 - it)
        cnt = count_ge(trial)
        take = cnt >= ksel
        return jnp.where(take, trial, cand), jnp.where(take, cnt, have)

    zero8 = jnp.zeros((8, tq), I32)
    cnt0 = count_ge(zero8)
    start = (jnp.where(cnt0 >= ksel, zero8, INT_MIN), jnp.where(cnt0 >= ksel, cnt0, float(2 ** 30)))
    thr8, have8 = lax.fori_loop(0, 31, bit_step, start)
    thr = thr8[0:1, :]
    overfull = jnp.max(jnp.where(have8 > ksel, 1.0, 0.0)) > 0.0

    @pl.when(jnp.logical_not(overfull))
    def _():
        def bias_tile(j, _):
            off = pl.multiple_of(j * tk, tk)
            bias = jnp.where(key_scr[pl.ds(off, tk), :] >= thr, 0.0, NEG)
            key_scr[pl.ds(off, tk), :] = pltpu.bitcast(bias, I32)
            return 0

        lax.fori_loop(0, n_kt, bias_tile, 0)

    @pl.when(overfull)
    def _():
        no_thr8 = thr8 == INT_MIN
        cnt_gt = count_ge(jnp.where(no_thr8, thr8, thr8 + 1))
        need = jnp.where(no_thr8, 0.0, ksel - cnt_gt)[0:1, :]
        prefix = jnp.where(lax.broadcasted_iota(I32, (tk, tk), 0) >= lax.broadcasted_iota(I32, (tk, tk), 1),
                           1.0, 0.0).astype(BF16)

        def bias_tile(j, carry):
            off = pl.multiple_of(j * tk, tk)
            key = key_scr[pl.ds(off, tk), :]
            eq = key == thr
            eqf = jnp.where(eq, 1.0, 0.0)
            rank_incl = _dot(prefix, eqf.astype(BF16)) + carry
            tie_ok = (rank_incl - eqf) < need
            bias = jnp.where(key > thr, 0.0, jnp.where(eq, jnp.where(tie_ok, 0.0, NEG), NEG))
            key_scr[pl.ds(off, tk), :] = pltpu.bitcast(bias, I32)
            return rank_incl[tk - 1:tk, :]

        lax.fori_loop(0, n_kt, bias_tile, jnp.zeros((1, tq), F32))

    acc_scr[...] = jnp.zeros_like(acc_scr)
    ones_v = jnp.ones((ONES_ROWS, tk), BF16)
    heads = [slice(h * HEAD_DIM, (h + 1) * HEAD_DIM) for h in range(nh)]

    def logits(j, slot, ms):
        off = pl.multiple_of(j * tk, tk)
        bias = pltpu.bitcast(key_scr[pl.ds(off, tk), :], F32)
        for h, hs in enumerate(heads):
            s_scr[slot, h] = _dot_t(k_ref[pl.ds(off, tk), hs], q_ref[:, hs]) + bias
        out = []
        for h, m in enumerate(ms):
            s = s_scr[slot, h]
            top = jnp.max(_fold8([s[g, :] for g in groups], jnp.maximum), axis=0, keepdims=True)
            out.append(jnp.maximum(m, top))
        return tuple(out)

    def accumulate(j, slot, m_old, m_new):
        off = pl.multiple_of(j * tk, tk)
        for h in range(nh):
            p_scr[h] = jnp.exp2(s_scr[slot, h] - m_new[h]).astype(BF16)
        for h, hs in enumerate(heads):
            vt_ext = jnp.concatenate([vt_ref[hs, pl.ds(off, tk)], ones_v], axis=0)
            acc_scr[h] = jnp.exp2(m_old[h] - m_new[h]) * acc_scr[h] + _dot(vt_ext, p_scr[h])

    m_none = tuple(jnp.full((1, tq), NEG, F32) for _ in range(nh))
    m_first = logits(0, 0, m_none)

    def attn_step(j, st):
        m_old, m_cur = st
        slot = j % 2
        accumulate(j - 1, 1 - slot, m_old, m_cur)
        return m_cur, logits(j, slot, m_cur)

    m_old, m_cur = lax.fori_loop(1, n_kt, attn_step, (m_none, m_first))
    accumulate(n_kt - 1, (n_kt - 1) % 2, m_old, m_cur)
    for h, hs in enumerate(heads):
        out_t = acc_scr[h, :HEAD_DIM, :] / acc_scr[h, HEAD_DIM:HEAD_DIM + 1, :]
        o_ref[:, hs] = out_t.T.astype(o_ref.dtype)


def _dsat(q, k_all, vt_all, qi_cat, ki_cat, kwt, nb, seq, n_keys, n_valid, past, nh):
    tq = _tile(seq, 256, LANES)
    tk = 256
    nq = seq // tq
    width = nh * HEAD_DIM
    n_sel = min(TOPK_MAX, n_valid // 4)
    once = pl.Buffered(1)
    (q, q0), (k_all, k0) = ((a, c // width) for a, c in (q, k_all))
    return pl.pallas_call(
        functools.partial(_dsat_kernel, tq=tq, tk=tk, past=past, n_valid=n_valid, n_sel=n_sel, nh=nh),
        grid=(nb, nq),
        in_specs=[pl.BlockSpec((tq, width), lambda b, i: (b * nq + i, q0)),
                  pl.BlockSpec((n_keys, width), lambda b, i: (b, k0), pipeline_mode=once),
                  pl.BlockSpec((width, n_keys), lambda b, i: (b, 0), pipeline_mode=once),
                  pl.BlockSpec((tq, H_IDX * IDX_K), lambda b, i: (b * nq + i, 0)),
                  pl.BlockSpec((n_keys, IDX_K), lambda b, i: (b, 0), pipeline_mode=once),
                  pl.BlockSpec((LANES, tq), lambda b, i: (0, b * nq + i))],
        out_specs=pl.BlockSpec((tq, width), lambda b, i: (b * nq + i, 0)),
        out_shape=jax.ShapeDtypeStruct((q.shape[0], width), BF16),
        scratch_shapes=[pltpu.VMEM((n_keys, tq), I32), pltpu.VMEM((nh, HEAD_DIM + ONES_ROWS, tq), F32),
                        pltpu.VMEM((2, nh, tk, tq), F32), pltpu.VMEM((nh, tk, tq), BF16)],
        compiler_params=_params(("arbitrary", "arbitrary")),
        name="dsa",
    )(q, k_all, vt_all, qi_cat, ki_cat, kwt)


def _merge_kernel(h_ref, oa_ref, ob_ref, wg1_ref, wg2_ref, b1_ref, b2_ref, wa_ref, wb_ref, o_ref):
    h = h_ref[...]
    g1 = 1.0 / (1.0 + jnp.exp(-(_dot(h, wg1_ref[...]) + b1_ref[...])))
    g2 = 1.0 / (1.0 + jnp.exp(-(_dot(h, wg2_ref[...]) + b2_ref[...])))
    o_ref[...] = (g1 * _dot(oa_ref[...], wa_ref[...]) + g2 * _dot(ob_ref[...], wb_ref[...])).astype(BF16)


def _merge(h_hi, o_a, o_b, w_gate, b_gate, w_pa, w_pb):
    n, d = h_hi.shape
    width = o_a.shape[1]
    tm = _tile(n, 1024, 16)
    tn = _tile(d, 512, LANES)
    nj = d // tn
    b_gate = b_gate.reshape(1, 2 * d)
    row = lambda i, j: (i, 0)
    return pl.pallas_call(
        _merge_kernel,
        grid=(n // tm, nj),
        in_specs=[pl.BlockSpec((tm, d), row), pl.BlockSpec((tm, width), row), pl.BlockSpec((tm, width), row),
                  pl.BlockSpec((d, tn), lambda i, j: (0, j)),
                  pl.BlockSpec((d, tn), lambda i, j: (0, j + nj)),
                  pl.BlockSpec((1, tn), lambda i, j: (0, j)),
                  pl.BlockSpec((1, tn), lambda i, j: (0, j + nj)),
                  pl.BlockSpec((width, tn), lambda i, j: (0, j)),
                  pl.BlockSpec((width, tn), lambda i, j: (0, j))],
        out_specs=pl.BlockSpec((tm, tn), lambda i, j: (i, j)),
        out_shape=jax.ShapeDtypeStruct((n, d), BF16),
        compiler_params=_params(("arbitrary", "arbitrary")),
        name="merge",
    )(h_hi, o_a, o_b, w_gate, w_gate, b_gate, b_gate, w_pa, w_pb)


def _outproj_kernel(m_ref, w_ref, x_ref, mod_ref, gm_ref, gf_ref, x1_ref, h2_ref):
    mod = mod_ref[0]
    x1 = x_ref[...] + mod[2:3, :] * _rms(_dot(m_ref[...], w_ref[...]), gm_ref[...])
    x1_ref[...] = x1
    h2_ref[...] = (_rms(x1, gf_ref[...]) * (1.0 + mod[4:5, :]) + mod[3:4, :]).astype(BF16)


def _outproj(merged, w_out, x2, mod, g_post_mix, g_pre_ffn, seq):
    n, d = x2.shape
    tm = _tile(seq, 512, 16)
    tps = seq // tm
    row = lambda i: (i, 0)
    fix = lambda i: (0, 0)
    return pl.pallas_call(
        _outproj_kernel,
        grid=(n // tm,),
        in_specs=[pl.BlockSpec((tm, d), row), pl.BlockSpec((d, d), fix, pipeline_mode=pl.Buffered(1)),
                  pl.BlockSpec((tm, d), row),
                  pl.BlockSpec((1, N_MOD, d), lambda i: (i // tps, 0, 0)),
                  pl.BlockSpec((1, d), fix), pl.BlockSpec((1, d), fix)],
        out_specs=[pl.BlockSpec((tm, d), row), pl.BlockSpec((tm, d), row)],
        out_shape=[jax.ShapeDtypeStruct((n, d), F32), jax.ShapeDtypeStruct((n, d), BF16)],
        compiler_params=_params(("arbitrary",)),
        name="outproj",
    )(merged, w_out, x2, mod, g_post_mix.reshape(1, d), g_pre_ffn.reshape(1, d))


def _gelu_tanh(x):
    return 0.5 * x * (1.0 + jnp.tanh(0.7978845608028654 * (x + 0.044715 * (x * x * x))))


def _ffn_kernel(h_ref, halo_ref, x1_ref, mod_ref, st_ref, wg_ref, wu_ref, wd_ref, wc_ref, bc_ref, gp_ref,
                y_ref, nc_ref, g_scr, act_scr, *, tm, nsub, tps):
    i = pl.program_id(0)
    c = pl.program_id(1)
    last = pl.num_programs(1) - 1
    sub = tm // nsub
    cur = c % 2

    def step(down, gate_up):
        if gate_up:
            h = h_ref[...]
            wg = wg_ref[...]
            g = _dot(h, wg)
            u = _dot(h, wu_ref[...])
            wc = wc_ref[...]
            bc = bc_ref[...]
            g_scr[HALO:HALO + tm, :] = g
        if down == "set":
            y_ref[...] = _dot(act_scr[1 - cur], wd_ref[...])
        elif down == "add":
            y_ref[...] += _dot(act_scr[1 - cur], wd_ref[...])
        if not gate_up:
            return

        def conv(lo, rows):
            return (bc + wc[0:1, :] * g_scr[lo - 2:lo - 2 + rows, :]
                    + wc[1:2, :] * g_scr[lo - 1:lo - 1 + rows, :] + wc[2:3, :] * g_scr[lo:lo + rows, :])

        if nsub == 1:
            g_halo = _dot(halo_ref[...], wg)
            g_scr[0:HALO, :] = g_halo
            g_scr[HALO - 2:HALO, :] = jnp.where(i % tps == 0, st_ref[0], g_halo[HALO - 2:HALO, :])
            nc_ref[0] = g_scr[HALO + tm - 2:HALO + tm, :]
            act_scr[cur] = (_gelu_tanh(conv(HALO, tm)) * u).astype(BF16)
        else:
            for s in range(nsub):
                lo = HALO + s * sub
                nc_ref[s] = g_scr[lo + sub - 2:lo + sub, :]
                g_scr[lo - 2:lo, :] = st_ref[s]
                act_scr[cur, s * sub:(s + 1) * sub, :] = (
                    _gelu_tanh(conv(lo, sub)) * u[s * sub:(s + 1) * sub, :]).astype(BF16)

    pl.when(c == 0)(lambda: step(None, True))
    pl.when(c == 1)(lambda: step("set", True))
    pl.when(jnp.logical_and(c > 1, c < last))(lambda: step("add", True))

    @pl.when(c == last)
    def _():
        step("add", False)
        for s in range(nsub):
            rows = slice(s * sub, (s + 1) * sub)
            y_ref[rows, :] = x1_ref[rows, :] + mod_ref[s][5:6, :] * _rms(y_ref[rows, :], gp_ref[...])


def _ffn(h2, x1, mod, state, w_gate, w_up, w_down, w_conv, b_conv, g_post, seq):
    n, d = x1.shape
    dff = w_gate.shape[1]
    nb = n // seq
    tm = _row_tile(n, seq, 1024)
    tf = _tile(dff, 512, LANES)
    n_chunks = dff // tf
    assert n_chunks >= 2, "the lagged down projection needs at least two d_ff chunks"
    up = lambda c: jnp.minimum(c, n_chunks - 1)
    down = lambda c: jnp.maximum(c - 1, 0)
    per_tile_c = lambda i, c: (i, 0, up(c))
    if tm <= seq:
        nsub, tps = 1, seq // tm
        per_seq = lambda i, c: (i // tps, 0, 0)
        per_seq_c = lambda i, c: (i // tps, 0, up(c))
    else:
        nsub, tps = tm // seq, 1
        per_seq = lambda i, c: (i, 0, 0)
        per_seq_c = per_tile_c
    halo_blocks = tm // HALO
    row = lambda i, c: (i, 0)
    y, new_conv = pl.pallas_call(
        functools.partial(_ffn_kernel, tm=tm, nsub=nsub, tps=tps),
        grid=(n // tm, n_chunks + 1),
        in_specs=[pl.BlockSpec((tm, d), row, pipeline_mode=pl.Buffered(1)),
                  pl.BlockSpec((HALO, d), lambda i, c: (jnp.maximum(i * halo_blocks - 1, 0), 0)),
                  pl.BlockSpec((tm, d), row, pipeline_mode=pl.Buffered(1)),
                  pl.BlockSpec((nsub, N_MOD, d), per_seq),
                  pl.BlockSpec((nsub, CONV_W - 1, tf), per_seq_c),
                  pl.BlockSpec((d, tf), lambda i, c: (0, up(c))),
                  pl.BlockSpec((d, tf), lambda i, c: (0, up(c))),
                  pl.BlockSpec((tf, d), lambda i, c: (down(c), 0)),
                  pl.BlockSpec((CONV_W, tf), lambda i, c: (0, up(c))),
                  pl.BlockSpec((1, tf), lambda i, c: (0, up(c))),
                  pl.BlockSpec((1, d), lambda i, c: (0, 0))],
        out_specs=[pl.BlockSpec((tm, d), row, pipeline_mode=pl.Buffered(1)),
                   pl.BlockSpec((nsub, CONV_W - 1, tf), per_tile_c)],
        out_shape=[jax.ShapeDtypeStruct((n, d), F32),
                   jax.ShapeDtypeStruct((nb * tps, CONV_W - 1, dff), F32)],
        scratch_shapes=[pltpu.VMEM((HALO + tm, tf), F32), pltpu.VMEM((2, tm, tf), BF16)],
        compiler_params=_params(("arbitrary", "arbitrary")),
        name="ffn",
    )(h2, h2, x1, mod, state, w_gate, w_up, w_down, w_conv, b_conv.reshape(1, dff), g_post.reshape(1, d))
    return y, new_conv.reshape(nb, tps, CONV_W - 1, dff)[:, tps - 1]


def _rope_tables(pos):
    def angles(dim):
        half = dim // 2
        inv = ROPE_THETA ** (-jnp.arange(half, dtype=F32) * 2.0 / dim)
        ang = pos.astype(F32)[:, None] * inv[None, :]
        return jnp.cos(ang), jnp.sin(ang)

    cos, sin = angles(HEAD_DIM)
    cos_h = jnp.concatenate([cos, cos], axis=1)
    sin_h = jnp.concatenate([-sin, sin], axis=1)

    cos, sin = angles(D_IDX)
    zero = jnp.zeros_like(sin)
    n_rot = H_IDX + 1
    tail = IDX_COLS - n_rot * D_IDX
    t = pos.shape[0]
    c_i = jnp.concatenate([jnp.tile(jnp.concatenate([cos, cos], 1), (1, n_rot)), jnp.ones((t, tail), F32)], 1)
    s1_i = jnp.concatenate([jnp.tile(jnp.concatenate([-sin, zero], 1), (1, n_rot)), jnp.zeros((t, tail), F32)], 1)
    s2_i = jnp.concatenate([jnp.tile(jnp.concatenate([zero, sin], 1), (1, n_rot)), jnp.zeros((t, tail), F32)], 1)
    return cos_h, sin_h, c_i, s1_i, s2_i


def _with_past(past, new, nb, seq, n_keys):
    w = new.shape[1]
    if past is None and n_keys == seq:
        return new
    parts = [] if past is None else [past.astype(new.dtype).reshape(nb, -1, w)]
    parts.append(new.reshape(nb, seq, w))
    have = sum(p.shape[1] for p in parts)
    if n_keys > have:
        parts.append(jnp.zeros((nb, n_keys - have, w), new.dtype))
    return jnp.concatenate(parts, axis=1).reshape(nb * n_keys, w)


def _layer(x, mod, past, conv_state, p):
    nb, seq, d = x.shape
    n = nb * seq
    nh = p["w_pa"].shape[0] // HEAD_DIM
    width = nh * HEAD_DIM
    n_past = 0 if past is None else past["a_k"].shape[1]
    n_valid = n_past + seq
    n_keys = -(-n_valid // 256) * 256
    x2 = x.reshape(n, d)

    h_hi, h_lo = _norm(x2, p["g_pre_mix"], mod, seq)
    cos_h, sin_h, c_i, s1_i, s2_i = _rope_tables(n_past + jnp.arange(seq, dtype=I32))
    p16, kv32, vt_new = _inproj(h_hi, p["w_main"], cos_h, sin_h, seq, width)
    col = lambda a, g: a[:, g * width:(g + 1) * width]
    qi_hi, qi_lo, kw, ki_hi, ki_lo = _idxproj(h_hi, h_lo, p["w_idx_hi"], p["w_idx_lo"], c_i, s1_i, s2_i, seq)

    def keys(g, past_rows):
        if past_rows is None and n_keys == seq:
            return p16, g * width
        return _with_past(past_rows, col(p16, g), nb, seq, n_keys), 0

    q_hi = qi_hi.reshape(n, H_IDX, D_IDX)
    q_lo = qi_lo.reshape(n, H_IDX, D_IDX)
    qi_cat = jnp.concatenate([q_hi, q_hi, q_lo, jnp.zeros_like(q_hi)], axis=2).reshape(n, H_IDX * IDX_K)
    k_hi, k_lo = ki_hi[:, :D_IDX], ki_lo[:, :D_IDX]
    if past is None:
        pa_k = pa_v = pb_k = pb_v = pi_hi = pi_lo = None
    else:
        pa_k, pa_v, pb_k, pb_v = (past[k].reshape(nb, n_past, width) for k in ("a_k", "a_v", "b_k", "b_v"))
        pi_hi, pi_lo = _split(past["b_kidx"].astype(F32))
    k_hi = _with_past(pi_hi, k_hi, nb, seq, n_keys)
    k_lo = _with_past(pi_lo, k_lo, nb, seq, n_keys)
    ki_cat = jnp.concatenate([k_hi, k_lo, k_hi, jnp.zeros_like(k_hi)], axis=1)

    o_a = _sb((p16, 0), keys(1, pa_k), keys(2, pa_v), nb, seq, n_keys, n_past, nh)
    seq_q = max(seq, LANES)
    pad_q = lambda a: a if seq_q == seq else jnp.pad(
        a.reshape(nb, seq, -1), ((0, 0), (0, seq_q - seq), (0, 0))).reshape(nb * seq_q, -1)
    if vt_new is not None and past is None and n_keys == seq:
        vt_all = vt_new
    else:
        vt_all = _with_past(pb_v, col(p16, 5), nb, seq, n_keys).reshape(nb, n_keys, width)
        vt_all = jnp.swapaxes(vt_all, 1, 2).reshape(nb * width, n_keys)
    q_b = (p16, 3 * width) if seq_q == seq else (pad_q(col(p16, 3)), 0)
    o_b = _dsat(q_b, keys(4, pb_k), vt_all, pad_q(qi_cat), ki_cat,
                pad_q(kw).T, nb, seq_q, n_keys, n_valid, n_past, nh)
    if seq_q != seq:
        o_b = o_b.reshape(nb, seq_q, width)[:, :seq].reshape(n, width)

    merged = _merge(h_hi, o_a, o_b, p["w_gate"], p["b_gate"], p["w_pa"], p["w_pb"])
    x1, h2 = _outproj(merged, p["w_out"], x2, mod, p["g_post_mix"], p["g_pre_ffn"], seq)
    y, new_conv = _ffn(h2, x1, mod, conv_state, p["w_ffn_gate"], p["w_ffn_up"], p["w_ffn_down"],
                       p["w_conv"], p["b_conv"], p["g_post_ffn"], seq)

    heads = lambda g: kv32[g].reshape(nb, seq, nh, HEAD_DIM)
    new = (heads(1), heads(2), heads(4), heads(5), kw[:, :D_IDX].reshape(nb, seq, D_IDX), new_conv)
    return y.reshape(nb, seq, d), new


def kernel(x_prompt, x_sample, c_prompt, c_sample, cache_a_k, cache_a_v, cache_b_k, cache_b_v, cache_b_kidx,
           state_ffn_conv, w_mod, b_mod, g_pre_mix, w_in, w_merge_gate, b_merge_gate, w_proj_a, w_proj_b, w_out,
           g_post_mix, g_pre_ffn, w_ffn_gate, w_ffn_up, w_conv, b_conv, w_ffn_down, g_post_ffn):
    depth = w_mod.shape[0]
    nbp, nbs = x_prompt.shape[0], x_sample.shape[0]
    d = x_prompt.shape[2]
    width = w_proj_a.shape[1]
    dff = w_ffn_gate.shape[2]
    assert w_in.shape[2] == 6 * width + H_IDX * D_IDX + D_IDX + H_IDX
    assert cache_b_kidx.shape[-1] == D_IDX and cache_a_k.shape[-1] == HEAD_DIM

    y_p, y_s = x_prompt, x_sample
    st_p = [[] for _ in range(6)]
    st_s = [[] for _ in range(6)]
    for l in range(depth):
        w_idx = jnp.pad(w_in[l][:, 6 * width:], ((0, 0), (0, IDX_COLS - (w_in.shape[2] - 6 * width))))
        w_idx_hi, w_idx_lo = _split(w_idx)
        p = dict(
            g_pre_mix=g_pre_mix[l], w_main=w_in[l][:, :6 * width].astype(BF16), w_idx_hi=w_idx_hi, w_idx_lo=w_idx_lo,
            w_gate=w_merge_gate[l].astype(BF16), b_gate=b_merge_gate[l],
            w_pa=w_proj_a[l].astype(BF16), w_pb=w_proj_b[l].astype(BF16), w_out=w_out[l].astype(BF16),
            g_post_mix=g_post_mix[l], g_pre_ffn=g_pre_ffn[l],
            w_ffn_gate=w_ffn_gate[l].astype(BF16), w_ffn_up=w_ffn_up[l].astype(BF16),
            w_ffn_down=w_ffn_down[l].astype(BF16), w_conv=w_conv[l], b_conv=b_conv[l], g_post_ffn=g_post_ffn[l])

        c_all = jnp.concatenate([c_prompt, c_sample], axis=0)
        pad = -c_all.shape[0] % 16
        mod = _mod(jnp.pad(c_all, ((0, pad), (0, 0))), w_mod[l], b_mod[l]).reshape(-1, N_MOD, d)
        past = dict(a_k=cache_a_k[l], a_v=cache_a_v[l], b_k=cache_b_k[l], b_v=cache_b_v[l], b_kidx=cache_b_kidx[l])
        y_p, new_p = _layer(y_p, mod[:nbp], None, jnp.zeros((nbp, CONV_W - 1, dff), F32), p)
        y_s, new_s = _layer(y_s, mod[nbp:nbp + nbs], past, state_ffn_conv[l], p)
        for i in range(6):
            st_p[i].append(new_p[i])
            st_s[i].append(new_s[i])
    sp = [jnp.stack(s, axis=0) for s in st_p]
    ss = [jnp.stack(s, axis=0) for s in st_s]
    return (y_p, y_s, sp[0], sp[1], sp[2], sp[3], sp[4], sp[5], ss[0], ss[1], ss[2], ss[3], ss[4], ss[5])
```

```python
import functools

import jax
import jax.numpy as jnp
from jax import lax
from jax.experimental import pallas as pl
from jax.experimental.pallas import tpu as pltpu

F32 = jnp.float32
BF16 = jnp.bfloat16
I32 = jnp.int32

CHUNK = 64
HEAD_DIM = 128
H_IDX = 4
D_IDX = 64
TOPK_MAX = 256
CONV_W = 3
ROPE_THETA = 10000.0
RMS_EPS = 1e-6
N_MOD = 6
IDX_COLS = 384
IDX_K = 256
LANES = 128
HALO = 16
ONES_ROWS = 16
LOG2E = 1.4426950408889634
UNDERFLOW_LOG2 = 160.0
NEG = -1e30
INT_MIN = -2 ** 31
VMEM_LIMIT = 56 * 1024 * 1024


def _tile(n, pref, mult):
    t = min(pref, n)
    t -= t % mult
    while t >= mult:
        if n % t == 0:
            return t
        t -= mult
    return n


def _row_tile(n, seq, pref):
    if seq >= pref or n == seq:
        return _tile(seq, pref, 16)
    return _tile(n, pref, seq)


def _params(sem):
    return pltpu.CompilerParams(dimension_semantics=sem, vmem_limit_bytes=VMEM_LIMIT)


def _split(x):
    hi = x.astype(BF16)
    lo = (x - hi.astype(F32)).astype(BF16)
    return hi, lo


def _dot(a, b):
    return jnp.dot(a, b, preferred_element_type=F32)


def _dot_t(a, b):
    return lax.dot_general(a, b, (((1,), (1,)), ((), ())), preferred_element_type=F32)


def _rms(x, g):
    ms = jnp.mean(x * x, axis=-1, keepdims=True)
    return x * lax.rsqrt(ms + RMS_EPS) * g


def _mod_kernel(c_ref, w_ref, b_ref, o_ref):
    c = c_ref[...]
    s = c / (1.0 + jnp.exp(-c))
    s_hi, s_lo = _split(s)
    w_hi, w_lo = _split(w_ref[...])
    o_ref[...] = _dot(s_hi, w_hi) + _dot(s_hi, w_lo) + _dot(s_lo, w_hi) + b_ref[...]


def _mod(c, w, b):
    bc, d = c.shape
    n = w.shape[1]
    tn = _tile(n, 1024, LANES)
    return pl.pallas_call(
        _mod_kernel,
        grid=(n // tn,),
        in_specs=[pl.BlockSpec((bc, d), lambda j: (0, 0)),
                  pl.BlockSpec((d, tn), lambda j: (0, j)),
                  pl.BlockSpec((1, tn), lambda j: (0, j))],
        out_specs=pl.BlockSpec((bc, tn), lambda j: (0, j)),
        out_shape=jax.ShapeDtypeStruct((bc, n), F32),
        compiler_params=_params(("arbitrary",)),
        name="mod",
    )(c, w, b.reshape(1, n))


def _norm_kernel(x_ref, g_ref, mod_ref, hi_ref, lo_ref):
    m = mod_ref[0]
    h = _rms(x_ref[...], g_ref[...]) * (1.0 + m[1:2, :]) + m[0:1, :]
    hi, lo = _split(h)
    hi_ref[...] = hi
    lo_ref[...] = lo


def _norm(x2, g, mod, seq):
    n, d = x2.shape
    tr = _tile(seq, 512, 16)
    tps = seq // tr
    return pl.pallas_call(
        _norm_kernel,
        grid=(n // tr,),
        in_specs=[pl.BlockSpec((tr, d), lambda i: (i, 0)),
                  pl.BlockSpec((1, d), lambda i: (0, 0)),
                  pl.BlockSpec((1, N_MOD, d), lambda i: (i // tps, 0, 0))],
        out_specs=[pl.BlockSpec((tr, d), lambda i: (i, 0))] * 2,
        out_shape=[jax.ShapeDtypeStruct((n, d), BF16)] * 2,
        compiler_params=_params(("arbitrary",)),
        name="norm",
    )(x2, g.reshape(1, d), mod)


N_PROJ = 6


KV_GROUPS = (1, 2, 4, 5)


def _inproj_kernel(a_ref, w_ref, c_ref, s_ref, p16_ref, ka_ref, va_ref, kb_ref, vb_ref, *rest, tm, nh):
    r_scr = rest[-1]
    acc = _dot(a_ref[...], w_ref[...])
    c = c_ref[0]
    s = s_ref[0]
    for k in range(nh):
        sl = slice(k * HEAD_DIM, (k + 1) * HEAD_DIM)
        xs = acc[:, sl]
        r = xs * c + pltpu.roll(xs, HEAD_DIM // 2, 1) * s
        r_scr[:, sl] = r
        p16_ref[:, sl] = r.astype(BF16)
    for grp, ref in zip(KV_GROUPS, (ka_ref, va_ref, kb_ref, vb_ref)):
        @pl.when(pl.program_id(1) == grp)
        def _(ref=ref):
            for k in range(nh):
                ref[pl.ds(k, tm, stride=nh), :] = r_scr[:, k * HEAD_DIM:(k + 1) * HEAD_DIM]

    if len(rest) == 2:
        @pl.when(pl.program_id(1) == N_PROJ - 1)
        def _():
            rest[0][...] = r_scr[...].T.astype(BF16)


def _inproj(h_hi, w_main, cos_t, sin_t, seq, width):
    n, d = h_hi.shape
    nh = width // HEAD_DIM
    tm = _row_tile(n, seq, 1024)
    with_vt = tm <= seq
    tn = width
    npj = 1
    scale = LOG2E * HEAD_DIM ** -0.5
    one, zero = jnp.ones_like(cos_t), jnp.zeros_like(cos_t)
    c_t = jnp.stack([scale * one, one, one, scale * cos_t, cos_t, one])
    s_t = jnp.stack([zero, zero, zero, scale * sin_t, sin_t, zero])
    if tm <= seq:
        tps = seq // tm
        tab_map = lambda i, j: (j // npj, i % tps, 0)
    else:
        c_t = jnp.tile(c_t, (1, tm // seq, 1))
        s_t = jnp.tile(s_t, (1, tm // seq, 1))
        tab_map = lambda i, j: (j // npj, 0, 0)
    out = pl.pallas_call(
        functools.partial(_inproj_kernel, tm=tm, nh=nh),
        grid=(n // tm, N_PROJ * npj),
        in_specs=[pl.BlockSpec((tm, d), lambda i, j: (i, 0)),
                  pl.BlockSpec((d, tn), lambda i, j: (0, j)),
                  pl.BlockSpec((1, tm, HEAD_DIM), tab_map),
                  pl.BlockSpec((1, tm, HEAD_DIM), tab_map)],
        out_specs=[pl.BlockSpec((tm, tn), lambda i, j: (i, j))]
        + [pl.BlockSpec((tm * nh, HEAD_DIM), lambda i, j: (i, 0), pipeline_mode=pl.Buffered(1))] * len(KV_GROUPS)
        + ([pl.BlockSpec((width, tm), lambda i, j: (i // tps, i % tps))] if with_vt else []),
        out_shape=[jax.ShapeDtypeStruct((n, N_PROJ * width), BF16)]
        + [jax.ShapeDtypeStruct((n * nh, HEAD_DIM), F32)] * len(KV_GROUPS)
        + ([jax.ShapeDtypeStruct((n // seq * width, seq), BF16)] if with_vt else []),
        scratch_shapes=[pltpu.VMEM((tm, tn), F32)],
        compiler_params=_params(("arbitrary", "arbitrary")),
        name="inproj",
    )(h_hi, w_main, c_t, s_t)
    return out[0], dict(zip(KV_GROUPS, out[1:1 + len(KV_GROUPS)])), (out[-1] if with_vt else None)


def _idxproj_kernel(hi_ref, lo_ref, whi_ref, wlo_ref, c_ref, s1_ref, s2_ref,
                    qhi_ref, qlo_ref, kw_ref, khi_ref, klo_ref):
    a_hi = hi_ref[...]
    w_hi = whi_ref[...]
    acc = _dot(a_hi, w_hi) + _dot(a_hi, wlo_ref[...]) + _dot(lo_ref[...], w_hi)
    for s in range(IDX_COLS // LANES):
        sl = slice(s * LANES, (s + 1) * LANES)
        xs = acc[:, sl]
        r = (xs * c_ref[:, sl] + pltpu.roll(xs, LANES - D_IDX // 2, 1) * s1_ref[:, sl]
             + pltpu.roll(xs, D_IDX // 2, 1) * s2_ref[:, sl])
        hi, lo = _split(r)
        if s < 2:
            qhi_ref[:, sl] = hi
            qlo_ref[:, sl] = lo
        else:
            kw_ref[...] = r
            khi_ref[...] = hi
            klo_ref[...] = lo


def _idxproj(h_hi, h_lo, w_hi, w_lo, c_t, s1_t, s2_t, seq):
    n, d = h_hi.shape
    tm = _row_tile(n, seq, 512)
    if tm <= seq:
        tps = seq // tm
        tab_map = lambda i: (i % tps, 0)
    else:
        c_t, s1_t, s2_t = (jnp.tile(t, (tm // seq, 1)) for t in (c_t, s1_t, s2_t))
        tab_map = lambda i: (0, 0)
    row = lambda i: (i, 0)
    return pl.pallas_call(
        _idxproj_kernel,
        grid=(n // tm,),
        in_specs=[pl.BlockSpec((tm, d), row), pl.BlockSpec((tm, d), row),
                  pl.BlockSpec((d, IDX_COLS), lambda i: (0, 0)),
                  pl.BlockSpec((d, IDX_COLS), lambda i: (0, 0)),
                  pl.BlockSpec((tm, IDX_COLS), tab_map),
                  pl.BlockSpec((tm, IDX_COLS), tab_map),
                  pl.BlockSpec((tm, IDX_COLS), tab_map)],
        out_specs=[pl.BlockSpec((tm, 2 * LANES), row), pl.BlockSpec((tm, 2 * LANES), row),
                   pl.BlockSpec((tm, LANES), row), pl.BlockSpec((tm, LANES), row),
                   pl.BlockSpec((tm, LANES), row)],
        out_shape=[jax.ShapeDtypeStruct((n, 2 * LANES), BF16), jax.ShapeDtypeStruct((n, 2 * LANES), BF16),
                   jax.ShapeDtypeStruct((n, LANES), F32), jax.ShapeDtypeStruct((n, LANES), BF16),
                   jax.ShapeDtypeStruct((n, LANES), BF16)],
        compiler_params=_params(("arbitrary",)),
        name="idxproj",
    )(h_hi, h_lo, w_hi, w_lo, c_t, s1_t, s2_t)


def _sb_kernel(q_ref, k_ref, v_ref, o_ref, acc_scr, z_scr, i_scr, *, tq, tk, past, group):
    q0 = past + pl.program_id(2) * tq
    n_kt = (q0 + tq - 2) // tk + 1
    n_clear = q0 // tk
    row_pos = q0 + lax.broadcasted_iota(I32, (tq, 1), 0)
    col = lax.broadcasted_iota(I32, (1, tk), 1)
    suffix = jnp.where(lax.broadcasted_iota(I32, (tk, tk), 0) >= lax.broadcasted_iota(I32, (tk, tk), 1),
                       1.0, 0.0).astype(BF16)
    suffix2 = jnp.concatenate([suffix, suffix], axis=0)
    acc_scr[...] = jnp.zeros_like(acc_scr)

    heads = [slice(h * HEAD_DIM, (h + 1) * HEAD_DIM) for h in range(group)]

    def tiles(js, carries, masked):
        offs = [pl.multiple_of(j * tk, tk) for j in js]
        if masked:
            valids = [(off + col) < row_pos for off in offs]
        for t, off in enumerate(offs):
            for h, hs in enumerate(heads):
                z_scr[t * group + h] = _dot_t(q_ref[:, hs], k_ref[pl.ds(off, tk), hs])
        for t in range(len(js)):
            for h in range(group):
                z = z_scr[t * group + h]
                sp = jnp.maximum(z, 0.0) + jnp.log2(1.0 + jnp.exp2(-jnp.abs(z)))
                if masked:
                    sp = jnp.where(valids[t], sp, 0.0)
                hi, lo = _split(sp)
                i_scr[t * group + h] = _dot(jnp.concatenate([hi, lo], axis=1), suffix2)
        carries = list(carries)
        for t, off in enumerate(offs):
            for h, hs in enumerate(heads):
                w = jnp.exp2(z_scr[t * group + h] - i_scr[t * group + h] - carries[h])
                if masked:
                    w = jnp.where(valids[t], w, 0.0)
                acc_scr[h] += _dot(w.astype(BF16), v_ref[pl.ds(off, tk), hs])
                carries[h] = carries[h] + i_scr[t * group + h, :, 0:1]
        return tuple(carries)

    def least(carries):
        m = carries[0]
        for c in carries[1:]:
            m = jnp.minimum(m, c)
        return jnp.min(m)

    st = tuple(jnp.zeros((tq, 1), F32) for _ in range(group))
    st = lax.fori_loop(0, n_kt - n_clear, lambda t, c: tiles([n_kt - 1 - t], c, True), st)

    def more(state):
        t, low, _ = state
        return jnp.logical_and(t < n_clear, low < UNDERFLOW_LOG2)

    def step(state):
        t, _, carries = state
        carries = tiles([n_clear - 1 - t], carries, False)
        return t + 1, least(carries), carries

    lax.while_loop(more, step, (jnp.int32(0), least(st), st))
    for h in range(group):
        o_ref[:, h * HEAD_DIM:(h + 1) * HEAD_DIM] = acc_scr[h].astype(o_ref.dtype)


def _sb(q, k_all, v_all, nb, seq, n_keys, past, nh):
    tq = _tile(seq, 256, 16)
    tk = 256
    nq = seq // tq
    group = 4 if nh % 4 == 0 else (2 if nh % 2 == 0 else 1)
    gw = group * HEAD_DIM
    (q, q0), (k_all, k0), (v_all, v0) = ((a, c // gw) for a, c in (q, k_all, v_all))
    return pl.pallas_call(
        functools.partial(_sb_kernel, tq=tq, tk=tk, past=past, group=group),
        grid=(nb, nh // group, nq),
        in_specs=[pl.BlockSpec((tq, gw), lambda b, h, i: (b * nq + i, q0 + h)),
                  pl.BlockSpec((n_keys, gw), lambda b, h, i: (b, k0 + h)),
                  pl.BlockSpec((n_keys, gw), lambda b, h, i: (b, v0 + h))],
        out_specs=pl.BlockSpec((tq, gw), lambda b, h, i: (b * nq + i, h)),
        out_shape=jax.ShapeDtypeStruct((q.shape[0], nh * HEAD_DIM), BF16),
        scratch_shapes=[pltpu.VMEM((group, tq, HEAD_DIM), F32), pltpu.VMEM((group, tq, tk), F32),
                        pltpu.VMEM((group, tq, tk), F32)],
        compiler_params=_params(("arbitrary", "arbitrary", "arbitrary")),
        name="sb",
    )(q, k_all, v_all)


def _fold8(parts, op):
    while len(parts) > 1:
        parts = [op(parts[i], parts[i + 1]) if i + 1 < len(parts) else parts[i] for i in range(0, len(parts), 2)]
    return parts[0]


def _dsat_kernel(q_ref, k_ref, vt_ref, qi_ref, ki_ref, kwt_ref, o_ref, key_scr, acc_scr, s_scr, p_scr,
                 *, tq, tk, past, n_valid, n_sel, nh):
    q0 = past + pl.program_id(1) * tq
    col_pos = q0 + lax.broadcasted_iota(I32, (1, tq), 1)
    col_lim = jnp.minimum((col_pos // CHUNK + 1) * CHUNK, n_valid)
    n_kt = (jnp.minimum(((q0 + tq - 1) // CHUNK + 1) * CHUNK, n_valid) + tk - 1) // tk
    row = lax.broadcasted_iota(I32, (tk, 1), 0)
    ksel = float(n_sel)
    groups = [slice(8 * r, 8 * r + 8) for r in range(tk // 8)]

    wrow = kwt_ref[D_IDX:D_IDX + H_IDX, :] * ((H_IDX ** -0.5) * (D_IDX ** -0.5))

    def score_pair(t, _):
        offs = [pl.multiple_of(jnp.minimum(2 * t + u, n_kt - 1) * tk, tk) for u in range(2)]
        raw = [[_dot_t(ki_ref[pl.ds(off, tk), :], qi_ref[:, h * IDX_K:(h + 1) * IDX_K]) for h in range(H_IDX)]
               for off in offs]
        for off, dots in zip(offs, raw):
            score = jnp.zeros((tk, tq), F32)
            for h, s in enumerate(dots):
                score = score + wrow[h:h + 1, :] * jnp.maximum(s, 0.0)
            bits = pltpu.bitcast(score, I32)
            key = jnp.where(bits < 0, INT_MIN - bits, bits)
            key_scr[pl.ds(off, tk), :] = jnp.where((off + row) < col_lim, key, INT_MIN)
        return 0

    lax.fori_loop(0, (n_kt + 1) // 2, score_pair, 0)

    def count_ge(thr8):
        def body(t, cnt):
            for u in range(2):
                j = 2 * t + u
                off = pl.multiple_of(jnp.minimum(j, n_kt - 1) * tk, tk)
                key = key_scr[pl.ds(off, tk), :]
                part = _fold8([jnp.where(key[g, :] >= thr8, 1.0, 0.0) for g in groups], jnp.add)
                cnt = cnt + (part if u == 0 else jnp.where(j < n_kt, part, 0.0))
            return cnt

        cnt = lax.fori_loop(0, (n_kt + 1) // 2, body, jnp.zeros((8, tq), F32))
        return jnp.broadcast_to(jnp.sum(cnt, axis=0, keepdims=True), (8, tq))

    def bit_step(it, st):
        cand, have = st
        trial = cand | lax.shift_left(jnp.int32(1), 30 - it)
        cnt = count_ge(trial)
        take = cnt >= ksel
        return jnp.where(take, trial, cand), jnp.where(take, cnt, have)

    zero8 = jnp.zeros((8, tq), I32)
    cnt0 = count_ge(zero8)
    start = (jnp.where(cnt0 >= ksel, zero8, INT_MIN), jnp.where(cnt0 >= ksel, cnt0, float(2 ** 30)))
    thr8, have8 = lax.fori_loop(0, 31, bit_step, start)
    thr = thr8[0:1, :]
    overfull = jnp.max(jnp.where(have8 > ksel, 1.0, 0.0)) > 0.0

    @pl.when(jnp.logical_not(overfull))
    def _():
        def bias_tile(j, _):
            off = pl.multiple_of(j * tk, tk)
            bias = jnp.where(key_scr[pl.ds(off, tk), :] >= thr, 0.0, NEG)
            key_scr[pl.ds(off, tk), :] = pltpu.bitcast(bias, I32)
            return 0

        lax.fori_loop(0, n_kt, bias_tile, 0)

    @pl.when(overfull)
    def _():
        no_thr8 = thr8 == INT_MIN
        cnt_gt = count_ge(jnp.where(no_thr8, thr8, thr8 + 1))
        need = jnp.where(no_thr8, 0.0, ksel - cnt_gt)[0:1, :]
        prefix = jnp.where(lax.broadcasted_iota(I32, (tk, tk), 0) >= lax.broadcasted_iota(I32, (tk, tk), 1),
                           1.0, 0.0).astype(BF16)

        def rank_tiles(js, carry):
            offs = [pl.multiple_of(j * tk, tk) for j in js]
            keys = [key_scr[pl.ds(off, tk), :] for off in offs]
            eqfs = [jnp.where(key == thr, 1.0, 0.0) for key in keys]
            ties = [jnp.sum(_fold8([e[g, :] for g in groups], jnp.add), axis=0, keepdims=True) for e in eqfs]
            for off, key, eqf, tie in zip(offs, keys, eqfs, ties):
                rank_incl = _dot(prefix, eqf.astype(BF16)) + carry
                tie_ok = (rank_incl - eqf) < need
                bias = jnp.where(key > thr, 0.0, jnp.where(key == thr, jnp.where(tie_ok, 0.0, NEG), NEG))
                key_scr[pl.ds(off, tk), :] = pltpu.bitcast(bias, I32)
                carry = carry + tie
            return carry

        ties_before = lax.fori_loop(0, n_kt // 2, lambda t, c: rank_tiles([2 * t, 2 * t + 1], c),
                                    jnp.zeros((1, tq), F32))
        lax.fori_loop(0, n_kt % 2, lambda t, c: rank_tiles([n_kt - 1], c), ties_before)

    acc_scr[...] = jnp.zeros_like(acc_scr)
    ones_v = jnp.ones((ONES_ROWS, tk), BF16)
    heads = [slice(h * HEAD_DIM, (h + 1) * HEAD_DIM) for h in range(nh)]

    def logits(j, slot, ms):
        off = pl.multiple_of(j * tk, tk)
        bias = pltpu.bitcast(key_scr[pl.ds(off, tk), :], F32)
        for h, hs in enumerate(heads):
            s_scr[slot, h] = _dot_t(k_ref[pl.ds(off, tk), hs], q_ref[:, hs]) + bias
        out = []
        for h, m in enumerate(ms):
            s = s_scr[slot, h]
            top = jnp.max(_fold8([s[g, :] for g in groups], jnp.maximum), axis=0, keepdims=True)
            out.append(jnp.maximum(m, top))
        return tuple(out)

    def accumulate(j, slot, m_old, m_new):
        off = pl.multiple_of(j * tk, tk)
        for h in range(nh):
            p_scr[h] = jnp.exp2(s_scr[slot, h] - m_new[h]).astype(BF16)
        for h, hs in enumerate(heads):
            vt_ext = jnp.concatenate([vt_ref[hs, pl.ds(off, tk)], ones_v], axis=0)
            acc_scr[h] = jnp.exp2(m_old[h] - m_new[h]) * acc_scr[h] + _dot(vt_ext, p_scr[h])

    m_none = tuple(jnp.full((1, tq), NEG, F32) for _ in range(nh))
    m_first = logits(0, 0, m_none)

    def attn_step(j, st):
        m_old, m_cur = st
        slot = j % 2
        accumulate(j - 1, 1 - slot, m_old, m_cur)
        return m_cur, logits(j, slot, m_cur)

    m_old, m_cur = lax.fori_loop(1, n_kt, attn_step, (m_none, m_first))
    accumulate(n_kt - 1, (n_kt - 1) % 2, m_old, m_cur)
    for h, hs in enumerate(heads):
        out_t = acc_scr[h, :HEAD_DIM, :] / acc_scr[h, HEAD_DIM:HEAD_DIM + 1, :]
        o_ref[:, hs] = out_t.T.astype(o_ref.dtype)


def _dsat(q, k_all, vt_all, qi_cat, ki_cat, kwt, nb, seq, n_keys, n_valid, past, nh):
    tq = _tile(seq, 256, LANES)
    tk = 256
    nq = seq // tq
    width = nh * HEAD_DIM
    n_sel = min(TOPK_MAX, n_valid // 4)
    once = pl.Buffered(1)
    (q, q0), (k_all, k0) = ((a, c // width) for a, c in (q, k_all))
    return pl.pallas_call(
        functools.partial(_dsat_kernel, tq=tq, tk=tk, past=past, n_valid=n_valid, n_sel=n_sel, nh=nh),
        grid=(nb, nq),
        in_specs=[pl.BlockSpec((tq, width), lambda b, i: (b * nq + i, q0)),
                  pl.BlockSpec((n_keys, width), lambda b, i: (b, k0), pipeline_mode=once),
                  pl.BlockSpec((width, n_keys), lambda b, i: (b, 0), pipeline_mode=once),
                  pl.BlockSpec((tq, H_IDX * IDX_K), lambda b, i: (b * nq + i, 0)),
                  pl.BlockSpec((n_keys, IDX_K), lambda b, i: (b, 0), pipeline_mode=once),
                  pl.BlockSpec((LANES, tq), lambda b, i: (0, b * nq + i))],
        out_specs=pl.BlockSpec((tq, width), lambda b, i: (b * nq + i, 0)),
        out_shape=jax.ShapeDtypeStruct((q.shape[0], width), BF16),
        scratch_shapes=[pltpu.VMEM((n_keys, tq), I32), pltpu.VMEM((nh, HEAD_DIM + ONES_ROWS, tq), F32),
                        pltpu.VMEM((2, nh, tk, tq), F32), pltpu.VMEM((nh, tk, tq), BF16)],
        compiler_params=_params(("arbitrary", "arbitrary")),
        name="dsa",
    )(q, k_all, vt_all, qi_cat, ki_cat, kwt)


def _merge_kernel(h_ref, oa_ref, ob_ref, wg1_ref, wg2_ref, b1_ref, b2_ref, wa_ref, wb_ref, o_ref):
    h = h_ref[...]
    g1 = 1.0 / (1.0 + jnp.exp(-(_dot(h, wg1_ref[...]) + b1_ref[...])))
    g2 = 1.0 / (1.0 + jnp.exp(-(_dot(h, wg2_ref[...]) + b2_ref[...])))
    o_ref[...] = (g1 * _dot(oa_ref[...], wa_ref[...]) + g2 * _dot(ob_ref[...], wb_ref[...])).astype(BF16)


def _merge(h_hi, o_a, o_b, w_gate, b_gate, w_pa, w_pb):
    n, d = h_hi.shape
    width = o_a.shape[1]
    tm = _tile(n, 1024, 16)
    tn = _tile(d, 512, LANES)
    nj = d // tn
    b_gate = b_gate.reshape(1, 2 * d)
    row = lambda i, j: (i, 0)
    return pl.pallas_call(
        _merge_kernel,
        grid=(n // tm, nj),
        in_specs=[pl.BlockSpec((tm, d), row), pl.BlockSpec((tm, width), row), pl.BlockSpec((tm, width), row),
                  pl.BlockSpec((d, tn), lambda i, j: (0, j)),
                  pl.BlockSpec((d, tn), lambda i, j: (0, j + nj)),
                  pl.BlockSpec((1, tn), lambda i, j: (0, j)),
                  pl.BlockSpec((1, tn), lambda i, j: (0, j + nj)),
                  pl.BlockSpec((width, tn), lambda i, j: (0, j)),
                  pl.BlockSpec((width, tn), lambda i, j: (0, j))],
        out_specs=pl.BlockSpec((tm, tn), lambda i, j: (i, j)),
        out_shape=jax.ShapeDtypeStruct((n, d), BF16),
        compiler_params=_params(("arbitrary", "arbitrary")),
        name="merge",
    )(h_hi, o_a, o_b, w_gate, w_gate, b_gate, b_gate, w_pa, w_pb)


def _outproj_kernel(m_ref, w_ref, x_ref, mod_ref, gm_ref, gf_ref, x1_ref, h2_ref):
    mod = mod_ref[0]
    x1 = x_ref[...] + mod[2:3, :] * _rms(_dot(m_ref[...], w_ref[...]), gm_ref[...])
    x1_ref[...] = x1
    h2_ref[...] = (_rms(x1, gf_ref[...]) * (1.0 + mod[4:5, :]) + mod[3:4, :]).astype(BF16)


def _outproj(merged, w_out, x2, mod, g_post_mix, g_pre_ffn, seq):
    n, d = x2.shape
    tm = _tile(seq, 512, 16)
    tps = seq // tm
    row = lambda i: (i, 0)
    fix = lambda i: (0, 0)
    return pl.pallas_call(
        _outproj_kernel,
        grid=(n // tm,),
        in_specs=[pl.BlockSpec((tm, d), row), pl.BlockSpec((d, d), fix, pipeline_mode=pl.Buffered(1)),
                  pl.BlockSpec((tm, d), row),
                  pl.BlockSpec((1, N_MOD, d), lambda i: (i // tps, 0, 0)),
                  pl.BlockSpec((1, d), fix), pl.BlockSpec((1, d), fix)],
        out_specs=[pl.BlockSpec((tm, d), row), pl.BlockSpec((tm, d), row)],
        out_shape=[jax.ShapeDtypeStruct((n, d), F32), jax.ShapeDtypeStruct((n, d), BF16)],
        compiler_params=_params(("arbitrary",)),
        name="outproj",
    )(merged, w_out, x2, mod, g_post_mix.reshape(1, d), g_pre_ffn.reshape(1, d))


def _gelu_tanh(x):
    return 0.5 * x * (1.0 + jnp.tanh(0.7978845608028654 * (x + 0.044715 * (x * x * x))))


def _ffn_kernel(h_ref, halo_ref, x1_ref, mod_ref, st_ref, wg_ref, wu_ref, wd_ref, wc_ref, bc_ref, gp_ref,
                y_ref, nc_ref, g_scr, act_scr, *, tm, nsub, tps):
    i = pl.program_id(0)
    c = pl.program_id(1)
    last = pl.num_programs(1) - 1
    sub = tm // nsub
    cur = c % 2

    def step(down, gate_up):
        if gate_up:
            h = h_ref[...]
            wg = wg_ref[...]
            g = _dot(h, wg)
            u = _dot(h, wu_ref[...])
            wc = wc_ref[...]
            bc = bc_ref[...]
            g_scr[HALO:HALO + tm, :] = g
        if down == "set":
            y_ref[...] = _dot(act_scr[1 - cur], wd_ref[...])
        elif down == "add":
            y_ref[...] += _dot(act_scr[1 - cur], wd_ref[...])
        if not gate_up:
            return

        def conv(lo, rows):
            return (bc + wc[0:1, :] * g_scr[lo - 2:lo - 2 + rows, :]
                    + wc[1:2, :] * g_scr[lo - 1:lo - 1 + rows, :] + wc[2:3, :] * g_scr[lo:lo + rows, :])

        if nsub == 1:
            g_halo = _dot(halo_ref[...], wg)
            g_scr[0:HALO, :] = g_halo
            g_scr[HALO - 2:HALO, :] = jnp.where(i % tps == 0, st_ref[0], g_halo[HALO - 2:HALO, :])
            nc_ref[0] = g_scr[HALO + tm - 2:HALO + tm, :]
            act_scr[cur] = (_gelu_tanh(conv(HALO, tm)) * u).astype(BF16)
        else:
            for s in range(nsub):
                lo = HALO + s * sub
                nc_ref[s] = g_scr[lo + sub - 2:lo + sub, :]
                g_scr[lo - 2:lo, :] = st_ref[s]
                act_scr[cur, s * sub:(s + 1) * sub, :] = (
                    _gelu_tanh(conv(lo, sub)) * u[s * sub:(s + 1) * sub, :]).astype(BF16)

    pl.when(c == 0)(lambda: step(None, True))
    pl.when(c == 1)(lambda: step("set", True))
    pl.when(jnp.logical_and(c > 1, c < last))(lambda: step("add", True))

    @pl.when(c == last)
    def _():
        step("add", False)
        for s in range(nsub):
            rows = slice(s * sub, (s + 1) * sub)
            y_ref[rows, :] = x1_ref[rows, :] + mod_ref[s][5:6, :] * _rms(y_ref[rows, :], gp_ref[...])


def _ffn(h2, x1, mod, state, w_gate, w_up, w_down, w_conv, b_conv, g_post, seq):
    n, d = x1.shape
    dff = w_gate.shape[1]
    nb = n // seq
    tm = _row_tile(n, seq, 1024)
    tf = _tile(dff, 512, LANES)
    n_chunks = dff // tf
    assert n_chunks >= 2, "the lagged down projection needs at least two d_ff chunks"
    up = lambda c: jnp.minimum(c, n_chunks - 1)
    down = lambda c: jnp.maximum(c - 1, 0)
    per_tile_c = lambda i, c: (i, 0, up(c))
    if tm <= seq:
        nsub, tps = 1, seq // tm
        per_seq = lambda i, c: (i // tps, 0, 0)
        per_seq_c = lambda i, c: (i // tps, 0, up(c))
    else:
        nsub, tps = tm // seq, 1
        per_seq = lambda i, c: (i, 0, 0)
        per_seq_c = per_tile_c
    halo_blocks = tm // HALO
    row = lambda i, c: (i, 0)
    y, new_conv = pl.pallas_call(
        functools.partial(_ffn_kernel, tm=tm, nsub=nsub, tps=tps),
        grid=(n // tm, n_chunks + 1),
        in_specs=[pl.BlockSpec((tm, d), row, pipeline_mode=pl.Buffered(1)),
                  pl.BlockSpec((HALO, d), lambda i, c: (jnp.maximum(i * halo_blocks - 1, 0), 0)),
                  pl.BlockSpec((tm, d), row, pipeline_mode=pl.Buffered(1)),
                  pl.BlockSpec((nsub, N_MOD, d), per_seq),
                  pl.BlockSpec((nsub, CONV_W - 1, tf), per_seq_c),
                  pl.BlockSpec((d, tf), lambda i, c: (0, up(c))),
                  pl.BlockSpec((d, tf), lambda i, c: (0, up(c))),
                  pl.BlockSpec((tf, d), lambda i, c: (down(c), 0)),
                  pl.BlockSpec((CONV_W, tf), lambda i, c: (0, up(c))),
                  pl.BlockSpec((1, tf), lambda i, c: (0, up(c))),
                  pl.BlockSpec((1, d), lambda i, c: (0, 0))],
        out_specs=[pl.BlockSpec((tm, d), row, pipeline_mode=pl.Buffered(1)),
                   pl.BlockSpec((nsub, CONV_W - 1, tf), per_tile_c)],
        out_shape=[jax.ShapeDtypeStruct((n, d), F32),
                   jax.ShapeDtypeStruct((nb * tps, CONV_W - 1, dff), F32)],
        scratch_shapes=[pltpu.VMEM((HALO + tm, tf), F32), pltpu.VMEM((2, tm, tf), BF16)],
        compiler_params=_params(("arbitrary", "arbitrary")),
        name="ffn",
    )(h2, h2, x1, mod, state, w_gate, w_up, w_down, w_conv, b_conv.reshape(1, dff), g_post.reshape(1, d))
    return y, new_conv.reshape(nb, tps, CONV_W - 1, dff)[:, tps - 1]


def _rope_tables(pos):
    def angles(dim):
        half = dim // 2
        inv = ROPE_THETA ** (-jnp.arange(half, dtype=F32) * 2.0 / dim)
        ang = pos.astype(F32)[:, None] * inv[None, :]
        return jnp.cos(ang), jnp.sin(ang)

    cos, sin = angles(HEAD_DIM)
    cos_h = jnp.concatenate([cos, cos], axis=1)
    sin_h = jnp.concatenate([-sin, sin], axis=1)

    cos, sin = angles(D_IDX)
    zero = jnp.zeros_like(sin)
    n_rot = H_IDX + 1
    tail = IDX_COLS - n_rot * D_IDX
    t = pos.shape[0]
    c_i = jnp.concatenate([jnp.tile(jnp.concatenate([cos, cos], 1), (1, n_rot)), jnp.ones((t, tail), F32)], 1)
    s1_i = jnp.concatenate([jnp.tile(jnp.concatenate([-sin, zero], 1), (1, n_rot)), jnp.zeros((t, tail), F32)], 1)
    s2_i = jnp.concatenate([jnp.tile(jnp.concatenate([zero, sin], 1), (1, n_rot)), jnp.zeros((t, tail), F32)], 1)
    return cos_h, sin_h, c_i, s1_i, s2_i


def _with_past(past, new, nb, seq, n_keys):
    w = new.shape[1]
    if past is None and n_keys == seq:
        return new
    parts = [] if past is None else [past.astype(new.dtype).reshape(nb, -1, w)]
    parts.append(new.reshape(nb, seq, w))
    have = sum(p.shape[1] for p in parts)
    if n_keys > have:
        parts.append(jnp.zeros((nb, n_keys - have, w), new.dtype))
    return jnp.concatenate(parts, axis=1).reshape(nb * n_keys, w)


def _layer(x, mod, past, conv_state, p):
    nb, seq, d = x.shape
    n = nb * seq
    nh = p["w_pa"].shape[0] // HEAD_DIM
    width = nh * HEAD_DIM
    n_past = 0 if past is None else past["a_k"].shape[1]
    n_valid = n_past + seq
    n_keys = -(-n_valid // 256) * 256
    x2 = x.reshape(n, d)

    h_hi, h_lo = _norm(x2, p["g_pre_mix"], mod, seq)
    cos_h, sin_h, c_i, s1_i, s2_i = _rope_tables(n_past + jnp.arange(seq, dtype=I32))
    p16, kv32, vt_new = _inproj(h_hi, p["w_main"], cos_h, sin_h, seq, width)
    col = lambda a, g: a[:, g * width:(g + 1) * width]
    qi_hi, qi_lo, kw, ki_hi, ki_lo = _idxproj(h_hi, h_lo, p["w_idx_hi"], p["w_idx_lo"], c_i, s1_i, s2_i, seq)

    def keys(g, past_rows):
        if past_rows is None and n_keys == seq:
            return p16, g * width
        return _with_past(past_rows, col(p16, g), nb, seq, n_keys), 0

    q_hi = qi_hi.reshape(n, H_IDX, D_IDX)
    q_lo = qi_lo.reshape(n, H_IDX, D_IDX)
    qi_cat = jnp.concatenate([q_hi, q_hi, q_lo, jnp.zeros_like(q_hi)], axis=2).reshape(n, H_IDX * IDX_K)
    k_hi, k_lo = ki_hi[:, :D_IDX], ki_lo[:, :D_IDX]
    if past is None:
        pa_k = pa_v = pb_k = pb_v = pi_hi = pi_lo = None
    else:
        pa_k, pa_v, pb_k, pb_v = (past[k].reshape(nb, n_past, width) for k in ("a_k", "a_v", "b_k", "b_v"))
        pi_hi, pi_lo = _split(past["b_kidx"].astype(F32))
    k_hi = _with_past(pi_hi, k_hi, nb, seq, n_keys)
    k_lo = _with_past(pi_lo, k_lo, nb, seq, n_keys)
    ki_cat = jnp.concatenate([k_hi, k_lo, k_hi, jnp.zeros_like(k_hi)], axis=1)

    o_a = _sb((p16, 0), keys(1, pa_k), keys(2, pa_v), nb, seq, n_keys, n_past, nh)
    seq_q = max(seq, LANES)
    pad_q = lambda a: a if seq_q == seq else jnp.pad(
        a.reshape(nb, seq, -1), ((0, 0), (0, seq_q - seq), (0, 0))).reshape(nb * seq_q, -1)
    if vt_new is not None and past is None and n_keys == seq:
        vt_all = vt_new
    else:
        vt_all = _with_past(pb_v, col(p16, 5), nb, seq, n_keys).reshape(nb, n_keys, width)
        vt_all = jnp.swapaxes(vt_all, 1, 2).reshape(nb * width, n_keys)
    q_b = (p16, 3 * width) if seq_q == seq else (pad_q(col(p16, 3)), 0)
    o_b = _dsat(q_b, keys(4, pb_k), vt_all, pad_q(qi_cat), ki_cat,
                pad_q(kw).T, nb, seq_q, n_keys, n_valid, n_past, nh)
    if seq_q != seq:
        o_b = o_b.reshape(nb, seq_q, width)[:, :seq].reshape(n, width)

    merged = _merge(h_hi, o_a, o_b, p["w_gate"], p["b_gate"], p["w_pa"], p["w_pb"])
    x1, h2 = _outproj(merged, p["w_out"], x2, mod, p["g_post_mix"], p["g_pre_ffn"], seq)
    y, new_conv = _ffn(h2, x1, mod, conv_state, p["w_ffn_gate"], p["w_ffn_up"], p["w_ffn_down"],
                       p["w_conv"], p["b_conv"], p["g_post_ffn"], seq)

    heads = lambda g: kv32[g].reshape(nb, seq, nh, HEAD_DIM)
    new = (heads(1), heads(2), heads(4), heads(5), kw[:, :D_IDX].reshape(nb, seq, D_IDX), new_conv)
    return y.reshape(nb, seq, d), new


def kernel(x_prompt, x_sample, c_prompt, c_sample, cache_a_k, cache_a_v, cache_b_k, cache_b_v, cache_b_kidx,
           state_ffn_conv, w_mod, b_mod, g_pre_mix, w_in, w_merge_gate, b_merge_gate, w_proj_a, w_proj_b, w_out,
           g_post_mix, g_pre_ffn, w_ffn_gate, w_ffn_up, w_conv, b_conv, w_ffn_down, g_post_ffn):
    depth = w_mod.shape[0]
    nbp, nbs = x_prompt.shape[0], x_sample.shape[0]
    d = x_prompt.shape[2]
    width = w_proj_a.shape[1]
    dff = w_ffn_gate.shape[2]
    assert w_in.shape[2] == 6 * width + H_IDX * D_IDX + D_IDX + H_IDX
    assert cache_b_kidx.shape[-1] == D_IDX and cache_a_k.shape[-1] == HEAD_DIM

    y_p, y_s = x_prompt, x_sample
    st_p = [[] for _ in range(6)]
    st_s = [[] for _ in range(6)]
    for l in range(depth):
        w_idx = jnp.pad(w_in[l][:, 6 * width:], ((0, 0), (0, IDX_COLS - (w_in.shape[2] - 6 * width))))
        w_idx_hi, w_idx_lo = _split(w_idx)
        p = dict(
            g_pre_mix=g_pre_mix[l], w_main=w_in[l][:, :6 * width].astype(BF16), w_idx_hi=w_idx_hi, w_idx_lo=w_idx_lo,
            w_gate=w_merge_gate[l].astype(BF16), b_gate=b_merge_gate[l],
            w_pa=w_proj_a[l].astype(BF16), w_pb=w_proj_b[l].astype(BF16), w_out=w_out[l].astype(BF16),
            g_post_mix=g_post_mix[l], g_pre_ffn=g_pre_ffn[l],
            w_ffn_gate=w_ffn_gate[l].astype(BF16), w_ffn_up=w_ffn_up[l].astype(BF16),
            w_ffn_down=w_ffn_down[l].astype(BF16), w_conv=w_conv[l], b_conv=b_conv[l], g_post_ffn=g_post_ffn[l])

        c_all = jnp.concatenate([c_prompt, c_sample], axis=0)
        pad = -c_all.shape[0] % 16
        mod = _mod(jnp.pad(c_all, ((0, pad), (0, 0))), w_mod[l], b_mod[l]).reshape(-1, N_MOD, d)
        past = dict(a_k=cache_a_k[l], a_v=cache_a_v[l], b_k=cache_b_k[l], b_v=cache_b_v[l], b_kidx=cache_b_kidx[l])
        y_p, new_p = _layer(y_p, mod[:nbp], None, jnp.zeros((nbp, CONV_W - 1, dff), F32), p)
        y_s, new_s = _layer(y_s, mod[nbp:nbp + nbs], past, state_ffn_conv[l], p)
        for i in range(6):
            st_p[i].append(new_p[i])
            st_s[i].append(new_s[i])
    sp = [jnp.stack(s, axis=0) for s in st_p]
    ss = [jnp.stack(s, axis=0) for s in st_s]
    return (y_p, y_s, sp[0], sp[1], sp[2], sp[3], sp[4], sp[5], ss[0], ss[1], ss[2], ss[3], ss[4], ss[5])
```
